```python
import math
import jax, jax.numpy as jnp
from jax import lax
import numpy as np

D_MODEL = 1024
BATCH = 2
SEQ = 8192
DEPTH = 1

CHUNK = 64
NORM_EPS = 1e-6
GDN_HEADS = 8
GDN_DK = 128
GDN_DV = 128
GDN_CONV = 4
MLA_HEADS = 8
MLA_Q_LORA = 512
MLA_KV_LORA = 256
MLA_NOPE = 128
MLA_ROPE = 64
MLA_V = 128
ROPE_THETA = 10000.0
Q_BLOCK = 128
N_EXPERTS = 32
TOP_K = 4
D_EXPERT = 1024
SWIGLU_LIMIT = 7.0
SWIGLU_ALPHA = 1.702
MOE_BLOCK = 128

GDN_QK_W = GDN_HEADS * GDN_DK
GDN_V_W = GDN_HEADS * GDN_DV
GDN_CONV_W = 2 * GDN_QK_W + GDN_V_W
COL_SIZES = (GDN_QK_W, GDN_QK_W, GDN_V_W, GDN_V_W, GDN_HEADS, GDN_HEADS,
             MLA_Q_LORA, MLA_KV_LORA + MLA_ROPE, D_MODEL, D_MODEL)
IN_COLS = sum(COL_SIZES)
COL_SPLITS = tuple(int(c) for c in np.cumsum(COL_SIZES)[:-1])

kernel_name = 'hybrid_gdn_mla_moe_block'


def rmsnorm(x, w):
    xf = x.astype(jnp.float32)
    xf = xf * lax.rsqrt(jnp.mean(xf * xf, axis=-1, keepdims=True) + NORM_EPS)
    return (xf * w.astype(jnp.float32)).astype(x.dtype)


def l2norm(x):
    xf = x.astype(jnp.float32)
    return xf * lax.rsqrt(jnp.sum(xf * xf, axis=-1, keepdims=True) + NORM_EPS)


def causal_depthwise_conv(x, w):
    k_taps, c = w.shape
    return lax.conv_general_dilated(
        x, w[:, None, :].astype(x.dtype), window_strides=(1,),
        padding=[(k_taps - 1, 0)], dimension_numbers=('NWC', 'WIO', 'NWC'),
        feature_group_count=c)


def to_chunks(t):
    b, s, h = t.shape[:3]
    t = t.reshape((b, s // CHUNK, CHUNK, h) + t.shape[3:])
    return jnp.moveaxis(t, 3, 1)


def gated_delta_rule(q, k, v, g, beta):
    b, s, h, dk = q.shape
    dv = v.shape[-1]
    qc, kc, vc = to_chunks(q), to_chunks(k), to_chunks(v)
    gc = jnp.cumsum(to_chunks(g), axis=-1)
    bc = to_chunks(beta)[..., None]
    idx = jnp.arange(CHUNK)
    incl = idx[:, None] >= idx[None, :]
    strict = idx[:, None] > idx[None, :]
    decay = jnp.exp(jnp.where(incl, gc[..., :, None] - gc[..., None, :], -jnp.inf))
    kb = kc * bc
    lower = jnp.where(strict, jnp.einsum('bhnck,bhnmk->bhncm', kb, kc) * decay, 0.0)
    a_mat = jnp.eye(CHUNK, dtype=jnp.float32) + lower
    u = lax.linalg.triangular_solve(a_mat, vc * bc, left_side=True, lower=True)
    w = lax.linalg.triangular_solve(a_mat, kb * jnp.exp(gc)[..., None], left_side=True, lower=True)
    qk = jnp.where(incl, jnp.einsum('bhnck,bhnmk->bhncm', qc, kc) * decay, 0.0)
    q_dec = qc * jnp.exp(gc)[..., None]
    k_dec = kc * jnp.exp(gc[..., -1:] - gc)[..., None]
    g_last = jnp.exp(gc[..., -1])

    def step(state, inp):
        u_i, w_i, qk_i, qd_i, kd_i, gl_i = inp
        v_new = u_i - jnp.einsum('bhck,bhkv->bhcv', w_i, state)
        o_i = jnp.einsum('bhck,bhkv->bhcv', qd_i, state) + jnp.einsum('bhcm,bhmv->bhcv', qk_i, v_new)
        state = state * gl_i[..., None, None] + jnp.einsum('bhck,bhcv->bhkv', kd_i, v_new)
        return state, o_i

    xs = tuple(jnp.moveaxis(t, 2, 0) for t in (u, w, qk, q_dec, k_dec, g_last))
    state0 = jnp.zeros((b, h, dk, dv), jnp.float32)
    _, o = lax.scan(step, state0, xs)
    return jnp.transpose(o, (1, 0, 3, 2, 4)).reshape(b, s, h, dv)


def rope_tables(s):
    inv = 1.0 / (ROPE_THETA ** (jnp.arange(0, MLA_ROPE, 2, dtype=jnp.float32) / MLA_ROPE))
    ang = jnp.arange(s, dtype=jnp.float32)[:, None] * inv[None, :]
    ang = jnp.concatenate([ang, ang], axis=-1)
    return jnp.cos(ang), jnp.sin(ang)


def apply_rope(x, cos, sin):
    shape = (x.shape[1],) + (1,) * (x.ndim - 3) + (x.shape[-1],)
    xf = x.astype(jnp.float32)
    half = x.shape[-1] // 2
    rot = jnp.concatenate([-xf[..., half:], xf[..., :half]], axis=-1)
    return (xf * cos.reshape(shape) + rot * sin.reshape(shape)).astype(x.dtype)


def chunk_causal_attention(q, k, v):
    b, s, h, dqk = q.shape
    n_blocks = s // Q_BLOCK
    scale = dqk ** -0.5
    k_chunk = jnp.arange(s) // CHUNK

    def one_block(i):
        start = i * Q_BLOCK
        qb = lax.dynamic_slice_in_dim(q, start, Q_BLOCK, axis=1)
        sc = jnp.einsum('bqhd,bkhd->bhqk', qb, k, preferred_element_type=jnp.float32) * scale
        q_chunk = (start + jnp.arange(Q_BLOCK)) // CHUNK
        mask = k_chunk[None, :] <= q_chunk[:, None]
        p = jax.nn.softmax(jnp.where(mask, sc, -jnp.inf), axis=-1).astype(v.dtype)
        return jnp.einsum('bhqk,bkhd->bqhd', p, v)

    o = lax.map(one_block, jnp.arange(n_blocks))
    return jnp.moveaxis(o, 0, 1).reshape(b, s, h, v.shape[-1])


def hybrid_mixer(h, cos, sin, w_in, gdn_conv_w, gdn_a_log, gdn_dt_bias, gdn_norm_w, w_gdn_o,
                 mla_q_norm_w, w_mla_q_b, mla_kv_norm_w, w_mla_kv_b, w_mla_o, w_out):
    b, s, _ = h.shape
    f32 = jnp.float32
    proj = jnp.einsum('bsd,de->bse', h, w_in)
    q_g, k_g, v_g, z_g, b_g, a_g, c_q, c_kv, gate_a, gate_b = jnp.split(proj, COL_SPLITS, axis=-1)

    qkv = jax.nn.silu(causal_depthwise_conv(jnp.concatenate([q_g, k_g, v_g], axis=-1), gdn_conv_w))
    q_g, k_g, v_g = jnp.split(qkv, (GDN_QK_W, 2 * GDN_QK_W), axis=-1)
    q_g = l2norm(q_g.reshape(b, s, GDN_HEADS, GDN_DK)) * (GDN_DK ** -0.5)
    k_g = l2norm(k_g.reshape(b, s, GDN_HEADS, GDN_DK))
    v_g = v_g.reshape(b, s, GDN_HEADS, GDN_DV).astype(f32)
    beta = jax.nn.sigmoid(b_g.astype(f32))
    g = -jnp.exp(gdn_a_log.astype(f32)) * jax.nn.softplus(a_g.astype(f32) + gdn_dt_bias.astype(f32))
    o_a = gated_delta_rule(q_g, k_g, v_g, g, beta)
    o_a = rmsnorm(o_a, gdn_norm_w) * jax.nn.silu(z_g.reshape(b, s, GDN_HEADS, GDN_DV).astype(f32))
    y_a = jnp.einsum('bse,ed->bsd', o_a.reshape(b, s, GDN_V_W).astype(h.dtype), w_gdn_o)

    q = jnp.einsum('bsr,re->bse', rmsnorm(c_q, mla_q_norm_w), w_mla_q_b)
    q = q.reshape(b, s, MLA_HEADS, MLA_NOPE + MLA_ROPE)
    q_nope, q_rope = q[..., :MLA_NOPE], apply_rope(q[..., MLA_NOPE:], cos, sin)
    kv_lat = rmsnorm(c_kv[..., :MLA_KV_LORA], mla_kv_norm_w)
    k_rope = apply_rope(c_kv[..., MLA_KV_LORA:], cos, sin)
    kv = jnp.einsum('bsr,re->bse', kv_lat, w_mla_kv_b).reshape(b, s, MLA_HEADS, MLA_NOPE + MLA_V)
    k_nope, v = kv[..., :MLA_NOPE], kv[..., MLA_NOPE:]
    q_full = jnp.concatenate([q_nope, q_rope], axis=-1)
    k_full = jnp.concatenate([k_nope, jnp.broadcast_to(k_rope[:, :, None, :], (b, s, MLA_HEADS, MLA_ROPE))], axis=-1)
    o_b = chunk_causal_attention(q_full, k_full, v)
    y_b = jnp.einsum('bse,ed->bsd', o_b.reshape(b, s, MLA_HEADS * MLA_V), w_mla_o)

    merged = jax.nn.sigmoid(gate_a) * y_a + jax.nn.sigmoid(gate_b) * y_b
    return jnp.einsum('bsd,de->bse', merged, w_out)


def moe_ffn(h, w_router, b_router, w_gate_up, b_gate_up, w_down, b_down):
    b, s, d = h.shape
    f32 = jnp.float32
    n_tok = b * s
    hf = h.reshape(n_tok, d)
    logits = (hf @ w_router).astype(f32) + b_router.astype(f32)
    top_logits, top_idx = lax.top_k(logits, TOP_K)
    top_w = jax.nn.softmax(top_logits, axis=-1)
    n_assign = n_tok * TOP_K
    flat_e = top_idx.reshape(-1)
    order = jnp.argsort(flat_e)
    sorted_e = flat_e[order]
    counts = jnp.bincount(flat_e, length=N_EXPERTS)
    start = jnp.cumsum(counts) - counts
    padded = (counts + MOE_BLOCK - 1) // MOE_BLOCK * MOE_BLOCK
    pad_end = jnp.cumsum(padded)
    pad_start = pad_end - padded
    dest = pad_start[sorted_e] + jnp.arange(n_assign) - start[sorted_e]
    n_pad = n_assign + N_EXPERTS * MOE_BLOCK
    n_blocks = n_pad // MOE_BLOCK
    slot_tok = jnp.zeros((n_pad,), jnp.int32).at[dest].set((order // TOP_K).astype(jnp.int32))
    slot_w = jnp.zeros((n_pad,), f32).at[dest].set(top_w.reshape(-1)[order])
    block_e = jnp.minimum(jnp.searchsorted(pad_end, jnp.arange(n_blocks) * MOE_BLOCK, side='right'), N_EXPERTS - 1)
    xs = hf[slot_tok].reshape(n_blocks, MOE_BLOCK, d)

    def expert_block(args):
        xb, e = args
        gu = xb @ w_gate_up[e] + b_gate_up[e]
        gate = jnp.minimum(gu[:, ::2], SWIGLU_LIMIT)
        up = jnp.clip(gu[:, 1::2], -SWIGLU_LIMIT, SWIGLU_LIMIT)
        act = (up + 1.0) * (gate * jax.nn.sigmoid(gate * SWIGLU_ALPHA))
        return act @ w_down[e] + b_down[e]

    ys = lax.map(expert_block, (xs, block_e)).reshape(n_pad, d)
    y = jax.ops.segment_sum(ys.astype(f32) * slot_w[:, None], slot_tok, num_segments=n_tok)
    return y.astype(h.dtype).reshape(b, s, d)


def setup_inputs(seed: int = 0) -> dict:
    key = jax.random.key(seed)
    ks = iter(jax.random.split(key, 24))
    f32 = jnp.float32

    def nrm(shape, scale):
        return jax.random.normal(next(ks), shape, f32) * scale

    L = DEPTH
    x = nrm((BATCH, SEQ, D_MODEL), 1.0)
    norm_mix_w = 1.0 + nrm((L, D_MODEL), 0.05)
    w_in = nrm((L, D_MODEL, IN_COLS), D_MODEL ** -0.5)
    gdn_conv_w = nrm((L, GDN_CONV, GDN_CONV_W), GDN_CONV ** -0.5)
    gdn_a_log = jnp.log(jax.random.uniform(next(ks), (L, GDN_HEADS), f32, 1.0, 16.0))
    dt = jnp.exp(jax.random.uniform(next(ks), (L, GDN_HEADS), f32, math.log(1e-3), math.log(1e-1)))
    gdn_dt_bias = dt + jnp.log(-jnp.expm1(-dt))
    gdn_norm_w = 1.0 + nrm((L, GDN_DV), 0.05)
    w_gdn_o = nrm((L, GDN_V_W, D_MODEL), GDN_V_W ** -0.5)
    mla_q_norm_w = 1.0 + nrm((L, MLA_Q_LORA), 0.05)
    w_mla_q_b = nrm((L, MLA_Q_LORA, MLA_HEADS * (MLA_NOPE + MLA_ROPE)), MLA_Q_LORA ** -0.5)
    mla_kv_norm_w = 1.0 + nrm((L, MLA_KV_LORA), 0.05)
    w_mla_kv_b = nrm((L, MLA_KV_LORA, MLA_HEADS * (MLA_NOPE + MLA_V)), MLA_KV_LORA ** -0.5)
    w_mla_o = nrm((L, MLA_HEADS * MLA_V, D_MODEL), (MLA_HEADS * MLA_V) ** -0.5)
    w_out = nrm((L, D_MODEL, D_MODEL), D_MODEL ** -0.5)
    norm_ffn_w = 1.0 + nrm((L, D_MODEL), 0.05)
    w_router = nrm((L, D_MODEL, N_EXPERTS), D_MODEL ** -0.5)
    b_router = nrm((L, N_EXPERTS), 0.01)
    w_gate_up = nrm((L, N_EXPERTS, D_MODEL, 2 * D_EXPERT), D_MODEL ** -0.5)
    b_gate_up = nrm((L, N_EXPERTS, 2 * D_EXPERT), 0.01)
    w_down = nrm((L, N_EXPERTS, D_EXPERT, D_MODEL), D_EXPERT ** -0.5)
    b_down = nrm((L, N_EXPERTS, D_MODEL), 0.01)
    norm_final_w = 1.0 + nrm((D_MODEL,), 0.05)
    return {'x': x, 'norm_mix_w': norm_mix_w, 'w_in': w_in, 'gdn_conv_w': gdn_conv_w,
            'gdn_a_log': gdn_a_log, 'gdn_dt_bias': gdn_dt_bias, 'gdn_norm_w': gdn_norm_w,
            'w_gdn_o': w_gdn_o, 'mla_q_norm_w': mla_q_norm_w, 'w_mla_q_b': w_mla_q_b,
            'mla_kv_norm_w': mla_kv_norm_w, 'w_mla_kv_b': w_mla_kv_b, 'w_mla_o': w_mla_o,
            'w_out': w_out, 'norm_ffn_w': norm_ffn_w, 'w_router': w_router, 'b_router': b_router,
            'w_gate_up': w_gate_up, 'b_gate_up': b_gate_up, 'w_down': w_down, 'b_down': b_down,
            'norm_final_w': norm_final_w}


def reference(x, norm_mix_w, w_in, gdn_conv_w, gdn_a_log, gdn_dt_bias, gdn_norm_w, w_gdn_o,
              mla_q_norm_w, w_mla_q_b, mla_kv_norm_w, w_mla_kv_b, w_mla_o, w_out,
              norm_ffn_w, w_router, b_router, w_gate_up, b_gate_up, w_down, b_down, norm_final_w):
    cos, sin = rope_tables(x.shape[1])
    for layer in range(DEPTH):
        h = rmsnorm(x, norm_mix_w[layer])
        x = x + hybrid_mixer(h, cos, sin, w_in[layer], gdn_conv_w[layer], gdn_a_log[layer],
                             gdn_dt_bias[layer], gdn_norm_w[layer], w_gdn_o[layer],
                             mla_q_norm_w[layer], w_mla_q_b[layer], mla_kv_norm_w[layer],
                             w_mla_kv_b[layer], w_mla_o[layer], w_out[layer])
        h = rmsnorm(x, norm_ffn_w[layer])
        x = x + moe_ffn(h, w_router[layer], b_router[layer], w_gate_up[layer], b_gate_up[layer],
                        w_down[layer], b_down[layer])
    return rmsnorm(x, norm_final_w)
```

```python
import functools

import jax
import jax.numpy as jnp
import numpy as np
from jax import lax
from jax.experimental import pallas as pl
from jax.experimental.pallas import tpu as pltpu

F32 = jnp.float32
BF16 = jnp.bfloat16

CHUNK = 64
NORM_EPS = 1e-6
GDN_HEADS = 8
GDN_D = 128
GDN_CONV = 4
MLA_HEADS = 8
MLA_Q_LORA = 512
MLA_KV_LORA = 256
MLA_NOPE = 128
MLA_ROPE = 64
MLA_V = 128
ROPE_THETA = 10000.0
N_EXPERTS = 32
TOP_K = 4
SWIGLU_LIMIT = 7.0
SWIGLU_ALPHA = 1.702

LANES = 128
MOE_ROWS = 256
VMEM_LIMIT = 48 * 1024 * 1024

NEG_BIG = -1e30


def _cp(sem):
    return pltpu.CompilerParams(dimension_semantics=sem, vmem_limit_bytes=VMEM_LIMIT)


def _dot(a, b):
    return jnp.dot(a, b, preferred_element_type=F32)


def _dot_nt(a, b):
    return lax.dot_general(a, b, (((1,), (1,)), ((), ())), preferred_element_type=F32)


def _dot_tn(a, b):
    return lax.dot_general(a, b, (((0,), (0,)), ((), ())), preferred_element_type=F32)


def _split3(x):
    hi = x.astype(BF16)
    r = x - hi.astype(F32)
    mid = r.astype(BF16)
    lo = (r - mid.astype(F32)).astype(BF16)
    return hi, mid, lo


def _rms(x, w):
    ms = jnp.mean(x * x, axis=-1, keepdims=True)
    return x * lax.rsqrt(ms + NORM_EPS) * w


def _in_proj_kernel(x_ref, nw_ref, w_ref, wab_ref, p_ref, ab_ref, h_ref):
    @pl.when(pl.program_id(1) == 0)
    def _():
        hb = _rms(x_ref[...], nw_ref[...]).astype(BF16)
        h_ref[...] = hb
        ab_ref[...] = _dot(hb, wab_ref[...])

    p_ref[...] = _dot(h_ref[...], w_ref[...]).astype(BF16)


def _in_proj(x2, norm_w, w_p, w_ab, tm=512, tn=1024):
    t, d = x2.shape
    n = w_p.shape[1]
    return pl.pallas_call(
        _in_proj_kernel,
        grid=(t // tm, n // tn),
        in_specs=[
            pl.BlockSpec((tm, d), lambda i, j: (i, 0)),
            pl.BlockSpec((1, d), lambda i, j: (0, 0)),
            pl.BlockSpec((d, tn), lambda i, j: (0, j)),
            pl.BlockSpec((d, LANES), lambda i, j: (0, 0)),
        ],
        out_specs=[
            pl.BlockSpec((tm, tn), lambda i, j: (i, j)),
            pl.BlockSpec((tm, LANES), lambda i, j: (i, 0)),
        ],
        out_shape=[
            jax.ShapeDtypeStruct((t, n), BF16),
            jax.ShapeDtypeStruct((t, LANES), F32),
        ],
        scratch_shapes=[pltpu.VMEM((tm, d), BF16)],
        compiler_params=_cp(("parallel", "arbitrary")),
        name="in_proj",
    )(x2, norm_w, w_p, w_ab)


def _gdn_prep_kernel(cur_ref, prev_ref, ab_ref, cw_ref, alog_ref, dtb_ref,
                     qkv_ref, cols_ref, gct_ref, *, tiles_per_seq):
    tm = cur_ref.shape[0]
    i = pl.program_id(0)
    halo_on = (i % tiles_per_seq) != 0
    n_blk = cur_ref.shape[1] // LANES
    q_scale = GDN_D ** -0.5
    for cb in range(n_blk):
        cs = slice(cb * LANES, (cb + 1) * LANES)
        cur = cur_ref[:, cs].astype(F32)
        halo = prev_ref[:, cs].astype(F32)[8:16, :]
        halo = jnp.where(halo_on, halo, 0.0)
        xe = jnp.concatenate([halo, cur], axis=0)
        w = cw_ref[:, cs]
        y = w[0:1, :] * xe[5:5 + tm, :]
        for j in range(1, GDN_CONV):
            y = y + w[j:j + 1, :] * xe[5 + j:5 + j + tm, :]
        y = y * jax.nn.sigmoid(y)
        if cb < 2 * GDN_HEADS:
            ss = jnp.sum(y * y, axis=-1, keepdims=True)
            y = y * lax.rsqrt(ss + NORM_EPS)
            if cb < GDN_HEADS:
                y = y * q_scale
        qkv_ref[:, cs] = y.astype(BF16)

    ab = ab_ref[...]
    g = -jnp.exp(alog_ref[...]) * jax.nn.softplus(ab + dtb_ref[...])
    row = lax.broadcasted_iota(jnp.int32, (tm, tm), 0)
    col = lax.broadcasted_iota(jnp.int32, (tm, tm), 1)
    tri = ((col <= row) & ((row // CHUNK) == (col // CHUNK))).astype(BF16)
    g_hi, g_mid, g_lo = _split3(g)
    gc = _dot(tri, g_hi) + _dot(tri, g_mid) + _dot(tri, g_lo)
    lane = lax.broadcasted_iota(jnp.int32, (tm, LANES), 1)
    cols_ref[...] = jnp.where(lane < GDN_HEADS, gc, jax.nn.sigmoid(ab))
    for c in range(tm // CHUNK):
        blk = gc[c * CHUNK:(c + 1) * CHUNK, :]
        blk = jnp.concatenate([blk, jnp.zeros_like(blk)], axis=0)
        gct_ref[c] = blk.T[0:GDN_HEADS, 0:CHUNK]


def _gdn_prep(p, ab, conv_w, alog_row, dtb_row, seq, tm=256):
    t = p.shape[0]
    cw = 3 * GDN_HEADS * GDN_D
    tiles_per_seq = seq // tm
    kern = functools.partial(_gdn_prep_kernel, tiles_per_seq=tiles_per_seq)
    return pl.pallas_call(
        kern,
        grid=(t // tm,),
        in_specs=[
            pl.BlockSpec((tm, cw), lambda i: (i, 0)),
            pl.BlockSpec((16, cw), lambda i: (jnp.maximum(i * (tm // 16) - 1, 0), 0)),
            pl.BlockSpec((tm, LANES), lambda i: (i, 0)),
            pl.BlockSpec((GDN_CONV, cw), lambda i: (0, 0)),
            pl.BlockSpec((1, LANES), lambda i: (0, 0)),
            pl.BlockSpec((1, LANES), lambda i: (0, 0)),
        ],
        out_specs=[
            pl.BlockSpec((tm, cw), lambda i: (i, 0)),
            pl.BlockSpec((tm, LANES), lambda i: (i, 0)),
            pl.BlockSpec((tm // CHUNK, GDN_HEADS, CHUNK), lambda i: (i, 0, 0)),
        ],
        out_shape=[
            jax.ShapeDtypeStruct((t, cw), BF16),
            jax.ShapeDtypeStruct((t, LANES), F32),
            jax.ShapeDtypeStruct((t // CHUNK, GDN_HEADS, CHUNK), F32),
        ],
        compiler_params=_cp(("parallel",)),
        name="gdn_prep",
    )(p, p, ab, conv_w, alog_row, dtb_row)


def _gdn_chunk_kernel(q_ref, k_ref, v_ref, z_ref, cols_ref, gct_ref, nw_ref, o_ref, s_ref):
    c = CHUNK

    @pl.when(pl.program_id(1) == 0)
    def _():
        s_ref[...] = jnp.zeros_like(s_ref)

    cols = cols_ref[...]
    g_last_row = cols[c - 1:c, :]
    e_g = jnp.exp(cols)
    e_kd = jnp.exp(g_last_row - cols)
    e_last = jnp.exp(g_last_row)
    gct = gct_ref[0]
    ri = lax.broadcasted_iota(jnp.int32, (c, c), 0)
    ci = lax.broadcasted_iota(jnp.int32, (c, c), 1)
    incl = ri >= ci
    strict = ri > ci
    eye = (ri == ci).astype(F32)
    nw = nw_ref[...]

    for h in range(GDN_HEADS):
        hs = slice(h * GDN_D, (h + 1) * GDN_D)
        q = q_ref[:, hs]
        k = k_ref[:, hs]
        v = v_ref[:, hs].astype(F32)
        kf = k.astype(F32)
        gc_c = cols[:, h:h + 1]
        beta = cols[:, GDN_HEADS + h:GDN_HEADS + h + 1]
        eg = e_g[:, h:h + 1]
        ekd = e_kd[:, h:h + 1]
        egl = e_last[:, h:h + 1]
        gc_r = gct[h:h + 1, :]

        kb = kf * beta
        kbb = kb.astype(BF16)
        kq = _dot_nt(jnp.concatenate([kbb, q], axis=0), k)
        dec = jnp.exp(jnp.minimum(gc_c - gc_r, 0.0))
        a = jnp.where(strict, -kq[0:c, :] * dec, 0.0)
        qk = jnp.where(incl, kq[c:2 * c, :] * dec, 0.0)

        tinv = eye + a
        pw = a
        for _ in range(5):
            pwb = pw.astype(BF16)
            pw = _dot(pwb, pwb)
            tinv = tinv + _dot(tinv.astype(BF16), pw.astype(BF16))

        rhs = jnp.concatenate([v * beta, kb * eg], axis=1).astype(BF16)
        uw = _dot(tinv.astype(BF16), rhs)
        u = uw[:, 0:GDN_D]
        w = uw[:, GDN_D:2 * GDN_D]

        s = s_ref[h]
        qd = (q.astype(F32) * eg).astype(BF16)
        r = _dot(jnp.concatenate([w.astype(BF16), qd], axis=0), s.astype(BF16))
        v_new = u - r[0:c, :]
        v_new_b = v_new.astype(BF16)
        o = r[c:2 * c, :] + _dot(qk.astype(BF16), v_new_b)
        kd = (kf * ekd).astype(BF16)
        s_ref[h] = s * egl + _dot_tn(kd, v_new_b)

        z = z_ref[:, hs].astype(F32)
        o = _rms(o, nw) * (z * jax.nn.sigmoid(z))
        o_ref[:, hs] = o.astype(BF16)


def _gdn_chunk(qkvn, p, cols, gct, norm_w, batch, seq):
    t = qkvn.shape[0]
    nc = seq // CHUNK
    hw = GDN_HEADS * GDN_D
    row = lambda b, c: b * nc + c
    return pl.pallas_call(
        _gdn_chunk_kernel,
        grid=(batch, nc),
        in_specs=[
            pl.BlockSpec((CHUNK, hw), lambda b, c: (row(b, c), 0)),
            pl.BlockSpec((CHUNK, hw), lambda b, c: (row(b, c), 1)),
            pl.BlockSpec((CHUNK, hw), lambda b, c: (row(b, c), 2)),
            pl.BlockSpec((CHUNK, hw), lambda b, c: (row(b, c), 3)),
            pl.BlockSpec((CHUNK, LANES), lambda b, c: (row(b, c), 0)),
            pl.BlockSpec((1, GDN_HEADS, CHUNK), lambda b, c: (row(b, c), 0, 0)),
            pl.BlockSpec((1, GDN_D), lambda b, c: (0, 0)),
        ],
        out_specs=pl.BlockSpec((CHUNK, hw), lambda b, c: (row(b, c), 0)),
        out_shape=jax.ShapeDtypeStruct((t, hw), BF16),
        scratch_shapes=[pltpu.VMEM((GDN_HEADS, GDN_D, GDN_D), F32)],
        compiler_params=_cp(("parallel", "arbitrary")),
        name="gdn_chunk",
    )(qkvn, qkvn, qkvn, p, cols, gct, norm_w)


def _rope(x, cos, sin_signed):
    lane = lax.broadcasted_iota(jnp.int32, x.shape, 1)
    fwd = pltpu.roll(x, LANES - MLA_ROPE // 2, 1)
    bwd = pltpu.roll(x, MLA_ROPE // 2, 1)
    rot = jnp.where(lane < MLA_ROPE // 2, fwd, bwd)
    return x * cos + rot * sin_signed


def _mla_prep_kernel(cq_ref, ckv_ref, kr_ref, cos_ref, sin_ref, qnw_ref, kvnw_ref, wq_ref, wkv_ref,
                     q_ref, kn_ref, kro_ref, v_ref):
    cos = cos_ref[...]
    sin = sin_ref[...]
    cq = _rms(cq_ref[...].astype(F32), qnw_ref[...]).astype(BF16)
    hd = 2 * LANES
    scale = (MLA_NOPE + MLA_ROPE) ** -0.5
    for h in range(MLA_HEADS):
        qh = _dot(cq, wq_ref[:, h * hd:(h + 1) * hd]) * scale
        q_ref[:, h * hd:h * hd + LANES] = qh[:, 0:LANES].astype(BF16)
        q_ref[:, h * hd + LANES:(h + 1) * hd] = _rope(qh[:, LANES:hd], cos, sin).astype(BF16)
    kvl = _rms(ckv_ref[...].astype(F32), kvnw_ref[...]).astype(BF16)
    hw = MLA_HEADS * MLA_NOPE
    kn_ref[...] = _dot(kvl, wkv_ref[:, 0:hw]).astype(BF16)
    v_ref[...] = _dot(kvl, wkv_ref[:, hw:2 * hw]).astype(BF16)
    kro_ref[...] = _rope(kr_ref[...].astype(F32), cos, sin).astype(BF16)


def _mla_prep(p, cos_t, sin_t, qnw, kvnw, wq, wkv, seq, tm=512):
    t = p.shape[0]
    tiles_per_seq = seq // tm
    hw = MLA_HEADS * MLA_NOPE
    cq_blk = 6144 // MLA_Q_LORA
    ckv_blk = 6656 // MLA_KV_LORA
    kr_blk = 6912 // LANES
    return pl.pallas_call(
        _mla_prep_kernel,
        grid=(t // tm,),
        in_specs=[
            pl.BlockSpec((tm, MLA_Q_LORA), lambda i: (i, cq_blk)),
            pl.BlockSpec((tm, MLA_KV_LORA), lambda i: (i, ckv_blk)),
            pl.BlockSpec((tm, LANES), lambda i: (i, kr_blk)),
            pl.BlockSpec((tm, LANES), lambda i: (i % tiles_per_seq, 0)),
            pl.BlockSpec((tm, LANES), lambda i: (i % tiles_per_seq, 0)),
            pl.BlockSpec((1, MLA_Q_LORA), lambda i: (0, 0)),
            pl.BlockSpec((1, MLA_KV_LORA), lambda i: (0, 0)),
            pl.BlockSpec((MLA_Q_LORA, 2 * hw), lambda i: (0, 0)),
            pl.BlockSpec((MLA_KV_LORA, 2 * hw), lambda i: (0, 0)),
        ],
        out_specs=[
            pl.BlockSpec((tm, 2 * hw), lambda i: (i, 0)),
            pl.BlockSpec((tm, hw), lambda i: (i, 0)),
            pl.BlockSpec((tm, LANES), lambda i: (i, 0)),
            pl.BlockSpec((tm, hw), lambda i: (i, 0)),
        ],
        out_shape=[
            jax.ShapeDtypeStruct((t, 2 * hw), BF16),
            jax.ShapeDtypeStruct((t, hw), BF16),
            jax.ShapeDtypeStruct((t, LANES), BF16),
            jax.ShapeDtypeStruct((t, hw), BF16),
        ],
        compiler_params=_cp(("parallel",)),
        name="mla_prep",
    )(p, p, p, cos_t, sin_t, qnw, kvnw, wq, wkv)


def _mla_attn_kernel(qt_ref, kt_ref, q_ref, kn_ref, kr_ref, v_ref, o_ref, m_ref, l_ref, acc_ref):
    qi = qt_ref[pl.program_id(2)]
    ki = kt_ref[pl.program_id(2)]
    tq = q_ref.shape[0]
    tk = kn_ref.shape[0]

    @pl.when(ki == 0)
    def _():
        m_ref[...] = jnp.full_like(m_ref, NEG_BIG)
        l_ref[...] = jnp.zeros_like(l_ref)
        acc_ref[...] = jnp.zeros_like(acc_ref)

    def step(masked):
        k = jnp.concatenate([kn_ref[...], kr_ref[...]], axis=1)
        s = _dot_nt(q_ref[...], k)
        if masked:
            rq = lax.broadcasted_iota(jnp.int32, (tq, tk), 0) // CHUNK
            ck = lax.broadcasted_iota(jnp.int32, (tq, tk), 1) // CHUNK
            s = jnp.where(ck <= rq, s, NEG_BIG)
        m_prev = m_ref[...]
        m_new = jnp.maximum(m_prev, jnp.max(s, axis=-1, keepdims=True))
        alpha = jnp.exp(m_prev - m_new)
        p = jnp.exp(s - m_new)
        l_ref[...] = alpha * l_ref[...] + jnp.sum(p, axis=-1, keepdims=True)
        acc_ref[...] = alpha * acc_ref[...] + _dot(p.astype(BF16), v_ref[...])
        m_ref[...] = m_new

    @pl.when(ki < qi)
    def _():
        step(False)

    @pl.when(ki == qi)
    def _():
        step(True)
        o_ref[...] = (acc_ref[...] / l_ref[...]).astype(BF16)


def _mla_attn(q, kn, kr, v, batch, seq, tq=512):
    t = q.shape[0]
    nq = seq // tq
    pairs = [(qi, ki) for qi in range(nq) for ki in range(qi + 1)]
    qt = jnp.asarray(np.array([pr[0] for pr in pairs], np.int32))
    kt = jnp.asarray(np.array([pr[1] for pr in pairs], np.int32))
    return pl.pallas_call(
        _mla_attn_kernel,
        grid_spec=pltpu.PrefetchScalarGridSpec(
            num_scalar_prefetch=2,
            grid=(batch, MLA_HEADS, len(pairs)),
            in_specs=[
                pl.BlockSpec((tq, 2 * LANES), lambda b, h, pr, qt, kt: (b * nq + qt[pr], h)),
                pl.BlockSpec((tq, MLA_NOPE), lambda b, h, pr, qt, kt: (b * nq + kt[pr], h)),
                pl.BlockSpec((tq, LANES), lambda b, h, pr, qt, kt: (b * nq + kt[pr], 0)),
                pl.BlockSpec((tq, MLA_V), lambda b, h, pr, qt, kt: (b * nq + kt[pr], h)),
            ],
            out_specs=pl.BlockSpec((tq, MLA_V), lambda b, h, pr, qt, kt: (b * nq + qt[pr], h)),
            scratch_shapes=[
                pltpu.VMEM((tq, 1), F32),
                pltpu.VMEM((tq, 1), F32),
                pltpu.VMEM((tq, MLA_V), F32),
            ],
        ),
        out_shape=jax.ShapeDtypeStruct((t, MLA_HEADS * MLA_V), BF16),
        compiler_params=_cp(("parallel", "parallel", "arbitrary")),
        name="mla_attn",
    )(qt, kt, q, kn, kr, v)


def _mix_out_kernel(x_ref, oa_ref, ob_ref, ga_ref, gb_ref, wga_ref, wmo_ref, wout_ref, nw_ref,
                    wr_hi_ref, wr_lo_ref, br_ref, x1_ref, h2_ref, sel_ref, idx_ref, tw_ref):
    ya = _dot(oa_ref[...], wga_ref[...])
    yb = _dot(ob_ref[...], wmo_ref[...])
    merged = (jax.nn.sigmoid(ga_ref[...].astype(F32)) * ya
              + jax.nn.sigmoid(gb_ref[...].astype(F32)) * yb)
    x1 = x_ref[...] + _dot(merged.astype(BF16), wout_ref[...])
    x1_ref[...] = x1
    h2 = _rms(x1, nw_ref[...])
    h2_ref[...] = h2

    h_hi = h2.astype(BF16)
    h_lo = (h2 - h_hi.astype(F32)).astype(BF16)
    logits = (_dot(h_hi, wr_hi_ref[...]) + _dot(h_hi, wr_lo_ref[...]) + _dot(h_lo, wr_hi_ref[...])
              + br_ref[...])
    lane = lax.broadcasted_iota(jnp.int32, logits.shape, 1)
    work = jnp.where(lane < N_EXPERTS, logits, -jnp.inf)
    sel = jnp.zeros(logits.shape, F32)
    idx_out = jnp.zeros(logits.shape, jnp.int32)
    tw_out = jnp.zeros(logits.shape, F32)
    top = None
    denom = None
    for kk in range(TOP_K):
        mx = jnp.max(work, axis=-1, keepdims=True)
        am = jnp.min(jnp.where(work == mx, lane, LANES), axis=-1, keepdims=True)
        hit = lane == am
        if kk == 0:
            top = mx
            e = jnp.ones_like(mx)
            denom = e
        else:
            e = jnp.exp(mx - top)
            denom = denom + e
        sel = jnp.where(hit, 1.0, sel)
        idx_out = jnp.where(lane == kk, am, idx_out)
        tw_out = jnp.where(lane == kk, e, tw_out)
        work = jnp.where(hit, -jnp.inf, work)
    sel_ref[...] = sel.astype(BF16)
    idx_ref[...] = idx_out
    tw_ref[...] = tw_out / denom


def _mix_out(x2, oa, ob, p, wga, wmo, wout, nw, wr_hi, wr_lo, br, tm=256):
    t, d = x2.shape
    full = lambda i: (0, 0)
    return pl.pallas_call(
        _mix_out_kernel,
        grid=(t // tm,),
        in_specs=[
            pl.BlockSpec((tm, d), lambda i: (i, 0)),
            pl.BlockSpec((tm, d), lambda i: (i, 0)),
            pl.BlockSpec((tm, d), lambda i: (i, 0)),
            pl.BlockSpec((tm, d), lambda i: (i, 4)),
            pl.BlockSpec((tm, d), lambda i: (i, 5)),
            pl.BlockSpec((d, d), full),
            pl.BlockSpec((d, d), full),
            pl.BlockSpec((d, d), full),
            pl.BlockSpec((1, d), full),
            pl.BlockSpec((d, LANES), full),
            pl.BlockSpec((d, LANES), full),
            pl.BlockSpec((1, LANES), full),
        ],
        out_specs=[
            pl.BlockSpec((tm, d), lambda i: (i, 0)),
            pl.BlockSpec((tm, d), lambda i: (i, 0)),
            pl.BlockSpec((tm, LANES), lambda i: (i, 0)),
            pl.BlockSpec((tm, LANES), lambda i: (i, 0)),
            pl.BlockSpec((tm, LANES), lambda i: (i, 0)),
        ],
        out_shape=[
            jax.ShapeDtypeStruct((t, d), F32),
            jax.ShapeDtypeStruct((t, d), F32),
            jax.ShapeDtypeStruct((t, LANES), BF16),
            jax.ShapeDtypeStruct((t, LANES), jnp.int32),
            jax.ShapeDtypeStruct((t, LANES), F32),
        ],
        compiler_params=_cp(("parallel",)),
        name="mix_out",
    )(x2, oa, ob, p, p, wga, wmo, wout, nw, wr_hi, wr_lo, br)


def _route_pos_kernel(sel_ref, idx_ref, pos_ref, cnt_ref, carry_ref):
    tm = sel_ref.shape[0]
    i = pl.program_id(0)

    @pl.when(i == 0)
    def _():
        carry_ref[...] = jnp.zeros_like(carry_ref)

    sel = sel_ref[...]
    row = lax.broadcasted_iota(jnp.int32, (tm, tm), 0)
    col = lax.broadcasted_iota(jnp.int32, (tm, tm), 1)
    tri = (col < row).astype(BF16)
    carry = carry_ref[0:1, :]
    pos = _dot(tri, sel) + carry
    lane = lax.broadcasted_iota(jnp.int32, (tm, LANES), 1)
    idx = idx_ref[...]
    out = jnp.zeros((tm, LANES), F32)
    for kk in range(TOP_K):
        hit = lane == idx[:, kk:kk + 1]
        pk = jnp.sum(jnp.where(hit, pos, 0.0), axis=-1, keepdims=True)
        out = jnp.where(lane == kk, pk, out)
    pos_ref[...] = out.astype(jnp.int32)
    total = carry + jnp.sum(sel.astype(F32), axis=0, keepdims=True)
    carry_ref[...] = jnp.broadcast_to(total, carry_ref.shape)
    cnt_ref[...] = jnp.broadcast_to(total, cnt_ref.shape).astype(jnp.int32)


def _route_pos(sel, idx, tm=512):
    t = sel.shape[0]
    return pl.pallas_call(
        _route_pos_kernel,
        grid=(t // tm,),
        in_specs=[
            pl.BlockSpec((tm, LANES), lambda i: (i, 0)),
            pl.BlockSpec((tm, LANES), lambda i: (i, 0)),
        ],
        out_specs=[
            pl.BlockSpec((tm, LANES), lambda i: (i, 0)),
            pl.BlockSpec((8, LANES), lambda i: (0, 0)),
        ],
        out_shape=[
            jax.ShapeDtypeStruct((t, LANES), jnp.int32),
            jax.ShapeDtypeStruct((8, LANES), jnp.int32),
        ],
        scratch_shapes=[pltpu.VMEM((8, LANES), F32)],
        compiler_params=_cp(("arbitrary",)),
        name="route_pos",
    )(sel, idx)


def _dispatch_kernel(dest_ref, h_ref, xs_in_ref, xs_ref, sem):
    del xs_in_ref
    tm = h_ref.shape[0]
    base = pl.program_id(0) * tm * TOP_K

    def copy(r, kk):
        d = dest_ref[base + r * TOP_K + kk]
        return pltpu.make_async_copy(h_ref.at[pl.ds(r, 1)], xs_ref.at[pl.ds(d, 1)], sem)

    def issue(r, carry):
        for kk in range(TOP_K):
            copy(r, kk).start()
        return carry

    def drain(r, carry):
        for kk in range(TOP_K):
            copy(r, kk).wait()
        return carry

    lax.fori_loop(0, tm, issue, 0)
    lax.fori_loop(0, tm, drain, 0)


def _dispatch(dest_flat, h2, n_pad, tm=128):
    t, d = h2.shape
    xs0 = jnp.zeros((n_pad, d), h2.dtype)
    return pl.pallas_call(
        _dispatch_kernel,
        grid_spec=pltpu.PrefetchScalarGridSpec(
            num_scalar_prefetch=1,
            grid=(t // tm,),
            in_specs=[
                pl.BlockSpec((tm, d), lambda i, dest: (i, 0)),
                pl.BlockSpec(memory_space=pl.ANY),
            ],
            out_specs=pl.BlockSpec(memory_space=pl.ANY),
            scratch_shapes=[pltpu.SemaphoreType.DMA(())],
        ),
        out_shape=jax.ShapeDtypeStruct((n_pad, d), h2.dtype),
        input_output_aliases={2: 0},
        compiler_params=_cp(("arbitrary",)),
        name="dispatch",
    )(dest_flat, h2, xs0)


def _experts_kernel(be_ref, nv_ref, xs_ref, wg_ref, wu_ref, wd_ref, bg_ref, bu_ref, bd_ref, ys_ref):
    j = pl.program_id(0)

    @pl.when(j < nv_ref[0])
    def _():
        x = xs_ref[...].astype(BF16)
        g = _dot(x, wg_ref[0]) + bg_ref[0]
        u = _dot(x, wu_ref[0]) + bu_ref[0]
        gate = jnp.minimum(g, SWIGLU_LIMIT)
        up = jnp.clip(u, -SWIGLU_LIMIT, SWIGLU_LIMIT)
        act = (up + 1.0) * (gate * jax.nn.sigmoid(gate * SWIGLU_ALPHA))
        ys_ref[...] = _dot(act.astype(BF16), wd_ref[0]) + bd_ref[0]

    @pl.when(j >= nv_ref[0])
    def _():
        ys_ref[...] = jnp.zeros_like(ys_ref)


def _experts(block_e, n_valid, xs, wg, wu, wd, bg, bu, bd):
    n_pad, d = xs.shape
    de = wg.shape[2]
    n_blocks = n_pad // MOE_ROWS
    xrow = lambda j, be, nv: (jnp.minimum(j, nv[0] - 1), 0)
    wsel = lambda j, be, nv: (be[j], 0, 0)
    return pl.pallas_call(
        _experts_kernel,
        grid_spec=pltpu.PrefetchScalarGridSpec(
            num_scalar_prefetch=2,
            grid=(n_blocks,),
            in_specs=[
                pl.BlockSpec((MOE_ROWS, d), xrow),
                pl.BlockSpec((1, d, de), wsel),
                pl.BlockSpec((1, d, de), wsel),
                pl.BlockSpec((1, de, d), wsel),
                pl.BlockSpec((1, 1, de), wsel),
                pl.BlockSpec((1, 1, de), wsel),
                pl.BlockSpec((1, 1, d), wsel),
            ],
            out_specs=pl.BlockSpec((MOE_ROWS, d), lambda j, be, nv: (j, 0)),
        ),
        out_shape=jax.ShapeDtypeStruct((n_pad, d), F32),
        compiler_params=_cp(("arbitrary",)),
        name="experts",
    )(block_e, n_valid, xs, wg, wu, wd, bg, bu, bd)


def _combine_kernel(dest_ref, x1_ref, tw_ref, nw_ref, ys_ref, o_ref, buf_ref, sem, *, final_norm):
    tm = x1_ref.shape[0]
    base = pl.program_id(0) * tm * TOP_K

    def copy(r, kk):
        d = dest_ref[base + r * TOP_K + kk]
        return pltpu.make_async_copy(ys_ref.at[pl.ds(d, 1)], buf_ref.at[kk, pl.ds(r, 1)], sem)

    def issue(r, carry):
        for kk in range(TOP_K):
            copy(r, kk).start()
        return carry

    def drain(r, carry):
        for kk in range(TOP_K):
            copy(r, kk).wait()
        return carry

    lax.fori_loop(0, tm, issue, 0)
    lax.fori_loop(0, tm, drain, 0)

    tw = tw_ref[...]
    y = tw[:, 0:1] * buf_ref[0]
    for kk in range(1, TOP_K):
        y = y + tw[:, kk:kk + 1] * buf_ref[kk]
    out = x1_ref[...] + y
    if final_norm:
        out = _rms(out, nw_ref[...])
    o_ref[...] = out


def _combine(dest_flat, x1, tw, nw, ys, final_norm, tm=128):
    t, d = x1.shape
    kern = functools.partial(_combine_kernel, final_norm=final_norm)
    return pl.pallas_call(
        kern,
        grid_spec=pltpu.PrefetchScalarGridSpec(
            num_scalar_prefetch=1,
            grid=(t // tm,),
            in_specs=[
                pl.BlockSpec((tm, d), lambda i, dest: (i, 0)),
                pl.BlockSpec((tm, LANES), lambda i, dest: (i, 0)),
                pl.BlockSpec((1, d), lambda i, dest: (0, 0)),
                pl.BlockSpec(memory_space=pl.ANY),
            ],
            out_specs=pl.BlockSpec((tm, d), lambda i, dest: (i, 0)),
            scratch_shapes=[pltpu.VMEM((TOP_K, tm, d), F32), pltpu.SemaphoreType.DMA(())],
        ),
        out_shape=jax.ShapeDtypeStruct((t, d), F32),
        compiler_params=_cp(("arbitrary",)),
        name="combine",
    )(dest_flat, x1, tw, nw, ys)


def _rope_tables(seq):
    half = MLA_ROPE // 2
    inv = 1.0 / (ROPE_THETA ** (jnp.arange(0, MLA_ROPE, 2, dtype=F32) / MLA_ROPE))
    ang = jnp.arange(seq, dtype=F32)[:, None] * inv[None, :]
    cos, sin = jnp.cos(ang), jnp.sin(ang)
    zeros = jnp.zeros((seq, LANES - MLA_ROPE), F32)
    cos_t = jnp.concatenate([cos, cos, zeros], axis=-1)
    sin_t = jnp.concatenate([-sin, sin, zeros], axis=-1)
    del half
    return cos_t, sin_t


def _pad_cols(a, width):
    return jnp.pad(a, ((0, 0), (0, width - a.shape[1])))


def _layer(x2, batch, seq, final_norm_w, final_norm, cos_t, sin_t,
           norm_mix_w, w_in, gdn_conv_w, gdn_a_log, gdn_dt_bias, gdn_norm_w, w_gdn_o,
           mla_q_norm_w, w_mla_q_b, mla_kv_norm_w, w_mla_kv_b, w_mla_o, w_out,
           norm_ffn_w, w_router, b_router, w_gate_up, b_gate_up, w_down, b_down):
    t, d = x2.shape
    qk_w = GDN_HEADS * GDN_D
    o_b = 4 * qk_w
    o_a = o_b + GDN_HEADS
    o_cq = o_a + GDN_HEADS
    o_ckv = o_cq + MLA_Q_LORA
    o_kr = o_ckv + MLA_KV_LORA
    o_ga = o_kr + MLA_ROPE
    o_gb = o_ga + d
    w_p = jnp.concatenate([
        w_in[:, 0:o_b], w_in[:, o_ga:o_gb + d], w_in[:, o_cq:o_ckv], w_in[:, o_ckv:o_kr],
        _pad_cols(w_in[:, o_kr:o_ga], 2 * LANES)], axis=1).astype(BF16)
    w_ab = _pad_cols(jnp.concatenate([w_in[:, o_a:o_cq], w_in[:, o_b:o_a]], axis=1), LANES).astype(BF16)

    p, ab = _in_proj(x2, norm_mix_w[None, :], w_p, w_ab)

    alog_row = _pad_cols(gdn_a_log[None, :].astype(F32), LANES)
    dtb_row = _pad_cols(gdn_dt_bias[None, :].astype(F32), LANES)
    qkvn, cols, gct = _gdn_prep(p, ab, gdn_conv_w.astype(F32), alog_row, dtb_row, seq)
    o_gdn = _gdn_chunk(qkvn, p, cols, gct, gdn_norm_w[None, :].astype(F32), batch, seq)

    hd = MLA_NOPE + MLA_ROPE
    wq = w_mla_q_b.reshape(MLA_Q_LORA, MLA_HEADS, hd)
    wq = jnp.pad(wq, ((0, 0), (0, 0), (0, 2 * LANES - hd))).reshape(MLA_Q_LORA, MLA_HEADS * 2 * LANES)
    wkv = w_mla_kv_b.reshape(MLA_KV_LORA, MLA_HEADS, MLA_NOPE + MLA_V)
    wkv = jnp.concatenate([wkv[:, :, :MLA_NOPE].reshape(MLA_KV_LORA, -1),
                           wkv[:, :, MLA_NOPE:].reshape(MLA_KV_LORA, -1)], axis=1)
    q, kn, kr, v = _mla_prep(p, cos_t, sin_t, mla_q_norm_w[None, :].astype(F32),
                             mla_kv_norm_w[None, :].astype(F32), wq.astype(BF16), wkv.astype(BF16), seq)
    o_mla = _mla_attn(q, kn, kr, v, batch, seq)

    wr = _pad_cols(w_router.astype(F32), LANES)
    wr_hi = wr.astype(BF16)
    wr_lo = (wr - wr_hi.astype(F32)).astype(BF16)
    br = _pad_cols(b_router[None, :].astype(F32), LANES)
    x1, h2, sel, idx, tw = _mix_out(x2, o_gdn, o_mla, p, w_gdn_o.astype(BF16), w_mla_o.astype(BF16),
                                    w_out.astype(BF16), norm_ffn_w[None, :].astype(F32), wr_hi, wr_lo, br)

    pos, cnt = _route_pos(sel, idx)
    counts = cnt[0, :N_EXPERTS]
    padded = (counts + MOE_ROWS - 1) // MOE_ROWS * MOE_ROWS
    pad_end = jnp.cumsum(padded)
    pad_start = pad_end - padded
    idx4 = idx[:, :TOP_K]
    dest = (pad_start[idx4] + pos[:, :TOP_K]).astype(jnp.int32).reshape(-1)
    n_pad = t * TOP_K + N_EXPERTS * MOE_ROWS
    n_blocks = n_pad // MOE_ROWS
    block_e = jnp.minimum(
        jnp.searchsorted(pad_end, jnp.arange(n_blocks, dtype=jnp.int32) * MOE_ROWS, side='right'),
        N_EXPERTS - 1).astype(jnp.int32)
    n_valid = (pad_end[-1:] // MOE_ROWS).astype(jnp.int32)

    xs = _dispatch(dest, h2, n_pad)
    de = w_gate_up.shape[2] // 2
    wg = w_gate_up[:, :, 0::2].astype(BF16)
    wu = w_gate_up[:, :, 1::2].astype(BF16)
    bg = b_gate_up[:, None, 0::2].astype(F32)
    bu = b_gate_up[:, None, 1::2].astype(F32)
    del de
    ys = _experts(block_e, n_valid, xs, wg, wu, w_down.astype(BF16), bg, bu,
                  b_down[:, None, :].astype(F32))
    return _combine(dest, x1, tw, final_norm_w[None, :].astype(F32), ys, final_norm)


def kernel(x, norm_mix_w, w_in, gdn_conv_w, gdn_a_log, gdn_dt_bias, gdn_norm_w, w_gdn_o, mla_q_norm_w, w_mla_q_b, mla_kv_norm_w, w_mla_kv_b, w_mla_o, w_out, norm_ffn_w, w_router, b_router, w_gate_up, b_gate_up, w_down, b_down, norm_final_w):
    batch, seq, d = x.shape
    depth = w_in.shape[0]
    cos_t, sin_t = _rope_tables(seq)
    x2 = x.reshape(batch * seq, d)
    for layer in range(depth):
        x2 = _layer(x2, batch, seq, norm_final_w, layer == depth - 1, cos_t, sin_t,
                    norm_mix_w[layer], w_in[layer], gdn_conv_w[layer], gdn_a_log[layer],
                    gdn_dt_bias[layer], gdn_norm_w[layer], w_gdn_o[layer], mla_q_norm_w[layer],
                    w_mla_q_b[layer], mla_kv_norm_w[layer], w_mla_kv_b[layer], w_mla_o[layer],
                    w_out[layer], norm_ffn_w[layer], w_router[layer], b_router[layer],
                    w_gate_up[layer], b_gate_up[layer], w_down[layer], b_down[layer])
    return x2.reshape(batch, seq, d)
```

```python
import functools

import jax
import jax.numpy as jnp
import numpy as np
from jax import lax
from jax.experimental import pallas as pl
from jax.experimental.pallas import tpu as pltpu

F32 = jnp.float32
BF16 = jnp.bfloat16

CHUNK = 64
NORM_EPS = 1e-6
GDN_HEADS = 8
GDN_D = 128
GDN_CONV = 4
MLA_HEADS = 8
MLA_Q_LORA = 512
MLA_KV_LORA = 256
MLA_NOPE = 128
MLA_ROPE = 64
MLA_V = 128
ROPE_THETA = 10000.0
N_EXPERTS = 32
TOP_K = 4
SWIGLU_LIMIT = 7.0
SWIGLU_ALPHA = 1.702

LANES = 128
MOE_ROWS = 256
VMEM_LIMIT = 48 * 1024 * 1024
EXPERTS_VMEM_LIMIT = 56 * 1024 * 1024

NEG_BIG = -1e30
LOG2_E = 1.4426950408889634


def _cp(sem):
    return pltpu.CompilerParams(dimension_semantics=sem, vmem_limit_bytes=VMEM_LIMIT)


def _dot(a, b):
    return jnp.dot(a, b, preferred_element_type=F32)


def _dot_nt(a, b):
    return lax.dot_general(a, b, (((1,), (1,)), ((), ())), preferred_element_type=F32)


def _dot_tn(a, b):
    return lax.dot_general(a, b, (((0,), (0,)), ((), ())), preferred_element_type=F32)


def _split3(x):
    hi = x.astype(BF16)
    r = x - hi.astype(F32)
    mid = r.astype(BF16)
    lo = (r - mid.astype(F32)).astype(BF16)
    return hi, mid, lo


def _rms(x, w):
    ms = jnp.mean(x * x, axis=-1, keepdims=True)
    return x * lax.rsqrt(ms + NORM_EPS) * w


def _in_proj_kernel(x_ref, nw_ref, w_ref, wab_ref, p_ref, ab_ref, h_ref):
    @pl.when(pl.program_id(1) == 0)
    def _():
        hb = _rms(x_ref[...], nw_ref[...]).astype(BF16)
        h_ref[...] = hb
        ab_ref[...] = _dot(hb, wab_ref[...])

    p_ref[...] = _dot(h_ref[...], w_ref[...]).astype(BF16)


def _in_proj(x2, norm_w, w_p, w_ab, tm=512, tn=1024):
    t, d = x2.shape
    n = w_p.shape[1]
    return pl.pallas_call(
        _in_proj_kernel,
        grid=(t // tm, n // tn),
        in_specs=[
            pl.BlockSpec((tm, d), lambda i, j: (i, 0)),
            pl.BlockSpec((1, d), lambda i, j: (0, 0)),
            pl.BlockSpec((d, tn), lambda i, j: (0, j)),
            pl.BlockSpec((d, LANES), lambda i, j: (0, 0)),
        ],
        out_specs=[
            pl.BlockSpec((tm, tn), lambda i, j: (i, j)),
            pl.BlockSpec((tm, LANES), lambda i, j: (i, 0)),
        ],
        out_shape=[
            jax.ShapeDtypeStruct((t, n), BF16),
            jax.ShapeDtypeStruct((t, LANES), F32),
        ],
        scratch_shapes=[pltpu.VMEM((tm, d), BF16)],
        compiler_params=_cp(("parallel", "arbitrary")),
        name="in_proj",
    )(x2, norm_w, w_p, w_ab)


def _gdn_prep_kernel(cur_ref, prev_ref, ab_ref, cw_ref, alog_ref, dtb_ref,
                     qkv_ref, cols_ref, gct_ref, *, tiles_per_seq):
    tm = cur_ref.shape[0]
    i = pl.program_id(0)
    halo_on = (i % tiles_per_seq) != 0
    n_blk = cur_ref.shape[1] // LANES
    q_scale = GDN_D ** -0.5
    for cb in range(n_blk):
        cs = slice(cb * LANES, (cb + 1) * LANES)
        cur = cur_ref[:, cs].astype(F32)
        halo = prev_ref[:, cs].astype(F32)[8:16, :]
        halo = jnp.where(halo_on, halo, 0.0)
        xe = jnp.concatenate([halo, cur], axis=0)
        w = cw_ref[:, cs]
        y = w[0:1, :] * xe[5:5 + tm, :]
        for j in range(1, GDN_CONV):
            y = y + w[j:j + 1, :] * xe[5 + j:5 + j + tm, :]
        y = y * jax.nn.sigmoid(y)
        if cb < 2 * GDN_HEADS:
            ss = jnp.sum(y * y, axis=-1, keepdims=True)
            y = y * lax.rsqrt(ss + NORM_EPS)
            if cb < GDN_HEADS:
                y = y * q_scale
        qkv_ref[:, cs] = y.astype(BF16)

    ab = ab_ref[...]
    g = -jnp.exp(alog_ref[...]) * jax.nn.softplus(ab + dtb_ref[...])
    row = lax.broadcasted_iota(jnp.int32, (tm, tm), 0)
    col = lax.broadcasted_iota(jnp.int32, (tm, tm), 1)
    tri = ((col <= row) & ((row // CHUNK) == (col // CHUNK))).astype(BF16)
    g_hi, g_mid, g_lo = _split3(g)
    gc = _dot(tri, g_hi) + _dot(tri, g_mid) + _dot(tri, g_lo)
    lane = lax.broadcasted_iota(jnp.int32, (tm, LANES), 1)
    cols_ref[...] = jnp.where(lane < GDN_HEADS, gc, jax.nn.sigmoid(ab))
    for c in range(tm // CHUNK):
        blk = gc[c * CHUNK:(c + 1) * CHUNK, :]
        blk = jnp.concatenate([blk, jnp.zeros_like(blk)], axis=0)
        gct_ref[c] = blk.T[0:GDN_HEADS, 0:CHUNK]


def _gdn_prep(p, ab, conv_w, alog_row, dtb_row, seq, tm=256):
    t = p.shape[0]
    cw = 3 * GDN_HEADS * GDN_D
    tiles_per_seq = seq // tm
    kern = functools.partial(_gdn_prep_kernel, tiles_per_seq=tiles_per_seq)
    return pl.pallas_call(
        kern,
        grid=(t // tm,),
        in_specs=[
            pl.BlockSpec((tm, cw), lambda i: (i, 0)),
            pl.BlockSpec((16, cw), lambda i: (jnp.maximum(i * (tm // 16) - 1, 0), 0)),
            pl.BlockSpec((tm, LANES), lambda i: (i, 0)),
            pl.BlockSpec((GDN_CONV, cw), lambda i: (0, 0)),
            pl.BlockSpec((1, LANES), lambda i: (0, 0)),
            pl.BlockSpec((1, LANES), lambda i: (0, 0)),
        ],
        out_specs=[
            pl.BlockSpec((tm, cw), lambda i: (i, 0)),
            pl.BlockSpec((tm, LANES), lambda i: (i, 0)),
            pl.BlockSpec((tm // CHUNK, GDN_HEADS, CHUNK), lambda i: (i, 0, 0)),
        ],
        out_shape=[
            jax.ShapeDtypeStruct((t, cw), BF16),
            jax.ShapeDtypeStruct((t, LANES), F32),
            jax.ShapeDtypeStruct((t // CHUNK, GDN_HEADS, CHUNK), F32),
        ],
        compiler_params=_cp(("parallel",)),
        name="gdn_prep",
    )(p, p, ab, conv_w, alog_row, dtb_row)


def _gdn_chunk_kernel(q_ref, k_ref, v_ref, z_ref, cols_ref, gct_ref, nw_ref, o_ref, s_ref):
    c = CHUNK
    nb = q_ref.shape[0]
    units = [(b, h) for b in range(nb) for h in range(GDN_HEADS)]

    @pl.when(pl.program_id(0) == 0)
    def _():
        s_ref[...] = jnp.zeros_like(s_ref)

    ri = lax.broadcasted_iota(jnp.int32, (c, c), 0)
    ci = lax.broadcasted_iota(jnp.int32, (c, c), 1)
    incl = ri >= ci
    strict = ri > ci
    eye = (ri == ci).astype(F32)
    nw = nw_ref[...]

    cols, e_g, e_kd, e_last, gct = [], [], [], [], []
    for b in range(nb):
        cb = cols_ref[b]
        last = cb[c - 1:c, :]
        cols.append(cb)
        e_g.append(jnp.exp(cb))
        e_kd.append(jnp.exp(last - cb))
        e_last.append(jnp.exp(last))
        gct.append(gct_ref[b, 0])

    kq, kb_l, kf_l = [], [], []
    for b, h in units:
        hs = slice(h * GDN_D, (h + 1) * GDN_D)
        k = k_ref[b, :, hs]
        kf = k.astype(F32)
        kb = kf * cols[b][:, GDN_HEADS + h:GDN_HEADS + h + 1]
        kq.append(_dot_nt(jnp.concatenate([kb.astype(BF16), q_ref[b, :, hs]], axis=0), k))
        kb_l.append(kb)
        kf_l.append(kf)

    a_l, qk_l = [], []
    for i, (b, h) in enumerate(units):
        dec = jnp.exp(jnp.minimum(cols[b][:, h:h + 1] - gct[b][h:h + 1, :], 0.0))
        a_l.append(jnp.where(strict, -kq[i][0:c, :] * dec, 0.0))
        qk_l.append(jnp.where(incl, kq[i][c:2 * c, :] * dec, 0.0).astype(BF16))

    tinv = [eye + a for a in a_l]
    pw = a_l
    for _ in range(5):
        pwb = [x.astype(BF16) for x in pw]
        pw = [_dot(x, x) for x in pwb]
        tinv = [t + _dot(t.astype(BF16), x.astype(BF16)) for t, x in zip(tinv, pw)]

    uw = []
    for i, (b, h) in enumerate(units):
        hs = slice(h * GDN_D, (h + 1) * GDN_D)
        beta = cols[b][:, GDN_HEADS + h:GDN_HEADS + h + 1]
        rhs = jnp.concatenate([v_ref[b, :, hs].astype(F32) * beta,
                               kb_l[i] * e_g[b][:, h:h + 1]], axis=1).astype(BF16)
        uw.append(_dot(tinv[i].astype(BF16), rhs))

    r_l = []
    for i, (b, h) in enumerate(units):
        hs = slice(h * GDN_D, (h + 1) * GDN_D)
        qd = (q_ref[b, :, hs].astype(F32) * e_g[b][:, h:h + 1]).astype(BF16)
        lhs = jnp.concatenate([uw[i][:, GDN_D:2 * GDN_D].astype(BF16), qd], axis=0)
        r_l.append(_dot(lhs, s_ref[b * GDN_HEADS + h].astype(BF16)))

    for i, (b, h) in enumerate(units):
        hs = slice(h * GDN_D, (h + 1) * GDN_D)
        v_new = (uw[i][:, 0:GDN_D] - r_l[i][0:c, :]).astype(BF16)
        o = r_l[i][c:2 * c, :] + _dot(qk_l[i], v_new)
        kd = (kf_l[i] * e_kd[b][:, h:h + 1]).astype(BF16)
        u = b * GDN_HEADS + h
        s_ref[u] = s_ref[u] * e_last[b][:, h:h + 1] + _dot_tn(kd, v_new)
        z = z_ref[b, :, hs].astype(F32)
        o_ref[b, :, hs] = (_rms(o, nw) * (z * jax.nn.sigmoid(z))).astype(BF16)


def _gdn_chunk(qkvn, p, cols, gct, norm_w, batch, seq):
    nc = seq // CHUNK
    hw = GDN_HEADS * GDN_D
    qkvn3 = qkvn.reshape(batch, seq, qkvn.shape[1])
    p3 = p.reshape(batch, seq, p.shape[1])
    cols3 = cols.reshape(batch, seq, LANES)
    gct4 = gct.reshape(batch, nc, GDN_HEADS, CHUNK)
    tile = lambda col: pl.BlockSpec((batch, CHUNK, hw), lambda c: (0, c, col))
    out = pl.pallas_call(
        _gdn_chunk_kernel,
        grid=(nc,),
        in_specs=[
            tile(0), tile(1), tile(2),
            tile(3),
            pl.BlockSpec((batch, CHUNK, LANES), lambda c: (0, c, 0)),
            pl.BlockSpec((batch, 1, GDN_HEADS, CHUNK), lambda c: (0, c, 0, 0)),
            pl.BlockSpec((1, GDN_D), lambda c: (0, 0)),
        ],
        out_specs=pl.BlockSpec((batch, CHUNK, hw), lambda c: (0, c, 0)),
        out_shape=jax.ShapeDtypeStruct((batch, seq, hw), BF16),
        scratch_shapes=[pltpu.VMEM((batch * GDN_HEADS, GDN_D, GDN_D), F32)],
        compiler_params=_cp(("arbitrary",)),
        name="gdn_chunk",
    )(qkvn3, qkvn3, qkvn3, p3, cols3, gct4, norm_w)
    return out.reshape(batch * seq, hw)


def _rope(x, cos, sin_signed):
    lane = lax.broadcasted_iota(jnp.int32, x.shape, 1)
    fwd = pltpu.roll(x, LANES - MLA_ROPE // 2, 1)
    bwd = pltpu.roll(x, MLA_ROPE // 2, 1)
    rot = jnp.where(lane < MLA_ROPE // 2, fwd, bwd)
    return x * cos + rot * sin_signed


def _mla_prep_kernel(cq_ref, ckv_ref, kr_ref, cos_ref, sin_ref, qnw_ref, kvnw_ref, wq_ref, wkn_ref,
                     wvt_ref, q_ref, kn_ref, kro_ref, vt_ref):
    cos = cos_ref[...]
    sin = sin_ref[...]
    cq = _rms(cq_ref[...].astype(F32), qnw_ref[...]).astype(BF16)
    hd = 2 * LANES
    scale = (MLA_NOPE + MLA_ROPE) ** -0.5 * LOG2_E
    for h in range(MLA_HEADS):
        qh = _dot(cq, wq_ref[:, h * hd:(h + 1) * hd]) * scale
        q_ref[:, h * hd:h * hd + LANES] = qh[:, 0:LANES].astype(BF16)
        q_ref[:, h * hd + LANES:(h + 1) * hd] = _rope(qh[:, LANES:hd], cos, sin).astype(BF16)
    kvl = _rms(ckv_ref[...].astype(F32), kvnw_ref[...]).astype(BF16)
    kn_ref[...] = _dot(kvl, wkn_ref[...]).astype(BF16)
    vt_ref[...] = _dot_nt(wvt_ref[...], kvl).astype(BF16)
    kro_ref[...] = _rope(kr_ref[...].astype(F32), cos, sin).astype(BF16)


def _mla_prep(p, cos_t, sin_t, qnw, kvnw, wq, wkn, wvt, seq, tm=512):
    t = p.shape[0]
    tiles_per_seq = seq // tm
    hw = MLA_HEADS * MLA_NOPE
    cq_blk = 6144 // MLA_Q_LORA
    ckv_blk = 6656 // MLA_KV_LORA
    kr_blk = 6912 // LANES
    return pl.pallas_call(
        _mla_prep_kernel,
        grid=(t // tm,),
        in_specs=[
            pl.BlockSpec((tm, MLA_Q_LORA), lambda i: (i, cq_blk)),
            pl.BlockSpec((tm, MLA_KV_LORA), lambda i: (i, ckv_blk)),
            pl.BlockSpec((tm, LANES), lambda i: (i, kr_blk)),
            pl.BlockSpec((tm, LANES), lambda i: (i % tiles_per_seq, 0)),
            pl.BlockSpec((tm, LANES), lambda i: (i % tiles_per_seq, 0)),
            pl.BlockSpec((1, MLA_Q_LORA), lambda i: (0, 0)),
            pl.BlockSpec((1, MLA_KV_LORA), lambda i: (0, 0)),
            pl.BlockSpec((MLA_Q_LORA, 2 * hw), lambda i: (0, 0)),
            pl.BlockSpec((MLA_KV_LORA, hw), lambda i: (0, 0)),
            pl.BlockSpec((hw, MLA_KV_LORA), lambda i: (0, 0)),
        ],
        out_specs=[
            pl.BlockSpec((tm, 2 * hw), lambda i: (i, 0)),
            pl.BlockSpec((tm, hw), lambda i: (i, 0)),
            pl.BlockSpec((tm, LANES), lambda i: (i, 0)),
            pl.BlockSpec((hw, tm), lambda i: (0, i)),
        ],
        out_shape=[
            jax.ShapeDtypeStruct((t, 2 * hw), BF16),
            jax.ShapeDtypeStruct((t, hw), BF16),
            jax.ShapeDtypeStruct((t, LANES), BF16),
            jax.ShapeDtypeStruct((hw, t), BF16),
        ],
        compiler_params=_cp(("parallel",)),
        name="mla_prep",
    )(p, p, p, cos_t, sin_t, qnw, kvnw, wq, wkn, wvt)


ATTN_HEADS_PER_STEP = 2


def _mla_attn_kernel(qt_ref, kt_ref, q_ref, kn_ref, kr_ref, vt_ref, o_ref, m_ref, l_ref, acc_ref):
    qi = qt_ref[pl.program_id(2)]
    ki = kt_ref[pl.program_id(2)]
    tq = q_ref.shape[0]
    tk = kn_ref.shape[0]
    hd = 2 * LANES

    @pl.when(ki == 0)
    def _():
        m_ref[...] = jnp.full_like(m_ref, NEG_BIG)
        l_ref[...] = jnp.zeros_like(l_ref)
        acc_ref[...] = jnp.zeros_like(acc_ref)

    def step(masked):
        kr = kr_ref[...]
        for h in range(ATTN_HEADS_PER_STEP):
            k = jnp.concatenate([kn_ref[:, h * MLA_NOPE:(h + 1) * MLA_NOPE], kr], axis=1)
            s = _dot_nt(k, q_ref[:, h * hd:(h + 1) * hd])
            if masked:
                ck = lax.broadcasted_iota(jnp.int32, (tk, tq), 0) // CHUNK
                cq = lax.broadcasted_iota(jnp.int32, (tk, tq), 1) // CHUNK
                s = jnp.where(ck <= cq, s, NEG_BIG)
            m_prev = m_ref[h]
            m_new = jnp.maximum(m_prev, jnp.max(s, axis=0, keepdims=True))
            alpha = jnp.exp2(m_prev - m_new)
            p = jnp.exp2(s - m_new)
            l_ref[h] = alpha * l_ref[h] + jnp.sum(p, axis=0, keepdims=True)
            acc_ref[h] = alpha * acc_ref[h] + _dot(vt_ref[h * MLA_V:(h + 1) * MLA_V, :], p.astype(BF16))
            m_ref[h] = m_new

    @pl.when(ki < qi)
    def _():
        step(False)

    @pl.when(ki == qi)
    def _():
        step(True)
        for h in range(ATTN_HEADS_PER_STEP):
            o_ref[:, h * MLA_V:(h + 1) * MLA_V] = (acc_ref[h] / l_ref[h]).T.astype(BF16)


def _mla_attn(q, kn, kr, vt, batch, seq, tq=512):
    t = q.shape[0]
    nq = seq // tq
    hps = ATTN_HEADS_PER_STEP
    pairs = [(qi, ki) for qi in range(nq) for ki in range(qi + 1)]
    qt = jnp.asarray(np.array([pr[0] for pr in pairs], np.int32))
    kt = jnp.asarray(np.array([pr[1] for pr in pairs], np.int32))
    return pl.pallas_call(
        _mla_attn_kernel,
        grid_spec=pltpu.PrefetchScalarGridSpec(
            num_scalar_prefetch=2,
            grid=(batch, MLA_HEADS // hps, len(pairs)),
            in_specs=[
                pl.BlockSpec((tq, hps * 2 * LANES), lambda b, h, pr, qt, kt: (b * nq + qt[pr], h)),
                pl.BlockSpec((tq, hps * MLA_NOPE), lambda b, h, pr, qt, kt: (b * nq + kt[pr], h)),
                pl.BlockSpec((tq, LANES), lambda b, h, pr, qt, kt: (b * nq + kt[pr], 0)),
                pl.BlockSpec((hps * MLA_V, tq), lambda b, h, pr, qt, kt: (h, b * nq + kt[pr])),
            ],
            out_specs=pl.BlockSpec((tq, hps * MLA_V), lambda b, h, pr, qt, kt: (b * nq + qt[pr], h)),
            scratch_shapes=[
                pltpu.VMEM((hps, 1, tq), F32),
                pltpu.VMEM((hps, 1, tq), F32),
                pltpu.VMEM((hps, MLA_V, tq), F32),
            ],
        ),
        out_shape=jax.ShapeDtypeStruct((t, MLA_HEADS * MLA_V), BF16),
        compiler_params=_cp(("parallel", "parallel", "arbitrary")),
        name="mla_attn",
    )(qt, kt, q, kn, kr, vt)


def _mix_out_kernel(x_ref, oa_ref, ob_ref, ga_ref, gb_ref, wga_ref, wmo_ref, wout_ref, nw_ref,
                    wr_hi_ref, wr_lo_ref, br_ref, x1_ref, h2_ref, sel_ref, idx_ref, tw_ref):
    ya = _dot(oa_ref[...], wga_ref[...])
    yb = _dot(ob_ref[...], wmo_ref[...])
    merged = (jax.nn.sigmoid(ga_ref[...].astype(F32)) * ya
              + jax.nn.sigmoid(gb_ref[...].astype(F32)) * yb)
    x1 = x_ref[...] + _dot(merged.astype(BF16), wout_ref[...])
    x1_ref[...] = x1
    h2 = _rms(x1, nw_ref[...])
    h2_ref[...] = h2

    h_hi = h2.astype(BF16)
    h_lo = (h2 - h_hi.astype(F32)).astype(BF16)
    logits = (_dot(h_hi, wr_hi_ref[...]) + _dot(h_hi, wr_lo_ref[...]) + _dot(h_lo, wr_hi_ref[...])
              + br_ref[...])
    lane = lax.broadcasted_iota(jnp.int32, logits.shape, 1)
    work = jnp.where(lane < N_EXPERTS, logits, -jnp.inf)
    sel = jnp.zeros(logits.shape, F32)
    idx_out = jnp.zeros(logits.shape, jnp.int32)
    tw_out = jnp.zeros(logits.shape, F32)
    top = None
    denom = None
    for kk in range(TOP_K):
        mx = jnp.max(work, axis=-1, keepdims=True)
        am = jnp.min(jnp.where(work == mx, lane, LANES), axis=-1, keepdims=True)
        hit = lane == am
        if kk == 0:
            top = mx
            e = jnp.ones_like(mx)
            denom = e
        else:
            e = jnp.exp(mx - top)
            denom = denom + e
        sel = jnp.where(hit, 1.0, sel)
        idx_out = jnp.where(lane == kk, am, idx_out)
        tw_out = jnp.where(lane == kk, e, tw_out)
        work = jnp.where(hit, -jnp.inf, work)
    sel_ref[...] = sel.astype(BF16)
    idx_ref[...] = idx_out
    tw_ref[...] = tw_out / denom


def _mix_out(x2, oa, ob, p, wga, wmo, wout, nw, wr_hi, wr_lo, br, tm=256):
    t, d = x2.shape
    full = lambda i: (0, 0)
    return pl.pallas_call(
        _mix_out_kernel,
        grid=(t // tm,),
        in_specs=[
            pl.BlockSpec((tm, d), lambda i: (i, 0)),
            pl.BlockSpec((tm, d), lambda i: (i, 0)),
            pl.BlockSpec((tm, d), lambda i: (i, 0)),
            pl.BlockSpec((tm, d), lambda i: (i, 4)),
            pl.BlockSpec((tm, d), lambda i: (i, 5)),
            pl.BlockSpec((d, d), full),
            pl.BlockSpec((d, d), full),
            pl.BlockSpec((d, d), full),
            pl.BlockSpec((1, d), full),
            pl.BlockSpec((d, LANES), full),
            pl.BlockSpec((d, LANES), full),
            pl.BlockSpec((1, LANES), full),
        ],
        out_specs=[
            pl.BlockSpec((tm, d), lambda i: (i, 0)),
            pl.BlockSpec((tm, d), lambda i: (i, 0)),
            pl.BlockSpec((tm, LANES), lambda i: (i, 0)),
            pl.BlockSpec((tm, LANES), lambda i: (i, 0)),
            pl.BlockSpec((tm, LANES), lambda i: (i, 0)),
        ],
        out_shape=[
            jax.ShapeDtypeStruct((t, d), F32),
            jax.ShapeDtypeStruct((t, d), F32),
            jax.ShapeDtypeStruct((t, LANES), BF16),
            jax.ShapeDtypeStruct((t, LANES), jnp.int32),
            jax.ShapeDtypeStruct((t, LANES), F32),
        ],
        compiler_params=_cp(("parallel",)),
        name="mix_out",
    )(x2, oa, ob, p, p, wga, wmo, wout, nw, wr_hi, wr_lo, br)


def _route_pos_kernel(sel_ref, idx_ref, pos_ref, cnt_ref, carry_ref):
    tm = sel_ref.shape[0]
    i = pl.program_id(0)

    @pl.when(i == 0)
    def _():
        carry_ref[...] = jnp.zeros_like(carry_ref)

    sel = sel_ref[...]
    row = lax.broadcasted_iota(jnp.int32, (tm, tm), 0)
    col = lax.broadcasted_iota(jnp.int32, (tm, tm), 1)
    tri = (col < row).astype(BF16)
    carry = carry_ref[0:1, :]
    pos = _dot(tri, sel) + carry
    lane = lax.broadcasted_iota(jnp.int32, (tm, LANES), 1)
    idx = idx_ref[...]
    out = jnp.zeros((tm, LANES), F32)
    for kk in range(TOP_K):
        hit = lane == idx[:, kk:kk + 1]
        pk = jnp.sum(jnp.where(hit, pos, 0.0), axis=-1, keepdims=True)
        out = jnp.where(lane == kk, pk, out)
    pos_ref[...] = out.astype(jnp.int32)
    total = carry + jnp.sum(sel.astype(F32), axis=0, keepdims=True)
    carry_ref[...] = jnp.broadcast_to(total, carry_ref.shape)
    cnt_ref[...] = jnp.broadcast_to(total, cnt_ref.shape).astype(jnp.int32)


def _route_pos(sel, idx, tm=512):
    t = sel.shape[0]
    return pl.pallas_call(
        _route_pos_kernel,
        grid=(t // tm,),
        in_specs=[
            pl.BlockSpec((tm, LANES), lambda i: (i, 0)),
            pl.BlockSpec((tm, LANES), lambda i: (i, 0)),
        ],
        out_specs=[
            pl.BlockSpec((tm, LANES), lambda i: (i, 0)),
            pl.BlockSpec((8, LANES), lambda i: (0, 0)),
        ],
        out_shape=[
            jax.ShapeDtypeStruct((t, LANES), jnp.int32),
            jax.ShapeDtypeStruct((8, LANES), jnp.int32),
        ],
        scratch_shapes=[pltpu.VMEM((8, LANES), F32)],
        compiler_params=_cp(("arbitrary",)),
        name="route_pos",
    )(sel, idx)


def _dispatch_kernel(dest_ref, h_ref, xs_in_ref, xs_ref, sem):
    del xs_in_ref
    tm = h_ref.shape[0]
    base = pl.program_id(0) * tm * TOP_K

    def copy(r, kk):
        d = dest_ref[base + r * TOP_K + kk]
        return pltpu.make_async_copy(h_ref.at[pl.ds(r, 1)], xs_ref.at[pl.ds(d, 1)], sem)

    def issue(r, carry):
        for kk in range(TOP_K):
            copy(r, kk).start()
        return carry

    def drain(r, carry):
        for kk in range(TOP_K):
            copy(r, kk).wait()
        return carry

    lax.fori_loop(0, tm, issue, 0)
    lax.fori_loop(0, tm, drain, 0)


def _dispatch(dest_flat, h2, n_pad, tm=128):
    t, d = h2.shape
    xs0 = jnp.zeros((n_pad, d), h2.dtype)
    return pl.pallas_call(
        _dispatch_kernel,
        grid_spec=pltpu.PrefetchScalarGridSpec(
            num_scalar_prefetch=1,
            grid=(t // tm,),
            in_specs=[
                pl.BlockSpec((tm, d), lambda i, dest: (i, 0)),
                pl.BlockSpec(memory_space=pl.ANY),
            ],
            out_specs=pl.BlockSpec(memory_space=pl.ANY),
            scratch_shapes=[pltpu.SemaphoreType.DMA(())],
        ),
        out_shape=jax.ShapeDtypeStruct((n_pad, d), h2.dtype),
        input_output_aliases={2: 0},
        compiler_params=_cp(("arbitrary",)),
        name="dispatch",
    )(dest_flat, h2, xs0)


def _experts_kernel(be_ref, nv_ref, xs_ref, wgu_ref, wd_ref, bg_ref, bu_ref, bd_ref, ys_ref,
                    wg_s, wu_s, wd_s):
    j = pl.program_id(0)
    grp = 2 * LANES
    prev = be_ref[jnp.maximum(j - 1, 0)]

    @pl.when((j == 0) | (be_ref[j] != prev))
    def _():
        r = lax.broadcasted_iota(jnp.int32, (grp, grp), 0)
        c = lax.broadcasted_iota(jnp.int32, (grp, grp), 1)
        src = jnp.where(c < LANES, 2 * c, 2 * (c - LANES) + 1)
        pick = (r == src).astype(BF16)
        for g in range(wgu_ref.shape[2] // grp):
            y = _dot(wgu_ref[0, :, g * grp:(g + 1) * grp].astype(BF16), pick)
            wg_s[:, g * LANES:(g + 1) * LANES] = y[:, 0:LANES].astype(BF16)
            wu_s[:, g * LANES:(g + 1) * LANES] = y[:, LANES:grp].astype(BF16)
        wd_s[...] = wd_ref[0].astype(BF16)

    @pl.when(j < nv_ref[0])
    def _():
        x = xs_ref[...].astype(BF16)
        g = _dot(x, wg_s[...]) + bg_ref[0]
        u = _dot(x, wu_s[...]) + bu_ref[0]
        gate = jnp.minimum(g, SWIGLU_LIMIT)
        up = jnp.clip(u, -SWIGLU_LIMIT, SWIGLU_LIMIT)
        act = (up + 1.0) * (gate * jax.nn.sigmoid(gate * SWIGLU_ALPHA))
        ys_ref[...] = _dot(act.astype(BF16), wd_s[...]) + bd_ref[0]

    @pl.when(j >= nv_ref[0])
    def _():
        ys_ref[...] = jnp.zeros_like(ys_ref)


def _experts(block_e, n_valid, xs, wgu, wd, bg, bu, bd):
    n_pad, d = xs.shape
    de = wd.shape[1]
    n_blocks = n_pad // MOE_ROWS
    xrow = lambda j, be, nv: (jnp.minimum(j, nv[0] - 1), 0)
    wsel = lambda j, be, nv: (be[j], 0, 0)
    return pl.pallas_call(
        _experts_kernel,
        grid_spec=pltpu.PrefetchScalarGridSpec(
            num_scalar_prefetch=2,
            grid=(n_blocks,),
            in_specs=[
                pl.BlockSpec((MOE_ROWS, d), xrow),
                pl.BlockSpec((1, d, 2 * de), wsel),
                pl.BlockSpec((1, de, d), wsel),
                pl.BlockSpec((1, 1, de), wsel),
                pl.BlockSpec((1, 1, de), wsel),
                pl.BlockSpec((1, 1, d), wsel),
            ],
            out_specs=pl.BlockSpec((MOE_ROWS, d), lambda j, be, nv: (j, 0)),
            scratch_shapes=[
                pltpu.VMEM((d, de), BF16),
                pltpu.VMEM((d, de), BF16),
                pltpu.VMEM((de, d), BF16),
            ],
        ),
        out_shape=jax.ShapeDtypeStruct((n_pad, d), F32),
        compiler_params=pltpu.CompilerParams(dimension_semantics=("arbitrary",),
                                             vmem_limit_bytes=EXPERTS_VMEM_LIMIT),
        name="experts",
    )(block_e, n_valid, xs, wgu, wd, bg, bu, bd)


def _combine_kernel(dest_ref, x1_ref, tw_ref, nw_ref, ys_ref, o_ref, buf_ref, sem, *, final_norm):
    tm = x1_ref.shape[0]
    base = pl.program_id(0) * tm * TOP_K

    def copy(r, kk):
        d = dest_ref[base + r * TOP_K + kk]
        return pltpu.make_async_copy(ys_ref.at[pl.ds(d, 1)], buf_ref.at[kk, pl.ds(r, 1)], sem)

    def issue(r, carry):
        for kk in range(TOP_K):
            copy(r, kk).start()
        return carry

    def drain(r, carry):
        for kk in range(TOP_K):
            copy(r, kk).wait()
        return carry

    lax.fori_loop(0, tm, issue, 0)
    lax.fori_loop(0, tm, drain, 0)

    tw = tw_ref[...]
    y = tw[:, 0:1] * buf_ref[0]
    for kk in range(1, TOP_K):
        y = y + tw[:, kk:kk + 1] * buf_ref[kk]
    out = x1_ref[...] + y
    if final_norm:
        out = _rms(out, nw_ref[...])
    o_ref[...] = out


def _combine(dest_flat, x1, tw, nw, ys, final_norm, tm=128):
    t, d = x1.shape
    kern = functools.partial(_combine_kernel, final_norm=final_norm)
    return pl.pallas_call(
        kern,
        grid_spec=pltpu.PrefetchScalarGridSpec(
            num_scalar_prefetch=1,
            grid=(t // tm,),
            in_specs=[
                pl.BlockSpec((tm, d), lambda i, dest: (i, 0)),
                pl.BlockSpec((tm, LANES), lambda i, dest: (i, 0)),
                pl.BlockSpec((1, d), lambda i, dest: (0, 0)),
                pl.BlockSpec(memory_space=pl.ANY),
            ],
            out_specs=pl.BlockSpec((tm, d), lambda i, dest: (i, 0)),
            scratch_shapes=[pltpu.VMEM((TOP_K, tm, d), F32), pltpu.SemaphoreType.DMA(())],
        ),
        out_shape=jax.ShapeDtypeStruct((t, d), F32),
        compiler_params=_cp(("arbitrary",)),
        name="combine",
    )(dest_flat, x1, tw, nw, ys)


def _rope_tables(seq):
    half = MLA_ROPE // 2
    inv = 1.0 / (ROPE_THETA ** (jnp.arange(0, MLA_ROPE, 2, dtype=F32) / MLA_ROPE))
    ang = jnp.arange(seq, dtype=F32)[:, None] * inv[None, :]
    cos, sin = jnp.cos(ang), jnp.sin(ang)
    zeros = jnp.zeros((seq, LANES - MLA_ROPE), F32)
    cos_t = jnp.concatenate([cos, cos, zeros], axis=-1)
    sin_t = jnp.concatenate([-sin, sin, zeros], axis=-1)
    del half
    return cos_t, sin_t


def _pad_cols(a, width):
    return jnp.pad(a, ((0, 0), (0, width - a.shape[1])))


def _layer(x2, batch, seq, final_norm_w, final_norm, cos_t, sin_t,
           norm_mix_w, w_in, gdn_conv_w, gdn_a_log, gdn_dt_bias, gdn_norm_w, w_gdn_o,
           mla_q_norm_w, w_mla_q_b, mla_kv_norm_w, w_mla_kv_b, w_mla_o, w_out,
           norm_ffn_w, w_router, b_router, w_gate_up, b_gate_up, w_down, b_down):
    t, d = x2.shape
    qk_w = GDN_HEADS * GDN_D
    o_b = 4 * qk_w
    o_a = o_b + GDN_HEADS
    o_cq = o_a + GDN_HEADS
    o_ckv = o_cq + MLA_Q_LORA
    o_kr = o_ckv + MLA_KV_LORA
    o_ga = o_kr + MLA_ROPE
    o_gb = o_ga + d
    w_p = jnp.concatenate([
        w_in[:, 0:o_b], w_in[:, o_ga:o_gb + d], w_in[:, o_cq:o_ckv], w_in[:, o_ckv:o_kr],
        _pad_cols(w_in[:, o_kr:o_ga], 2 * LANES)], axis=1).astype(BF16)
    w_ab = _pad_cols(jnp.concatenate([w_in[:, o_a:o_cq], w_in[:, o_b:o_a]], axis=1), LANES).astype(BF16)

    p, ab = _in_proj(x2, norm_mix_w[None, :], w_p, w_ab)

    alog_row = _pad_cols(gdn_a_log[None, :].astype(F32), LANES)
    dtb_row = _pad_cols(gdn_dt_bias[None, :].astype(F32), LANES)
    qkvn, cols, gct = _gdn_prep(p, ab, gdn_conv_w.astype(F32), alog_row, dtb_row, seq)
    o_gdn = _gdn_chunk(qkvn, p, cols, gct, gdn_norm_w[None, :].astype(F32), batch, seq)

    hd = MLA_NOPE + MLA_ROPE
    wq = w_mla_q_b.reshape(MLA_Q_LORA, MLA_HEADS, hd)
    wq = jnp.pad(wq, ((0, 0), (0, 0), (0, 2 * LANES - hd))).reshape(MLA_Q_LORA, MLA_HEADS * 2 * LANES)
    wkv = w_mla_kv_b.reshape(MLA_KV_LORA, MLA_HEADS, MLA_NOPE + MLA_V)
    wkn = wkv[:, :, :MLA_NOPE].reshape(MLA_KV_LORA, -1)
    wvt = wkv[:, :, MLA_NOPE:].reshape(MLA_KV_LORA, -1).T
    q, kn, kr, vt = _mla_prep(p, cos_t, sin_t, mla_q_norm_w[None, :].astype(F32),
                              mla_kv_norm_w[None, :].astype(F32), wq.astype(BF16), wkn.astype(BF16),
                              wvt.astype(BF16), seq)
    o_mla = _mla_attn(q, kn, kr, vt, batch, seq)

    wr = _pad_cols(w_router.astype(F32), LANES)
    wr_hi = wr.astype(BF16)
    wr_lo = (wr - wr_hi.astype(F32)).astype(BF16)
    br = _pad_cols(b_router[None, :].astype(F32), LANES)
    x1, h2, sel, idx, tw = _mix_out(x2, o_gdn, o_mla, p, w_gdn_o.astype(BF16), w_mla_o.astype(BF16),
                                    w_out.astype(BF16), norm_ffn_w[None, :].astype(F32), wr_hi, wr_lo, br)

    pos, cnt = _route_pos(sel, idx)
    counts = cnt[0, :N_EXPERTS]
    padded = (counts + MOE_ROWS - 1) // MOE_ROWS * MOE_ROWS
    pad_end = jnp.cumsum(padded)
    pad_start = pad_end - padded
    idx4 = idx[:, :TOP_K]
    dest = (pad_start[idx4] + pos[:, :TOP_K]).astype(jnp.int32).reshape(-1)
    n_pad = t * TOP_K + N_EXPERTS * MOE_ROWS
    n_blocks = n_pad // MOE_ROWS
    blk_start = jnp.arange(n_blocks, dtype=jnp.int32) * MOE_ROWS
    block_e = jnp.minimum(jnp.sum((pad_end[None, :] <= blk_start[:, None]).astype(jnp.int32), axis=1),
                          N_EXPERTS - 1).astype(jnp.int32)
    n_valid = (pad_end[-1:] // MOE_ROWS).astype(jnp.int32)

    xs = _dispatch(dest, h2, n_pad)
    bg = b_gate_up[:, None, 0::2].astype(F32)
    bu = b_gate_up[:, None, 1::2].astype(F32)
    ys = _experts(block_e, n_valid, xs, w_gate_up, w_down, bg, bu, b_down[:, None, :].astype(F32))
    return _combine(dest, x1, tw, final_norm_w[None, :].astype(F32), ys, final_norm)


def kernel(x, norm_mix_w, w_in, gdn_conv_w, gdn_a_log, gdn_dt_bias, gdn_norm_w, w_gdn_o, mla_q_norm_w, w_mla_q_b, mla_kv_norm_w, w_mla_kv_b, w_mla_o, w_out, norm_ffn_w, w_router, b_router, w_gate_up, b_gate_up, w_down, b_down, norm_final_w):
    batch, seq, d = x.shape
    depth = w_in.shape[0]
    cos_t, sin_t = _rope_tables(seq)
    x2 = x.reshape(batch * seq, d)
    for layer in range(depth):
        x2 = _layer(x2, batch, seq, norm_final_w, layer == depth - 1, cos_t, sin_t,
                    norm_mix_w[layer], w_in[layer], gdn_conv_w[layer], gdn_a_log[layer],
                    gdn_dt_bias[layer], gdn_norm_w[layer], w_gdn_o[layer], mla_q_norm_w[layer],
                    w_mla_q_b[layer], mla_kv_norm_w[layer], w_mla_kv_b[layer], w_mla_o[layer],
                    w_out[layer], norm_ffn_w[layer], w_router[layer], b_router[layer],
                    w_gate_up[layer], b_gate_up[layer], w_down[layer], b_down[layer])
    return x2.reshape(batch, seq, d)
```

```python
import functools

import jax
import jax.numpy as jnp
import numpy as np
from jax import lax
from jax.experimental import pallas as pl
from jax.experimental.pallas import tpu as pltpu

F32 = jnp.float32
BF16 = jnp.bfloat16

CHUNK = 64
NORM_EPS = 1e-6
GDN_HEADS = 8
GDN_D = 128
GDN_CONV = 4
MLA_HEADS = 8
MLA_Q_LORA = 512
MLA_KV_LORA = 256
MLA_NOPE = 128
MLA_ROPE = 64
MLA_V = 128
ROPE_THETA = 10000.0
N_EXPERTS = 32
TOP_K = 4
SWIGLU_LIMIT = 7.0
SWIGLU_ALPHA = 1.702

LANES = 128
MOE_ROWS = 512
VMEM_LIMIT = 48 * 1024 * 1024
EXPERTS_VMEM_LIMIT = 56 * 1024 * 1024

NEG_BIG = -1e30
LOG2_E = 1.4426950408889634


def _cp(sem):
    return pltpu.CompilerParams(dimension_semantics=sem, vmem_limit_bytes=VMEM_LIMIT)


def _dot(a, b):
    return jnp.dot(a, b, preferred_element_type=F32)


def _dot_nt(a, b):
    return lax.dot_general(a, b, (((1,), (1,)), ((), ())), preferred_element_type=F32)


def _dot_tn(a, b):
    return lax.dot_general(a, b, (((0,), (0,)), ((), ())), preferred_element_type=F32)


def _split3(x):
    hi = x.astype(BF16)
    r = x - hi.astype(F32)
    mid = r.astype(BF16)
    lo = (r - mid.astype(F32)).astype(BF16)
    return hi, mid, lo


def _rms(x, w):
    ms = jnp.mean(x * x, axis=-1, keepdims=True)
    return x * lax.rsqrt(ms + NORM_EPS) * w


def _in_proj_kernel(x_ref, nw_ref, w_ref, wab_ref, p_ref, ab_ref, h_ref):
    @pl.when(pl.program_id(1) == 0)
    def _():
        hb = _rms(x_ref[...], nw_ref[...]).astype(BF16)
        h_ref[...] = hb
        ab_ref[...] = _dot(hb, wab_ref[...])

    p_ref[...] = _dot(h_ref[...], w_ref[...]).astype(BF16)


def _in_proj(x2, norm_w, w_p, w_ab, tm=1024, tn=1024):
    t, d = x2.shape
    n = w_p.shape[1]
    return pl.pallas_call(
        _in_proj_kernel,
        grid=(t // tm, n // tn),
        in_specs=[
            pl.BlockSpec((tm, d), lambda i, j: (i, 0)),
            pl.BlockSpec((1, d), lambda i, j: (0, 0)),
            pl.BlockSpec((d, tn), lambda i, j: (0, j)),
            pl.BlockSpec((d, LANES), lambda i, j: (0, 0)),
        ],
        out_specs=[
            pl.BlockSpec((tm, tn), lambda i, j: (i, j)),
            pl.BlockSpec((tm, LANES), lambda i, j: (i, 0)),
        ],
        out_shape=[
            jax.ShapeDtypeStruct((t, n), BF16),
            jax.ShapeDtypeStruct((t, LANES), F32),
        ],
        scratch_shapes=[pltpu.VMEM((tm, d), BF16)],
        compiler_params=_cp(("parallel", "arbitrary")),
        name="in_proj",
    )(x2, norm_w, w_p, w_ab)


def _gdn_prep_kernel(cur_ref, prev_ref, ab_ref, cw_ref, alog_ref, dtb_ref,
                     qkv_ref, cols_ref, gct_ref, *, tiles_per_seq):
    tm = cur_ref.shape[0]
    i = pl.program_id(0)
    halo_on = (i % tiles_per_seq) != 0
    n_blk = cur_ref.shape[1] // LANES
    q_scale = GDN_D ** -0.5
    for cb in range(n_blk):
        cs = slice(cb * LANES, (cb + 1) * LANES)
        cur = cur_ref[:, cs].astype(F32)
        halo = prev_ref[:, cs].astype(F32)[8:16, :]
        halo = jnp.where(halo_on, halo, 0.0)
        xe = jnp.concatenate([halo, cur], axis=0)
        w = cw_ref[:, cs]
        y = w[0:1, :] * xe[5:5 + tm, :]
        for j in range(1, GDN_CONV):
            y = y + w[j:j + 1, :] * xe[5 + j:5 + j + tm, :]
        y = y * jax.nn.sigmoid(y)
        if cb < 2 * GDN_HEADS:
            ss = jnp.sum(y * y, axis=-1, keepdims=True)
            y = y * lax.rsqrt(ss + NORM_EPS)
            if cb < GDN_HEADS:
                y = y * q_scale
        qkv_ref[:, cs] = y.astype(BF16)

    ab = ab_ref[...]
    g = -jnp.exp(alog_ref[...]) * jax.nn.softplus(ab + dtb_ref[...])
    row = lax.broadcasted_iota(jnp.int32, (tm, tm), 0)
    col = lax.broadcasted_iota(jnp.int32, (tm, tm), 1)
    tri = ((col <= row) & ((row // CHUNK) == (col // CHUNK))).astype(BF16)
    g_hi, g_mid, g_lo = _split3(g)
    gc = _dot(tri, g_hi) + _dot(tri, g_mid) + _dot(tri, g_lo)
    lane = lax.broadcasted_iota(jnp.int32, (tm, LANES), 1)
    cols_ref[...] = jnp.where(lane < GDN_HEADS, gc, jax.nn.sigmoid(ab))
    for c in range(tm // CHUNK):
        blk = gc[c * CHUNK:(c + 1) * CHUNK, :]
        blk = jnp.concatenate([blk, jnp.zeros_like(blk)], axis=0)
        gct_ref[c] = blk.T[0:GDN_HEADS, 0:CHUNK]


def _gdn_prep(p, ab, conv_w, alog_row, dtb_row, seq, tm=256):
    t = p.shape[0]
    cw = 3 * GDN_HEADS * GDN_D
    tiles_per_seq = seq // tm
    kern = functools.partial(_gdn_prep_kernel, tiles_per_seq=tiles_per_seq)
    return pl.pallas_call(
        kern,
        grid=(t // tm,),
        in_specs=[
            pl.BlockSpec((tm, cw), lambda i: (i, 0)),
            pl.BlockSpec((16, cw), lambda i: (jnp.maximum(i * (tm // 16) - 1, 0), 0)),
            pl.BlockSpec((tm, LANES), lambda i: (i, 0)),
            pl.BlockSpec((GDN_CONV, cw), lambda i: (0, 0)),
            pl.BlockSpec((1, LANES), lambda i: (0, 0)),
            pl.BlockSpec((1, LANES), lambda i: (0, 0)),
        ],
        out_specs=[
            pl.BlockSpec((tm, cw), lambda i: (i, 0)),
            pl.BlockSpec((tm, LANES), lambda i: (i, 0)),
            pl.BlockSpec((tm // CHUNK, GDN_HEADS, CHUNK), lambda i: (i, 0, 0)),
        ],
        out_shape=[
            jax.ShapeDtypeStruct((t, cw), BF16),
            jax.ShapeDtypeStruct((t, LANES), F32),
            jax.ShapeDtypeStruct((t // CHUNK, GDN_HEADS, CHUNK), F32),
        ],
        compiler_params=_cp(("parallel",)),
        name="gdn_prep",
    )(p, p, ab, conv_w, alog_row, dtb_row)


def _gdn_chunk_kernel(q_ref, k_ref, v_ref, z_ref, cols_ref, gct_ref, nw_ref, o_ref, s_ref):
    c = CHUNK
    nb = q_ref.shape[0]
    units = [(b, h) for b in range(nb) for h in range(GDN_HEADS)]

    @pl.when(pl.program_id(0) == 0)
    def _():
        s_ref[...] = jnp.zeros_like(s_ref)

    ri = lax.broadcasted_iota(jnp.int32, (c, c), 0)
    ci = lax.broadcasted_iota(jnp.int32, (c, c), 1)
    incl = ri >= ci
    strict = ri > ci
    eye = (ri == ci).astype(F32)
    nw = nw_ref[...]

    cols, e_g, e_kd, e_last, gct = [], [], [], [], []
    for b in range(nb):
        cb = cols_ref[b]
        last = cb[c - 1:c, :]
        cols.append(cb)
        e_g.append(jnp.exp(cb))
        e_kd.append(jnp.exp(last - cb))
        e_last.append(jnp.exp(last))
        gct.append(gct_ref[b, 0])

    kq, kb_l, kf_l = [], [], []
    for b, h in units:
        hs = slice(h * GDN_D, (h + 1) * GDN_D)
        k = k_ref[b, :, hs]
        kf = k.astype(F32)
        kb = kf * cols[b][:, GDN_HEADS + h:GDN_HEADS + h + 1]
        kq.append(_dot_nt(jnp.concatenate([kb.astype(BF16), q_ref[b, :, hs]], axis=0), k))
        kb_l.append(kb)
        kf_l.append(kf)

    a_l, qk_l = [], []
    for i, (b, h) in enumerate(units):
        dec = jnp.exp(jnp.minimum(cols[b][:, h:h + 1] - gct[b][h:h + 1, :], 0.0))
        a_l.append(jnp.where(strict, -kq[i][0:c, :] * dec, 0.0))
        qk_l.append(jnp.where(incl, kq[i][c:2 * c, :] * dec, 0.0).astype(BF16))

    tinv = [eye + a for a in a_l]
    pw = a_l
    for _ in range(5):
        pwb = [x.astype(BF16) for x in pw]
        pw = [_dot(x, x) for x in pwb]
        tinv = [t + _dot(t.astype(BF16), x.astype(BF16)) for t, x in zip(tinv, pw)]

    uw = []
    for i, (b, h) in enumerate(units):
        hs = slice(h * GDN_D, (h + 1) * GDN_D)
        beta = cols[b][:, GDN_HEADS + h:GDN_HEADS + h + 1]
        rhs = jnp.concatenate([v_ref[b, :, hs].astype(F32) * beta,
                               kb_l[i] * e_g[b][:, h:h + 1]], axis=1).astype(BF16)
        uw.append(_dot(tinv[i].astype(BF16), rhs))

    r_l = []
    for i, (b, h) in enumerate(units):
        hs = slice(h * GDN_D, (h + 1) * GDN_D)
        qd = (q_ref[b, :, hs].astype(F32) * e_g[b][:, h:h + 1]).astype(BF16)
        lhs = jnp.concatenate([uw[i][:, GDN_D:2 * GDN_D].astype(BF16), qd], axis=0)
        r_l.append(_dot(lhs, s_ref[b * GDN_HEADS + h].astype(BF16)))

    for i, (b, h) in enumerate(units):
        hs = slice(h * GDN_D, (h + 1) * GDN_D)
        v_new = (uw[i][:, 0:GDN_D] - r_l[i][0:c, :]).astype(BF16)
        o = r_l[i][c:2 * c, :] + _dot(qk_l[i], v_new)
        kd = (kf_l[i] * e_kd[b][:, h:h + 1]).astype(BF16)
        u = b * GDN_HEADS + h
        s_ref[u] = s_ref[u] * e_last[b][:, h:h + 1] + _dot_tn(kd, v_new)
        z = z_ref[b, :, hs].astype(F32)
        o_ref[b, :, hs] = (_rms(o, nw) * (z * jax.nn.sigmoid(z))).astype(BF16)


def _gdn_chunk(qkvn, p, cols, gct, norm_w, batch, seq):
    nc = seq // CHUNK
    hw = GDN_HEADS * GDN_D
    qkvn3 = qkvn.reshape(batch, seq, qkvn.shape[1])
    p3 = p.reshape(batch, seq, p.shape[1])
    cols3 = cols.reshape(batch, seq, LANES)
    gct4 = gct.reshape(batch, nc, GDN_HEADS, CHUNK)
    tile = lambda col: pl.BlockSpec((batch, CHUNK, hw), lambda c: (0, c, col))
    out = pl.pallas_call(
        _gdn_chunk_kernel,
        grid=(nc,),
        in_specs=[
            tile(0), tile(1), tile(2),
            tile(3),
            pl.BlockSpec((batch, CHUNK, LANES), lambda c: (0, c, 0)),
            pl.BlockSpec((batch, 1, GDN_HEADS, CHUNK), lambda c: (0, c, 0, 0)),
            pl.BlockSpec((1, GDN_D), lambda c: (0, 0)),
        ],
        out_specs=pl.BlockSpec((batch, CHUNK, hw), lambda c: (0, c, 0)),
        out_shape=jax.ShapeDtypeStruct((batch, seq, hw), BF16),
        scratch_shapes=[pltpu.VMEM((batch * GDN_HEADS, GDN_D, GDN_D), F32)],
        compiler_params=_cp(("arbitrary",)),
        name="gdn_chunk",
    )(qkvn3, qkvn3, qkvn3, p3, cols3, gct4, norm_w)
    return out.reshape(batch * seq, hw)


def _rope(x, cos, sin_signed):
    lane = lax.broadcasted_iota(jnp.int32, x.shape, 1)
    fwd = pltpu.roll(x, LANES - MLA_ROPE // 2, 1)
    bwd = pltpu.roll(x, MLA_ROPE // 2, 1)
    rot = jnp.where(lane < MLA_ROPE // 2, fwd, bwd)
    return x * cos + rot * sin_signed


def _mla_prep_kernel(cq_ref, ckv_ref, kr_ref, cos_ref, sin_ref, qnw_ref, kvnw_ref, wq_ref, wkn_ref,
                     wvt_ref, q_ref, kn_ref, kro_ref, vt_ref):
    cos = cos_ref[...]
    sin = sin_ref[...]
    cq = _rms(cq_ref[...].astype(F32), qnw_ref[...]).astype(BF16)
    hd = 2 * LANES
    scale = (MLA_NOPE + MLA_ROPE) ** -0.5 * LOG2_E
    for h in range(MLA_HEADS):
        qh = _dot(cq, wq_ref[:, h * hd:(h + 1) * hd]) * scale
        q_ref[:, h * hd:h * hd + LANES] = qh[:, 0:LANES].astype(BF16)
        q_ref[:, h * hd + LANES:(h + 1) * hd] = _rope(qh[:, LANES:hd], cos, sin).astype(BF16)
    kvl = _rms(ckv_ref[...].astype(F32), kvnw_ref[...]).astype(BF16)
    kn_ref[...] = _dot(kvl, wkn_ref[...]).astype(BF16)
    vt_ref[...] = _dot_nt(wvt_ref[...], kvl).astype(BF16)
    kro_ref[...] = _rope(kr_ref[...].astype(F32), cos, sin).astype(BF16)


def _mla_prep(p, cos_t, sin_t, qnw, kvnw, wq, wkn, wvt, seq, tm=512):
    t = p.shape[0]
    tiles_per_seq = seq // tm
    hw = MLA_HEADS * MLA_NOPE
    cq_blk = 6144 // MLA_Q_LORA
    ckv_blk = 6656 // MLA_KV_LORA
    kr_blk = 6912 // LANES
    return pl.pallas_call(
        _mla_prep_kernel,
        grid=(t // tm,),
        in_specs=[
            pl.BlockSpec((tm, MLA_Q_LORA), lambda i: (i, cq_blk)),
            pl.BlockSpec((tm, MLA_KV_LORA), lambda i: (i, ckv_blk)),
            pl.BlockSpec((tm, LANES), lambda i: (i, kr_blk)),
            pl.BlockSpec((tm, LANES), lambda i: (i % tiles_per_seq, 0)),
            pl.BlockSpec((tm, LANES), lambda i: (i % tiles_per_seq, 0)),
            pl.BlockSpec((1, MLA_Q_LORA), lambda i: (0, 0)),
            pl.BlockSpec((1, MLA_KV_LORA), lambda i: (0, 0)),
            pl.BlockSpec((MLA_Q_LORA, 2 * hw), lambda i: (0, 0)),
            pl.BlockSpec((MLA_KV_LORA, hw), lambda i: (0, 0)),
            pl.BlockSpec((hw, MLA_KV_LORA), lambda i: (0, 0)),
        ],
        out_specs=[
            pl.BlockSpec((tm, 2 * hw), lambda i: (i, 0)),
            pl.BlockSpec((tm, hw), lambda i: (i, 0)),
            pl.BlockSpec((tm, LANES), lambda i: (i, 0)),
            pl.BlockSpec((hw, tm), lambda i: (0, i)),
        ],
        out_shape=[
            jax.ShapeDtypeStruct((t, 2 * hw), BF16),
            jax.ShapeDtypeStruct((t, hw), BF16),
            jax.ShapeDtypeStruct((t, LANES), BF16),
            jax.ShapeDtypeStruct((hw, t), BF16),
        ],
        compiler_params=_cp(("parallel",)),
        name="mla_prep",
    )(p, p, p, cos_t, sin_t, qnw, kvnw, wq, wkn, wvt)


ATTN_HEADS_PER_STEP = 4
ATTN_SUM_ROWS = 16


def _mla_attn_kernel(qt_ref, kt_ref, q_ref, kn_ref, kr_ref, vt_ref, o_ref, m_ref, acc_ref):
    qi = qt_ref[pl.program_id(2)]
    ki = kt_ref[pl.program_id(2)]
    tq = q_ref.shape[0]
    tk = kn_ref.shape[0]
    hd = 2 * LANES

    @pl.when(ki == 0)
    def _():
        m_ref[...] = jnp.full_like(m_ref, NEG_BIG)
        acc_ref[...] = jnp.zeros_like(acc_ref)

    def step(masked):
        kr = kr_ref[...]
        ones = jnp.ones((ATTN_SUM_ROWS, tk), BF16)

        def scores(h):
            k = jnp.concatenate([kn_ref[:, h * MLA_NOPE:(h + 1) * MLA_NOPE], kr], axis=1)
            s = _dot_nt(k, q_ref[:, h * hd:(h + 1) * hd])
            if masked:
                ck = lax.broadcasted_iota(jnp.int32, (tk, tq), 0) // CHUNK
                cq = lax.broadcasted_iota(jnp.int32, (tk, tq), 1) // CHUNK
                s = jnp.where(ck <= cq, s, NEG_BIG)
            return s

        def update(h, s):
            m_prev = m_ref[h]
            m_new = jnp.maximum(m_prev, jnp.max(s, axis=0, keepdims=True))
            alpha = jnp.exp2(m_prev - m_new)
            p = jnp.exp2((s - m_new).astype(BF16))
            v_ext = jnp.concatenate([vt_ref[h * MLA_V:(h + 1) * MLA_V, :], ones], axis=0)
            acc_ref[h] = alpha * acc_ref[h] + _dot(v_ext, p)
            m_ref[h] = m_new

        s_prev = scores(0)
        for h in range(1, ATTN_HEADS_PER_STEP):
            s_next = scores(h)
            update(h - 1, s_prev)
            s_prev = s_next
        update(ATTN_HEADS_PER_STEP - 1, s_prev)

    @pl.when(ki < qi)
    def _():
        step(False)

    @pl.when(ki == qi)
    def _():
        step(True)
        for h in range(ATTN_HEADS_PER_STEP):
            acc = acc_ref[h]
            o = acc[0:MLA_V, :] / acc[MLA_V:MLA_V + 1, :]
            o_ref[:, h * MLA_V:(h + 1) * MLA_V] = o.T.astype(BF16)


def _mla_attn(q, kn, kr, vt, batch, seq, tq=512):
    t = q.shape[0]
    nq = seq // tq
    hps = ATTN_HEADS_PER_STEP
    pairs = [(qi, ki) for qi in range(nq) for ki in range(qi + 1)]
    qt = jnp.asarray(np.array([pr[0] for pr in pairs], np.int32))
    kt = jnp.asarray(np.array([pr[1] for pr in pairs], np.int32))
    return pl.pallas_call(
        _mla_attn_kernel,
        grid_spec=pltpu.PrefetchScalarGridSpec(
            num_scalar_prefetch=2,
            grid=(batch, MLA_HEADS // hps, len(pairs)),
            in_specs=[
                pl.BlockSpec((tq, hps * 2 * LANES), lambda b, h, pr, qt, kt: (b * nq + qt[pr], h)),
                pl.BlockSpec((tq, hps * MLA_NOPE), lambda b, h, pr, qt, kt: (b * nq + kt[pr], h)),
                pl.BlockSpec((tq, LANES), lambda b, h, pr, qt, kt: (b * nq + kt[pr], 0)),
                pl.BlockSpec((hps * MLA_V, tq), lambda b, h, pr, qt, kt: (h, b * nq + kt[pr])),
            ],
            out_specs=pl.BlockSpec((tq, hps * MLA_V), lambda b, h, pr, qt, kt: (b * nq + qt[pr], h)),
            scratch_shapes=[
                pltpu.VMEM((hps, 1, tq), F32),
                pltpu.VMEM((hps, MLA_V + ATTN_SUM_ROWS, tq), F32),
            ],
        ),
        out_shape=jax.ShapeDtypeStruct((t, MLA_HEADS * MLA_V), BF16),
        compiler_params=_cp(("parallel", "parallel", "arbitrary")),
        name="mla_attn",
    )(qt, kt, q, kn, kr, vt)


def _mix_out_kernel(x_ref, oa_ref, ob_ref, ga_ref, gb_ref, wga_ref, wmo_ref, wout_ref, nw_ref,
                    wr_hi_ref, wr_lo_ref, br_ref, x1_ref, h2_ref, sel_ref, idx_ref, tw_ref):
    ya = _dot(oa_ref[...], wga_ref[...])
    yb = _dot(ob_ref[...], wmo_ref[...])
    merged = (jax.nn.sigmoid(ga_ref[...].astype(F32)) * ya
              + jax.nn.sigmoid(gb_ref[...].astype(F32)) * yb)
    x1 = x_ref[...] + _dot(merged.astype(BF16), wout_ref[...])
    x1_ref[...] = x1
    h2 = _rms(x1, nw_ref[...])
    h2_ref[...] = h2

    h_hi = h2.astype(BF16)
    h_lo = (h2 - h_hi.astype(F32)).astype(BF16)
    logits = (_dot(h_hi, wr_hi_ref[...]) + _dot(h_hi, wr_lo_ref[...]) + _dot(h_lo, wr_hi_ref[...])
              + br_ref[...])
    lane = lax.broadcasted_iota(jnp.int32, logits.shape, 1)
    work = jnp.where(lane < N_EXPERTS, logits, -jnp.inf)
    sel = jnp.zeros(logits.shape, F32)
    idx_out = jnp.zeros(logits.shape, jnp.int32)
    tw_out = jnp.zeros(logits.shape, F32)
    top = None
    denom = None
    for kk in range(TOP_K):
        mx = jnp.max(work, axis=-1, keepdims=True)
        am = jnp.min(jnp.where(work == mx, lane, LANES), axis=-1, keepdims=True)
        hit = lane == am
        if kk == 0:
            top = mx
            e = jnp.ones_like(mx)
            denom = e
        else:
            e = jnp.exp(mx - top)
            denom = denom + e
        sel = jnp.where(hit, 1.0, sel)
        idx_out = jnp.where(lane == kk, am, idx_out)
        tw_out = jnp.where(lane == kk, e, tw_out)
        work = jnp.where(hit, -jnp.inf, work)
    sel_ref[...] = sel.astype(BF16)
    idx_ref[...] = idx_out
    tw_ref[...] = tw_out / denom


def _mix_out(x2, oa, ob, p, wga, wmo, wout, nw, wr_hi, wr_lo, br, tm=256):
    t, d = x2.shape
    full = lambda i: (0, 0)
    return pl.pallas_call(
        _mix_out_kernel,
        grid=(t // tm,),
        in_specs=[
            pl.BlockSpec((tm, d), lambda i: (i, 0)),
            pl.BlockSpec((tm, d), lambda i: (i, 0)),
            pl.BlockSpec((tm, d), lambda i: (i, 0)),
            pl.BlockSpec((tm, d), lambda i: (i, 4)),
            pl.BlockSpec((tm, d), lambda i: (i, 5)),
            pl.BlockSpec((d, d), full),
            pl.BlockSpec((d, d), full),
            pl.BlockSpec((d, d), full),
            pl.BlockSpec((1, d), full),
            pl.BlockSpec((d, LANES), full),
            pl.BlockSpec((d, LANES), full),
            pl.BlockSpec((1, LANES), full),
        ],
        out_specs=[
            pl.BlockSpec((tm, d), lambda i: (i, 0)),
            pl.BlockSpec((tm, d), lambda i: (i, 0)),
            pl.BlockSpec((tm, LANES), lambda i: (i, 0)),
            pl.BlockSpec((tm, LANES), lambda i: (i, 0)),
            pl.BlockSpec((tm, LANES), lambda i: (i, 0)),
        ],
        out_shape=[
            jax.ShapeDtypeStruct((t, d), F32),
            jax.ShapeDtypeStruct((t, d), F32),
            jax.ShapeDtypeStruct((t, LANES), BF16),
            jax.ShapeDtypeStruct((t, LANES), jnp.int32),
            jax.ShapeDtypeStruct((t, LANES), F32),
        ],
        compiler_params=_cp(("parallel",)),
        name="mix_out",
    )(x2, oa, ob, p, p, wga, wmo, wout, nw, wr_hi, wr_lo, br)


def _route_pos_kernel(sel_ref, idx_ref, pos_ref, cnt_ref, carry_ref):
    tm = sel_ref.shape[0]
    i = pl.program_id(0)

    @pl.when(i == 0)
    def _():
        carry_ref[...] = jnp.zeros_like(carry_ref)

    sel = sel_ref[...]
    row = lax.broadcasted_iota(jnp.int32, (tm, tm), 0)
    col = lax.broadcasted_iota(jnp.int32, (tm, tm), 1)
    tri = (col < row).astype(BF16)
    carry = carry_ref[0:1, :]
    pos = _dot(tri, sel) + carry
    lane = lax.broadcasted_iota(jnp.int32, (tm, LANES), 1)
    idx = idx_ref[...]
    out = jnp.zeros((tm, LANES), F32)
    for kk in range(TOP_K):
        hit = lane == idx[:, kk:kk + 1]
        pk = jnp.sum(jnp.where(hit, pos, 0.0), axis=-1, keepdims=True)
        out = jnp.where(lane == kk, pk, out)
    pos_ref[...] = out.astype(jnp.int32)
    total = carry + jnp.sum(sel.astype(F32), axis=0, keepdims=True)
    carry_ref[...] = jnp.broadcast_to(total, carry_ref.shape)
    cnt_ref[...] = jnp.broadcast_to(total, cnt_ref.shape).astype(jnp.int32)


def _route_pos(sel, idx, tm=512):
    t = sel.shape[0]
    return pl.pallas_call(
        _route_pos_kernel,
        grid=(t // tm,),
        in_specs=[
            pl.BlockSpec((tm, LANES), lambda i: (i, 0)),
            pl.BlockSpec((tm, LANES), lambda i: (i, 0)),
        ],
        out_specs=[
            pl.BlockSpec((tm, LANES), lambda i: (i, 0)),
            pl.BlockSpec((8, LANES), lambda i: (0, 0)),
        ],
        out_shape=[
            jax.ShapeDtypeStruct((t, LANES), jnp.int32),
            jax.ShapeDtypeStruct((8, LANES), jnp.int32),
        ],
        scratch_shapes=[pltpu.VMEM((8, LANES), F32)],
        compiler_params=_cp(("arbitrary",)),
        name="route_pos",
    )(sel, idx)


def _dispatch_kernel(dest_ref, h_ref, xs_in_ref, xs_ref, sem):
    del xs_in_ref
    tm = h_ref.shape[0]
    base = pl.program_id(0) * tm * TOP_K

    def copy(r, kk):
        d = dest_ref[base + r * TOP_K + kk]
        return pltpu.make_async_copy(h_ref.at[pl.ds(r, 1)], xs_ref.at[pl.ds(d, 1)], sem)

    def issue(r, carry):
        for kk in range(TOP_K):
            copy(r, kk).start(priority=kk % 2)
        return carry

    def drain(r, carry):
        for kk in range(TOP_K):
            copy(r, kk).wait()
        return carry

    lax.fori_loop(0, tm, issue, 0)
    lax.fori_loop(0, tm, drain, 0)


def _dispatch(dest_flat, h2, n_pad, tm=128):
    t, d = h2.shape
    xs0 = jnp.zeros((n_pad, d), h2.dtype)
    return pl.pallas_call(
        _dispatch_kernel,
        grid_spec=pltpu.PrefetchScalarGridSpec(
            num_scalar_prefetch=1,
            grid=(t // tm,),
            in_specs=[
                pl.BlockSpec((tm, d), lambda i, dest: (i, 0)),
                pl.BlockSpec(memory_space=pl.ANY),
            ],
            out_specs=pl.BlockSpec(memory_space=pl.ANY),
            scratch_shapes=[pltpu.SemaphoreType.DMA(())],
        ),
        out_shape=jax.ShapeDtypeStruct((n_pad, d), h2.dtype),
        input_output_aliases={2: 0},
        compiler_params=_cp(("arbitrary",)),
        name="dispatch",
    )(dest_flat, h2, xs0)


def _experts_kernel(be_ref, nv_ref, xs_ref, wgu_ref, wd_ref, bg_ref, bu_ref, bd_ref, ys_ref,
                    wg_s, wu_s, wd_s):
    j = pl.program_id(0)
    grp = 2 * LANES
    prev = be_ref[jnp.maximum(j - 1, 0)]

    @pl.when((j == 0) | (be_ref[j] != prev))
    def _():
        r = lax.broadcasted_iota(jnp.int32, (grp, grp), 0)
        c = lax.broadcasted_iota(jnp.int32, (grp, grp), 1)
        src = jnp.where(c < LANES, 2 * c, 2 * (c - LANES) + 1)
        pick = (r == src).astype(BF16)
        for g in range(wgu_ref.shape[2] // grp):
            y = _dot(wgu_ref[0, :, g * grp:(g + 1) * grp].astype(BF16), pick)
            wg_s[:, g * LANES:(g + 1) * LANES] = y[:, 0:LANES].astype(BF16)
            wu_s[:, g * LANES:(g + 1) * LANES] = y[:, LANES:grp].astype(BF16)
        wd_s[...] = wd_ref[0].astype(BF16)

    @pl.when(j < nv_ref[0])
    def _():
        x = xs_ref[...].astype(BF16)
        g = _dot(x, wg_s[...]) + bg_ref[0]
        u = _dot(x, wu_s[...]) + bu_ref[0]
        gate = jnp.minimum(g, SWIGLU_LIMIT)
        up = jnp.clip(u, -SWIGLU_LIMIT, SWIGLU_LIMIT)
        act = (up + 1.0) * (gate * jax.nn.sigmoid(gate * SWIGLU_ALPHA))
        ys_ref[...] = _dot(act.astype(BF16), wd_s[...]) + bd_ref[0]

    @pl.when(j >= nv_ref[0])
    def _():
        ys_ref[...] = jnp.zeros_like(ys_ref)


def _experts(block_e, n_valid, xs, wgu, wd, bg, bu, bd):
    n_pad, d = xs.shape
    de = wd.shape[1]
    n_blocks = n_pad // MOE_ROWS
    xrow = lambda j, be, nv: (jnp.minimum(j, nv[0] - 1), 0)
    wsel = lambda j, be, nv: (be[j], 0, 0)
    return pl.pallas_call(
        _experts_kernel,
        grid_spec=pltpu.PrefetchScalarGridSpec(
            num_scalar_prefetch=2,
            grid=(n_blocks,),
            in_specs=[
                pl.BlockSpec((MOE_ROWS, d), xrow),
                pl.BlockSpec((1, d, 2 * de), wsel),
                pl.BlockSpec((1, de, d), wsel),
                pl.BlockSpec((1, 1, de), wsel),
                pl.BlockSpec((1, 1, de), wsel),
                pl.BlockSpec((1, 1, d), wsel),
            ],
            out_specs=pl.BlockSpec((MOE_ROWS, d), lambda j, be, nv: (j, 0)),
            scratch_shapes=[
                pltpu.VMEM((d, de), BF16),
                pltpu.VMEM((d, de), BF16),
                pltpu.VMEM((de, d), BF16),
            ],
        ),
        out_shape=jax.ShapeDtypeStruct((n_pad, d), F32),
        compiler_params=pltpu.CompilerParams(dimension_semantics=("arbitrary",),
                                             vmem_limit_bytes=EXPERTS_VMEM_LIMIT),
        name="experts",
    )(block_e, n_valid, xs, wgu, wd, bg, bu, bd)


def _combine_kernel(dest_ref, x1_ref, tw_ref, nw_ref, ys_ref, o_ref, buf_ref, sem, *, final_norm):
    tm = x1_ref.shape[0]
    i = pl.program_id(0)
    n = pl.num_programs(0)

    def copy(step, slot, r, kk):
        d = dest_ref[(step * tm + r) * TOP_K + kk]
        return pltpu.make_async_copy(ys_ref.at[pl.ds(d, 1)], buf_ref.at[slot, kk, pl.ds(r, 1)],
                                     sem.at[slot])

    def issue(step, slot):
        def body(r, carry):
            for kk in range(TOP_K):
                copy(step, slot, r, kk).start(priority=kk % 2)
            return carry
        lax.fori_loop(0, tm, body, 0)

    def drain(step, slot):
        def body(r, carry):
            for kk in range(TOP_K):
                copy(step, slot, r, kk).wait()
            return carry
        lax.fori_loop(0, tm, body, 0)

    slot = i % 2

    @pl.when(i == 0)
    def _():
        issue(i, slot)

    @pl.when(i + 1 < n)
    def _():
        issue(i + 1, 1 - slot)

    drain(i, slot)

    tw = tw_ref[...]
    y = tw[:, 0:1] * buf_ref[slot, 0]
    for kk in range(1, TOP_K):
        y = y + tw[:, kk:kk + 1] * buf_ref[slot, kk]
    out = x1_ref[...] + y
    if final_norm:
        out = _rms(out, nw_ref[...])
    o_ref[...] = out


def _combine(dest_flat, x1, tw, nw, ys, final_norm, tm=128):
    t, d = x1.shape
    kern = functools.partial(_combine_kernel, final_norm=final_norm)
    return pl.pallas_call(
        kern,
        grid_spec=pltpu.PrefetchScalarGridSpec(
            num_scalar_prefetch=1,
            grid=(t // tm,),
            in_specs=[
                pl.BlockSpec((tm, d), lambda i, dest: (i, 0)),
                pl.BlockSpec((tm, LANES), lambda i, dest: (i, 0)),
                pl.BlockSpec((1, d), lambda i, dest: (0, 0)),
                pl.BlockSpec(memory_space=pl.ANY),
            ],
            out_specs=pl.BlockSpec((tm, d), lambda i, dest: (i, 0)),
            scratch_shapes=[pltpu.VMEM((2, TOP_K, tm, d), F32), pltpu.SemaphoreType.DMA((2,))],
        ),
        out_shape=jax.ShapeDtypeStruct((t, d), F32),
        compiler_params=_cp(("arbitrary",)),
        name="combine",
    )(dest_flat, x1, tw, nw, ys)


def _rope_tables(seq):
    half = MLA_ROPE // 2
    inv = 1.0 / (ROPE_THETA ** (jnp.arange(0, MLA_ROPE, 2, dtype=F32) / MLA_ROPE))
    ang = jnp.arange(seq, dtype=F32)[:, None] * inv[None, :]
    cos, sin = jnp.cos(ang), jnp.sin(ang)
    zeros = jnp.zeros((seq, LANES - MLA_ROPE), F32)
    cos_t = jnp.concatenate([cos, cos, zeros], axis=-1)
    sin_t = jnp.concatenate([-sin, sin, zeros], axis=-1)
    del half
    return cos_t, sin_t


def _pad_cols(a, width):
    return jnp.pad(a, ((0, 0), (0, width - a.shape[1])))


def _layer(x2, batch, seq, final_norm_w, final_norm, cos_t, sin_t,
           norm_mix_w, w_in, gdn_conv_w, gdn_a_log, gdn_dt_bias, gdn_norm_w, w_gdn_o,
           mla_q_norm_w, w_mla_q_b, mla_kv_norm_w, w_mla_kv_b, w_mla_o, w_out,
           norm_ffn_w, w_router, b_router, w_gate_up, b_gate_up, w_down, b_down):
    t, d = x2.shape
    qk_w = GDN_HEADS * GDN_D
    o_b = 4 * qk_w
    o_a = o_b + GDN_HEADS
    o_cq = o_a + GDN_HEADS
    o_ckv = o_cq + MLA_Q_LORA
    o_kr = o_ckv + MLA_KV_LORA
    o_ga = o_kr + MLA_ROPE
    o_gb = o_ga + d
    w_p = jnp.concatenate([
        w_in[:, 0:o_b], w_in[:, o_ga:o_gb + d], w_in[:, o_cq:o_ckv], w_in[:, o_ckv:o_kr],
        _pad_cols(w_in[:, o_kr:o_ga], 2 * LANES)], axis=1).astype(BF16)
    w_ab = _pad_cols(jnp.concatenate([w_in[:, o_a:o_cq], w_in[:, o_b:o_a]], axis=1), LANES).astype(BF16)

    p, ab = _in_proj(x2, norm_mix_w[None, :], w_p, w_ab)

    alog_row = _pad_cols(gdn_a_log[None, :].astype(F32), LANES)
    dtb_row = _pad_cols(gdn_dt_bias[None, :].astype(F32), LANES)
    qkvn, cols, gct = _gdn_prep(p, ab, gdn_conv_w.astype(F32), alog_row, dtb_row, seq)
    o_gdn = _gdn_chunk(qkvn, p, cols, gct, gdn_norm_w[None, :].astype(F32), batch, seq)

    hd = MLA_NOPE + MLA_ROPE
    wq = w_mla_q_b.reshape(MLA_Q_LORA, MLA_HEADS, hd)
    wq = jnp.pad(wq, ((0, 0), (0, 0), (0, 2 * LANES - hd))).reshape(MLA_Q_LORA, MLA_HEADS * 2 * LANES)
    wkv = w_mla_kv_b.reshape(MLA_KV_LORA, MLA_HEADS, MLA_NOPE + MLA_V)
    wkn = wkv[:, :, :MLA_NOPE].reshape(MLA_KV_LORA, -1)
    wvt = wkv[:, :, MLA_NOPE:].reshape(MLA_KV_LORA, -1).T
    q, kn, kr, vt = _mla_prep(p, cos_t, sin_t, mla_q_norm_w[None, :].astype(F32),
                              mla_kv_norm_w[None, :].astype(F32), wq.astype(BF16), wkn.astype(BF16),
                              wvt.astype(BF16), seq)
    o_mla = _mla_attn(q, kn, kr, vt, batch, seq)

    wr = _pad_cols(w_router.astype(F32), LANES)
    wr_hi = wr.astype(BF16)
    wr_lo = (wr - wr_hi.astype(F32)).astype(BF16)
    br = _pad_cols(b_router[None, :].astype(F32), LANES)
    x1, h2, sel, idx, tw = _mix_out(x2, o_gdn, o_mla, p, w_gdn_o.astype(BF16), w_mla_o.astype(BF16),
                                    w_out.astype(BF16), norm_ffn_w[None, :].astype(F32), wr_hi, wr_lo, br)

    pos, cnt = _route_pos(sel, idx)
    counts = cnt[0, :N_EXPERTS]
    padded = (counts + MOE_ROWS - 1) // MOE_ROWS * MOE_ROWS
    pad_end = jnp.cumsum(padded)
    pad_start = pad_end - padded
    idx4 = idx[:, :TOP_K]
    dest = (pad_start[idx4] + pos[:, :TOP_K]).astype(jnp.int32).reshape(-1)
    n_pad = t * TOP_K + N_EXPERTS * MOE_ROWS
    n_blocks = n_pad // MOE_ROWS
    blk_start = jnp.arange(n_blocks, dtype=jnp.int32) * MOE_ROWS
    block_e = jnp.minimum(jnp.sum((pad_end[None, :] <= blk_start[:, None]).astype(jnp.int32), axis=1),
                          N_EXPERTS - 1).astype(jnp.int32)
    n_valid = (pad_end[-1:] // MOE_ROWS).astype(jnp.int32)

    xs = _dispatch(dest, h2, n_pad)
    bg = b_gate_up[:, None, 0::2].astype(F32)
    bu = b_gate_up[:, None, 1::2].astype(F32)
    ys = _experts(block_e, n_valid, xs, w_gate_up, w_down, bg, bu, b_down[:, None, :].astype(F32))
    return _combine(dest, x1, tw, final_norm_w[None, :].astype(F32), ys, final_norm)


def kernel(x, norm_mix_w, w_in, gdn_conv_w, gdn_a_log, gdn_dt_bias, gdn_norm_w, w_gdn_o, mla_q_norm_w, w_mla_q_b, mla_kv_norm_w, w_mla_kv_b, w_mla_o, w_out, norm_ffn_w, w_router, b_router, w_gate_up, b_gate_up, w_down, b_down, norm_final_w):
    batch, seq, d = x.shape
    depth = w_in.shape[0]
    cos_t, sin_t = _rope_tables(seq)
    x2 = x.reshape(batch * seq, d)
    for layer in range(depth):
        x2 = _layer(x2, batch, seq, norm_final_w, layer == depth - 1, cos_t, sin_t,
                    norm_mix_w[layer], w_in[layer], gdn_conv_w[layer], gdn_a_log[layer],
                    gdn_dt_bias[layer], gdn_norm_w[layer], w_gdn_o[layer], mla_q_norm_w[layer],
                    w_mla_q_b[layer], mla_kv_norm_w[layer], w_mla_kv_b[layer], w_mla_o[layer],
                    w_out[layer], norm_ffn_w[layer], w_router[layer], b_router[layer],
                    w_gate_up[layer], b_gate_up[layer], w_down[layer], b_down[layer])
    return x2.reshape(batch, seq, d)
```

```python
import functools

import jax
import jax.numpy as jnp
import numpy as np
from jax import lax
from jax.experimental import pallas as pl
from jax.experimental.pallas import tpu as pltpu

F32 = jnp.float32
BF16 = jnp.bfloat16

CHUNK = 64
NORM_EPS = 1e-6
GDN_HEADS = 8
GDN_D = 128
GDN_CONV = 4
MLA_HEADS = 8
MLA_Q_LORA = 512
MLA_KV_LORA = 256
MLA_NOPE = 128
MLA_ROPE = 64
MLA_V = 128
ROPE_THETA = 10000.0
N_EXPERTS = 32
TOP_K = 4
SWIGLU_LIMIT = 7.0
SWIGLU_ALPHA = 1.702

LANES = 128
ROW_TILE_SUBLANES = 8
MOE_ROWS = 512
ZERO_ROWS = 64
VMEM_LIMIT = 48 * 1024 * 1024
EXPERTS_VMEM_LIMIT = 56 * 1024 * 1024

NEG_BIG = -1e30
LOG2_E = 1.4426950408889634


def _cp(sem):
    return pltpu.CompilerParams(dimension_semantics=sem, vmem_limit_bytes=VMEM_LIMIT)


def _dot(a, b):
    return jnp.dot(a, b, preferred_element_type=F32)


def _dot_nt(a, b):
    return lax.dot_general(a, b, (((1,), (1,)), ((), ())), preferred_element_type=F32)


def _dot_tn(a, b):
    return lax.dot_general(a, b, (((0,), (0,)), ((), ())), preferred_element_type=F32)


def _split3(x):
    hi = x.astype(BF16)
    r = x - hi.astype(F32)
    mid = r.astype(BF16)
    lo = (r - mid.astype(F32)).astype(BF16)
    return hi, mid, lo


def _rms(x, w):
    ms = jnp.mean(x * x, axis=-1, keepdims=True)
    return x * lax.rsqrt(ms + NORM_EPS) * w


def _store_row_tiles(ref, x):
    rows = x.shape[0]
    for s in range(ROW_TILE_SUBLANES):
        ref[pl.ds(s, rows, stride=ROW_TILE_SUBLANES), :] = x[:, s * LANES:(s + 1) * LANES]


def _load_row_tiles(ref, rows, *lead):
    return jnp.concatenate(
        [ref[(*lead, pl.ds(s, rows, stride=ROW_TILE_SUBLANES), slice(None))]
         for s in range(ROW_TILE_SUBLANES)], axis=1)


def _row_tile(ref, r, *lead):
    return ref.at[(*lead, pl.ds(pl.multiple_of(r * ROW_TILE_SUBLANES, ROW_TILE_SUBLANES),
                                ROW_TILE_SUBLANES))]


def _in_proj_kernel(x_ref, nw_ref, w_ref, wab_ref, p_ref, ab_ref, h_ref):
    @pl.when(pl.program_id(1) == 0)
    def _():
        hb = _rms(x_ref[...], nw_ref[...]).astype(BF16)
        h_ref[...] = hb
        ab_ref[...] = _dot(hb, wab_ref[...])

    p_ref[...] = _dot(h_ref[...], w_ref[...]).astype(BF16)


def _in_proj(x2, norm_w, w_p, w_ab, tm=1024, tn=1024):
    t, d = x2.shape
    n = w_p.shape[1]
    return pl.pallas_call(
        _in_proj_kernel,
        grid=(t // tm, n // tn),
        in_specs=[
            pl.BlockSpec((tm, d), lambda i, j: (i, 0)),
            pl.BlockSpec((1, d), lambda i, j: (0, 0)),
            pl.BlockSpec((d, tn), lambda i, j: (0, j)),
            pl.BlockSpec((d, LANES), lambda i, j: (0, 0)),
        ],
        out_specs=[
            pl.BlockSpec((tm, tn), lambda i, j: (i, j)),
            pl.BlockSpec((tm, LANES), lambda i, j: (i, 0)),
        ],
        out_shape=[
            jax.ShapeDtypeStruct((t, n), BF16),
            jax.ShapeDtypeStruct((t, LANES), F32),
        ],
        scratch_shapes=[pltpu.VMEM((tm, d), BF16)],
        compiler_params=_cp(("parallel", "arbitrary")),
        name="in_proj",
    )(x2, norm_w, w_p, w_ab)


def _gdn_prep_kernel(cur_ref, prev_ref, ab_ref, cw_ref, alog_ref, dtb_ref,
                     qkv_ref, cols_ref, gct_ref, *, tiles_per_seq):
    tm = cur_ref.shape[0]
    i = pl.program_id(0)
    halo_on = (i % tiles_per_seq) != 0
    n_blk = cur_ref.shape[1] // LANES
    q_scale = GDN_D ** -0.5
    for cb in range(n_blk):
        cs = slice(cb * LANES, (cb + 1) * LANES)
        cur = cur_ref[:, cs].astype(F32)
        halo = prev_ref[:, cs].astype(F32)[8:16, :]
        halo = jnp.where(halo_on, halo, 0.0)
        xe = jnp.concatenate([halo, cur], axis=0)
        w = cw_ref[:, cs]
        y = w[0:1, :] * xe[5:5 + tm, :]
        for j in range(1, GDN_CONV):
            y = y + w[j:j + 1, :] * xe[5 + j:5 + j + tm, :]
        y = y * jax.nn.sigmoid(y)
        if cb < 2 * GDN_HEADS:
            ss = jnp.sum(y * y, axis=-1, keepdims=True)
            y = y * lax.rsqrt(ss + NORM_EPS)
            if cb < GDN_HEADS:
                y = y * q_scale
        qkv_ref[:, cs] = y.astype(BF16)

    ab = ab_ref[...]
    g = -jnp.exp(alog_ref[...]) * jax.nn.softplus(ab + dtb_ref[...])
    row = lax.broadcasted_iota(jnp.int32, (tm, tm), 0)
    col = lax.broadcasted_iota(jnp.int32, (tm, tm), 1)
    tri = ((col <= row) & ((row // CHUNK) == (col // CHUNK))).astype(BF16)
    g_hi, g_mid, g_lo = _split3(g)
    gc = _dot(tri, g_hi) + _dot(tri, g_mid) + _dot(tri, g_lo)
    lane = lax.broadcasted_iota(jnp.int32, (tm, LANES), 1)
    cols_ref[...] = jnp.where(lane < GDN_HEADS, gc, jax.nn.sigmoid(ab))
    for c in range(tm // CHUNK):
        blk = gc[c * CHUNK:(c + 1) * CHUNK, :]
        blk = jnp.concatenate([blk, jnp.zeros_like(blk)], axis=0)
        gct_ref[c] = blk.T[0:GDN_HEADS, 0:CHUNK]


def _gdn_prep(p, ab, conv_w, alog_row, dtb_row, seq, tm=256):
    t = p.shape[0]
    cw = 3 * GDN_HEADS * GDN_D
    tiles_per_seq = seq // tm
    kern = functools.partial(_gdn_prep_kernel, tiles_per_seq=tiles_per_seq)
    return pl.pallas_call(
        kern,
        grid=(t // tm,),
        in_specs=[
            pl.BlockSpec((tm, cw), lambda i: (i, 0)),
            pl.BlockSpec((16, cw), lambda i: (jnp.maximum(i * (tm // 16) - 1, 0), 0)),
            pl.BlockSpec((tm, LANES), lambda i: (i, 0)),
            pl.BlockSpec((GDN_CONV, cw), lambda i: (0, 0)),
            pl.BlockSpec((1, LANES), lambda i: (0, 0)),
            pl.BlockSpec((1, LANES), lambda i: (0, 0)),
        ],
        out_specs=[
            pl.BlockSpec((tm, cw), lambda i: (i, 0)),
            pl.BlockSpec((tm, LANES), lambda i: (i, 0)),
            pl.BlockSpec((tm // CHUNK, GDN_HEADS, CHUNK), lambda i: (i, 0, 0)),
        ],
        out_shape=[
            jax.ShapeDtypeStruct((t, cw), BF16),
            jax.ShapeDtypeStruct((t, LANES), F32),
            jax.ShapeDtypeStruct((t // CHUNK, GDN_HEADS, CHUNK), F32),
        ],
        compiler_params=_cp(("parallel",)),
        name="gdn_prep",
    )(p, p, ab, conv_w, alog_row, dtb_row)


def _gdn_chunk_kernel(q_ref, k_ref, v_ref, z_ref, cols_ref, gct_ref, nw_ref, o_ref, s_ref):
    c = CHUNK
    nb = q_ref.shape[0]
    units = [(b, h) for b in range(nb) for h in range(GDN_HEADS)]

    @pl.when(pl.program_id(0) == 0)
    def _():
        s_ref[...] = jnp.zeros_like(s_ref)

    ri = lax.broadcasted_iota(jnp.int32, (c, c), 0)
    ci = lax.broadcasted_iota(jnp.int32, (c, c), 1)
    incl = ri >= ci
    strict = ri > ci
    eye = (ri == ci).astype(F32)
    nw = nw_ref[...]

    cols, e_g, e_kd, e_last, gct = [], [], [], [], []
    for b in range(nb):
        cb = cols_ref[b]
        last = cb[c - 1:c, :]
        cols.append(cb)
        e_g.append(jnp.exp(cb))
        e_kd.append(jnp.exp(last - cb))
        e_last.append(jnp.exp(last))
        gct.append(gct_ref[b, 0])

    kq, kb_l, kf_l = [], [], []
    for b, h in units:
        hs = slice(h * GDN_D, (h + 1) * GDN_D)
        k = k_ref[b, :, hs]
        kf = k.astype(F32)
        kb = kf * cols[b][:, GDN_HEADS + h:GDN_HEADS + h + 1]
        kq.append(_dot_nt(jnp.concatenate([kb.astype(BF16), q_ref[b, :, hs]], axis=0), k))
        kb_l.append(kb)
        kf_l.append(kf)

    a_l, qk_l = [], []
    for i, (b, h) in enumerate(units):
        dec = jnp.exp(jnp.minimum(cols[b][:, h:h + 1] - gct[b][h:h + 1, :], 0.0))
        a_l.append(jnp.where(strict, -kq[i][0:c, :] * dec, 0.0))
        qk_l.append(jnp.where(incl, kq[i][c:2 * c, :] * dec, 0.0).astype(BF16))

    tinv = [eye + a for a in a_l]
    pw = a_l
    for _ in range(5):
        pwb = [x.astype(BF16) for x in pw]
        pw = [_dot(x, x) for x in pwb]
        tinv = [t + _dot(t.astype(BF16), x.astype(BF16)) for t, x in zip(tinv, pw)]

    uw = []
    for i, (b, h) in enumerate(units):
        hs = slice(h * GDN_D, (h + 1) * GDN_D)
        beta = cols[b][:, GDN_HEADS + h:GDN_HEADS + h + 1]
        rhs = jnp.concatenate([v_ref[b, :, hs].astype(F32) * beta,
                               kb_l[i] * e_g[b][:, h:h + 1]], axis=1).astype(BF16)
        uw.append(_dot(tinv[i].astype(BF16), rhs))

    r_l = []
    for i, (b, h) in enumerate(units):
        hs = slice(h * GDN_D, (h + 1) * GDN_D)
        qd = (q_ref[b, :, hs].astype(F32) * e_g[b][:, h:h + 1]).astype(BF16)
        lhs = jnp.concatenate([uw[i][:, GDN_D:2 * GDN_D].astype(BF16), qd], axis=0)
        r_l.append(_dot(lhs, s_ref[b * GDN_HEADS + h].astype(BF16)))

    for i, (b, h) in enumerate(units):
        hs = slice(h * GDN_D, (h + 1) * GDN_D)
        v_new = (uw[i][:, 0:GDN_D] - r_l[i][0:c, :]).astype(BF16)
        o = r_l[i][c:2 * c, :] + _dot(qk_l[i], v_new)
        kd = (kf_l[i] * e_kd[b][:, h:h + 1]).astype(BF16)
        u = b * GDN_HEADS + h
        s_ref[u] = s_ref[u] * e_last[b][:, h:h + 1] + _dot_tn(kd, v_new)
        z = z_ref[b, :, hs].astype(F32)
        o_ref[b, :, hs] = (_rms(o, nw) * (z * jax.nn.sigmoid(z))).astype(BF16)


def _gdn_chunk(qkvn, p, cols, gct, norm_w, batch, seq):
    nc = seq // CHUNK
    hw = GDN_HEADS * GDN_D
    qkvn3 = qkvn.reshape(batch, seq, qkvn.shape[1])
    p3 = p.reshape(batch, seq, p.shape[1])
    cols3 = cols.reshape(batch, seq, LANES)
    gct4 = gct.reshape(batch, nc, GDN_HEADS, CHUNK)
    tile = lambda col: pl.BlockSpec((batch, CHUNK, hw), lambda c: (0, c, col))
    out = pl.pallas_call(
        _gdn_chunk_kernel,
        grid=(nc,),
        in_specs=[
            tile(0), tile(1), tile(2),
            tile(3),
            pl.BlockSpec((batch, CHUNK, LANES), lambda c: (0, c, 0)),
            pl.BlockSpec((batch, 1, GDN_HEADS, CHUNK), lambda c: (0, c, 0, 0)),
            pl.BlockSpec((1, GDN_D), lambda c: (0, 0)),
        ],
        out_specs=pl.BlockSpec((batch, CHUNK, hw), lambda c: (0, c, 0)),
        out_shape=jax.ShapeDtypeStruct((batch, seq, hw), BF16),
        scratch_shapes=[pltpu.VMEM((batch * GDN_HEADS, GDN_D, GDN_D), F32)],
        compiler_params=_cp(("arbitrary",)),
        name="gdn_chunk",
    )(qkvn3, qkvn3, qkvn3, p3, cols3, gct4, norm_w)
    return out.reshape(batch * seq, hw)


def _rope(x, cos, sin_signed):
    lane = lax.broadcasted_iota(jnp.int32, x.shape, 1)
    fwd = pltpu.roll(x, LANES - MLA_ROPE // 2, 1)
    bwd = pltpu.roll(x, MLA_ROPE // 2, 1)
    rot = jnp.where(lane < MLA_ROPE // 2, fwd, bwd)
    return x * cos + rot * sin_signed


def _mla_prep_kernel(cq_ref, ckv_ref, kr_ref, cos_ref, sin_ref, qnw_ref, kvnw_ref, wq_ref, wkn_ref,
                     wvt_ref, q_ref, kn_ref, kro_ref, vt_ref):
    cos = cos_ref[...]
    sin = sin_ref[...]
    cq = _rms(cq_ref[...].astype(F32), qnw_ref[...]).astype(BF16)
    hd = 2 * LANES
    scale = (MLA_NOPE + MLA_ROPE) ** -0.5 * LOG2_E
    for h in range(MLA_HEADS):
        qh = _dot(cq, wq_ref[:, h * hd:(h + 1) * hd]) * scale
        q_ref[:, h * hd:h * hd + LANES] = qh[:, 0:LANES].astype(BF16)
        q_ref[:, h * hd + LANES:(h + 1) * hd] = _rope(qh[:, LANES:hd], cos, sin).astype(BF16)
    kvl = _rms(ckv_ref[...].astype(F32), kvnw_ref[...]).astype(BF16)
    kn_ref[...] = _dot(kvl, wkn_ref[...]).astype(BF16)
    vt_ref[...] = _dot_nt(wvt_ref[...], kvl).astype(BF16)
    kro_ref[...] = _rope(kr_ref[...].astype(F32), cos, sin).astype(BF16)


def _mla_prep(p, cos_t, sin_t, qnw, kvnw, wq, wkn, wvt, seq, tm=512):
    t = p.shape[0]
    tiles_per_seq = seq // tm
    hw = MLA_HEADS * MLA_NOPE
    cq_blk = 6144 // MLA_Q_LORA
    ckv_blk = 6656 // MLA_KV_LORA
    kr_blk = 6912 // LANES
    return pl.pallas_call(
        _mla_prep_kernel,
        grid=(t // tm,),
        in_specs=[
            pl.BlockSpec((tm, MLA_Q_LORA), lambda i: (i, cq_blk)),
            pl.BlockSpec((tm, MLA_KV_LORA), lambda i: (i, ckv_blk)),
            pl.BlockSpec((tm, LANES), lambda i: (i, kr_blk)),
            pl.BlockSpec((tm, LANES), lambda i: (i % tiles_per_seq, 0)),
            pl.BlockSpec((tm, LANES), lambda i: (i % tiles_per_seq, 0)),
            pl.BlockSpec((1, MLA_Q_LORA), lambda i: (0, 0)),
            pl.BlockSpec((1, MLA_KV_LORA), lambda i: (0, 0)),
            pl.BlockSpec((MLA_Q_LORA, 2 * hw), lambda i: (0, 0)),
            pl.BlockSpec((MLA_KV_LORA, hw), lambda i: (0, 0)),
            pl.BlockSpec((hw, MLA_KV_LORA), lambda i: (0, 0)),
        ],
        out_specs=[
            pl.BlockSpec((tm, 2 * hw), lambda i: (i, 0)),
            pl.BlockSpec((tm, hw), lambda i: (i, 0)),
            pl.BlockSpec((tm, LANES), lambda i: (i, 0)),
            pl.BlockSpec((hw, tm), lambda i: (0, i)),
        ],
        out_shape=[
            jax.ShapeDtypeStruct((t, 2 * hw), BF16),
            jax.ShapeDtypeStruct((t, hw), BF16),
            jax.ShapeDtypeStruct((t, LANES), BF16),
            jax.ShapeDtypeStruct((hw, t), BF16),
        ],
        compiler_params=_cp(("parallel",)),
        name="mla_prep",
    )(p, p, p, cos_t, sin_t, qnw, kvnw, wq, wkn, wvt)


ATTN_HEADS_PER_STEP = 4
ATTN_SUM_ROWS = 16


def _mla_attn_kernel(qt_ref, kt_ref, q_ref, kn_ref, kr_ref, vt_ref, o_ref, m_ref, acc_ref):
    qi = qt_ref[pl.program_id(2)]
    ki = kt_ref[pl.program_id(2)]
    tq = q_ref.shape[0]
    tk = kn_ref.shape[0]
    hd = 2 * LANES

    @pl.when(ki == 0)
    def _():
        m_ref[...] = jnp.full_like(m_ref, NEG_BIG)
        acc_ref[...] = jnp.zeros_like(acc_ref)

    def step(masked):
        kr = kr_ref[...]
        ones = jnp.ones((ATTN_SUM_ROWS, tk), BF16)

        def scores(h):
            k = jnp.concatenate([kn_ref[:, h * MLA_NOPE:(h + 1) * MLA_NOPE], kr], axis=1)
            s = _dot_nt(k, q_ref[:, h * hd:(h + 1) * hd])
            if masked:
                ck = lax.broadcasted_iota(jnp.int32, (tk, tq), 0) // CHUNK
                cq = lax.broadcasted_iota(jnp.int32, (tk, tq), 1) // CHUNK
                s = jnp.where(ck <= cq, s, NEG_BIG)
            return s

        def update(h, s):
            m_prev = m_ref[h]
            m_new = jnp.maximum(m_prev, jnp.max(s, axis=0, keepdims=True))
            alpha = jnp.exp2(m_prev - m_new)
            p = jnp.exp2((s - m_new).astype(BF16))
            v_ext = jnp.concatenate([vt_ref[h * MLA_V:(h + 1) * MLA_V, :], ones], axis=0)
            acc_ref[h] = alpha * acc_ref[h] + _dot(v_ext, p)
            m_ref[h] = m_new

        s_prev = scores(0)
        for h in range(1, ATTN_HEADS_PER_STEP):
            s_next = scores(h)
            update(h - 1, s_prev)
            s_prev = s_next
        update(ATTN_HEADS_PER_STEP - 1, s_prev)

    @pl.when(ki < qi)
    def _():
        step(False)

    @pl.when(ki == qi)
    def _():
        step(True)
        for h in range(ATTN_HEADS_PER_STEP):
            acc = acc_ref[h]
            o = acc[0:MLA_V, :] / acc[MLA_V:MLA_V + 1, :]
            o_ref[:, h * MLA_V:(h + 1) * MLA_V] = o.T.astype(BF16)


def _mla_attn(q, kn, kr, vt, batch, seq, tq=512):
    t = q.shape[0]
    nq = seq // tq
    hps = ATTN_HEADS_PER_STEP
    pairs = [(qi, ki) for qi in range(nq) for ki in range(qi + 1)]
    qt = jnp.asarray(np.array([pr[0] for pr in pairs], np.int32))
    kt = jnp.asarray(np.array([pr[1] for pr in pairs], np.int32))
    return pl.pallas_call(
        _mla_attn_kernel,
        grid_spec=pltpu.PrefetchScalarGridSpec(
            num_scalar_prefetch=2,
            grid=(batch, MLA_HEADS // hps, len(pairs)),
            in_specs=[
                pl.BlockSpec((tq, hps * 2 * LANES), lambda b, h, pr, qt, kt: (b * nq + qt[pr], h)),
                pl.BlockSpec((tq, hps * MLA_NOPE), lambda b, h, pr, qt, kt: (b * nq + kt[pr], h)),
                pl.BlockSpec((tq, LANES), lambda b, h, pr, qt, kt: (b * nq + kt[pr], 0)),
                pl.BlockSpec((hps * MLA_V, tq), lambda b, h, pr, qt, kt: (h, b * nq + kt[pr])),
            ],
            out_specs=pl.BlockSpec((tq, hps * MLA_V), lambda b, h, pr, qt, kt: (b * nq + qt[pr], h)),
            scratch_shapes=[
                pltpu.VMEM((hps, 1, tq), F32),
                pltpu.VMEM((hps, MLA_V + ATTN_SUM_ROWS, tq), F32),
            ],
        ),
        out_shape=jax.ShapeDtypeStruct((t, MLA_HEADS * MLA_V), BF16),
        compiler_params=_cp(("parallel", "parallel", "arbitrary")),
        name="mla_attn",
    )(qt, kt, q, kn, kr, vt)


def _mix_out_kernel(x_ref, oa_ref, ob_ref, ga_ref, gb_ref, wga_ref, wmo_ref, wout_ref, nw_ref,
                    wr_hi_ref, wr_lo_ref, br_ref, x1_ref, h2_ref, sel_ref, idx_ref, tw_ref):
    ya = _dot(oa_ref[...], wga_ref[...])
    yb = _dot(ob_ref[...], wmo_ref[...])
    merged = (jax.nn.sigmoid(ga_ref[...].astype(F32)) * ya
              + jax.nn.sigmoid(gb_ref[...].astype(F32)) * yb)
    x1 = x_ref[...] + _dot(merged.astype(BF16), wout_ref[...])
    x1_ref[...] = x1
    h2 = _rms(x1, nw_ref[...])
    _store_row_tiles(h2_ref, h2)

    h_hi = h2.astype(BF16)
    h_lo = (h2 - h_hi.astype(F32)).astype(BF16)
    logits = (_dot(h_hi, wr_hi_ref[...]) + _dot(h_hi, wr_lo_ref[...]) + _dot(h_lo, wr_hi_ref[...])
              + br_ref[...])
    lane = lax.broadcasted_iota(jnp.int32, logits.shape, 1)
    work = jnp.where(lane < N_EXPERTS, logits, -jnp.inf)
    sel = jnp.zeros(logits.shape, F32)
    idx_out = jnp.zeros(logits.shape, jnp.int32)
    tw_out = jnp.zeros(logits.shape, F32)
    top = None
    denom = None
    for kk in range(TOP_K):
        mx = jnp.max(work, axis=-1, keepdims=True)
        am = jnp.min(jnp.where(work == mx, lane, LANES), axis=-1, keepdims=True)
        hit = lane == am
        if kk == 0:
            top = mx
            e = jnp.ones_like(mx)
            denom = e
        else:
            e = jnp.exp(mx - top)
            denom = denom + e
        sel = jnp.where(hit, 1.0, sel)
        idx_out = jnp.where(lane == kk, am, idx_out)
        tw_out = jnp.where(lane == kk, e, tw_out)
        work = jnp.where(hit, -jnp.inf, work)
    sel_ref[...] = sel.astype(BF16)
    idx_ref[...] = idx_out
    tw_ref[...] = tw_out / denom


def _mix_out(x2, oa, ob, p, wga, wmo, wout, nw, wr_hi, wr_lo, br, tm=256):
    t, d = x2.shape
    full = lambda i: (0, 0)
    return pl.pallas_call(
        _mix_out_kernel,
        grid=(t // tm,),
        in_specs=[
            pl.BlockSpec((tm, d), lambda i: (i, 0)),
            pl.BlockSpec((tm, d), lambda i: (i, 0)),
            pl.BlockSpec((tm, d), lambda i: (i, 0)),
            pl.BlockSpec((tm, d), lambda i: (i, 4)),
            pl.BlockSpec((tm, d), lambda i: (i, 5)),
            pl.BlockSpec((d, d), full),
            pl.BlockSpec((d, d), full),
            pl.BlockSpec((d, d), full),
            pl.BlockSpec((1, d), full),
            pl.BlockSpec((d, LANES), full),
            pl.BlockSpec((d, LANES), full),
            pl.BlockSpec((1, LANES), full),
        ],
        out_specs=[
            pl.BlockSpec((tm, d), lambda i: (i, 0)),
            pl.BlockSpec((tm * ROW_TILE_SUBLANES, LANES), lambda i: (i, 0)),
            pl.BlockSpec((tm, LANES), lambda i: (i, 0)),
            pl.BlockSpec((tm, LANES), lambda i: (i, 0)),
            pl.BlockSpec((tm, LANES), lambda i: (i, 0)),
        ],
        out_shape=[
            jax.ShapeDtypeStruct((t, d), F32),
            jax.ShapeDtypeStruct((t * ROW_TILE_SUBLANES, LANES), F32),
            jax.ShapeDtypeStruct((t, LANES), BF16),
            jax.ShapeDtypeStruct((t, LANES), jnp.int32),
            jax.ShapeDtypeStruct((t, LANES), F32),
        ],
        compiler_params=_cp(("parallel",)),
        name="mix_out",
    )(x2, oa, ob, p, p, wga, wmo, wout, nw, wr_hi, wr_lo, br)


def _route_pos_kernel(sel_ref, idx_ref, pos_ref, cnt_ref, carry_ref):
    tm = sel_ref.shape[0]
    i = pl.program_id(0)

    @pl.when(i == 0)
    def _():
        carry_ref[...] = jnp.zeros_like(carry_ref)

    sel = sel_ref[...]
    row = lax.broadcasted_iota(jnp.int32, (tm, tm), 0)
    col = lax.broadcasted_iota(jnp.int32, (tm, tm), 1)
    tri = (col < row).astype(BF16)
    carry = carry_ref[0:1, :]
    pos = _dot(tri, sel) + carry
    lane = lax.broadcasted_iota(jnp.int32, (tm, LANES), 1)
    idx = idx_ref[...]
    out = jnp.zeros((tm, LANES), F32)
    for kk in range(TOP_K):
        hit = lane == idx[:, kk:kk + 1]
        pk = jnp.sum(jnp.where(hit, pos, 0.0), axis=-1, keepdims=True)
        out = jnp.where(lane == kk, pk, out)
    pos_ref[...] = out.astype(jnp.int32)
    total = carry + jnp.sum(sel.astype(F32), axis=0, keepdims=True)
    carry_ref[...] = jnp.broadcast_to(total, carry_ref.shape)
    cnt_ref[...] = jnp.broadcast_to(total, cnt_ref.shape).astype(jnp.int32)


def _route_pos(sel, idx, tm=512):
    t = sel.shape[0]
    return pl.pallas_call(
        _route_pos_kernel,
        grid=(t // tm,),
        in_specs=[
            pl.BlockSpec((tm, LANES), lambda i: (i, 0)),
            pl.BlockSpec((tm, LANES), lambda i: (i, 0)),
        ],
        out_specs=[
            pl.BlockSpec((tm, LANES), lambda i: (i, 0)),
            pl.BlockSpec((8, LANES), lambda i: (0, 0)),
        ],
        out_shape=[
            jax.ShapeDtypeStruct((t, LANES), jnp.int32),
            jax.ShapeDtypeStruct((8, LANES), jnp.int32),
        ],
        scratch_shapes=[pltpu.VMEM((8, LANES), F32)],
        compiler_params=_cp(("arbitrary",)),
        name="route_pos",
    )(sel, idx)


def _dispatch_kernel(dest_ref, fill_lo_ref, fill_hi_ref, h_ref, xs_ref, zero_ref, sem):
    tm = h_ref.shape[0] // ROW_TILE_SUBLANES
    base = pl.program_id(0) * tm * TOP_K

    @pl.when(pl.program_id(0) == 0)
    def _():
        zero_ref[...] = jnp.zeros_like(zero_ref)

        def fill(d):
            return pltpu.make_async_copy(zero_ref.at[pl.ds(0, ROW_TILE_SUBLANES)], _row_tile(xs_ref, d), sem)

        def per_expert(e, carry):
            lo, hi = fill_lo_ref[e], fill_hi_ref[e]
            lax.fori_loop(lo, hi, lambda d, c: (fill(d).start(), c)[1], 0)
            lax.fori_loop(lo, hi, lambda d, c: (fill(d).wait(), c)[1], 0)
            return carry

        lax.fori_loop(0, N_EXPERTS, per_expert, 0)

        def fill_tail(c):
            rows = ZERO_ROWS * ROW_TILE_SUBLANES
            return pltpu.make_async_copy(zero_ref, xs_ref.at[pl.ds(pl.multiple_of(c * rows, rows), rows)],
                                         sem)

        lo = fill_hi_ref[N_EXPERTS - 1] // ZERO_ROWS
        hi = xs_ref.shape[0] // (ZERO_ROWS * ROW_TILE_SUBLANES)
        lax.fori_loop(lo, hi, lambda c, a: (fill_tail(c).start(), a)[1], 0)
        lax.fori_loop(lo, hi, lambda c, a: (fill_tail(c).wait(), a)[1], 0)

    def copy(r, kk):
        d = dest_ref[base + r * TOP_K + kk]
        return pltpu.make_async_copy(_row_tile(h_ref, r), _row_tile(xs_ref, d), sem)

    def issue(r, carry):
        for kk in range(TOP_K):
            copy(r, kk).start()
        return carry

    def drain(r, carry):
        for kk in range(TOP_K):
            copy(r, kk).wait()
        return carry

    lax.fori_loop(0, tm, issue, 0)
    lax.fori_loop(0, tm, drain, 0)


def _dispatch(dest_flat, fill_lo, fill_hi, h2, n_pad, tm=128):
    t = h2.shape[0] // ROW_TILE_SUBLANES
    return pl.pallas_call(
        _dispatch_kernel,
        grid_spec=pltpu.PrefetchScalarGridSpec(
            num_scalar_prefetch=3,
            grid=(t // tm,),
            in_specs=[
                pl.BlockSpec((tm * ROW_TILE_SUBLANES, LANES), lambda i, *_: (i, 0)),
            ],
            out_specs=pl.BlockSpec(memory_space=pl.ANY),
            scratch_shapes=[pltpu.VMEM((ZERO_ROWS * ROW_TILE_SUBLANES, LANES), F32),
                            pltpu.SemaphoreType.DMA(())],
        ),
        out_shape=jax.ShapeDtypeStruct((n_pad * ROW_TILE_SUBLANES, LANES), h2.dtype),
        compiler_params=_cp(("arbitrary",)),
        name="dispatch",
    )(dest_flat, fill_lo, fill_hi, h2)


def _experts_kernel(be_ref, nv_ref, xs_ref, wgu_ref, wd_ref, bg_ref, bu_ref, bd_ref, ys_ref,
                    wg_s, wu_s, wd_s):
    j = pl.program_id(0)
    grp = 2 * LANES
    prev = be_ref[jnp.maximum(j - 1, 0)]

    @pl.when((j == 0) | (be_ref[j] != prev))
    def _():
        r = lax.broadcasted_iota(jnp.int32, (grp, grp), 0)
        c = lax.broadcasted_iota(jnp.int32, (grp, grp), 1)
        src = jnp.where(c < LANES, 2 * c, 2 * (c - LANES) + 1)
        pick = (r == src).astype(BF16)
        for g in range(wgu_ref.shape[2] // grp):
            y = _dot(wgu_ref[0, :, g * grp:(g + 1) * grp].astype(BF16), pick)
            wg_s[:, g * LANES:(g + 1) * LANES] = y[:, 0:LANES].astype(BF16)
            wu_s[:, g * LANES:(g + 1) * LANES] = y[:, LANES:grp].astype(BF16)
        wd_s[...] = wd_ref[0].astype(BF16)

    @pl.when(j < nv_ref[0])
    def _():
        x = _load_row_tiles(xs_ref, MOE_ROWS).astype(BF16)
        g = _dot(x, wg_s[...]) + bg_ref[0]
        u = _dot(x, wu_s[...]) + bu_ref[0]
        gate = jnp.minimum(g, SWIGLU_LIMIT)
        up = jnp.clip(u, -SWIGLU_LIMIT, SWIGLU_LIMIT)
        act = (up + 1.0) * (gate * jax.nn.sigmoid(gate * SWIGLU_ALPHA))
        _store_row_tiles(ys_ref, _dot(act.astype(BF16), wd_s[...]) + bd_ref[0])

    @pl.when(j >= nv_ref[0])
    def _():
        ys_ref[...] = jnp.zeros_like(ys_ref)


def _experts(block_e, n_valid, xs, wgu, wd, bg, bu, bd):
    n_pad = xs.shape[0] // ROW_TILE_SUBLANES
    blk = (MOE_ROWS * ROW_TILE_SUBLANES, LANES)
    de, d = wd.shape[1:]
    n_blocks = n_pad // MOE_ROWS
    xrow = lambda j, be, nv: (jnp.minimum(j, nv[0] - 1), 0)
    wsel = lambda j, be, nv: (be[j], 0, 0)
    return pl.pallas_call(
        _experts_kernel,
        grid_spec=pltpu.PrefetchScalarGridSpec(
            num_scalar_prefetch=2,
            grid=(n_blocks,),
            in_specs=[
                pl.BlockSpec(blk, xrow),
                pl.BlockSpec((1, d, 2 * de), wsel),
                pl.BlockSpec((1, de, d), wsel),
                pl.BlockSpec((1, 1, de), wsel),
                pl.BlockSpec((1, 1, de), wsel),
                pl.BlockSpec((1, 1, d), wsel),
            ],
            out_specs=pl.BlockSpec(blk, lambda j, be, nv: (j, 0)),
            scratch_shapes=[
                pltpu.VMEM((d, de), BF16),
                pltpu.VMEM((d, de), BF16),
                pltpu.VMEM((de, d), BF16),
            ],
        ),
        out_shape=jax.ShapeDtypeStruct(xs.shape, F32),
        compiler_params=pltpu.CompilerParams(dimension_semantics=("arbitrary",),
                                             vmem_limit_bytes=EXPERTS_VMEM_LIMIT),
        name="experts",
    )(block_e, n_valid, xs, wgu, wd, bg, bu, bd)


def _combine_kernel(dest_ref, x1_ref, tw_ref, nw_ref, ys_ref, o_ref, buf_ref, sem, *, final_norm):
    tm = x1_ref.shape[0]
    i = pl.program_id(0)
    n = pl.num_programs(0)

    def copy(step, slot, r, kk):
        d = dest_ref[(step * tm + r) * TOP_K + kk]
        return pltpu.make_async_copy(_row_tile(ys_ref, d), _row_tile(buf_ref, r, slot, kk),
                                     sem.at[slot])

    def issue(step, slot):
        def body(r, carry):
            for kk in range(TOP_K):
                copy(step, slot, r, kk).start()
            return carry
        lax.fori_loop(0, tm, body, 0)

    def drain(step, slot):
        def body(r, carry):
            for kk in range(TOP_K):
                copy(step, slot, r, kk).wait()
            return carry
        lax.fori_loop(0, tm, body, 0)

    slot = i % 2

    @pl.when(i == 0)
    def _():
        issue(i, slot)

    @pl.when(i + 1 < n)
    def _():
        issue(i + 1, 1 - slot)

    drain(i, slot)

    tw = tw_ref[...]
    y = tw[:, 0:1] * _load_row_tiles(buf_ref, tm, slot, 0)
    for kk in range(1, TOP_K):
        y = y + tw[:, kk:kk + 1] * _load_row_tiles(buf_ref, tm, slot, kk)
    out = x1_ref[...] + y
    if final_norm:
        out = _rms(out, nw_ref[...])
    o_ref[...] = out


def _combine(dest_flat, x1, tw, nw, ys, final_norm, tm=128):
    t, d = x1.shape
    kern = functools.partial(_combine_kernel, final_norm=final_norm)
    return pl.pallas_call(
        kern,
        grid_spec=pltpu.PrefetchScalarGridSpec(
            num_scalar_prefetch=1,
            grid=(t // tm,),
            in_specs=[
                pl.BlockSpec((tm, d), lambda i, dest: (i, 0)),
                pl.BlockSpec((tm, LANES), lambda i, dest: (i, 0)),
                pl.BlockSpec((1, d), lambda i, dest: (0, 0)),
                pl.BlockSpec(memory_space=pl.ANY),
            ],
            out_specs=pl.BlockSpec((tm, d), lambda i, dest: (i, 0)),
            scratch_shapes=[pltpu.VMEM((2, TOP_K, tm * ROW_TILE_SUBLANES, LANES), F32),
                            pltpu.SemaphoreType.DMA((2,))],
        ),
        out_shape=jax.ShapeDtypeStruct((t, d), F32),
        compiler_params=_cp(("arbitrary",)),
        name="combine",
    )(dest_flat, x1, tw, nw, ys)


def _rope_tables(seq):
    half = MLA_ROPE // 2
    inv = 1.0 / (ROPE_THETA ** (jnp.arange(0, MLA_ROPE, 2, dtype=F32) / MLA_ROPE))
    ang = jnp.arange(seq, dtype=F32)[:, None] * inv[None, :]
    cos, sin = jnp.cos(ang), jnp.sin(ang)
    zeros = jnp.zeros((seq, LANES - MLA_ROPE), F32)
    cos_t = jnp.concatenate([cos, cos, zeros], axis=-1)
    sin_t = jnp.concatenate([-sin, sin, zeros], axis=-1)
    del half
    return cos_t, sin_t


def _pad_cols(a, width):
    return jnp.pad(a, ((0, 0), (0, width - a.shape[1])))


def _layer(x2, batch, seq, final_norm_w, final_norm, cos_t, sin_t,
           norm_mix_w, w_in, gdn_conv_w, gdn_a_log, gdn_dt_bias, gdn_norm_w, w_gdn_o,
           mla_q_norm_w, w_mla_q_b, mla_kv_norm_w, w_mla_kv_b, w_mla_o, w_out,
           norm_ffn_w, w_router, b_router, w_gate_up, b_gate_up, w_down, b_down):
    t, d = x2.shape
    qk_w = GDN_HEADS * GDN_D
    o_b = 4 * qk_w
    o_a = o_b + GDN_HEADS
    o_cq = o_a + GDN_HEADS
    o_ckv = o_cq + MLA_Q_LORA
    o_kr = o_ckv + MLA_KV_LORA
    o_ga = o_kr + MLA_ROPE
    o_gb = o_ga + d
    w_p = jnp.concatenate([
        w_in[:, 0:o_b], w_in[:, o_ga:o_gb + d], w_in[:, o_cq:o_ckv], w_in[:, o_ckv:o_kr],
        _pad_cols(w_in[:, o_kr:o_ga], 2 * LANES)], axis=1).astype(BF16)
    w_ab = _pad_cols(jnp.concatenate([w_in[:, o_a:o_cq], w_in[:, o_b:o_a]], axis=1), LANES).astype(BF16)

    p, ab = _in_proj(x2, norm_mix_w[None, :], w_p, w_ab)

    alog_row = _pad_cols(gdn_a_log[None, :].astype(F32), LANES)
    dtb_row = _pad_cols(gdn_dt_bias[None, :].astype(F32), LANES)
    qkvn, cols, gct = _gdn_prep(p, ab, gdn_conv_w.astype(F32), alog_row, dtb_row, seq)
    o_gdn = _gdn_chunk(qkvn, p, cols, gct, gdn_norm_w[None, :].astype(F32), batch, seq)

    hd = MLA_NOPE + MLA_ROPE
    wq = w_mla_q_b.reshape(MLA_Q_LORA, MLA_HEADS, hd)
    wq = jnp.pad(wq, ((0, 0), (0, 0), (0, 2 * LANES - hd))).reshape(MLA_Q_LORA, MLA_HEADS * 2 * LANES)
    wkv = w_mla_kv_b.reshape(MLA_KV_LORA, MLA_HEADS, MLA_NOPE + MLA_V)
    wkn = wkv[:, :, :MLA_NOPE].reshape(MLA_KV_LORA, -1)
    wvt = wkv[:, :, MLA_NOPE:].reshape(MLA_KV_LORA, -1).T
    q, kn, kr, vt = _mla_prep(p, cos_t, sin_t, mla_q_norm_w[None, :].astype(F32),
                              mla_kv_norm_w[None, :].astype(F32), wq.astype(BF16), wkn.astype(BF16),
                              wvt.astype(BF16), seq)
    o_mla = _mla_attn(q, kn, kr, vt, batch, seq)

    wr = _pad_cols(w_router.astype(F32), LANES)
    wr_hi = wr.astype(BF16)
    wr_lo = (wr - wr_hi.astype(F32)).astype(BF16)
    br = _pad_cols(b_router[None, :].astype(F32), LANES)
    x1, h2, sel, idx, tw = _mix_out(x2, o_gdn, o_mla, p, w_gdn_o.astype(BF16), w_mla_o.astype(BF16),
                                    w_out.astype(BF16), norm_ffn_w[None, :].astype(F32), wr_hi, wr_lo, br)

    pos, cnt = _route_pos(sel, idx)
    counts = cnt[0, :N_EXPERTS]
    padded = (counts + MOE_ROWS - 1) // MOE_ROWS * MOE_ROWS
    pad_end = jnp.cumsum(padded)
    pad_start = pad_end - padded
    idx4 = idx[:, :TOP_K]
    dest = (pad_start[idx4] + pos[:, :TOP_K]).astype(jnp.int32).reshape(-1)
    n_pad = t * TOP_K + N_EXPERTS * MOE_ROWS
    n_blocks = n_pad // MOE_ROWS
    blk_start = jnp.arange(n_blocks, dtype=jnp.int32) * MOE_ROWS
    block_e = jnp.minimum(jnp.sum((pad_end[None, :] <= blk_start[:, None]).astype(jnp.int32), axis=1),
                          N_EXPERTS - 1).astype(jnp.int32)
    n_valid = (pad_end[-1:] // MOE_ROWS).astype(jnp.int32)

    xs = _dispatch(dest, (pad_start + counts).astype(jnp.int32), pad_end.astype(jnp.int32), h2, n_pad)
    bg = b_gate_up[:, None, 0::2].astype(F32)
    bu = b_gate_up[:, None, 1::2].astype(F32)
    ys = _experts(block_e, n_valid, xs, w_gate_up, w_down, bg, bu, b_down[:, None, :].astype(F32))
    return _combine(dest, x1, tw, final_norm_w[None, :].astype(F32), ys, final_norm)


def kernel(x, norm_mix_w, w_in, gdn_conv_w, gdn_a_log, gdn_dt_bias, gdn_norm_w, w_gdn_o, mla_q_norm_w, w_mla_q_b, mla_kv_norm_w, w_mla_kv_b, w_mla_o, w_out, norm_ffn_w, w_router, b_router, w_gate_up, b_gate_up, w_down, b_down, norm_final_w):
    batch, seq, d = x.shape
    depth = w_in.shape[0]
    cos_t, sin_t = _rope_tables(seq)
    x2 = x.reshape(batch * seq, d)
    for layer in range(depth):
        x2 = _layer(x2, batch, seq, norm_final_w, layer == depth - 1, cos_t, sin_t,
                    norm_mix_w[layer], w_in[layer], gdn_conv_w[layer], gdn_a_log[layer],
                    gdn_dt_bias[layer], gdn_norm_w[layer], w_gdn_o[layer], mla_q_norm_w[layer],
                    w_mla_q_b[layer], mla_kv_norm_w[layer], w_mla_kv_b[layer], w_mla_o[layer],
                    w_out[layer], norm_ffn_w[layer], w_router[layer], b_router[layer],
                    w_gate_up[layer], b_gate_up[layer], w_down[layer], b_down[layer])
    return x2.reshape(batch, seq, d)
```

```python
import functools

import jax
import jax.numpy as jnp
import numpy as np
from jax import lax
from jax.experimental import pallas as pl
from jax.experimental.pallas import tpu as pltpu

F32 = jnp.float32
BF16 = jnp.bfloat16

CHUNK = 64
NORM_EPS = 1e-6
GDN_HEADS = 8
GDN_D = 128
GDN_CONV = 4
MLA_HEADS = 8
MLA_Q_LORA = 512
MLA_KV_LORA = 256
MLA_NOPE = 128
MLA_ROPE = 64
MLA_V = 128
ROPE_THETA = 10000.0
N_EXPERTS = 32
TOP_K = 4
SWIGLU_LIMIT = 7.0
SWIGLU_ALPHA = 1.702

LANES = 128
MOE_ROWS = 512
ROUTE_TILE = 256
SEG_ALIGN = 8
SEG_WINDOW = 64
ZERO_ROWS = 64
VMEM_LIMIT = 48 * 1024 * 1024
EXPERTS_VMEM_LIMIT = 56 * 1024 * 1024

NEG_BIG = -1e30
LOG2_E = 1.4426950408889634


def _cp(sem):
    return pltpu.CompilerParams(dimension_semantics=sem, vmem_limit_bytes=VMEM_LIMIT)


def _dot(a, b):
    return jnp.dot(a, b, preferred_element_type=F32)


def _dot_nt(a, b):
    return lax.dot_general(a, b, (((1,), (1,)), ((), ())), preferred_element_type=F32)


def _dot_tn(a, b):
    return lax.dot_general(a, b, (((0,), (0,)), ((), ())), preferred_element_type=F32)


def _split3(x):
    hi = x.astype(BF16)
    r = x - hi.astype(F32)
    mid = r.astype(BF16)
    lo = (r - mid.astype(F32)).astype(BF16)
    return hi, mid, lo


def _rms(x, w):
    ms = jnp.mean(x * x, axis=-1, keepdims=True)
    return x * lax.rsqrt(ms + NORM_EPS) * w


def _in_proj_kernel(x_ref, nw_ref, w_ref, wab_ref, p_ref, ab_ref, h_ref):
    @pl.when(pl.program_id(1) == 0)
    def _():
        hb = _rms(x_ref[...], nw_ref[...]).astype(BF16)
        h_ref[...] = hb
        ab_ref[...] = _dot(hb, wab_ref[...])

    p_ref[...] = _dot(h_ref[...], w_ref[...]).astype(BF16)


def _in_proj(x2, norm_w, w_p, w_ab, tm=1024, tn=1024):
    t, d = x2.shape
    n = w_p.shape[1]
    return pl.pallas_call(
        _in_proj_kernel,
        grid=(t // tm, n // tn),
        in_specs=[
            pl.BlockSpec((tm, d), lambda i, j: (i, 0)),
            pl.BlockSpec((1, d), lambda i, j: (0, 0)),
            pl.BlockSpec((d, tn), lambda i, j: (0, j)),
            pl.BlockSpec((d, LANES), lambda i, j: (0, 0)),
        ],
        out_specs=[
            pl.BlockSpec((tm, tn), lambda i, j: (i, j)),
            pl.BlockSpec((tm, LANES), lambda i, j: (i, 0)),
        ],
        out_shape=[
            jax.ShapeDtypeStruct((t, n), BF16),
            jax.ShapeDtypeStruct((t, LANES), F32),
        ],
        scratch_shapes=[pltpu.VMEM((tm, d), BF16)],
        compiler_params=_cp(("parallel", "arbitrary")),
        name="in_proj",
    )(x2, norm_w, w_p, w_ab)


def _gdn_prep_kernel(cur_ref, prev_ref, ab_ref, cw_ref, alog_ref, dtb_ref,
                     qkv_ref, cols_ref, gct_ref, *, tiles_per_seq):
    tm = cur_ref.shape[0]
    i = pl.program_id(0)
    halo_on = (i % tiles_per_seq) != 0
    n_blk = cur_ref.shape[1] // LANES
    q_scale = GDN_D ** -0.5
    for cb in range(n_blk):
        cs = slice(cb * LANES, (cb + 1) * LANES)
        cur = cur_ref[:, cs].astype(F32)
        halo = prev_ref[:, cs].astype(F32)[8:16, :]
        halo = jnp.where(halo_on, halo, 0.0)
        xe = jnp.concatenate([halo, cur], axis=0)
        w = cw_ref[:, cs]
        y = w[0:1, :] * xe[5:5 + tm, :]
        for j in range(1, GDN_CONV):
            y = y + w[j:j + 1, :] * xe[5 + j:5 + j + tm, :]
        y = y * jax.nn.sigmoid(y)
        if cb < 2 * GDN_HEADS:
            ss = jnp.sum(y * y, axis=-1, keepdims=True)
            y = y * lax.rsqrt(ss + NORM_EPS)
            if cb < GDN_HEADS:
                y = y * q_scale
        qkv_ref[:, cs] = y.astype(BF16)

    ab = ab_ref[...]
    g = -jnp.exp(alog_ref[...]) * jax.nn.softplus(ab + dtb_ref[...])
    row = lax.broadcasted_iota(jnp.int32, (tm, tm), 0)
    col = lax.broadcasted_iota(jnp.int32, (tm, tm), 1)
    tri = ((col <= row) & ((row // CHUNK) == (col // CHUNK))).astype(BF16)
    g_hi, g_mid, g_lo = _split3(g)
    gc = _dot(tri, g_hi) + _dot(tri, g_mid) + _dot(tri, g_lo)
    lane = lax.broadcasted_iota(jnp.int32, (tm, LANES), 1)
    cols_ref[...] = jnp.where(lane < GDN_HEADS, gc, jax.nn.sigmoid(ab))
    for c in range(tm // CHUNK):
        blk = gc[c * CHUNK:(c + 1) * CHUNK, :]
        blk = jnp.concatenate([blk, jnp.zeros_like(blk)], axis=0)
        gct_ref[c] = blk.T[0:GDN_HEADS, 0:CHUNK]


def _gdn_prep(p, ab, conv_w, alog_row, dtb_row, seq, tm=256):
    t = p.shape[0]
    cw = 3 * GDN_HEADS * GDN_D
    tiles_per_seq = seq // tm
    kern = functools.partial(_gdn_prep_kernel, tiles_per_seq=tiles_per_seq)
    return pl.pallas_call(
        kern,
        grid=(t // tm,),
        in_specs=[
            pl.BlockSpec((tm, cw), lambda i: (i, 0)),
            pl.BlockSpec((16, cw), lambda i: (jnp.maximum(i * (tm // 16) - 1, 0), 0)),
            pl.BlockSpec((tm, LANES), lambda i: (i, 0)),
            pl.BlockSpec((GDN_CONV, cw), lambda i: (0, 0)),
            pl.BlockSpec((1, LANES), lambda i: (0, 0)),
            pl.BlockSpec((1, LANES), lambda i: (0, 0)),
        ],
        out_specs=[
            pl.BlockSpec((tm, cw), lambda i: (i, 0)),
            pl.BlockSpec((tm, LANES), lambda i: (i, 0)),
            pl.BlockSpec((tm // CHUNK, GDN_HEADS, CHUNK), lambda i: (i, 0, 0)),
        ],
        out_shape=[
            jax.ShapeDtypeStruct((t, cw), BF16),
            jax.ShapeDtypeStruct((t, LANES), F32),
            jax.ShapeDtypeStruct((t // CHUNK, GDN_HEADS, CHUNK), F32),
        ],
        compiler_params=_cp(("parallel",)),
        name="gdn_prep",
    )(p, p, ab, conv_w, alog_row, dtb_row)


def _gdn_chunk_kernel(q_ref, k_ref, v_ref, z_ref, cols_ref, gct_ref, nw_ref, o_ref, s_ref):
    c = CHUNK
    nb = q_ref.shape[0]
    units = [(b, h) for b in range(nb) for h in range(GDN_HEADS)]

    @pl.when(pl.program_id(0) == 0)
    def _():
        s_ref[...] = jnp.zeros_like(s_ref)

    ri = lax.broadcasted_iota(jnp.int32, (c, c), 0)
    ci = lax.broadcasted_iota(jnp.int32, (c, c), 1)
    incl = ri >= ci
    strict = ri > ci
    eye = (ri == ci).astype(F32)
    nw = nw_ref[...]

    cols, e_g, e_kd, e_last, gct = [], [], [], [], []
    for b in range(nb):
        cb = cols_ref[b]
        last = cb[c - 1:c, :]
        cols.append(cb)
        e_g.append(jnp.exp(cb))
        e_kd.append(jnp.exp(last - cb))
        e_last.append(jnp.exp(last))
        gct.append(gct_ref[b, 0])

    kq, kb_l, kf_l = [], [], []
    for b, h in units:
        hs = slice(h * GDN_D, (h + 1) * GDN_D)
        k = k_ref[b, :, hs]
        kf = k.astype(F32)
        kb = kf * cols[b][:, GDN_HEADS + h:GDN_HEADS + h + 1]
        kq.append(_dot_nt(jnp.concatenate([kb.astype(BF16), q_ref[b, :, hs]], axis=0), k))
        kb_l.append(kb)
        kf_l.append(kf)

    a_l, qk_l = [], []
    for i, (b, h) in enumerate(units):
        dec = jnp.exp(jnp.minimum(cols[b][:, h:h + 1] - gct[b][h:h + 1, :], 0.0))
        a_l.append(jnp.where(strict, -kq[i][0:c, :] * dec, 0.0))
        qk_l.append(jnp.where(incl, kq[i][c:2 * c, :] * dec, 0.0).astype(BF16))

    tinv = [eye + a for a in a_l]
    pw = a_l
    for _ in range(5):
        pwb = [x.astype(BF16) for x in pw]
        pw = [_dot(x, x) for x in pwb]
        tinv = [t + _dot(t.astype(BF16), x.astype(BF16)) for t, x in zip(tinv, pw)]

    uw = []
    for i, (b, h) in enumerate(units):
        hs = slice(h * GDN_D, (h + 1) * GDN_D)
        beta = cols[b][:, GDN_HEADS + h:GDN_HEADS + h + 1]
        rhs = jnp.concatenate([v_ref[b, :, hs].astype(F32) * beta,
                               kb_l[i] * e_g[b][:, h:h + 1]], axis=1).astype(BF16)
        uw.append(_dot(tinv[i].astype(BF16), rhs))

    r_l = []
    for i, (b, h) in enumerate(units):
        hs = slice(h * GDN_D, (h + 1) * GDN_D)
        qd = (q_ref[b, :, hs].astype(F32) * e_g[b][:, h:h + 1]).astype(BF16)
        lhs = jnp.concatenate([uw[i][:, GDN_D:2 * GDN_D].astype(BF16), qd], axis=0)
        r_l.append(_dot(lhs, s_ref[b * GDN_HEADS + h].astype(BF16)))

    for i, (b, h) in enumerate(units):
        hs = slice(h * GDN_D, (h + 1) * GDN_D)
        v_new = (uw[i][:, 0:GDN_D] - r_l[i][0:c, :]).astype(BF16)
        o = r_l[i][c:2 * c, :] + _dot(qk_l[i], v_new)
        kd = (kf_l[i] * e_kd[b][:, h:h + 1]).astype(BF16)
        u = b * GDN_HEADS + h
        s_ref[u] = s_ref[u] * e_last[b][:, h:h + 1] + _dot_tn(kd, v_new)
        z = z_ref[b, :, hs].astype(F32)
        o_ref[b, :, hs] = (_rms(o, nw) * (z * jax.nn.sigmoid(z))).astype(BF16)


def _gdn_chunk(qkvn, p, cols, gct, norm_w, batch, seq):
    nc = seq // CHUNK
    hw = GDN_HEADS * GDN_D
    qkvn3 = qkvn.reshape(batch, seq, qkvn.shape[1])
    p3 = p.reshape(batch, seq, p.shape[1])
    cols3 = cols.reshape(batch, seq, LANES)
    gct4 = gct.reshape(batch, nc, GDN_HEADS, CHUNK)
    tile = lambda col: pl.BlockSpec((batch, CHUNK, hw), lambda c: (0, c, col))
    out = pl.pallas_call(
        _gdn_chunk_kernel,
        grid=(nc,),
        in_specs=[
            tile(0), tile(1), tile(2),
            tile(3),
            pl.BlockSpec((batch, CHUNK, LANES), lambda c: (0, c, 0)),
            pl.BlockSpec((batch, 1, GDN_HEADS, CHUNK), lambda c: (0, c, 0, 0)),
            pl.BlockSpec((1, GDN_D), lambda c: (0, 0)),
        ],
        out_specs=pl.BlockSpec((batch, CHUNK, hw), lambda c: (0, c, 0)),
        out_shape=jax.ShapeDtypeStruct((batch, seq, hw), BF16),
        scratch_shapes=[pltpu.VMEM((batch * GDN_HEADS, GDN_D, GDN_D), F32)],
        compiler_params=_cp(("arbitrary",)),
        name="gdn_chunk",
    )(qkvn3, qkvn3, qkvn3, p3, cols3, gct4, norm_w)
    return out.reshape(batch * seq, hw)


def _rope(x, cos, sin_signed):
    lane = lax.broadcasted_iota(jnp.int32, x.shape, 1)
    fwd = pltpu.roll(x, LANES - MLA_ROPE // 2, 1)
    bwd = pltpu.roll(x, MLA_ROPE // 2, 1)
    rot = jnp.where(lane < MLA_ROPE // 2, fwd, bwd)
    return x * cos + rot * sin_signed


def _mla_prep_kernel(cq_ref, ckv_ref, kr_ref, cos_ref, sin_ref, qnw_ref, kvnw_ref, wq_ref, wkn_ref,
                     wvt_ref, q_ref, kn_ref, kro_ref, vt_ref):
    cos = cos_ref[...]
    sin = sin_ref[...]
    cq = _rms(cq_ref[...].astype(F32), qnw_ref[...]).astype(BF16)
    hd = 2 * LANES
    scale = (MLA_NOPE + MLA_ROPE) ** -0.5 * LOG2_E
    for h in range(MLA_HEADS):
        qh = _dot(cq, wq_ref[:, h * hd:(h + 1) * hd]) * scale
        q_ref[:, h * hd:h * hd + LANES] = qh[:, 0:LANES].astype(BF16)
        q_ref[:, h * hd + LANES:(h + 1) * hd] = _rope(qh[:, LANES:hd], cos, sin).astype(BF16)
    kvl = _rms(ckv_ref[...].astype(F32), kvnw_ref[...]).astype(BF16)
    kn_ref[...] = _dot(kvl, wkn_ref[...]).astype(BF16)
    vt_ref[...] = _dot_nt(wvt_ref[...], kvl).astype(BF16)
    kro_ref[...] = _rope(kr_ref[...].astype(F32), cos, sin).astype(BF16)


def _mla_prep(p, cos_t, sin_t, qnw, kvnw, wq, wkn, wvt, seq, tm=512):
    t = p.shape[0]
    tiles_per_seq = seq // tm
    hw = MLA_HEADS * MLA_NOPE
    cq_blk = 6144 // MLA_Q_LORA
    ckv_blk = 6656 // MLA_KV_LORA
    kr_blk = 6912 // LANES
    return pl.pallas_call(
        _mla_prep_kernel,
        grid=(t // tm,),
        in_specs=[
            pl.BlockSpec((tm, MLA_Q_LORA), lambda i: (i, cq_blk)),
            pl.BlockSpec((tm, MLA_KV_LORA), lambda i: (i, ckv_blk)),
            pl.BlockSpec((tm, LANES), lambda i: (i, kr_blk)),
            pl.BlockSpec((tm, LANES), lambda i: (i % tiles_per_seq, 0)),
            pl.BlockSpec((tm, LANES), lambda i: (i % tiles_per_seq, 0)),
            pl.BlockSpec((1, MLA_Q_LORA), lambda i: (0, 0)),
            pl.BlockSpec((1, MLA_KV_LORA), lambda i: (0, 0)),
            pl.BlockSpec((MLA_Q_LORA, 2 * hw), lambda i: (0, 0)),
            pl.BlockSpec((MLA_KV_LORA, hw), lambda i: (0, 0)),
            pl.BlockSpec((hw, MLA_KV_LORA), lambda i: (0, 0)),
        ],
        out_specs=[
            pl.BlockSpec((tm, 2 * hw), lambda i: (i, 0)),
            pl.BlockSpec((tm, hw), lambda i: (i, 0)),
            pl.BlockSpec((tm, LANES), lambda i: (i, 0)),
            pl.BlockSpec((hw, tm), lambda i: (0, i)),
        ],
        out_shape=[
            jax.ShapeDtypeStruct((t, 2 * hw), BF16),
            jax.ShapeDtypeStruct((t, hw), BF16),
            jax.ShapeDtypeStruct((t, LANES), BF16),
            jax.ShapeDtypeStruct((hw, t), BF16),
        ],
        compiler_params=_cp(("parallel",)),
        name="mla_prep",
    )(p, p, p, cos_t, sin_t, qnw, kvnw, wq, wkn, wvt)


ATTN_HEADS_PER_STEP = 4
ATTN_SUM_ROWS = 16


def _mla_attn_kernel(qt_ref, kt_ref, q_ref, kn_ref, kr_ref, vt_ref, o_ref, m_ref, acc_ref):
    qi = qt_ref[pl.program_id(2)]
    ki = kt_ref[pl.program_id(2)]
    tq = q_ref.shape[0]
    tk = kn_ref.shape[0]
    hd = 2 * LANES

    @pl.when(ki == 0)
    def _():
        m_ref[...] = jnp.full_like(m_ref, NEG_BIG)
        acc_ref[...] = jnp.zeros_like(acc_ref)

    def step(masked):
        kr = kr_ref[...]
        ones = jnp.ones((ATTN_SUM_ROWS, tk), BF16)

        def scores(h):
            k = jnp.concatenate([kn_ref[:, h * MLA_NOPE:(h + 1) * MLA_NOPE], kr], axis=1)
            s = _dot_nt(k, q_ref[:, h * hd:(h + 1) * hd])
            if masked:
                ck = lax.broadcasted_iota(jnp.int32, (tk, tq), 0) // CHUNK
                cq = lax.broadcasted_iota(jnp.int32, (tk, tq), 1) // CHUNK
                s = jnp.where(ck <= cq, s, NEG_BIG)
            return s

        def update(h, s):
            m_prev = m_ref[h]
            m_new = jnp.maximum(m_prev, jnp.max(s, axis=0, keepdims=True))
            alpha = jnp.exp2(m_prev - m_new)
            p = jnp.exp2((s - m_new).astype(BF16))
            v_ext = jnp.concatenate([vt_ref[h * MLA_V:(h + 1) * MLA_V, :], ones], axis=0)
            acc_ref[h] = alpha * acc_ref[h] + _dot(v_ext, p)
            m_ref[h] = m_new

        s_prev = scores(0)
        for h in range(1, ATTN_HEADS_PER_STEP):
            s_next = scores(h)
            update(h - 1, s_prev)
            s_prev = s_next
        update(ATTN_HEADS_PER_STEP - 1, s_prev)

    @pl.when(ki < qi)
    def _():
        step(False)

    @pl.when(ki == qi)
    def _():
        step(True)
        for h in range(ATTN_HEADS_PER_STEP):
            acc = acc_ref[h]
            o = acc[0:MLA_V, :] / acc[MLA_V:MLA_V + 1, :]
            o_ref[:, h * MLA_V:(h + 1) * MLA_V] = o.T.astype(BF16)


def _mla_attn(q, kn, kr, vt, batch, seq, tq=512):
    t = q.shape[0]
    nq = seq // tq
    hps = ATTN_HEADS_PER_STEP
    pairs = [(qi, ki) for qi in range(nq) for ki in range(qi + 1)]
    qt = jnp.asarray(np.array([pr[0] for pr in pairs], np.int32))
    kt = jnp.asarray(np.array([pr[1] for pr in pairs], np.int32))
    return pl.pallas_call(
        _mla_attn_kernel,
        grid_spec=pltpu.PrefetchScalarGridSpec(
            num_scalar_prefetch=2,
            grid=(batch, MLA_HEADS // hps, len(pairs)),
            in_specs=[
                pl.BlockSpec((tq, hps * 2 * LANES), lambda b, h, pr, qt, kt: (b * nq + qt[pr], h)),
                pl.BlockSpec((tq, hps * MLA_NOPE), lambda b, h, pr, qt, kt: (b * nq + kt[pr], h)),
                pl.BlockSpec((tq, LANES), lambda b, h, pr, qt, kt: (b * nq + kt[pr], 0)),
                pl.BlockSpec((hps * MLA_V, tq), lambda b, h, pr, qt, kt: (h, b * nq + kt[pr])),
            ],
            out_specs=pl.BlockSpec((tq, hps * MLA_V), lambda b, h, pr, qt, kt: (b * nq + qt[pr], h)),
            scratch_shapes=[
                pltpu.VMEM((hps, 1, tq), F32),
                pltpu.VMEM((hps, MLA_V + ATTN_SUM_ROWS, tq), F32),
            ],
        ),
        out_shape=jax.ShapeDtypeStruct((t, MLA_HEADS * MLA_V), BF16),
        compiler_params=_cp(("parallel", "parallel", "arbitrary")),
        name="mla_attn",
    )(qt, kt, q, kn, kr, vt)


def _mix_out_kernel(x_ref, oa_ref, ob_ref, ga_ref, gb_ref, wga_ref, wmo_ref, wout_ref, nw_ref,
                    wr_hi_ref, wr_lo_ref, br_ref, x1_ref, h2_ref, sel_ref, cw_ref):
    ya = _dot(oa_ref[...], wga_ref[...])
    yb = _dot(ob_ref[...], wmo_ref[...])
    merged = (jax.nn.sigmoid(ga_ref[...].astype(F32)) * ya
              + jax.nn.sigmoid(gb_ref[...].astype(F32)) * yb)
    x1 = x_ref[...] + _dot(merged.astype(BF16), wout_ref[...])
    x1_ref[...] = x1
    h2 = _rms(x1, nw_ref[...])
    h_hi = h2.astype(BF16)
    h2_ref[...] = h_hi

    h_lo = (h2 - h_hi.astype(F32)).astype(BF16)
    logits = (_dot(h_hi, wr_hi_ref[...]) + _dot(h_hi, wr_lo_ref[...]) + _dot(h_lo, wr_hi_ref[...])
              + br_ref[...])
    lane = lax.broadcasted_iota(jnp.int32, logits.shape, 1)
    work = jnp.where(lane < N_EXPERTS, logits, -jnp.inf)
    sel = jnp.zeros(logits.shape, F32)
    cw = jnp.zeros(logits.shape, F32)
    top = None
    denom = None
    for kk in range(TOP_K):
        mx = jnp.max(work, axis=-1, keepdims=True)
        am = jnp.min(jnp.where(work == mx, lane, LANES), axis=-1, keepdims=True)
        hit = lane == am
        if kk == 0:
            top = mx
            e = jnp.ones_like(mx)
            denom = e
        else:
            e = jnp.exp(mx - top)
            denom = denom + e
        sel = jnp.where(hit, 1.0, sel)
        cw = jnp.where(hit, e, cw)
        work = jnp.where(hit, -jnp.inf, work)
    sel_ref[...] = sel.astype(BF16)
    cw_ref[...] = cw / denom


def _mix_out(x2, oa, ob, p, wga, wmo, wout, nw, wr_hi, wr_lo, br, tm=256):
    t, d = x2.shape
    full = lambda i: (0, 0)
    return pl.pallas_call(
        _mix_out_kernel,
        grid=(t // tm,),
        in_specs=[
            pl.BlockSpec((tm, d), lambda i: (i, 0)),
            pl.BlockSpec((tm, d), lambda i: (i, 0)),
            pl.BlockSpec((tm, d), lambda i: (i, 0)),
            pl.BlockSpec((tm, d), lambda i: (i, 4)),
            pl.BlockSpec((tm, d), lambda i: (i, 5)),
            pl.BlockSpec((d, d), full),
            pl.BlockSpec((d, d), full),
            pl.BlockSpec((d, d), full),
            pl.BlockSpec((1, d), full),
            pl.BlockSpec((d, LANES), full),
            pl.BlockSpec((d, LANES), full),
            pl.BlockSpec((1, LANES), full),
        ],
        out_specs=[
            pl.BlockSpec((tm, d), lambda i: (i, 0)),
            pl.BlockSpec((tm, d), lambda i: (i, 0)),
            pl.BlockSpec((tm, LANES), lambda i: (i, 0)),
            pl.BlockSpec((tm, LANES), lambda i: (i, 0)),
        ],
        out_shape=[
            jax.ShapeDtypeStruct((t, d), F32),
            jax.ShapeDtypeStruct((t, d), BF16),
            jax.ShapeDtypeStruct((t, LANES), BF16),
            jax.ShapeDtypeStruct((t, LANES), F32),
        ],
        compiler_params=_cp(("parallel",)),
        name="mix_out",
    )(x2, oa, ob, p, p, wga, wmo, wout, nw, wr_hi, wr_lo, br)


def _route_pos_kernel(sel_ref, lpos_ref, keyt_ref, offs_ref, cnt_ref, tot_ref, carry_ref):
    tm = sel_ref.shape[0]
    i = pl.program_id(0)

    @pl.when(i == 0)
    def _():
        carry_ref[...] = jnp.zeros_like(carry_ref)

    sel = sel_ref[...]
    row = lax.broadcasted_iota(jnp.int32, (tm, tm), 0)
    col = lax.broadcasted_iota(jnp.int32, (tm, tm), 1)
    before = (col < row).astype(BF16)
    lpos_ref[...] = _dot(before, sel)
    pos_t = _dot_tn(sel, (row < col).astype(BF16))
    sel_t = _dot_tn(sel, (row == col).astype(BF16))
    keyt_ref[0] = jnp.where(sel_t > 0.5, pos_t, -1.0)[0:N_EXPERTS, :]

    n = jnp.sum(sel.astype(F32), axis=0, keepdims=True)
    carry = carry_ref[0:1, :]
    offs_ref[0] = carry.astype(jnp.int32)
    cnt_ref[0] = n.astype(jnp.int32)
    total = carry + jnp.ceil(n * (1.0 / SEG_ALIGN)) * SEG_ALIGN
    carry_ref[...] = jnp.broadcast_to(total, carry_ref.shape)
    tot_ref[...] = jnp.broadcast_to(total, tot_ref.shape).astype(jnp.int32)


def _route_pos(sel):
    t = sel.shape[0]
    tm = ROUTE_TILE
    nt = t // tm
    return pl.pallas_call(
        _route_pos_kernel,
        grid=(nt,),
        in_specs=[pl.BlockSpec((tm, LANES), lambda i: (i, 0))],
        out_specs=[
            pl.BlockSpec((tm, LANES), lambda i: (i, 0)),
            pl.BlockSpec((1, N_EXPERTS, tm), lambda i: (i, 0, 0)),
            pl.BlockSpec((1, 1, LANES), lambda i: (i, 0, 0)),
            pl.BlockSpec((1, 1, LANES), lambda i: (i, 0, 0)),
            pl.BlockSpec((8, LANES), lambda i: (0, 0)),
        ],
        out_shape=[
            jax.ShapeDtypeStruct((t, LANES), F32),
            jax.ShapeDtypeStruct((nt, N_EXPERTS, tm), F32),
            jax.ShapeDtypeStruct((nt, 1, LANES), jnp.int32),
            jax.ShapeDtypeStruct((nt, 1, LANES), jnp.int32),
            jax.ShapeDtypeStruct((8, LANES), jnp.int32),
        ],
        scratch_shapes=[pltpu.VMEM((8, LANES), F32)],
        compiler_params=_cp(("arbitrary",)),
        name="route_pos",
    )(sel)


def _rows(ref, start, n):
    return ref.at[pl.ds(pl.multiple_of(start, n), n)]


def _seg_windows(cnt_ref, base):
    longest = lax.fori_loop(0, N_EXPERTS, lambda e, m: jnp.maximum(m, cnt_ref[base + e]), 0)
    return (longest + SEG_WINDOW - 1) // SEG_WINDOW


def _seg_copies(cnt_ref, seg_ref, base, win, e_lo, e_hi, make_copy, wait):
    def per_expert(e, carry):
        rows = jnp.clip(cnt_ref[base + e] - win * SEG_WINDOW, 0, SEG_WINDOW)
        n_chunks = (rows + SEG_ALIGN - 1) // SEG_ALIGN
        slot0 = seg_ref[base + e] + win * SEG_WINDOW
        stage0 = e * SEG_WINDOW

        def per_chunk(c, carry2):
            cp = make_copy(stage0 + c * SEG_ALIGN, slot0 + c * SEG_ALIGN)
            if wait:
                cp.wait()
            else:
                cp.start()
            return carry2

        return lax.fori_loop(0, n_chunks, per_chunk, carry)

    lax.fori_loop(e_lo, e_hi, per_expert, 0)


def _dispatch_kernel(seg_ref, cnt_ref, fill_lo_ref, fill_hi_ref, h_ref, keyt_ref, xs_ref,
                     stage_ref, zero_ref, sem):
    i = pl.program_id(0)
    tm = h_ref.shape[0]
    base = i * N_EXPERTS

    @pl.when(i == 0)
    def _():
        zero_ref[...] = jnp.zeros_like(zero_ref)

        def fill(c):
            return pltpu.make_async_copy(zero_ref.at[pl.ds(0, SEG_ALIGN)], _rows(xs_ref, c * SEG_ALIGN, SEG_ALIGN), sem)

        def per_expert(e, carry):
            lo, hi = fill_lo_ref[e] // SEG_ALIGN, fill_hi_ref[e] // SEG_ALIGN
            lax.fori_loop(lo, hi, lambda c, a: (fill(c).start(), a)[1], 0)
            lax.fori_loop(lo, hi, lambda c, a: (fill(c).wait(), a)[1], 0)
            return carry

        lax.fori_loop(0, N_EXPERTS, per_expert, 0)

        def fill_tail(c):
            return pltpu.make_async_copy(zero_ref, _rows(xs_ref, c * ZERO_ROWS, ZERO_ROWS), sem)

        lo = fill_hi_ref[N_EXPERTS - 1] // ZERO_ROWS
        hi = xs_ref.shape[0] // ZERO_ROWS
        lax.fori_loop(lo, hi, lambda c, a: (fill_tail(c).start(), a)[1], 0)
        lax.fori_loop(lo, hi, lambda c, a: (fill_tail(c).wait(), a)[1], 0)

    def make_copy(stage_row, slot):
        return pltpu.make_async_copy(_rows(stage_ref, stage_row, SEG_ALIGN), _rows(xs_ref, slot, SEG_ALIGN), sem)

    half = N_EXPERTS // 2
    j = lax.broadcasted_iota(jnp.int32, (SEG_WINDOW, tm), 0).astype(F32)

    def window(win, carry):
        key = keyt_ref[0] - (win * SEG_WINDOW).astype(F32)
        for hf in range(2):
            pick = jnp.concatenate([(key[e:e + 1, :] == j).astype(BF16)
                                    for e in range(hf * half, (hf + 1) * half)], axis=0)
            stage_ref[hf * half * SEG_WINDOW:(hf + 1) * half * SEG_WINDOW, :] = _dot(pick, h_ref[...])
            _seg_copies(cnt_ref, seg_ref, base, win, hf * half, (hf + 1) * half, make_copy, wait=False)
        _seg_copies(cnt_ref, seg_ref, base, win, 0, N_EXPERTS, make_copy, wait=True)
        return carry

    lax.fori_loop(0, _seg_windows(cnt_ref, base), window, 0)


def _dispatch(seg, cnt, fill_lo, fill_hi, h2, keyt, n_pad):
    t, d = h2.shape
    tm = ROUTE_TILE
    return pl.pallas_call(
        _dispatch_kernel,
        grid_spec=pltpu.PrefetchScalarGridSpec(
            num_scalar_prefetch=4,
            grid=(t // tm,),
            in_specs=[
                pl.BlockSpec((tm, d), lambda i, *_: (i, 0)),
                pl.BlockSpec((1, N_EXPERTS, tm), lambda i, *_: (i, 0, 0)),
            ],
            out_specs=pl.BlockSpec(memory_space=pl.ANY),
            scratch_shapes=[pltpu.VMEM((N_EXPERTS * SEG_WINDOW, d), F32),
                            pltpu.VMEM((ZERO_ROWS, d), F32),
                            pltpu.SemaphoreType.DMA(())],
        ),
        out_shape=jax.ShapeDtypeStruct((n_pad, d), F32),
        compiler_params=_cp(("arbitrary",)),
        name="dispatch",
    )(seg, cnt, fill_lo, fill_hi, h2, keyt)


def _experts_kernel(be_ref, nv_ref, xs_ref, wgu_ref, wd_ref, bg_ref, bu_ref, bd_ref, ys_ref,
                    wg_s, wu_s, wd_s):
    j = pl.program_id(0)
    grp = 2 * LANES
    prev = be_ref[jnp.maximum(j - 1, 0)]

    @pl.when((j == 0) | (be_ref[j] != prev))
    def _():
        r = lax.broadcasted_iota(jnp.int32, (grp, grp), 0)
        c = lax.broadcasted_iota(jnp.int32, (grp, grp), 1)
        src = jnp.where(c < LANES, 2 * c, 2 * (c - LANES) + 1)
        pick = (r == src).astype(BF16)
        for g in range(wgu_ref.shape[2] // grp):
            y = _dot(wgu_ref[0, :, g * grp:(g + 1) * grp].astype(BF16), pick)
            wg_s[:, g * LANES:(g + 1) * LANES] = y[:, 0:LANES].astype(BF16)
            wu_s[:, g * LANES:(g + 1) * LANES] = y[:, LANES:grp].astype(BF16)
        wd_s[...] = wd_ref[0].astype(BF16)

    @pl.when(j < nv_ref[0])
    def _():
        x = xs_ref[...].astype(BF16)
        g = _dot(x, wg_s[...]) + bg_ref[0]
        u = _dot(x, wu_s[...]) + bu_ref[0]
        gate = jnp.minimum(g, SWIGLU_LIMIT)
        up = jnp.clip(u, -SWIGLU_LIMIT, SWIGLU_LIMIT)
        act = (up + 1.0) * (gate * jax.nn.sigmoid(gate * SWIGLU_ALPHA))
        ys_ref[...] = _dot(act.astype(BF16), wd_s[...]) + bd_ref[0]

    @pl.when(j >= nv_ref[0])
    def _():
        ys_ref[...] = jnp.zeros_like(ys_ref)


def _experts(block_e, n_valid, xs, wgu, wd, bg, bu, bd):
    n_pad = xs.shape[0]
    de, d = wd.shape[1:]
    blk = (MOE_ROWS, d)
    n_blocks = n_pad // MOE_ROWS
    xrow = lambda j, be, nv: (jnp.minimum(j, nv[0] - 1), 0)
    wsel = lambda j, be, nv: (be[j], 0, 0)
    return pl.pallas_call(
        _experts_kernel,
        grid_spec=pltpu.PrefetchScalarGridSpec(
            num_scalar_prefetch=2,
            grid=(n_blocks,),
            in_specs=[
                pl.BlockSpec(blk, xrow),
                pl.BlockSpec((1, d, 2 * de), wsel),
                pl.BlockSpec((1, de, d), wsel),
                pl.BlockSpec((1, 1, de), wsel),
                pl.BlockSpec((1, 1, de), wsel),
                pl.BlockSpec((1, 1, d), wsel),
            ],
            out_specs=pl.BlockSpec(blk, lambda j, be, nv: (j, 0)),
            scratch_shapes=[
                pltpu.VMEM((d, de), BF16),
                pltpu.VMEM((d, de), BF16),
                pltpu.VMEM((de, d), BF16),
            ],
        ),
        out_shape=jax.ShapeDtypeStruct(xs.shape, F32),
        compiler_params=pltpu.CompilerParams(dimension_semantics=("arbitrary",),
                                             vmem_limit_bytes=EXPERTS_VMEM_LIMIT),
        name="experts",
    )(block_e, n_valid, xs, wgu, wd, bg, bu, bd)


def _combine_kernel(seg_ref, cnt_ref, x1_ref, lpos_ref, cw_ref, nw_ref, ys_ref, o_ref, stage_ref, sem,
                    *, final_norm):
    i = pl.program_id(0)
    tm, d = x1_ref.shape
    base = i * N_EXPERTS

    @pl.when(i == 0)
    def _():
        stage_ref[...] = jnp.zeros_like(stage_ref)

    def make_copy(stage_row, slot):
        return pltpu.make_async_copy(_rows(ys_ref, slot, SEG_ALIGN), _rows(stage_ref, stage_row, SEG_ALIGN), sem)

    lane = lax.broadcasted_iota(jnp.int32, (tm, LANES), 1)
    per_vreg = LANES // SEG_WINDOW
    assert per_vreg == 2

    def window(win, y):
        _seg_copies(cnt_ref, seg_ref, base, win, 0, N_EXPERTS, make_copy, wait=False)
        rank = lpos_ref[...] - (win * SEG_WINDOW).astype(F32)
        cw = cw_ref[...]
        j = jnp.where(lane < SEG_WINDOW, lane, lane - SEG_WINDOW).astype(F32)
        cols = []
        for e in range(0, N_EXPERTS, per_vreg):
            r = jnp.where(lane < SEG_WINDOW, rank[:, e:e + 1], rank[:, e + 1:e + 2])
            w = jnp.where(lane < SEG_WINDOW, cw[:, e:e + 1], cw[:, e + 1:e + 2])
            cols.append(jnp.where(r == j, w, 0.0).astype(BF16))
        take = jnp.concatenate(cols, axis=1)
        _seg_copies(cnt_ref, seg_ref, base, win, 0, N_EXPERTS, make_copy, wait=True)
        return y + _dot(take, stage_ref[...].astype(BF16))

    y = lax.fori_loop(0, _seg_windows(cnt_ref, base), window, jnp.zeros((tm, d), F32))
    out = x1_ref[...] + y
    if final_norm:
        out = _rms(out, nw_ref[...])
    o_ref[...] = out


def _combine(seg, cnt, x1, lpos, cw, nw, ys, final_norm):
    t, d = x1.shape
    tm = ROUTE_TILE
    kern = functools.partial(_combine_kernel, final_norm=final_norm)
    return pl.pallas_call(
        kern,
        grid_spec=pltpu.PrefetchScalarGridSpec(
            num_scalar_prefetch=2,
            grid=(t // tm,),
            in_specs=[
                pl.BlockSpec((tm, d), lambda i, *_: (i, 0)),
                pl.BlockSpec((tm, LANES), lambda i, *_: (i, 0)),
                pl.BlockSpec((tm, LANES), lambda i, *_: (i, 0)),
                pl.BlockSpec((1, d), lambda i, *_: (0, 0)),
                pl.BlockSpec(memory_space=pl.ANY),
            ],
            out_specs=pl.BlockSpec((tm, d), lambda i, *_: (i, 0)),
            scratch_shapes=[pltpu.VMEM((N_EXPERTS * SEG_WINDOW, d), F32), pltpu.SemaphoreType.DMA(())],
        ),
        out_shape=jax.ShapeDtypeStruct((t, d), F32),
        compiler_params=_cp(("arbitrary",)),
        name="combine",
    )(seg, cnt, x1, lpos, cw, nw, ys)


def _rope_tables(seq):
    half = MLA_ROPE // 2
    inv = 1.0 / (ROPE_THETA ** (jnp.arange(0, MLA_ROPE, 2, dtype=F32) / MLA_ROPE))
    ang = jnp.arange(seq, dtype=F32)[:, None] * inv[None, :]
    cos, sin = jnp.cos(ang), jnp.sin(ang)
    zeros = jnp.zeros((seq, LANES - MLA_ROPE), F32)
    cos_t = jnp.concatenate([cos, cos, zeros], axis=-1)
    sin_t = jnp.concatenate([-sin, sin, zeros], axis=-1)
    del half
    return cos_t, sin_t


def _pad_cols(a, width):
    return jnp.pad(a, ((0, 0), (0, width - a.shape[1])))


def _layer(x2, batch, seq, final_norm_w, final_norm, cos_t, sin_t,
           norm_mix_w, w_in, gdn_conv_w, gdn_a_log, gdn_dt_bias, gdn_norm_w, w_gdn_o,
           mla_q_norm_w, w_mla_q_b, mla_kv_norm_w, w_mla_kv_b, w_mla_o, w_out,
           norm_ffn_w, w_router, b_router, w_gate_up, b_gate_up, w_down, b_down):
    t, d = x2.shape
    qk_w = GDN_HEADS * GDN_D
    o_b = 4 * qk_w
    o_a = o_b + GDN_HEADS
    o_cq = o_a + GDN_HEADS
    o_ckv = o_cq + MLA_Q_LORA
    o_kr = o_ckv + MLA_KV_LORA
    o_ga = o_kr + MLA_ROPE
    o_gb = o_ga + d
    w_p = jnp.concatenate([
        w_in[:, 0:o_b], w_in[:, o_ga:o_gb + d], w_in[:, o_cq:o_ckv], w_in[:, o_ckv:o_kr],
        _pad_cols(w_in[:, o_kr:o_ga], 2 * LANES)], axis=1).astype(BF16)
    w_ab = _pad_cols(jnp.concatenate([w_in[:, o_a:o_cq], w_in[:, o_b:o_a]], axis=1), LANES).astype(BF16)

    p, ab = _in_proj(x2, norm_mix_w[None, :], w_p, w_ab)

    alog_row = _pad_cols(gdn_a_log[None, :].astype(F32), LANES)
    dtb_row = _pad_cols(gdn_dt_bias[None, :].astype(F32), LANES)
    qkvn, cols, gct = _gdn_prep(p, ab, gdn_conv_w.astype(F32), alog_row, dtb_row, seq)
    o_gdn = _gdn_chunk(qkvn, p, cols, gct, gdn_norm_w[None, :].astype(F32), batch, seq)

    hd = MLA_NOPE + MLA_ROPE
    wq = w_mla_q_b.reshape(MLA_Q_LORA, MLA_HEADS, hd)
    wq = jnp.pad(wq, ((0, 0), (0, 0), (0, 2 * LANES - hd))).reshape(MLA_Q_LORA, MLA_HEADS * 2 * LANES)
    wkv = w_mla_kv_b.reshape(MLA_KV_LORA, MLA_HEADS, MLA_NOPE + MLA_V)
    wkn = wkv[:, :, :MLA_NOPE].reshape(MLA_KV_LORA, -1)
    wvt = wkv[:, :, MLA_NOPE:].reshape(MLA_KV_LORA, -1).T
    q, kn, kr, vt = _mla_prep(p, cos_t, sin_t, mla_q_norm_w[None, :].astype(F32),
                              mla_kv_norm_w[None, :].astype(F32), wq.astype(BF16), wkn.astype(BF16),
                              wvt.astype(BF16), seq)
    o_mla = _mla_attn(q, kn, kr, vt, batch, seq)

    wr = _pad_cols(w_router.astype(F32), LANES)
    wr_hi = wr.astype(BF16)
    wr_lo = (wr - wr_hi.astype(F32)).astype(BF16)
    br = _pad_cols(b_router[None, :].astype(F32), LANES)
    x1, h2, sel, cw = _mix_out(x2, o_gdn, o_mla, p, w_gdn_o.astype(BF16), w_mla_o.astype(BF16),
                               w_out.astype(BF16), norm_ffn_w[None, :].astype(F32), wr_hi, wr_lo, br)

    lpos, keyt, offs, cnt, tot = _route_pos(sel)
    n_tiles = t // ROUTE_TILE
    used = tot[0, :N_EXPERTS]
    padded = (used + MOE_ROWS - 1) // MOE_ROWS * MOE_ROWS
    pad_end = jnp.cumsum(padded)
    pad_start = pad_end - padded
    seg = (pad_start[None, :] + offs[:, 0, :N_EXPERTS]).astype(jnp.int32).reshape(-1)
    cnt = cnt[:, 0, :N_EXPERTS].reshape(-1)
    worst_used = t * TOP_K + n_tiles * N_EXPERTS * (SEG_ALIGN - 1)
    n_pad = -(-worst_used // MOE_ROWS) * MOE_ROWS + N_EXPERTS * MOE_ROWS
    n_blocks = n_pad // MOE_ROWS
    blk_start = jnp.arange(n_blocks, dtype=jnp.int32) * MOE_ROWS
    block_e = jnp.minimum(jnp.sum((pad_end[None, :] <= blk_start[:, None]).astype(jnp.int32), axis=1),
                          N_EXPERTS - 1).astype(jnp.int32)
    n_valid = (pad_end[-1:] // MOE_ROWS).astype(jnp.int32)

    xs = _dispatch(seg, cnt, (pad_start + used).astype(jnp.int32), pad_end.astype(jnp.int32), h2, keyt, n_pad)
    bg = b_gate_up[:, None, 0::2].astype(F32)
    bu = b_gate_up[:, None, 1::2].astype(F32)
    ys = _experts(block_e, n_valid, xs, w_gate_up, w_down, bg, bu, b_down[:, None, :].astype(F32))
    return _combine(seg, cnt, x1, lpos, cw, final_norm_w[None, :].astype(F32), ys, final_norm)


def kernel(x, norm_mix_w, w_in, gdn_conv_w, gdn_a_log, gdn_dt_bias, gdn_norm_w, w_gdn_o, mla_q_norm_w, w_mla_q_b, mla_kv_norm_w, w_mla_kv_b, w_mla_o, w_out, norm_ffn_w, w_router, b_router, w_gate_up, b_gate_up, w_down, b_down, norm_final_w):
    batch, seq, d = x.shape
    depth = w_in.shape[0]
    cos_t, sin_t = _rope_tables(seq)
    x2 = x.reshape(batch * seq, d)
    for layer in range(depth):
        x2 = _layer(x2, batch, seq, norm_final_w, layer == depth - 1, cos_t, sin_t,
                    norm_mix_w[layer], w_in[layer], gdn_conv_w[layer], gdn_a_log[layer],
                    gdn_dt_bias[layer], gdn_norm_w[layer], w_gdn_o[layer], mla_q_norm_w[layer],
                    w_mla_q_b[layer], mla_kv_norm_w[layer], w_mla_kv_b[layer], w_mla_o[layer],
                    w_out[layer], norm_ffn_w[layer], w_router[layer], b_router[layer],
                    w_gate_up[layer], b_gate_up[layer], w_down[layer], b_down[layer])
    return x2.reshape(batch, seq, d)
```

```python
import functools

import jax
import jax.numpy as jnp
import numpy as np
from jax import lax
from jax.experimental import pallas as pl
from jax.experimental.pallas import tpu as pltpu

F32 = jnp.float32
BF16 = jnp.bfloat16

CHUNK = 64
NORM_EPS = 1e-6
GDN_HEADS = 8
GDN_D = 128
GDN_CONV = 4
MLA_HEADS = 8
MLA_Q_LORA = 512
MLA_KV_LORA = 256
MLA_NOPE = 128
MLA_ROPE = 64
MLA_V = 128
ROPE_THETA = 10000.0
N_EXPERTS = 32
TOP_K = 4
SWIGLU_LIMIT = 7.0
SWIGLU_ALPHA = 1.702

LANES = 128
MOE_ROWS = 512
ROUTE_TILE = 256
SEG_ALIGN = 8
SEG_WINDOW = 64
ZERO_ROWS = 64
VMEM_LIMIT = 48 * 1024 * 1024
EXPERTS_VMEM_LIMIT = 56 * 1024 * 1024

NEG_BIG = -1e30
LOG2_E = 1.4426950408889634


def _cp(sem):
    return pltpu.CompilerParams(dimension_semantics=sem, vmem_limit_bytes=VMEM_LIMIT)


def _dot(a, b):
    return jnp.dot(a, b, preferred_element_type=F32)


def _dot_nt(a, b):
    return lax.dot_general(a, b, (((1,), (1,)), ((), ())), preferred_element_type=F32)


def _dot_tn(a, b):
    return lax.dot_general(a, b, (((0,), (0,)), ((), ())), preferred_element_type=F32)


def _split3(x):
    hi = x.astype(BF16)
    r = x - hi.astype(F32)
    mid = r.astype(BF16)
    lo = (r - mid.astype(F32)).astype(BF16)
    return hi, mid, lo


def _rms(x, w):
    ms = jnp.mean(x * x, axis=-1, keepdims=True)
    return x * lax.rsqrt(ms + NORM_EPS) * w


def _in_proj_kernel(x_ref, nw_ref, w_ref, wab_ref, p_ref, ab_ref, h_ref):
    @pl.when(pl.program_id(1) == 0)
    def _():
        hb = _rms(x_ref[...], nw_ref[...]).astype(BF16)
        h_ref[...] = hb
        ab_ref[...] = _dot(hb, wab_ref[...])

    p_ref[...] = _dot(h_ref[...], w_ref[...]).astype(BF16)


def _in_proj(x2, norm_w, w_p, w_ab, tm=1024, tn=3584):
    t, d = x2.shape
    n = w_p.shape[1]
    return pl.pallas_call(
        _in_proj_kernel,
        grid=(t // tm, n // tn),
        in_specs=[
            pl.BlockSpec((tm, d), lambda i, j: (i, 0)),
            pl.BlockSpec((1, d), lambda i, j: (0, 0)),
            pl.BlockSpec((d, tn), lambda i, j: (0, j)),
            pl.BlockSpec((d, LANES), lambda i, j: (0, 0)),
        ],
        out_specs=[
            pl.BlockSpec((tm, tn), lambda i, j: (i, j)),
            pl.BlockSpec((tm, LANES), lambda i, j: (i, 0)),
        ],
        out_shape=[
            jax.ShapeDtypeStruct((t, n), BF16),
            jax.ShapeDtypeStruct((t, LANES), F32),
        ],
        scratch_shapes=[pltpu.VMEM((tm, d), BF16)],
        compiler_params=_cp(("parallel", "arbitrary")),
        name="in_proj",
    )(x2, norm_w, w_p, w_ab)


def _gdn_prep_kernel(cur_ref, prev_ref, ab_ref, cw_ref, alog_ref, dtb_ref,
                     qkv_ref, cols_ref, gct_ref, *, tiles_per_seq):
    tm = cur_ref.shape[0]
    i = pl.program_id(0)
    halo_on = (i % tiles_per_seq) != 0
    n_blk = cur_ref.shape[1] // LANES
    q_scale = GDN_D ** -0.5
    for cb in range(n_blk):
        cs = slice(cb * LANES, (cb + 1) * LANES)
        cur = cur_ref[:, cs].astype(F32)
        halo = prev_ref[:, cs].astype(F32)[8:16, :]
        halo = jnp.where(halo_on, halo, 0.0)
        xe = jnp.concatenate([halo, cur], axis=0)
        w = cw_ref[:, cs]
        y = w[0:1, :] * xe[5:5 + tm, :]
        for j in range(1, GDN_CONV):
            y = y + w[j:j + 1, :] * xe[5 + j:5 + j + tm, :]
        y = y * jax.nn.sigmoid(y)
        if cb < 2 * GDN_HEADS:
            ss = jnp.sum(y * y, axis=-1, keepdims=True)
            y = y * lax.rsqrt(ss + NORM_EPS)
            if cb < GDN_HEADS:
                y = y * q_scale
        qkv_ref[:, cs] = y.astype(BF16)

    ab = ab_ref[...]
    g = -jnp.exp(alog_ref[...]) * jax.nn.softplus(ab + dtb_ref[...])
    row = lax.broadcasted_iota(jnp.int32, (tm, tm), 0)
    col = lax.broadcasted_iota(jnp.int32, (tm, tm), 1)
    tri = ((col <= row) & ((row // CHUNK) == (col // CHUNK))).astype(BF16)
    g_hi, g_mid, g_lo = _split3(g)
    gc = _dot(tri, g_hi) + _dot(tri, g_mid) + _dot(tri, g_lo)
    lane = lax.broadcasted_iota(jnp.int32, (tm, LANES), 1)
    cols_ref[...] = jnp.where(lane < GDN_HEADS, gc, jax.nn.sigmoid(ab))
    for c in range(tm // CHUNK):
        blk = gc[c * CHUNK:(c + 1) * CHUNK, :]
        blk = jnp.concatenate([blk, jnp.zeros_like(blk)], axis=0)
        gct_ref[c] = blk.T[0:GDN_HEADS, 0:CHUNK]


def _gdn_prep(p, ab, conv_w, alog_row, dtb_row, seq, tm=256):
    t = p.shape[0]
    cw = 3 * GDN_HEADS * GDN_D
    tiles_per_seq = seq // tm
    kern = functools.partial(_gdn_prep_kernel, tiles_per_seq=tiles_per_seq)
    return pl.pallas_call(
        kern,
        grid=(t // tm,),
        in_specs=[
            pl.BlockSpec((tm, cw), lambda i: (i, 0)),
            pl.BlockSpec((16, cw), lambda i: (jnp.maximum(i * (tm // 16) - 1, 0), 0)),
            pl.BlockSpec((tm, LANES), lambda i: (i, 0)),
            pl.BlockSpec((GDN_CONV, cw), lambda i: (0, 0)),
            pl.BlockSpec((1, LANES), lambda i: (0, 0)),
            pl.BlockSpec((1, LANES), lambda i: (0, 0)),
        ],
        out_specs=[
            pl.BlockSpec((tm, cw), lambda i: (i, 0)),
            pl.BlockSpec((tm, LANES), lambda i: (i, 0)),
            pl.BlockSpec((tm // CHUNK, GDN_HEADS, CHUNK), lambda i: (i, 0, 0)),
        ],
        out_shape=[
            jax.ShapeDtypeStruct((t, cw), BF16),
            jax.ShapeDtypeStruct((t, LANES), F32),
            jax.ShapeDtypeStruct((t // CHUNK, GDN_HEADS, CHUNK), F32),
        ],
        compiler_params=_cp(("parallel",)),
        name="gdn_prep",
    )(p, p, ab, conv_w, alog_row, dtb_row)


def _gdn_chunk_kernel(q_ref, k_ref, v_ref, z_ref, cols_ref, gct_ref, nw_ref, o_ref, s_ref):
    c = CHUNK
    nb = q_ref.shape[0]
    units = [(b, h) for b in range(nb) for h in range(GDN_HEADS)]

    @pl.when(pl.program_id(0) == 0)
    def _():
        s_ref[...] = jnp.zeros_like(s_ref)

    ri = lax.broadcasted_iota(jnp.int32, (c, c), 0)
    ci = lax.broadcasted_iota(jnp.int32, (c, c), 1)
    incl = ri >= ci
    strict = ri > ci
    eye = (ri == ci).astype(F32)
    nw = nw_ref[...]

    cols, e_g, e_kd, e_last, gct = [], [], [], [], []
    for b in range(nb):
        cb = cols_ref[b]
        last = cb[c - 1:c, :]
        cols.append(cb)
        e_g.append(jnp.exp(cb))
        e_kd.append(jnp.exp(last - cb))
        e_last.append(jnp.exp(last))
        gct.append(gct_ref[b, 0])

    kq, kb_l, kf_l = [], [], []
    for b, h in units:
        hs = slice(h * GDN_D, (h + 1) * GDN_D)
        k = k_ref[b, :, hs]
        kf = k.astype(F32)
        kb = kf * cols[b][:, GDN_HEADS + h:GDN_HEADS + h + 1]
        kq.append(_dot_nt(jnp.concatenate([kb.astype(BF16), q_ref[b, :, hs]], axis=0), k))
        kb_l.append(kb)
        kf_l.append(kf)

    a_l, qk_l = [], []
    for i, (b, h) in enumerate(units):
        dec = jnp.exp(jnp.minimum(cols[b][:, h:h + 1] - gct[b][h:h + 1, :], 0.0))
        a_l.append(jnp.where(strict, -kq[i][0:c, :] * dec, 0.0))
        qk_l.append(jnp.where(incl, kq[i][c:2 * c, :] * dec, 0.0).astype(BF16))

    tinv = [eye + a for a in a_l]
    pw = a_l
    for _ in range(5):
        pwb = [x.astype(BF16) for x in pw]
        pw = [_dot(x, x) for x in pwb]
        tinv = [t + _dot(t.astype(BF16), x.astype(BF16)) for t, x in zip(tinv, pw)]

    uw = []
    for i, (b, h) in enumerate(units):
        hs = slice(h * GDN_D, (h + 1) * GDN_D)
        beta = cols[b][:, GDN_HEADS + h:GDN_HEADS + h + 1]
        rhs = jnp.concatenate([v_ref[b, :, hs].astype(F32) * beta,
                               kb_l[i] * e_g[b][:, h:h + 1]], axis=1).astype(BF16)
        uw.append(_dot(tinv[i].astype(BF16), rhs))

    r_l = []
    for i, (b, h) in enumerate(units):
        hs = slice(h * GDN_D, (h + 1) * GDN_D)
        qd = (q_ref[b, :, hs].astype(F32) * e_g[b][:, h:h + 1]).astype(BF16)
        lhs = jnp.concatenate([uw[i][:, GDN_D:2 * GDN_D].astype(BF16), qd], axis=0)
        r_l.append(_dot(lhs, s_ref[b * GDN_HEADS + h].astype(BF16)))

    for i, (b, h) in enumerate(units):
        hs = slice(h * GDN_D, (h + 1) * GDN_D)
        v_new = (uw[i][:, 0:GDN_D] - r_l[i][0:c, :]).astype(BF16)
        o = r_l[i][c:2 * c, :] + _dot(qk_l[i], v_new)
        kd = (kf_l[i] * e_kd[b][:, h:h + 1]).astype(BF16)
        u = b * GDN_HEADS + h
        s_ref[u] = s_ref[u] * e_last[b][:, h:h + 1] + _dot_tn(kd, v_new)
        z = z_ref[b, :, hs].astype(F32)
        o_ref[b, :, hs] = (_rms(o, nw) * (z * jax.nn.sigmoid(z))).astype(BF16)


def _gdn_chunk(qkvn, p, cols, gct, norm_w, batch, seq):
    nc = seq // CHUNK
    hw = GDN_HEADS * GDN_D
    qkvn3 = qkvn.reshape(batch, seq, qkvn.shape[1])
    p3 = p.reshape(batch, seq, p.shape[1])
    cols3 = cols.reshape(batch, seq, LANES)
    gct4 = gct.reshape(batch, nc, GDN_HEADS, CHUNK)
    tile = lambda col: pl.BlockSpec((batch, CHUNK, hw), lambda c: (0, c, col))
    out = pl.pallas_call(
        _gdn_chunk_kernel,
        grid=(nc,),
        in_specs=[
            tile(0), tile(1), tile(2),
            tile(3),
            pl.BlockSpec((batch, CHUNK, LANES), lambda c: (0, c, 0)),
            pl.BlockSpec((batch, 1, GDN_HEADS, CHUNK), lambda c: (0, c, 0, 0)),
            pl.BlockSpec((1, GDN_D), lambda c: (0, 0)),
        ],
        out_specs=pl.BlockSpec((batch, CHUNK, hw), lambda c: (0, c, 0)),
        out_shape=jax.ShapeDtypeStruct((batch, seq, hw), BF16),
        scratch_shapes=[pltpu.VMEM((batch * GDN_HEADS, GDN_D, GDN_D), F32)],
        compiler_params=_cp(("arbitrary",)),
        name="gdn_chunk",
    )(qkvn3, qkvn3, qkvn3, p3, cols3, gct4, norm_w)
    return out.reshape(batch * seq, hw)


def _rope(x, cos, sin_signed):
    lane = lax.broadcasted_iota(jnp.int32, x.shape, 1)
    fwd = pltpu.roll(x, LANES - MLA_ROPE // 2, 1)
    bwd = pltpu.roll(x, MLA_ROPE // 2, 1)
    rot = jnp.where(lane < MLA_ROPE // 2, fwd, bwd)
    return x * cos + rot * sin_signed


def _mla_prep_kernel(cq_ref, ckv_ref, kr_ref, cos_ref, sin_ref, cost_ref, sint_ref, qnw_ref, kvnw_ref,
                     wqt_ref, wkn_ref, wvt_ref, qt_ref, kn_ref, kro_ref, vt_ref):
    cos = cos_ref[...]
    sin = sin_ref[...]
    cos_t = cost_ref[...]
    sin_t = sint_ref[...]
    cq = _rms(cq_ref[...].astype(F32), qnw_ref[...]).astype(BF16)
    hd = 2 * LANES
    half = MLA_ROPE // 2
    scale = (MLA_NOPE + MLA_ROPE) ** -0.5 * LOG2_E
    for h in range(MLA_HEADS):
        qh = _dot_nt(wqt_ref[h * hd:(h + 1) * hd, :], cq) * scale
        lo = qh[MLA_NOPE:MLA_NOPE + half, :]
        hi = qh[MLA_NOPE + half:MLA_NOPE + MLA_ROPE, :]
        qt_ref[h * hd:h * hd + MLA_NOPE, :] = qh[0:MLA_NOPE, :].astype(BF16)
        qt_ref[h * hd + MLA_NOPE:h * hd + MLA_NOPE + half, :] = (lo * cos_t - hi * sin_t).astype(BF16)
        qt_ref[h * hd + MLA_NOPE + half:h * hd + MLA_NOPE + MLA_ROPE, :] = (hi * cos_t + lo * sin_t).astype(BF16)
        qt_ref[h * hd + MLA_NOPE + MLA_ROPE:(h + 1) * hd, :] = qh[MLA_NOPE + MLA_ROPE:hd, :].astype(BF16)
    kvl = _rms(ckv_ref[...].astype(F32), kvnw_ref[...]).astype(BF16)
    kn_ref[...] = _dot(kvl, wkn_ref[...]).astype(BF16)
    vt_ref[...] = _dot_nt(wvt_ref[...], kvl).astype(BF16)
    kro_ref[...] = _rope(kr_ref[...].astype(F32), cos, sin).astype(BF16)


def _mla_prep(p, tables, qnw, kvnw, wqt, wkn, wvt, seq, tm=512):
    cos_row, sin_row, cos_col, sin_col = tables
    t = p.shape[0]
    tiles_per_seq = seq // tm
    hw = MLA_HEADS * MLA_NOPE
    half = MLA_ROPE // 2
    cq_blk = 6144 // MLA_Q_LORA
    ckv_blk = 6656 // MLA_KV_LORA
    kr_blk = 6912 // LANES
    return pl.pallas_call(
        _mla_prep_kernel,
        grid=(t // tm,),
        in_specs=[
            pl.BlockSpec((tm, MLA_Q_LORA), lambda i: (i, cq_blk)),
            pl.BlockSpec((tm, MLA_KV_LORA), lambda i: (i, ckv_blk)),
            pl.BlockSpec((tm, LANES), lambda i: (i, kr_blk)),
            pl.BlockSpec((tm, LANES), lambda i: (i % tiles_per_seq, 0)),
            pl.BlockSpec((tm, LANES), lambda i: (i % tiles_per_seq, 0)),
            pl.BlockSpec((half, tm), lambda i: (0, i % tiles_per_seq)),
            pl.BlockSpec((half, tm), lambda i: (0, i % tiles_per_seq)),
            pl.BlockSpec((1, MLA_Q_LORA), lambda i: (0, 0)),
            pl.BlockSpec((1, MLA_KV_LORA), lambda i: (0, 0)),
            pl.BlockSpec((2 * hw, MLA_Q_LORA), lambda i: (0, 0)),
            pl.BlockSpec((MLA_KV_LORA, hw), lambda i: (0, 0)),
            pl.BlockSpec((hw, MLA_KV_LORA), lambda i: (0, 0)),
        ],
        out_specs=[
            pl.BlockSpec((2 * hw, tm), lambda i: (0, i)),
            pl.BlockSpec((tm, hw), lambda i: (i, 0)),
            pl.BlockSpec((tm, LANES), lambda i: (i, 0)),
            pl.BlockSpec((hw, tm), lambda i: (0, i)),
        ],
        out_shape=[
            jax.ShapeDtypeStruct((2 * hw, t), BF16),
            jax.ShapeDtypeStruct((t, hw), BF16),
            jax.ShapeDtypeStruct((t, LANES), BF16),
            jax.ShapeDtypeStruct((hw, t), BF16),
        ],
        compiler_params=_cp(("parallel",)),
        name="mla_prep",
    )(p, p, p, cos_row, sin_row, cos_col, sin_col, qnw, kvnw, wqt, wkn, wvt)


ATTN_HEADS_PER_STEP = 4
ATTN_SUM_ROWS = 16


def _mla_attn_kernel(qt_ref, kt_ref, q_ref, kn_ref, kr_ref, vt_ref, o_ref, m_ref, acc_ref):
    qi = qt_ref[pl.program_id(2)]
    ki = kt_ref[pl.program_id(2)]
    tq = q_ref.shape[1]
    tk = kn_ref.shape[0]
    hd = 2 * LANES

    @pl.when(ki == 0)
    def _():
        m_ref[...] = jnp.full_like(m_ref, NEG_BIG)
        acc_ref[...] = jnp.zeros_like(acc_ref)

    def step(masked):
        kr = kr_ref[...]
        ones = jnp.ones((ATTN_SUM_ROWS, tk), BF16)

        def scores(h):
            k = jnp.concatenate([kn_ref[:, h * MLA_NOPE:(h + 1) * MLA_NOPE], kr], axis=1)
            s = _dot(k, q_ref[h * hd:(h + 1) * hd, :])
            if masked:
                ck = lax.broadcasted_iota(jnp.int32, (tk, tq), 0) // CHUNK
                cq = lax.broadcasted_iota(jnp.int32, (tk, tq), 1) // CHUNK
                s = jnp.where(ck <= cq, s, NEG_BIG)
            return s

        def update(h, s):
            m_prev = m_ref[h]
            m_new = jnp.maximum(m_prev, jnp.max(s, axis=0, keepdims=True))
            alpha = jnp.exp2(m_prev - m_new)
            p = jnp.exp2((s - m_new).astype(BF16))
            v_ext = jnp.concatenate([vt_ref[h * MLA_V:(h + 1) * MLA_V, :], ones], axis=0)
            acc_ref[h] = alpha * acc_ref[h] + _dot(v_ext, p)
            m_ref[h] = m_new

        s_prev = scores(0)
        for h in range(1, ATTN_HEADS_PER_STEP):
            s_next = scores(h)
            update(h - 1, s_prev)
            s_prev = s_next
        update(ATTN_HEADS_PER_STEP - 1, s_prev)

    @pl.when(ki < qi)
    def _():
        step(False)

    @pl.when(ki == qi)
    def _():
        step(True)
        for h in range(ATTN_HEADS_PER_STEP):
            acc = acc_ref[h]
            o = acc[0:MLA_V, :] / acc[MLA_V:MLA_V + 1, :]
            o_ref[:, h * MLA_V:(h + 1) * MLA_V] = o.T.astype(BF16)


def _mla_attn(qt_all, kn, kr, vt, batch, seq, tq=512):
    t = kn.shape[0]
    nq = seq // tq
    hps = ATTN_HEADS_PER_STEP
    pairs = [(qi, ki) for qi in range(nq) for ki in range(qi + 1)]
    qt = jnp.asarray(np.array([pr[0] for pr in pairs], np.int32))
    kt = jnp.asarray(np.array([pr[1] for pr in pairs], np.int32))
    return pl.pallas_call(
        _mla_attn_kernel,
        grid_spec=pltpu.PrefetchScalarGridSpec(
            num_scalar_prefetch=2,
            grid=(batch, MLA_HEADS // hps, len(pairs)),
            in_specs=[
                pl.BlockSpec((hps * 2 * LANES, tq), lambda b, h, pr, qt, kt: (h, b * nq + qt[pr])),
                pl.BlockSpec((tq, hps * MLA_NOPE), lambda b, h, pr, qt, kt: (b * nq + kt[pr], h)),
                pl.BlockSpec((tq, LANES), lambda b, h, pr, qt, kt: (b * nq + kt[pr], 0)),
                pl.BlockSpec((hps * MLA_V, tq), lambda b, h, pr, qt, kt: (h, b * nq + kt[pr])),
            ],
            out_specs=pl.BlockSpec((tq, hps * MLA_V), lambda b, h, pr, qt, kt: (b * nq + qt[pr], h)),
            scratch_shapes=[
                pltpu.VMEM((hps, 1, tq), F32),
                pltpu.VMEM((hps, MLA_V + ATTN_SUM_ROWS, tq), F32),
            ],
        ),
        out_shape=jax.ShapeDtypeStruct((t, MLA_HEADS * MLA_V), BF16),
        compiler_params=_cp(("parallel", "parallel", "arbitrary")),
        name="mla_attn",
    )(qt, kt, qt_all, kn, kr, vt)


def _mix_out_kernel(x_ref, oa_ref, ob_ref, ga_ref, gb_ref, wga_ref, wmo_ref, wout_ref, nw_ref,
                    wr_hi_ref, wr_lo_ref, br_ref, x1_ref, h2_ref, sel_ref, cw_ref):
    ya = _dot(oa_ref[...], wga_ref[...])
    yb = _dot(ob_ref[...], wmo_ref[...])
    merged = (jax.nn.sigmoid(ga_ref[...].astype(F32)) * ya
              + jax.nn.sigmoid(gb_ref[...].astype(F32)) * yb)
    x1 = x_ref[...] + _dot(merged.astype(BF16), wout_ref[...])
    x1_ref[...] = x1
    h2 = _rms(x1, nw_ref[...])
    h_hi = h2.astype(BF16)
    h2_ref[...] = h_hi

    h_lo = (h2 - h_hi.astype(F32)).astype(BF16)
    logits = (_dot(h_hi, wr_hi_ref[...]) + _dot(h_hi, wr_lo_ref[...]) + _dot(h_lo, wr_hi_ref[...])
              + br_ref[...])
    lane = lax.broadcasted_iota(jnp.int32, logits.shape, 1)
    work = jnp.where(lane < N_EXPERTS, logits, -jnp.inf)
    sel = jnp.zeros(logits.shape, F32)
    cw = jnp.zeros(logits.shape, F32)
    top = None
    denom = None
    for kk in range(TOP_K):
        mx = jnp.max(work, axis=-1, keepdims=True)
        am = jnp.min(jnp.where(work == mx, lane, LANES), axis=-1, keepdims=True)
        hit = lane == am
        if kk == 0:
            top = mx
            e = jnp.ones_like(mx)
            denom = e
        else:
            e = jnp.exp(mx - top)
            denom = denom + e
        sel = jnp.where(hit, 1.0, sel)
        cw = jnp.where(hit, e, cw)
        work = jnp.where(hit, -jnp.inf, work)
    sel_ref[...] = sel.astype(BF16)
    cw_ref[...] = cw / denom


def _mix_out(x2, oa, ob, p, wga, wmo, wout, nw, wr_hi, wr_lo, br, tm=256):
    t, d = x2.shape
    full = lambda i: (0, 0)
    return pl.pallas_call(
        _mix_out_kernel,
        grid=(t // tm,),
        in_specs=[
            pl.BlockSpec((tm, d), lambda i: (i, 0)),
            pl.BlockSpec((tm, d), lambda i: (i, 0)),
            pl.BlockSpec((tm, d), lambda i: (i, 0)),
            pl.BlockSpec((tm, d), lambda i: (i, 4)),
            pl.BlockSpec((tm, d), lambda i: (i, 5)),
            pl.BlockSpec((d, d), full),
            pl.BlockSpec((d, d), full),
            pl.BlockSpec((d, d), full),
            pl.BlockSpec((1, d), full),
            pl.BlockSpec((d, LANES), full),
            pl.BlockSpec((d, LANES), full),
            pl.BlockSpec((1, LANES), full),
        ],
        out_specs=[
            pl.BlockSpec((tm, d), lambda i: (i, 0)),
            pl.BlockSpec((tm, d), lambda i: (i, 0)),
            pl.BlockSpec((tm, LANES), lambda i: (i, 0)),
            pl.BlockSpec((tm, LANES), lambda i: (i, 0)),
        ],
        out_shape=[
            jax.ShapeDtypeStruct((t, d), F32),
            jax.ShapeDtypeStruct((t, d), BF16),
            jax.ShapeDtypeStruct((t, LANES), BF16),
            jax.ShapeDtypeStruct((t, LANES), F32),
        ],
        compiler_params=_cp(("parallel",)),
        name="mix_out",
    )(x2, oa, ob, p, p, wga, wmo, wout, nw, wr_hi, wr_lo, br)


def _route_pos_kernel(sel_ref, lpos_ref, keyt_ref, offs_ref, cnt_ref, tot_ref, carry_ref):
    tm = sel_ref.shape[0]
    i = pl.program_id(0)

    @pl.when(i == 0)
    def _():
        carry_ref[...] = jnp.zeros_like(carry_ref)

    sel = sel_ref[...]
    row = lax.broadcasted_iota(jnp.int32, (tm, tm), 0)
    col = lax.broadcasted_iota(jnp.int32, (tm, tm), 1)
    before = (col < row).astype(BF16)
    lpos_ref[...] = _dot(before, sel)
    pos_t = _dot_tn(sel, (row < col).astype(BF16))
    sel_t = _dot_tn(sel, (row == col).astype(BF16))
    keyt_ref[0] = jnp.where(sel_t > 0.5, pos_t, -1.0)[0:N_EXPERTS, :]

    n = jnp.sum(sel.astype(F32), axis=0, keepdims=True)
    carry = carry_ref[0:1, :]
    offs_ref[0] = carry.astype(jnp.int32)
    cnt_ref[0] = n.astype(jnp.int32)
    total = carry + jnp.ceil(n * (1.0 / SEG_ALIGN)) * SEG_ALIGN
    carry_ref[...] = jnp.broadcast_to(total, carry_ref.shape)
    tot_ref[...] = jnp.broadcast_to(total, tot_ref.shape).astype(jnp.int32)


def _route_pos(sel):
    t = sel.shape[0]
    tm = ROUTE_TILE
    nt = t // tm
    return pl.pallas_call(
        _route_pos_kernel,
        grid=(nt,),
        in_specs=[pl.BlockSpec((tm, LANES), lambda i: (i, 0))],
        out_specs=[
            pl.BlockSpec((tm, LANES), lambda i: (i, 0)),
            pl.BlockSpec((1, N_EXPERTS, tm), lambda i: (i, 0, 0)),
            pl.BlockSpec((1, 1, LANES), lambda i: (i, 0, 0)),
            pl.BlockSpec((1, 1, LANES), lambda i: (i, 0, 0)),
            pl.BlockSpec((8, LANES), lambda i: (0, 0)),
        ],
        out_shape=[
            jax.ShapeDtypeStruct((t, LANES), F32),
            jax.ShapeDtypeStruct((nt, N_EXPERTS, tm), F32),
            jax.ShapeDtypeStruct((nt, 1, LANES), jnp.int32),
            jax.ShapeDtypeStruct((nt, 1, LANES), jnp.int32),
            jax.ShapeDtypeStruct((8, LANES), jnp.int32),
        ],
        scratch_shapes=[pltpu.VMEM((8, LANES), F32)],
        compiler_params=_cp(("arbitrary",)),
        name="route_pos",
    )(sel)


def _rows(ref, start, n):
    return ref.at[pl.ds(pl.multiple_of(start, n), n)]


def _seg_windows(cnt_ref, base):
    longest = lax.fori_loop(0, N_EXPERTS, lambda e, m: jnp.maximum(m, cnt_ref[base + e]), 0)
    return lax.shift_right_logical(longest + (SEG_WINDOW - 1), SEG_WINDOW.bit_length() - 1)


def _seg_copies(cnt_ref, seg_ref, base, win, e_lo, e_hi, make_copy, wait):
    first = win * SEG_WINDOW

    def per_expert(e, carry):
        rows = jnp.minimum(jnp.maximum(cnt_ref[base + e] - first, 0), SEG_WINDOW)
        n_chunks = lax.shift_right_logical(rows + (SEG_ALIGN - 1), SEG_ALIGN.bit_length() - 1)
        slot0 = seg_ref[base + e] + first
        stage0 = e * SEG_WINDOW

        def per_chunk(c, carry2):
            cp = make_copy(stage0 + c * SEG_ALIGN, slot0 + c * SEG_ALIGN)
            if wait:
                cp.wait()
            else:
                cp.start()
            return carry2

        return lax.fori_loop(0, n_chunks, per_chunk, carry)

    lax.fori_loop(e_lo, e_hi, per_expert, 0)


def _dispatch_kernel(seg_ref, cnt_ref, fill_lo_ref, fill_hi_ref, h_ref, keyt_ref, xs_ref,
                     stage_ref, zero_ref, sem):
    i = pl.program_id(0)
    tm = h_ref.shape[0]
    base = i * N_EXPERTS

    @pl.when(i == 0)
    def _():
        zero_ref[...] = jnp.zeros_like(zero_ref)

        def fill(c):
            return pltpu.make_async_copy(zero_ref.at[pl.ds(0, SEG_ALIGN)], _rows(xs_ref, c * SEG_ALIGN, SEG_ALIGN), sem)

        def per_expert(e, carry):
            lo, hi = fill_lo_ref[e] // SEG_ALIGN, fill_hi_ref[e] // SEG_ALIGN
            lax.fori_loop(lo, hi, lambda c, a: (fill(c).start(), a)[1], 0)
            lax.fori_loop(lo, hi, lambda c, a: (fill(c).wait(), a)[1], 0)
            return carry

        lax.fori_loop(0, N_EXPERTS, per_expert, 0)

        def fill_tail(c):
            return pltpu.make_async_copy(zero_ref, _rows(xs_ref, c * ZERO_ROWS, ZERO_ROWS), sem)

        lo = fill_hi_ref[N_EXPERTS - 1] // ZERO_ROWS
        hi = xs_ref.shape[0] // ZERO_ROWS
        lax.fori_loop(lo, hi, lambda c, a: (fill_tail(c).start(), a)[1], 0)
        lax.fori_loop(lo, hi, lambda c, a: (fill_tail(c).wait(), a)[1], 0)

    def make_copy(stage_row, slot):
        return pltpu.make_async_copy(_rows(stage_ref, stage_row, SEG_ALIGN), _rows(xs_ref, slot, SEG_ALIGN), sem)

    half = N_EXPERTS // 2
    j = lax.broadcasted_iota(jnp.int32, (SEG_WINDOW, tm), 0).astype(F32)

    def window(win, carry):
        key = keyt_ref[0] - (win * SEG_WINDOW).astype(F32)
        for hf in range(2):
            pick = jnp.concatenate([(key[e:e + 1, :] == j).astype(BF16)
                                    for e in range(hf * half, (hf + 1) * half)], axis=0)
            stage_ref[hf * half * SEG_WINDOW:(hf + 1) * half * SEG_WINDOW, :] = _dot(pick, h_ref[...])
            _seg_copies(cnt_ref, seg_ref, base, win, hf * half, (hf + 1) * half, make_copy, wait=False)
        _seg_copies(cnt_ref, seg_ref, base, win, 0, N_EXPERTS, make_copy, wait=True)
        return carry

    lax.fori_loop(0, _seg_windows(cnt_ref, base), window, 0)


def _dispatch(seg, cnt, fill_lo, fill_hi, h2, keyt, n_pad):
    t, d = h2.shape
    tm = ROUTE_TILE
    return pl.pallas_call(
        _dispatch_kernel,
        grid_spec=pltpu.PrefetchScalarGridSpec(
            num_scalar_prefetch=4,
            grid=(t // tm,),
            in_specs=[
                pl.BlockSpec((tm, d), lambda i, *_: (i, 0)),
                pl.BlockSpec((1, N_EXPERTS, tm), lambda i, *_: (i, 0, 0)),
            ],
            out_specs=pl.BlockSpec(memory_space=pl.ANY),
            scratch_shapes=[pltpu.VMEM((N_EXPERTS * SEG_WINDOW, d), F32),
                            pltpu.VMEM((ZERO_ROWS, d), F32),
                            pltpu.SemaphoreType.DMA(())],
        ),
        out_shape=jax.ShapeDtypeStruct((n_pad, d), F32),
        compiler_params=_cp(("arbitrary",)),
        name="dispatch",
    )(seg, cnt, fill_lo, fill_hi, h2, keyt)


def _experts_kernel(be_ref, nv_ref, xs_ref, wgu_ref, wd_ref, bg_ref, bu_ref, bd_ref, ys_ref,
                    wg_s, wu_s, wd_s):
    j = pl.program_id(0)
    grp = 2 * LANES
    prev = be_ref[jnp.maximum(j - 1, 0)]

    @pl.when((j == 0) | (be_ref[j] != prev))
    def _():
        r = lax.broadcasted_iota(jnp.int32, (grp, grp), 0)
        c = lax.broadcasted_iota(jnp.int32, (grp, grp), 1)
        src = jnp.where(c < LANES, 2 * c, 2 * (c - LANES) + 1)
        pick = (r == src).astype(BF16)
        for g in range(wgu_ref.shape[2] // grp):
            y = _dot(wgu_ref[0, :, g * grp:(g + 1) * grp].astype(BF16), pick)
            wg_s[:, g * LANES:(g + 1) * LANES] = y[:, 0:LANES].astype(BF16)
            wu_s[:, g * LANES:(g + 1) * LANES] = y[:, LANES:grp].astype(BF16)
        wd_s[...] = wd_ref[0].astype(BF16)

    @pl.when(j < nv_ref[0])
    def _():
        x = xs_ref[...].astype(BF16)
        g = _dot(x, wg_s[...]) + bg_ref[0]
        u = _dot(x, wu_s[...]) + bu_ref[0]
        gate = jnp.minimum(g, SWIGLU_LIMIT)
        up = jnp.clip(u, -SWIGLU_LIMIT, SWIGLU_LIMIT)
        act = (up + 1.0) * (gate * jax.nn.sigmoid(gate * SWIGLU_ALPHA))
        ys_ref[...] = _dot(act.astype(BF16), wd_s[...]) + bd_ref[0]

    @pl.when(j >= nv_ref[0])
    def _():
        ys_ref[...] = jnp.zeros_like(ys_ref)


def _experts(block_e, n_valid, xs, wgu, wd, bg, bu, bd):
    n_pad = xs.shape[0]
    de, d = wd.shape[1:]
    blk = (MOE_ROWS, d)
    n_blocks = n_pad // MOE_ROWS
    xrow = lambda j, be, nv: (jnp.minimum(j, nv[0] - 1), 0)
    wsel = lambda j, be, nv: (be[j], 0, 0)
    return pl.pallas_call(
        _experts_kernel,
        grid_spec=pltpu.PrefetchScalarGridSpec(
            num_scalar_prefetch=2,
            grid=(n_blocks,),
            in_specs=[
                pl.BlockSpec(blk, xrow),
                pl.BlockSpec((1, d, 2 * de), wsel),
                pl.BlockSpec((1, de, d), wsel),
                pl.BlockSpec((1, 1, de), wsel),
                pl.BlockSpec((1, 1, de), wsel),
                pl.BlockSpec((1, 1, d), wsel),
            ],
            out_specs=pl.BlockSpec(blk, lambda j, be, nv: (j, 0)),
            scratch_shapes=[
                pltpu.VMEM((d, de), BF16),
                pltpu.VMEM((d, de), BF16),
                pltpu.VMEM((de, d), BF16),
            ],
        ),
        out_shape=jax.ShapeDtypeStruct(xs.shape, F32),
        compiler_params=pltpu.CompilerParams(dimension_semantics=("arbitrary",),
                                             vmem_limit_bytes=EXPERTS_VMEM_LIMIT),
        name="experts",
    )(block_e, n_valid, xs, wgu, wd, bg, bu, bd)


def _combine_kernel(seg_ref, cnt_ref, x1_ref, lpos_ref, cw_ref, nw_ref, ys_ref, o_ref, stage_ref, sem,
                    *, final_norm):
    i = pl.program_id(0)
    tm, d = x1_ref.shape
    base = i * N_EXPERTS

    @pl.when(i == 0)
    def _():
        stage_ref[...] = jnp.zeros_like(stage_ref)

    def make_copy(stage_row, slot):
        return pltpu.make_async_copy(_rows(ys_ref, slot, SEG_ALIGN), _rows(stage_ref, stage_row, SEG_ALIGN), sem)

    n_stage = N_EXPERTS * SEG_WINDOW
    owner = lax.broadcasted_iota(jnp.int32, (LANES, n_stage), 1) // SEG_WINDOW
    expand = (owner == lax.broadcasted_iota(jnp.int32, (LANES, n_stage), 0)).astype(BF16)
    j = (lax.broadcasted_iota(jnp.int32, (tm, n_stage), 1) % SEG_WINDOW).astype(F32)
    cw_wide = _dot(cw_ref[...].astype(BF16), expand)

    def window(win, y):
        _seg_copies(cnt_ref, seg_ref, base, win, 0, N_EXPERTS, make_copy, wait=False)
        rank = (lpos_ref[...] - (win * SEG_WINDOW).astype(F32)).astype(BF16)
        take = jnp.where(_dot(rank, expand) == j, cw_wide, 0.0).astype(BF16)
        _seg_copies(cnt_ref, seg_ref, base, win, 0, N_EXPERTS, make_copy, wait=True)
        return y + _dot(take, stage_ref[...].astype(BF16))

    y = lax.fori_loop(0, _seg_windows(cnt_ref, base), window, jnp.zeros((tm, d), F32))
    out = x1_ref[...] + y
    if final_norm:
        out = _rms(out, nw_ref[...])
    o_ref[...] = out


def _combine(seg, cnt, x1, lpos, cw, nw, ys, final_norm):
    t, d = x1.shape
    tm = ROUTE_TILE
    kern = functools.partial(_combine_kernel, final_norm=final_norm)
    return pl.pallas_call(
        kern,
        grid_spec=pltpu.PrefetchScalarGridSpec(
            num_scalar_prefetch=2,
            grid=(t // tm,),
            in_specs=[
                pl.BlockSpec((tm, d), lambda i, *_: (i, 0)),
                pl.BlockSpec((tm, LANES), lambda i, *_: (i, 0)),
                pl.BlockSpec((tm, LANES), lambda i, *_: (i, 0)),
                pl.BlockSpec((1, d), lambda i, *_: (0, 0)),
                pl.BlockSpec(memory_space=pl.ANY),
            ],
            out_specs=pl.BlockSpec((tm, d), lambda i, *_: (i, 0)),
            scratch_shapes=[pltpu.VMEM((N_EXPERTS * SEG_WINDOW, d), F32), pltpu.SemaphoreType.DMA(())],
        ),
        out_shape=jax.ShapeDtypeStruct((t, d), F32),
        compiler_params=_cp(("arbitrary",)),
        name="combine",
    )(seg, cnt, x1, lpos, cw, nw, ys)


def _rope_tables(seq):
    half = MLA_ROPE // 2
    inv = 1.0 / (ROPE_THETA ** (jnp.arange(0, MLA_ROPE, 2, dtype=F32) / MLA_ROPE))
    ang = jnp.arange(seq, dtype=F32)[:, None] * inv[None, :]
    cos, sin = jnp.cos(ang), jnp.sin(ang)
    zeros = jnp.zeros((seq, LANES - MLA_ROPE), F32)
    cos_row = jnp.concatenate([cos, cos, zeros], axis=-1)
    sin_row = jnp.concatenate([-sin, sin, zeros], axis=-1)
    del half
    return cos_row, sin_row, cos.T, sin.T


def _pad_cols(a, width):
    return jnp.pad(a, ((0, 0), (0, width - a.shape[1])))


def _layer(x2, batch, seq, final_norm_w, final_norm, rope_tables,
           norm_mix_w, w_in, gdn_conv_w, gdn_a_log, gdn_dt_bias, gdn_norm_w, w_gdn_o,
           mla_q_norm_w, w_mla_q_b, mla_kv_norm_w, w_mla_kv_b, w_mla_o, w_out,
           norm_ffn_w, w_router, b_router, w_gate_up, b_gate_up, w_down, b_down):
    t, d = x2.shape
    qk_w = GDN_HEADS * GDN_D
    o_b = 4 * qk_w
    o_a = o_b + GDN_HEADS
    o_cq = o_a + GDN_HEADS
    o_ckv = o_cq + MLA_Q_LORA
    o_kr = o_ckv + MLA_KV_LORA
    o_ga = o_kr + MLA_ROPE
    o_gb = o_ga + d
    w_p = jnp.concatenate([
        w_in[:, 0:o_b], w_in[:, o_ga:o_gb + d], w_in[:, o_cq:o_ckv], w_in[:, o_ckv:o_kr],
        _pad_cols(w_in[:, o_kr:o_ga], 2 * LANES)], axis=1).astype(BF16)
    w_ab = _pad_cols(jnp.concatenate([w_in[:, o_a:o_cq], w_in[:, o_b:o_a]], axis=1), LANES).astype(BF16)

    p, ab = _in_proj(x2, norm_mix_w[None, :], w_p, w_ab)

    alog_row = _pad_cols(gdn_a_log[None, :].astype(F32), LANES)
    dtb_row = _pad_cols(gdn_dt_bias[None, :].astype(F32), LANES)
    qkvn, cols, gct = _gdn_prep(p, ab, gdn_conv_w.astype(F32), alog_row, dtb_row, seq)
    o_gdn = _gdn_chunk(qkvn, p, cols, gct, gdn_norm_w[None, :].astype(F32), batch, seq)

    hd = MLA_NOPE + MLA_ROPE
    wq = w_mla_q_b.reshape(MLA_Q_LORA, MLA_HEADS, hd)
    wqt = jnp.pad(wq, ((0, 0), (0, 0), (0, 2 * LANES - hd))).reshape(MLA_Q_LORA, MLA_HEADS * 2 * LANES).T
    wkv = w_mla_kv_b.reshape(MLA_KV_LORA, MLA_HEADS, MLA_NOPE + MLA_V)
    wkn = wkv[:, :, :MLA_NOPE].reshape(MLA_KV_LORA, -1)
    wvt = wkv[:, :, MLA_NOPE:].reshape(MLA_KV_LORA, -1).T
    qt, kn, kr, vt = _mla_prep(p, rope_tables, mla_q_norm_w[None, :].astype(F32),
                               mla_kv_norm_w[None, :].astype(F32), wqt.astype(BF16), wkn.astype(BF16),
                               wvt.astype(BF16), seq)
    o_mla = _mla_attn(qt, kn, kr, vt, batch, seq)

    wr = _pad_cols(w_router.astype(F32), LANES)
    wr_hi = wr.astype(BF16)
    wr_lo = (wr - wr_hi.astype(F32)).astype(BF16)
    br = _pad_cols(b_router[None, :].astype(F32), LANES)
    x1, h2, sel, cw = _mix_out(x2, o_gdn, o_mla, p, w_gdn_o.astype(BF16), w_mla_o.astype(BF16),
                               w_out.astype(BF16), norm_ffn_w[None, :].astype(F32), wr_hi, wr_lo, br)

    lpos, keyt, offs, cnt, tot = _route_pos(sel)
    n_tiles = t // ROUTE_TILE
    used = tot[0, :N_EXPERTS]
    padded = (used + MOE_ROWS - 1) // MOE_ROWS * MOE_ROWS
    pad_end = jnp.cumsum(padded)
    pad_start = pad_end - padded
    seg = (pad_start[None, :] + offs[:, 0, :N_EXPERTS]).astype(jnp.int32).reshape(-1)
    cnt = cnt[:, 0, :N_EXPERTS].reshape(-1)
    worst_used = t * TOP_K + n_tiles * N_EXPERTS * (SEG_ALIGN - 1)
    n_pad = -(-worst_used // MOE_ROWS) * MOE_ROWS + N_EXPERTS * MOE_ROWS
    n_blocks = n_pad // MOE_ROWS
    blk_start = jnp.arange(n_blocks, dtype=jnp.int32) * MOE_ROWS
    block_e = jnp.minimum(jnp.sum((pad_end[None, :] <= blk_start[:, None]).astype(jnp.int32), axis=1),
                          N_EXPERTS - 1).astype(jnp.int32)
    n_valid = (pad_end[-1:] // MOE_ROWS).astype(jnp.int32)

    xs = _dispatch(seg, cnt, (pad_start + used).astype(jnp.int32), pad_end.astype(jnp.int32), h2, keyt, n_pad)
    bg = b_gate_up[:, None, 0::2].astype(F32)
    bu = b_gate_up[:, None, 1::2].astype(F32)
    ys = _experts(block_e, n_valid, xs, w_gate_up, w_down, bg, bu, b_down[:, None, :].astype(F32))
    return _combine(seg, cnt, x1, lpos, cw, final_norm_w[None, :].astype(F32), ys, final_norm)


def kernel(x, norm_mix_w, w_in, gdn_conv_w, gdn_a_log, gdn_dt_bias, gdn_norm_w, w_gdn_o, mla_q_norm_w, w_mla_q_b, mla_kv_norm_w, w_mla_kv_b, w_mla_o, w_out, norm_ffn_w, w_router, b_router, w_gate_up, b_gate_up, w_down, b_down, norm_final_w):
    batch, seq, d = x.shape
    depth = w_in.shape[0]
    rope_tables = _rope_tables(seq)
    x2 = x.reshape(batch * seq, d)
    for layer in range(depth):
        x2 = _layer(x2, batch, seq, norm_final_w, layer == depth - 1, rope_tables,
                    norm_mix_w[layer], w_in[layer], gdn_conv_w[layer], gdn_a_log[layer],
                    gdn_dt_bias[layer], gdn_norm_w[layer], w_gdn_o[layer], mla_q_norm_w[layer],
                    w_mla_q_b[layer], mla_kv_norm_w[layer], w_mla_kv_b[layer], w_mla_o[layer],
                    w_out[layer], norm_ffn_w[layer], w_router[layer], b_router[layer],
                    w_gate_up[layer], b_gate_up[layer], w_down[layer], b_down[layer])
    return x2.reshape(batch, seq, d)
```

```python
import functools

import jax
import jax.numpy as jnp
import numpy as np
from jax import lax
from jax.experimental import pallas as pl
from jax.experimental.pallas import tpu as pltpu

F32 = jnp.float32
BF16 = jnp.bfloat16

CHUNK = 64
NORM_EPS = 1e-6
GDN_HEADS = 8
GDN_D = 128
GDN_CONV = 4
MLA_HEADS = 8
MLA_Q_LORA = 512
MLA_KV_LORA = 256
MLA_NOPE = 128
MLA_ROPE = 64
MLA_V = 128
ROPE_THETA = 10000.0
N_EXPERTS = 32
TOP_K = 4
SWIGLU_LIMIT = 7.0
SWIGLU_ALPHA = 1.702

LANES = 128
MOE_ROWS = 512
ROUTE_TILE = 256
SEG_ALIGN = 8
SEG_WINDOW = 64
ZERO_ROWS = 64
VMEM_LIMIT = 48 * 1024 * 1024
EXPERTS_VMEM_LIMIT = 56 * 1024 * 1024

NEG_BIG = -1e30
LOG2_E = 1.4426950408889634


def _cp(sem):
    return pltpu.CompilerParams(dimension_semantics=sem, vmem_limit_bytes=VMEM_LIMIT)


def _dot(a, b):
    return jnp.dot(a, b, preferred_element_type=F32)


def _dot_nt(a, b):
    return lax.dot_general(a, b, (((1,), (1,)), ((), ())), preferred_element_type=F32)


def _dot_tn(a, b):
    return lax.dot_general(a, b, (((0,), (0,)), ((), ())), preferred_element_type=F32)


def _split3(x):
    hi = x.astype(BF16)
    r = x - hi.astype(F32)
    mid = r.astype(BF16)
    lo = (r - mid.astype(F32)).astype(BF16)
    return hi, mid, lo


def _rms(x, w):
    ms = jnp.mean(x * x, axis=-1, keepdims=True)
    return x * lax.rsqrt(ms + NORM_EPS) * w


def _in_proj_kernel(x_ref, nw_ref, w_ref, wab_ref, p_ref, ab_ref, h_ref):
    @pl.when(pl.program_id(1) == 0)
    def _():
        hb = _rms(x_ref[...], nw_ref[...]).astype(BF16)
        h_ref[...] = hb
        ab_ref[...] = _dot(hb, wab_ref[...])

    p_ref[...] = _dot(h_ref[...], w_ref[...]).astype(BF16)


def _in_proj(x2, norm_w, w_p, w_ab, tm=1024, tn=3584):
    t, d = x2.shape
    n = w_p.shape[1]
    return pl.pallas_call(
        _in_proj_kernel,
        grid=(t // tm, n // tn),
        in_specs=[
            pl.BlockSpec((tm, d), lambda i, j: (i, 0)),
            pl.BlockSpec((1, d), lambda i, j: (0, 0)),
            pl.BlockSpec((d, tn), lambda i, j: (0, j)),
            pl.BlockSpec((d, LANES), lambda i, j: (0, 0)),
        ],
        out_specs=[
            pl.BlockSpec((tm, tn), lambda i, j: (i, j)),
            pl.BlockSpec((tm, LANES), lambda i, j: (i, 0)),
        ],
        out_shape=[
            jax.ShapeDtypeStruct((t, n), BF16),
            jax.ShapeDtypeStruct((t, LANES), F32),
        ],
        scratch_shapes=[pltpu.VMEM((tm, d), BF16)],
        compiler_params=_cp(("parallel", "arbitrary")),
        name="in_proj",
    )(x2, norm_w, w_p, w_ab)


def _gdn_prep_kernel(cur_ref, prev_ref, ab_ref, cw_ref, alog_ref, dtb_ref,
                     qkv_ref, cols_ref, gct_ref, *, tiles_per_seq):
    tm = cur_ref.shape[0]
    i = pl.program_id(0)
    halo_on = (i % tiles_per_seq) != 0
    n_blk = cur_ref.shape[1] // LANES
    q_scale = GDN_D ** -0.5
    for cb in range(n_blk):
        cs = slice(cb * LANES, (cb + 1) * LANES)
        cur = cur_ref[:, cs].astype(F32)
        halo = prev_ref[:, cs].astype(F32)[8:16, :]
        halo = jnp.where(halo_on, halo, 0.0)
        xe = jnp.concatenate([halo, cur], axis=0)
        w = cw_ref[:, cs]
        y = w[0:1, :] * xe[5:5 + tm, :]
        for j in range(1, GDN_CONV):
            y = y + w[j:j + 1, :] * xe[5 + j:5 + j + tm, :]
        y = y * jax.nn.sigmoid(y)
        if cb < 2 * GDN_HEADS:
            ss = jnp.sum(y * y, axis=-1, keepdims=True)
            y = y * lax.rsqrt(ss + NORM_EPS)
            if cb < GDN_HEADS:
                y = y * q_scale
        qkv_ref[:, cs] = y.astype(BF16)

    ab = ab_ref[...]
    g = -jnp.exp(alog_ref[...]) * jax.nn.softplus(ab + dtb_ref[...])
    row = lax.broadcasted_iota(jnp.int32, (tm, tm), 0)
    col = lax.broadcasted_iota(jnp.int32, (tm, tm), 1)
    tri = ((col <= row) & ((row // CHUNK) == (col // CHUNK))).astype(BF16)
    g_hi, g_mid, g_lo = _split3(g)
    gc = _dot(tri, g_hi) + _dot(tri, g_mid) + _dot(tri, g_lo)
    lane = lax.broadcasted_iota(jnp.int32, (tm, LANES), 1)
    cols_ref[...] = jnp.where(lane < GDN_HEADS, gc, jax.nn.sigmoid(ab))
    for c in range(tm // CHUNK):
        blk = gc[c * CHUNK:(c + 1) * CHUNK, :]
        blk = jnp.concatenate([blk, jnp.zeros_like(blk)], axis=0)
        gct_ref[c] = blk.T[0:GDN_HEADS, 0:CHUNK]


def _gdn_prep(p, ab, conv_w, alog_row, dtb_row, seq, tm=256):
    t = p.shape[0]
    cw = 3 * GDN_HEADS * GDN_D
    tiles_per_seq = seq // tm
    kern = functools.partial(_gdn_prep_kernel, tiles_per_seq=tiles_per_seq)
    return pl.pallas_call(
        kern,
        grid=(t // tm,),
        in_specs=[
            pl.BlockSpec((tm, cw), lambda i: (i, 0)),
            pl.BlockSpec((16, cw), lambda i: (jnp.maximum(i * (tm // 16) - 1, 0), 0)),
            pl.BlockSpec((tm, LANES), lambda i: (i, 0)),
            pl.BlockSpec((GDN_CONV, cw), lambda i: (0, 0)),
            pl.BlockSpec((1, LANES), lambda i: (0, 0)),
            pl.BlockSpec((1, LANES), lambda i: (0, 0)),
        ],
        out_specs=[
            pl.BlockSpec((tm, cw), lambda i: (i, 0)),
            pl.BlockSpec((tm, LANES), lambda i: (i, 0)),
            pl.BlockSpec((tm // CHUNK, GDN_HEADS, CHUNK), lambda i: (i, 0, 0)),
        ],
        out_shape=[
            jax.ShapeDtypeStruct((t, cw), BF16),
            jax.ShapeDtypeStruct((t, LANES), F32),
            jax.ShapeDtypeStruct((t // CHUNK, GDN_HEADS, CHUNK), F32),
        ],
        compiler_params=_cp(("parallel",)),
        name="gdn_prep",
    )(p, p, ab, conv_w, alog_row, dtb_row)


def _gdn_chunk_kernel(q_ref, k_ref, v_ref, z_ref, cols_ref, gct_ref, nw_ref, o_ref, s_ref):
    c = CHUNK
    nb = q_ref.shape[0]
    units = [(b, h) for b in range(nb) for h in range(GDN_HEADS)]

    @pl.when(pl.program_id(0) == 0)
    def _():
        s_ref[...] = jnp.zeros_like(s_ref)

    ri = lax.broadcasted_iota(jnp.int32, (c, c), 0)
    ci = lax.broadcasted_iota(jnp.int32, (c, c), 1)
    incl = ri >= ci
    strict = ri > ci
    eye = (ri == ci).astype(F32)
    nw = nw_ref[...]

    cols, e_g, e_kd, e_last, gct = [], [], [], [], []
    for b in range(nb):
        cb = cols_ref[b]
        last = cb[c - 1:c, :]
        cols.append(cb)
        e_g.append(jnp.exp(cb))
        e_kd.append(jnp.exp(last - cb))
        e_last.append(jnp.exp(last))
        gct.append(gct_ref[b, 0])

    kq, kb_l, kf_l = [], [], []
    for b, h in units:
        hs = slice(h * GDN_D, (h + 1) * GDN_D)
        k = k_ref[b, :, hs]
        kf = k.astype(F32)
        kb = kf * cols[b][:, GDN_HEADS + h:GDN_HEADS + h + 1]
        kq.append(_dot_nt(jnp.concatenate([kb.astype(BF16), q_ref[b, :, hs]], axis=0), k))
        kb_l.append(kb)
        kf_l.append(kf)

    a_l, qk_l = [], []
    for i, (b, h) in enumerate(units):
        dec = jnp.exp(jnp.minimum(cols[b][:, h:h + 1] - gct[b][h:h + 1, :], 0.0))
        a_l.append(jnp.where(strict, -kq[i][0:c, :] * dec, 0.0))
        qk_l.append(jnp.where(incl, kq[i][c:2 * c, :] * dec, 0.0).astype(BF16))

    tinv = [eye + a for a in a_l]
    pw = a_l
    for _ in range(5):
        pwb = [x.astype(BF16) for x in pw]
        pw = [_dot(x, x) for x in pwb]
        tinv = [t + _dot(t.astype(BF16), x.astype(BF16)) for t, x in zip(tinv, pw)]

    uw = []
    for i, (b, h) in enumerate(units):
        hs = slice(h * GDN_D, (h + 1) * GDN_D)
        beta = cols[b][:, GDN_HEADS + h:GDN_HEADS + h + 1]
        rhs = jnp.concatenate([v_ref[b, :, hs].astype(F32) * beta,
                               kb_l[i] * e_g[b][:, h:h + 1]], axis=1).astype(BF16)
        uw.append(_dot(tinv[i].astype(BF16), rhs))

    r_l = []
    for i, (b, h) in enumerate(units):
        hs = slice(h * GDN_D, (h + 1) * GDN_D)
        qd = (q_ref[b, :, hs].astype(F32) * e_g[b][:, h:h + 1]).astype(BF16)
        lhs = jnp.concatenate([uw[i][:, GDN_D:2 * GDN_D].astype(BF16), qd], axis=0)
        r_l.append(_dot(lhs, s_ref[b * GDN_HEADS + h].astype(BF16)))

    for i, (b, h) in enumerate(units):
        hs = slice(h * GDN_D, (h + 1) * GDN_D)
        v_new = (uw[i][:, 0:GDN_D] - r_l[i][0:c, :]).astype(BF16)
        o = r_l[i][c:2 * c, :] + _dot(qk_l[i], v_new)
        kd = (kf_l[i] * e_kd[b][:, h:h + 1]).astype(BF16)
        u = b * GDN_HEADS + h
        s_ref[u] = s_ref[u] * e_last[b][:, h:h + 1] + _dot_tn(kd, v_new)
        z = z_ref[b, :, hs].astype(F32)
        o_ref[b, :, hs] = (_rms(o, nw) * (z * jax.nn.sigmoid(z))).astype(BF16)


def _gdn_chunk(qkvn, p, cols, gct, norm_w, batch, seq):
    nc = seq // CHUNK
    hw = GDN_HEADS * GDN_D
    qkvn3 = qkvn.reshape(batch, seq, qkvn.shape[1])
    p3 = p.reshape(batch, seq, p.shape[1])
    cols3 = cols.reshape(batch, seq, LANES)
    gct4 = gct.reshape(batch, nc, GDN_HEADS, CHUNK)
    tile = lambda col: pl.BlockSpec((batch, CHUNK, hw), lambda c: (0, c, col))
    out = pl.pallas_call(
        _gdn_chunk_kernel,
        grid=(nc,),
        in_specs=[
            tile(0), tile(1), tile(2),
            tile(3),
            pl.BlockSpec((batch, CHUNK, LANES), lambda c: (0, c, 0)),
            pl.BlockSpec((batch, 1, GDN_HEADS, CHUNK), lambda c: (0, c, 0, 0)),
            pl.BlockSpec((1, GDN_D), lambda c: (0, 0)),
        ],
        out_specs=pl.BlockSpec((batch, CHUNK, hw), lambda c: (0, c, 0)),
        out_shape=jax.ShapeDtypeStruct((batch, seq, hw), BF16),
        scratch_shapes=[pltpu.VMEM((batch * GDN_HEADS, GDN_D, GDN_D), F32)],
        compiler_params=_cp(("arbitrary",)),
        name="gdn_chunk",
    )(qkvn3, qkvn3, qkvn3, p3, cols3, gct4, norm_w)
    return out.reshape(batch * seq, hw)


def _rope(x, cos, sin_signed):
    lane = lax.broadcasted_iota(jnp.int32, x.shape, 1)
    fwd = pltpu.roll(x, LANES - MLA_ROPE // 2, 1)
    bwd = pltpu.roll(x, MLA_ROPE // 2, 1)
    rot = jnp.where(lane < MLA_ROPE // 2, fwd, bwd)
    return x * cos + rot * sin_signed


def _mla_prep_kernel(cq_ref, ckv_ref, kr_ref, cos_ref, sin_ref, cost_ref, sint_ref, qnw_ref, kvnw_ref,
                     wqt_ref, wkn_ref, wvt_ref, qt_ref, kn_ref, kro_ref, vt_ref):
    cos = cos_ref[...]
    sin = sin_ref[...]
    cos_t = cost_ref[...]
    sin_t = sint_ref[...]
    cq = _rms(cq_ref[...].astype(F32), qnw_ref[...]).astype(BF16)
    hd = 2 * LANES
    half = MLA_ROPE // 2
    scale = (MLA_NOPE + MLA_ROPE) ** -0.5 * LOG2_E
    for h in range(MLA_HEADS):
        qh = _dot_nt(wqt_ref[h * hd:(h + 1) * hd, :], cq) * scale
        lo = qh[MLA_NOPE:MLA_NOPE + half, :]
        hi = qh[MLA_NOPE + half:MLA_NOPE + MLA_ROPE, :]
        qt_ref[h * hd:h * hd + MLA_NOPE, :] = qh[0:MLA_NOPE, :].astype(BF16)
        qt_ref[h * hd + MLA_NOPE:h * hd + MLA_NOPE + half, :] = (lo * cos_t - hi * sin_t).astype(BF16)
        qt_ref[h * hd + MLA_NOPE + half:h * hd + MLA_NOPE + MLA_ROPE, :] = (hi * cos_t + lo * sin_t).astype(BF16)
        qt_ref[h * hd + MLA_NOPE + MLA_ROPE:(h + 1) * hd, :] = qh[MLA_NOPE + MLA_ROPE:hd, :].astype(BF16)
    kvl = _rms(ckv_ref[...].astype(F32), kvnw_ref[...]).astype(BF16)
    kn_ref[...] = _dot(kvl, wkn_ref[...]).astype(BF16)
    vt_ref[...] = _dot_nt(wvt_ref[...], kvl).astype(BF16)
    kro_ref[...] = _rope(kr_ref[...].astype(F32), cos, sin).astype(BF16)


def _mla_prep(p, tables, qnw, kvnw, wqt, wkn, wvt, seq, tm=512):
    cos_row, sin_row, cos_col, sin_col = tables
    t = p.shape[0]
    tiles_per_seq = seq // tm
    hw = MLA_HEADS * MLA_NOPE
    half = MLA_ROPE // 2
    cq_blk = 6144 // MLA_Q_LORA
    ckv_blk = 6656 // MLA_KV_LORA
    kr_blk = 6912 // LANES
    return pl.pallas_call(
        _mla_prep_kernel,
        grid=(t // tm,),
        in_specs=[
            pl.BlockSpec((tm, MLA_Q_LORA), lambda i: (i, cq_blk)),
            pl.BlockSpec((tm, MLA_KV_LORA), lambda i: (i, ckv_blk)),
            pl.BlockSpec((tm, LANES), lambda i: (i, kr_blk)),
            pl.BlockSpec((tm, LANES), lambda i: (i % tiles_per_seq, 0)),
            pl.BlockSpec((tm, LANES), lambda i: (i % tiles_per_seq, 0)),
            pl.BlockSpec((half, tm), lambda i: (0, i % tiles_per_seq)),
            pl.BlockSpec((half, tm), lambda i: (0, i % tiles_per_seq)),
            pl.BlockSpec((1, MLA_Q_LORA), lambda i: (0, 0)),
            pl.BlockSpec((1, MLA_KV_LORA), lambda i: (0, 0)),
            pl.BlockSpec((2 * hw, MLA_Q_LORA), lambda i: (0, 0)),
            pl.BlockSpec((MLA_KV_LORA, hw), lambda i: (0, 0)),
            pl.BlockSpec((hw, MLA_KV_LORA), lambda i: (0, 0)),
        ],
        out_specs=[
            pl.BlockSpec((2 * hw, tm), lambda i: (0, i)),
            pl.BlockSpec((tm, hw), lambda i: (i, 0)),
            pl.BlockSpec((tm, LANES), lambda i: (i, 0)),
            pl.BlockSpec((hw, tm), lambda i: (0, i)),
        ],
        out_shape=[
            jax.ShapeDtypeStruct((2 * hw, t), BF16),
            jax.ShapeDtypeStruct((t, hw), BF16),
            jax.ShapeDtypeStruct((t, LANES), BF16),
            jax.ShapeDtypeStruct((hw, t), BF16),
        ],
        compiler_params=_cp(("parallel",)),
        name="mla_prep",
    )(p, p, p, cos_row, sin_row, cos_col, sin_col, qnw, kvnw, wqt, wkn, wvt)


ATTN_HEADS_PER_STEP = 8
ATTN_SUM_ROWS = 16


def _mla_attn_kernel(qt_ref, kt_ref, q_ref, kn_ref, kr_ref, vt_ref, o_ref, m_ref, acc_ref):
    qi = qt_ref[pl.program_id(2)]
    ki = kt_ref[pl.program_id(2)]
    tq = q_ref.shape[1]
    tk = kn_ref.shape[0]
    hd = 2 * LANES

    @pl.when(ki == 0)
    def _():
        m_ref[...] = jnp.full_like(m_ref, NEG_BIG)
        acc_ref[...] = jnp.zeros_like(acc_ref)

    def step(masked):
        kr = kr_ref[...]
        ones = jnp.ones((ATTN_SUM_ROWS, tk), BF16)

        def scores(h):
            k = jnp.concatenate([kn_ref[:, h * MLA_NOPE:(h + 1) * MLA_NOPE], kr], axis=1)
            s = _dot(k, q_ref[h * hd:(h + 1) * hd, :])
            if masked:
                ck = lax.broadcasted_iota(jnp.int32, (tk, tq), 0) // CHUNK
                cq = lax.broadcasted_iota(jnp.int32, (tk, tq), 1) // CHUNK
                s = jnp.where(ck <= cq, s, NEG_BIG)
            return s

        def update(h, s):
            m_prev = m_ref[h]
            m_new = jnp.maximum(m_prev, jnp.max(s, axis=0, keepdims=True))
            alpha = jnp.exp2(m_prev - m_new)
            p = jnp.exp2((s - m_new).astype(BF16))
            v_ext = jnp.concatenate([vt_ref[h * MLA_V:(h + 1) * MLA_V, :], ones], axis=0)
            acc_ref[h] = alpha * acc_ref[h] + _dot(v_ext, p)
            m_ref[h] = m_new

        s_prev = scores(0)
        for h in range(1, ATTN_HEADS_PER_STEP):
            s_next = scores(h)
            update(h - 1, s_prev)
            s_prev = s_next
        update(ATTN_HEADS_PER_STEP - 1, s_prev)

    @pl.when(ki < qi)
    def _():
        step(False)

    @pl.when(ki == qi)
    def _():
        step(True)
        for h in range(ATTN_HEADS_PER_STEP):
            acc = acc_ref[h]
            o = acc[0:MLA_V, :] / acc[MLA_V:MLA_V + 1, :]
            o_ref[:, h * MLA_V:(h + 1) * MLA_V] = o.T.astype(BF16)


def _mla_attn(qt_all, kn, kr, vt, batch, seq, tq=512):
    t = kn.shape[0]
    nq = seq // tq
    hps = ATTN_HEADS_PER_STEP
    pairs = [(qi, ki) for qi in range(nq) for ki in range(qi + 1)]
    qt = jnp.asarray(np.array([pr[0] for pr in pairs], np.int32))
    kt = jnp.asarray(np.array([pr[1] for pr in pairs], np.int32))
    return pl.pallas_call(
        _mla_attn_kernel,
        grid_spec=pltpu.PrefetchScalarGridSpec(
            num_scalar_prefetch=2,
            grid=(batch, MLA_HEADS // hps, len(pairs)),
            in_specs=[
                pl.BlockSpec((hps * 2 * LANES, tq), lambda b, h, pr, qt, kt: (h, b * nq + qt[pr])),
                pl.BlockSpec((tq, hps * MLA_NOPE), lambda b, h, pr, qt, kt: (b * nq + kt[pr], h)),
                pl.BlockSpec((tq, LANES), lambda b, h, pr, qt, kt: (b * nq + kt[pr], 0)),
                pl.BlockSpec((hps * MLA_V, tq), lambda b, h, pr, qt, kt: (h, b * nq + kt[pr])),
            ],
            out_specs=pl.BlockSpec((tq, hps * MLA_V), lambda b, h, pr, qt, kt: (b * nq + qt[pr], h)),
            scratch_shapes=[
                pltpu.VMEM((hps, 1, tq), F32),
                pltpu.VMEM((hps, MLA_V + ATTN_SUM_ROWS, tq), F32),
            ],
        ),
        out_shape=jax.ShapeDtypeStruct((t, MLA_HEADS * MLA_V), BF16),
        compiler_params=_cp(("parallel", "parallel", "arbitrary")),
        name="mla_attn",
    )(qt, kt, qt_all, kn, kr, vt)


def _mix_out_kernel(x_ref, oa_ref, ob_ref, ga_ref, gb_ref, wga_ref, wmo_ref, wout_ref, nw_ref,
                    wr_hi_ref, wr_lo_ref, br_ref, x1_ref, h2_ref, sel_ref, cw_ref):
    ya = _dot(oa_ref[...], wga_ref[...])
    yb = _dot(ob_ref[...], wmo_ref[...])
    merged = (jax.nn.sigmoid(ga_ref[...].astype(F32)) * ya
              + jax.nn.sigmoid(gb_ref[...].astype(F32)) * yb)
    x1 = x_ref[...] + _dot(merged.astype(BF16), wout_ref[...])
    x1_ref[...] = x1
    h2 = _rms(x1, nw_ref[...])
    h_hi = h2.astype(BF16)
    h2_ref[...] = h_hi

    h_lo = (h2 - h_hi.astype(F32)).astype(BF16)
    logits = (_dot(h_hi, wr_hi_ref[...]) + _dot(h_hi, wr_lo_ref[...]) + _dot(h_lo, wr_hi_ref[...])
              + br_ref[...])
    lane = lax.broadcasted_iota(jnp.int32, logits.shape, 1)
    work = jnp.where(lane < N_EXPERTS, logits, -jnp.inf)
    sel = jnp.zeros(logits.shape, F32)
    cw = jnp.zeros(logits.shape, F32)
    top = None
    denom = None
    for kk in range(TOP_K):
        mx = jnp.max(work, axis=-1, keepdims=True)
        am = jnp.min(jnp.where(work == mx, lane, LANES), axis=-1, keepdims=True)
        hit = lane == am
        if kk == 0:
            top = mx
            e = jnp.ones_like(mx)
            denom = e
        else:
            e = jnp.exp(mx - top)
            denom = denom + e
        sel = jnp.where(hit, 1.0, sel)
        cw = jnp.where(hit, e, cw)
        work = jnp.where(hit, -jnp.inf, work)
    sel_ref[...] = sel.astype(BF16)
    cw_ref[...] = cw / denom


def _mix_out(x2, oa, ob, p, wga, wmo, wout, nw, wr_hi, wr_lo, br, tm=256):
    t, d = x2.shape
    full = lambda i: (0, 0)
    return pl.pallas_call(
        _mix_out_kernel,
        grid=(t // tm,),
        in_specs=[
            pl.BlockSpec((tm, d), lambda i: (i, 0)),
            pl.BlockSpec((tm, d), lambda i: (i, 0)),
            pl.BlockSpec((tm, d), lambda i: (i, 0)),
            pl.BlockSpec((tm, d), lambda i: (i, 4)),
            pl.BlockSpec((tm, d), lambda i: (i, 5)),
            pl.BlockSpec((d, d), full),
            pl.BlockSpec((d, d), full),
            pl.BlockSpec((d, d), full),
            pl.BlockSpec((1, d), full),
            pl.BlockSpec((d, LANES), full),
            pl.BlockSpec((d, LANES), full),
            pl.BlockSpec((1, LANES), full),
        ],
        out_specs=[
            pl.BlockSpec((tm, d), lambda i: (i, 0)),
            pl.BlockSpec((tm, d), lambda i: (i, 0)),
            pl.BlockSpec((tm, LANES), lambda i: (i, 0)),
            pl.BlockSpec((tm, LANES), lambda i: (i, 0)),
        ],
        out_shape=[
            jax.ShapeDtypeStruct((t, d), F32),
            jax.ShapeDtypeStruct((t, d), BF16),
            jax.ShapeDtypeStruct((t, LANES), BF16),
            jax.ShapeDtypeStruct((t, LANES), F32),
        ],
        compiler_params=_cp(("parallel",)),
        name="mix_out",
    )(x2, oa, ob, p, p, wga, wmo, wout, nw, wr_hi, wr_lo, br)


def _route_pos_kernel(sel_ref, lpos_ref, keyt_ref, offs_ref, cnt_ref, tot_ref, carry_ref):
    tm = sel_ref.shape[0]
    i = pl.program_id(0)

    @pl.when(i == 0)
    def _():
        carry_ref[...] = jnp.zeros_like(carry_ref)

    sel = sel_ref[...]
    row = lax.broadcasted_iota(jnp.int32, (tm, tm), 0)
    col = lax.broadcasted_iota(jnp.int32, (tm, tm), 1)
    before = (col < row).astype(BF16)
    lpos_ref[...] = _dot(before, sel)
    pos_t = _dot_tn(sel, (row < col).astype(BF16))
    sel_t = _dot_tn(sel, (row == col).astype(BF16))
    keyt_ref[0] = jnp.where(sel_t > 0.5, pos_t, -1.0)[0:N_EXPERTS, :]

    n = jnp.sum(sel.astype(F32), axis=0, keepdims=True)
    carry = carry_ref[0:1, :]
    offs_ref[0] = carry.astype(jnp.int32)
    cnt_ref[0] = n.astype(jnp.int32)
    total = carry + jnp.ceil(n * (1.0 / SEG_ALIGN)) * SEG_ALIGN
    carry_ref[...] = jnp.broadcast_to(total, carry_ref.shape)
    tot_ref[...] = jnp.broadcast_to(total, tot_ref.shape).astype(jnp.int32)


def _route_pos(sel):
    t = sel.shape[0]
    tm = ROUTE_TILE
    nt = t // tm
    return pl.pallas_call(
        _route_pos_kernel,
        grid=(nt,),
        in_specs=[pl.BlockSpec((tm, LANES), lambda i: (i, 0))],
        out_specs=[
            pl.BlockSpec((tm, LANES), lambda i: (i, 0)),
            pl.BlockSpec((1, N_EXPERTS, tm), lambda i: (i, 0, 0)),
            pl.BlockSpec((1, 1, LANES), lambda i: (i, 0, 0)),
            pl.BlockSpec((1, 1, LANES), lambda i: (i, 0, 0)),
            pl.BlockSpec((8, LANES), lambda i: (0, 0)),
        ],
        out_shape=[
            jax.ShapeDtypeStruct((t, LANES), F32),
            jax.ShapeDtypeStruct((nt, N_EXPERTS, tm), F32),
            jax.ShapeDtypeStruct((nt, 1, LANES), jnp.int32),
            jax.ShapeDtypeStruct((nt, 1, LANES), jnp.int32),
            jax.ShapeDtypeStruct((8, LANES), jnp.int32),
        ],
        scratch_shapes=[pltpu.VMEM((8, LANES), F32)],
        compiler_params=_cp(("arbitrary",)),
        name="route_pos",
    )(sel)


def _rows(ref, start, n):
    return ref.at[pl.ds(pl.multiple_of(start, n), n)]


def _seg_windows(cnt_ref, base):
    longest = lax.fori_loop(0, N_EXPERTS, lambda e, m: jnp.maximum(m, cnt_ref[base + e]), 0)
    return lax.shift_right_logical(longest + (SEG_WINDOW - 1), SEG_WINDOW.bit_length() - 1)


def _seg_copies(cnt_ref, seg_ref, base, win, e_lo, e_hi, make_copy, wait):
    first = win * SEG_WINDOW

    def per_expert(e, carry):
        rows = jnp.minimum(jnp.maximum(cnt_ref[base + e] - first, 0), SEG_WINDOW)
        n_chunks = lax.shift_right_logical(rows + (SEG_ALIGN - 1), SEG_ALIGN.bit_length() - 1)
        slot0 = seg_ref[base + e] + first
        stage0 = e * SEG_WINDOW

        def per_chunk(c, carry2):
            cp = make_copy(stage0 + c * SEG_ALIGN, slot0 + c * SEG_ALIGN)
            if wait:
                cp.wait()
            else:
                cp.start()
            return carry2

        return lax.fori_loop(0, n_chunks, per_chunk, carry)

    lax.fori_loop(e_lo, e_hi, per_expert, 0)


def _dispatch_kernel(seg_ref, cnt_ref, fill_lo_ref, fill_hi_ref, h_ref, keyt_ref, xs_ref,
                     stage_ref, zero_ref, sem):
    i = pl.program_id(0)
    tm = h_ref.shape[0]
    base = i * N_EXPERTS

    @pl.when(i == 0)
    def _():
        zero_ref[...] = jnp.zeros_like(zero_ref)

        def fill(c):
            return pltpu.make_async_copy(zero_ref.at[pl.ds(0, SEG_ALIGN)], _rows(xs_ref, c * SEG_ALIGN, SEG_ALIGN),
                                         sem.at[2])

        def per_expert(e, carry):
            lo, hi = fill_lo_ref[e] // SEG_ALIGN, fill_hi_ref[e] // SEG_ALIGN
            lax.fori_loop(lo, hi, lambda c, a: (fill(c).start(), a)[1], 0)
            lax.fori_loop(lo, hi, lambda c, a: (fill(c).wait(), a)[1], 0)
            return carry

        lax.fori_loop(0, N_EXPERTS, per_expert, 0)

        def fill_tail(c):
            return pltpu.make_async_copy(zero_ref, _rows(xs_ref, c * ZERO_ROWS, ZERO_ROWS), sem.at[2])

        lo = fill_hi_ref[N_EXPERTS - 1] // ZERO_ROWS
        hi = xs_ref.shape[0] // ZERO_ROWS
        lax.fori_loop(lo, hi, lambda c, a: (fill_tail(c).start(), a)[1], 0)
        lax.fori_loop(lo, hi, lambda c, a: (fill_tail(c).wait(), a)[1], 0)

    buf = i % 2

    def copier(b):
        def make_copy(stage_row, slot):
            return pltpu.make_async_copy(_rows(stage_ref.at[b], stage_row, SEG_ALIGN),
                                         _rows(xs_ref, slot, SEG_ALIGN), sem.at[b])
        return make_copy

    half = N_EXPERTS // 2
    j = lax.broadcasted_iota(jnp.int32, (SEG_WINDOW, tm), 0).astype(F32)

    def build_and_send(win):
        key = keyt_ref[0] - jnp.asarray(win * SEG_WINDOW, F32)
        for hf in range(2):
            pick = jnp.concatenate([(key[e:e + 1, :] == j).astype(BF16)
                                    for e in range(hf * half, (hf + 1) * half)], axis=0)
            stage_ref[buf, hf * half * SEG_WINDOW:(hf + 1) * half * SEG_WINDOW, :] = _dot(pick, h_ref[...])
            _seg_copies(cnt_ref, seg_ref, base, win, hf * half, (hf + 1) * half, copier(buf), wait=False)

    n_win = _seg_windows(cnt_ref, base)
    build_and_send(0)

    @pl.when(i > 0)
    def _():
        prev = base - N_EXPERTS
        last = jnp.maximum(_seg_windows(cnt_ref, prev) - 1, 0)
        _seg_copies(cnt_ref, seg_ref, prev, last, 0, N_EXPERTS, copier(1 - buf), wait=True)

    def more(win, carry):
        _seg_copies(cnt_ref, seg_ref, base, win - 1, 0, N_EXPERTS, copier(buf), wait=True)
        build_and_send(win)
        return carry

    lax.fori_loop(1, n_win, more, 0)

    @pl.when(i == pl.num_programs(0) - 1)
    def _():
        _seg_copies(cnt_ref, seg_ref, base, jnp.maximum(n_win - 1, 0), 0, N_EXPERTS, copier(buf), wait=True)


def _dispatch(seg, cnt, fill_lo, fill_hi, h2, keyt, n_pad):
    t, d = h2.shape
    tm = ROUTE_TILE
    return pl.pallas_call(
        _dispatch_kernel,
        grid_spec=pltpu.PrefetchScalarGridSpec(
            num_scalar_prefetch=4,
            grid=(t // tm,),
            in_specs=[
                pl.BlockSpec((tm, d), lambda i, *_: (i, 0)),
                pl.BlockSpec((1, N_EXPERTS, tm), lambda i, *_: (i, 0, 0)),
            ],
            out_specs=pl.BlockSpec(memory_space=pl.ANY),
            scratch_shapes=[pltpu.VMEM((2, N_EXPERTS * SEG_WINDOW, d), F32),
                            pltpu.VMEM((ZERO_ROWS, d), F32),
                            pltpu.SemaphoreType.DMA((3,))],
        ),
        out_shape=jax.ShapeDtypeStruct((n_pad, d), F32),
        compiler_params=_cp(("arbitrary",)),
        name="dispatch",
    )(seg, cnt, fill_lo, fill_hi, h2, keyt)


def _experts_kernel(be_ref, nv_ref, xs_ref, wgu_ref, wd_ref, bg_ref, bu_ref, bd_ref, ys_ref,
                    wg_s, wu_s, wd_s):
    j = pl.program_id(0)
    grp = 2 * LANES
    prev = be_ref[jnp.maximum(j - 1, 0)]

    @pl.when((j == 0) | (be_ref[j] != prev))
    def _():
        r = lax.broadcasted_iota(jnp.int32, (grp, grp), 0)
        c = lax.broadcasted_iota(jnp.int32, (grp, grp), 1)
        src = jnp.where(c < LANES, 2 * c, 2 * (c - LANES) + 1)
        pick = (r == src).astype(BF16)
        for g in range(wgu_ref.shape[2] // grp):
            y = _dot(wgu_ref[0, :, g * grp:(g + 1) * grp].astype(BF16), pick)
            wg_s[:, g * LANES:(g + 1) * LANES] = y[:, 0:LANES].astype(BF16)
            wu_s[:, g * LANES:(g + 1) * LANES] = y[:, LANES:grp].astype(BF16)
        wd_s[...] = wd_ref[0].astype(BF16)

    @pl.when(j < nv_ref[0])
    def _():
        x = xs_ref[...].astype(BF16)
        g = _dot(x, wg_s[...]) + bg_ref[0]
        u = _dot(x, wu_s[...]) + bu_ref[0]
        gate = jnp.minimum(g, SWIGLU_LIMIT)
        up = jnp.clip(u, -SWIGLU_LIMIT, SWIGLU_LIMIT)
        act = (up + 1.0) * (gate * jax.nn.sigmoid(gate * SWIGLU_ALPHA))
        ys_ref[...] = _dot(act.astype(BF16), wd_s[...]) + bd_ref[0]

    @pl.when(j >= nv_ref[0])
    def _():
        ys_ref[...] = jnp.zeros_like(ys_ref)


def _experts(block_e, n_valid, xs, wgu, wd, bg, bu, bd):
    n_pad = xs.shape[0]
    de, d = wd.shape[1:]
    blk = (MOE_ROWS, d)
    n_blocks = n_pad // MOE_ROWS
    xrow = lambda j, be, nv: (jnp.minimum(j, nv[0] - 1), 0)
    wsel = lambda j, be, nv: (be[j], 0, 0)
    return pl.pallas_call(
        _experts_kernel,
        grid_spec=pltpu.PrefetchScalarGridSpec(
            num_scalar_prefetch=2,
            grid=(n_blocks,),
            in_specs=[
                pl.BlockSpec(blk, xrow),
                pl.BlockSpec((1, d, 2 * de), wsel),
                pl.BlockSpec((1, de, d), wsel),
                pl.BlockSpec((1, 1, de), wsel),
                pl.BlockSpec((1, 1, de), wsel),
                pl.BlockSpec((1, 1, d), wsel),
            ],
            out_specs=pl.BlockSpec(blk, lambda j, be, nv: (j, 0)),
            scratch_shapes=[
                pltpu.VMEM((d, de), BF16),
                pltpu.VMEM((d, de), BF16),
                pltpu.VMEM((de, d), BF16),
            ],
        ),
        out_shape=jax.ShapeDtypeStruct(xs.shape, F32),
        compiler_params=pltpu.CompilerParams(dimension_semantics=("arbitrary",),
                                             vmem_limit_bytes=EXPERTS_VMEM_LIMIT),
        name="experts",
    )(block_e, n_valid, xs, wgu, wd, bg, bu, bd)


def _combine_kernel(seg_ref, cnt_ref, x1_ref, lpos_ref, cw_ref, nw_ref, ys_ref, o_ref, stage_ref, sem,
                    *, final_norm):
    i = pl.program_id(0)
    tm, d = x1_ref.shape
    base = i * N_EXPERTS

    @pl.when(i == 0)
    def _():
        stage_ref[...] = jnp.zeros_like(stage_ref)

    buf = i % 2

    def copier(b):
        def make_copy(stage_row, slot):
            return pltpu.make_async_copy(_rows(ys_ref, slot, SEG_ALIGN),
                                         _rows(stage_ref.at[b], stage_row, SEG_ALIGN), sem.at[b])
        return make_copy

    @pl.when(i == 0)
    def _():
        _seg_copies(cnt_ref, seg_ref, base, 0, 0, N_EXPERTS, copier(buf), wait=False)

    @pl.when(i + 1 < pl.num_programs(0))
    def _():
        _seg_copies(cnt_ref, seg_ref, base + N_EXPERTS, 0, 0, N_EXPERTS, copier(1 - buf), wait=False)

    n_stage = N_EXPERTS * SEG_WINDOW
    owner = lax.broadcasted_iota(jnp.int32, (LANES, n_stage), 1) // SEG_WINDOW
    expand = (owner == lax.broadcasted_iota(jnp.int32, (LANES, n_stage), 0)).astype(BF16)
    j = (lax.broadcasted_iota(jnp.int32, (tm, n_stage), 1) % SEG_WINDOW).astype(F32)
    cw_wide = _dot(cw_ref[...].astype(BF16), expand)

    def window(win, y):
        rank = (lpos_ref[...] - jnp.asarray(win * SEG_WINDOW, F32)).astype(BF16)
        take = jnp.where(_dot(rank, expand) == j, cw_wide, 0.0).astype(BF16)
        _seg_copies(cnt_ref, seg_ref, base, win, 0, N_EXPERTS, copier(buf), wait=True)
        return y + _dot(take, stage_ref[buf].astype(BF16))

    def more(win, y):
        _seg_copies(cnt_ref, seg_ref, base, win, 0, N_EXPERTS, copier(buf), wait=False)
        return window(win, y)

    y = window(0, jnp.zeros((tm, d), F32))
    y = lax.fori_loop(1, _seg_windows(cnt_ref, base), more, y)
    out = x1_ref[...] + y
    if final_norm:
        out = _rms(out, nw_ref[...])
    o_ref[...] = out


def _combine(seg, cnt, x1, lpos, cw, nw, ys, final_norm):
    t, d = x1.shape
    tm = ROUTE_TILE
    kern = functools.partial(_combine_kernel, final_norm=final_norm)
    return pl.pallas_call(
        kern,
        grid_spec=pltpu.PrefetchScalarGridSpec(
            num_scalar_prefetch=2,
            grid=(t // tm,),
            in_specs=[
                pl.BlockSpec((tm, d), lambda i, *_: (i, 0)),
                pl.BlockSpec((tm, LANES), lambda i, *_: (i, 0)),
                pl.BlockSpec((tm, LANES), lambda i, *_: (i, 0)),
                pl.BlockSpec((1, d), lambda i, *_: (0, 0)),
                pl.BlockSpec(memory_space=pl.ANY),
            ],
            out_specs=pl.BlockSpec((tm, d), lambda i, *_: (i, 0)),
            scratch_shapes=[pltpu.VMEM((2, N_EXPERTS * SEG_WINDOW, d), F32), pltpu.SemaphoreType.DMA((2,))],
        ),
        out_shape=jax.ShapeDtypeStruct((t, d), F32),
        compiler_params=_cp(("arbitrary",)),
        name="combine",
    )(seg, cnt, x1, lpos, cw, nw, ys)


def _rope_tables(seq):
    half = MLA_ROPE // 2
    inv = 1.0 / (ROPE_THETA ** (jnp.arange(0, MLA_ROPE, 2, dtype=F32) / MLA_ROPE))
    ang = jnp.arange(seq, dtype=F32)[:, None] * inv[None, :]
    cos, sin = jnp.cos(ang), jnp.sin(ang)
    zeros = jnp.zeros((seq, LANES - MLA_ROPE), F32)
    cos_row = jnp.concatenate([cos, cos, zeros], axis=-1)
    sin_row = jnp.concatenate([-sin, sin, zeros], axis=-1)
    del half
    return cos_row, sin_row, cos.T, sin.T


def _pad_cols(a, width):
    return jnp.pad(a, ((0, 0), (0, width - a.shape[1])))


def _layer(x2, batch, seq, final_norm_w, final_norm, rope_tables,
           norm_mix_w, w_in, gdn_conv_w, gdn_a_log, gdn_dt_bias, gdn_norm_w, w_gdn_o,
           mla_q_norm_w, w_mla_q_b, mla_kv_norm_w, w_mla_kv_b, w_mla_o, w_out,
           norm_ffn_w, w_router, b_router, w_gate_up, b_gate_up, w_down, b_down):
    t, d = x2.shape
    qk_w = GDN_HEADS * GDN_D
    o_b = 4 * qk_w
    o_a = o_b + GDN_HEADS
    o_cq = o_a + GDN_HEADS
    o_ckv = o_cq + MLA_Q_LORA
    o_kr = o_ckv + MLA_KV_LORA
    o_ga = o_kr + MLA_ROPE
    o_gb = o_ga + d
    w_p = jnp.concatenate([
        w_in[:, 0:o_b], w_in[:, o_ga:o_gb + d], w_in[:, o_cq:o_ckv], w_in[:, o_ckv:o_kr],
        _pad_cols(w_in[:, o_kr:o_ga], 2 * LANES)], axis=1).astype(BF16)
    w_ab = _pad_cols(jnp.concatenate([w_in[:, o_a:o_cq], w_in[:, o_b:o_a]], axis=1), LANES).astype(BF16)

    p, ab = _in_proj(x2, norm_mix_w[None, :], w_p, w_ab)

    alog_row = _pad_cols(gdn_a_log[None, :].astype(F32), LANES)
    dtb_row = _pad_cols(gdn_dt_bias[None, :].astype(F32), LANES)
    qkvn, cols, gct = _gdn_prep(p, ab, gdn_conv_w.astype(F32), alog_row, dtb_row, seq)
    o_gdn = _gdn_chunk(qkvn, p, cols, gct, gdn_norm_w[None, :].astype(F32), batch, seq)

    hd = MLA_NOPE + MLA_ROPE
    wq = w_mla_q_b.reshape(MLA_Q_LORA, MLA_HEADS, hd)
    wqt = jnp.pad(wq, ((0, 0), (0, 0), (0, 2 * LANES - hd))).reshape(MLA_Q_LORA, MLA_HEADS * 2 * LANES).T
    wkv = w_mla_kv_b.reshape(MLA_KV_LORA, MLA_HEADS, MLA_NOPE + MLA_V)
    wkn = wkv[:, :, :MLA_NOPE].reshape(MLA_KV_LORA, -1)
    wvt = wkv[:, :, MLA_NOPE:].reshape(MLA_KV_LORA, -1).T
    qt, kn, kr, vt = _mla_prep(p, rope_tables, mla_q_norm_w[None, :].astype(F32),
                               mla_kv_norm_w[None, :].astype(F32), wqt.astype(BF16), wkn.astype(BF16),
                               wvt.astype(BF16), seq)
    o_mla = _mla_attn(qt, kn, kr, vt, batch, seq)

    wr = _pad_cols(w_router.astype(F32), LANES)
    wr_hi = wr.astype(BF16)
    wr_lo = (wr - wr_hi.astype(F32)).astype(BF16)
    br = _pad_cols(b_router[None, :].astype(F32), LANES)
    x1, h2, sel, cw = _mix_out(x2, o_gdn, o_mla, p, w_gdn_o.astype(BF16), w_mla_o.astype(BF16),
                               w_out.astype(BF16), norm_ffn_w[None, :].astype(F32), wr_hi, wr_lo, br)

    lpos, keyt, offs, cnt, tot = _route_pos(sel)
    n_tiles = t // ROUTE_TILE
    used = tot[0, :N_EXPERTS]
    padded = (used + MOE_ROWS - 1) // MOE_ROWS * MOE_ROWS
    pad_end = jnp.cumsum(padded)
    pad_start = pad_end - padded
    seg = (pad_start[None, :] + offs[:, 0, :N_EXPERTS]).astype(jnp.int32).reshape(-1)
    cnt = cnt[:, 0, :N_EXPERTS].reshape(-1)
    worst_used = t * TOP_K + n_tiles * N_EXPERTS * (SEG_ALIGN - 1)
    n_pad = -(-worst_used // MOE_ROWS) * MOE_ROWS + N_EXPERTS * MOE_ROWS
    n_blocks = n_pad // MOE_ROWS
    blk_start = jnp.arange(n_blocks, dtype=jnp.int32) * MOE_ROWS
    block_e = jnp.minimum(jnp.sum((pad_end[None, :] <= blk_start[:, None]).astype(jnp.int32), axis=1),
                          N_EXPERTS - 1).astype(jnp.int32)
    n_valid = (pad_end[-1:] // MOE_ROWS).astype(jnp.int32)

    xs = _dispatch(seg, cnt, (pad_start + used).astype(jnp.int32), pad_end.astype(jnp.int32), h2, keyt, n_pad)
    bg = b_gate_up[:, None, 0::2].astype(F32)
    bu = b_gate_up[:, None, 1::2].astype(F32)
    ys = _experts(block_e, n_valid, xs, w_gate_up, w_down, bg, bu, b_down[:, None, :].astype(F32))
    return _combine(seg, cnt, x1, lpos, cw, final_norm_w[None, :].astype(F32), ys, final_norm)


def kernel(x, norm_mix_w, w_in, gdn_conv_w, gdn_a_log, gdn_dt_bias, gdn_norm_w, w_gdn_o, mla_q_norm_w, w_mla_q_b, mla_kv_norm_w, w_mla_kv_b, w_mla_o, w_out, norm_ffn_w, w_router, b_router, w_gate_up, b_gate_up, w_down, b_down, norm_final_w):
    batch, seq, d = x.shape
    depth = w_in.shape[0]
    rope_tables = _rope_tables(seq)
    x2 = x.reshape(batch * seq, d)
    for layer in range(depth):
        x2 = _layer(x2, batch, seq, norm_final_w, layer == depth - 1, rope_tables,
                    norm_mix_w[layer], w_in[layer], gdn_conv_w[layer], gdn_a_log[layer],
                    gdn_dt_bias[layer], gdn_norm_w[layer], w_gdn_o[layer], mla_q_norm_w[layer],
                    w_mla_q_b[layer], mla_kv_norm_w[layer], w_mla_kv_b[layer], w_mla_o[layer],
                    w_out[layer], norm_ffn_w[layer], w_router[layer], b_router[layer],
                    w_gate_up[layer], b_gate_up[layer], w_down[layer], b_down[layer])
    return x2.reshape(batch, seq, d)
```

```python
import functools

import jax
import jax.numpy as jnp
import numpy as np
from jax import lax
from jax.experimental import pallas as pl
from jax.experimental.pallas import tpu as pltpu

F32 = jnp.float32
BF16 = jnp.bfloat16

CHUNK = 64
NORM_EPS = 1e-6
GDN_HEADS = 8
GDN_D = 128
GDN_CONV = 4
MLA_HEADS = 8
MLA_Q_LORA = 512
MLA_KV_LORA = 256
MLA_NOPE = 128
MLA_ROPE = 64
MLA_V = 128
ROPE_THETA = 10000.0
N_EXPERTS = 32
TOP_K = 4
SWIGLU_LIMIT = 7.0
SWIGLU_ALPHA = 1.702

LANES = 128
MOE_ROWS = 512
ROUTE_TILE = 256
SEG_ALIGN = 8
SEG_WINDOW = 64
ZERO_ROWS = 64
VMEM_LIMIT = 48 * 1024 * 1024
EXPERTS_VMEM_LIMIT = 56 * 1024 * 1024

NEG_BIG = -1e30
LOG2_E = 1.4426950408889634


def _cp(sem):
    return pltpu.CompilerParams(dimension_semantics=sem, vmem_limit_bytes=VMEM_LIMIT)


def _dot(a, b):
    return jnp.dot(a, b, preferred_element_type=F32)


def _dot_nt(a, b):
    return lax.dot_general(a, b, (((1,), (1,)), ((), ())), preferred_element_type=F32)


def _dot_tn(a, b):
    return lax.dot_general(a, b, (((0,), (0,)), ((), ())), preferred_element_type=F32)


def _split3(x):
    hi = x.astype(BF16)
    r = x - hi.astype(F32)
    mid = r.astype(BF16)
    lo = (r - mid.astype(F32)).astype(BF16)
    return hi, mid, lo


def _rms(x, w):
    ms = jnp.mean(x * x, axis=-1, keepdims=True)
    return x * lax.rsqrt(ms + NORM_EPS) * w


def _in_proj_kernel(x_ref, nw_ref, w_ref, wab_ref, p_ref, ab_ref, h_ref):
    @pl.when(pl.program_id(1) == 0)
    def _():
        hb = _rms(x_ref[...], nw_ref[...]).astype(BF16)
        h_ref[...] = hb
        ab_ref[...] = _dot(hb, wab_ref[...])

    p_ref[...] = _dot(h_ref[...], w_ref[...]).astype(BF16)


def _in_proj(x2, norm_w, w_p, w_ab, tm=1024, tn=3584):
    t, d = x2.shape
    n = w_p.shape[1]
    return pl.pallas_call(
        _in_proj_kernel,
        grid=(t // tm, n // tn),
        in_specs=[
            pl.BlockSpec((tm, d), lambda i, j: (i, 0)),
            pl.BlockSpec((1, d), lambda i, j: (0, 0)),
            pl.BlockSpec((d, tn), lambda i, j: (0, j)),
            pl.BlockSpec((d, LANES), lambda i, j: (0, 0)),
        ],
        out_specs=[
            pl.BlockSpec((tm, tn), lambda i, j: (i, j)),
            pl.BlockSpec((tm, LANES), lambda i, j: (i, 0)),
        ],
        out_shape=[
            jax.ShapeDtypeStruct((t, n), BF16),
            jax.ShapeDtypeStruct((t, LANES), F32),
        ],
        scratch_shapes=[pltpu.VMEM((tm, d), BF16)],
        compiler_params=_cp(("parallel", "arbitrary")),
        name="in_proj",
    )(x2, norm_w, w_p, w_ab)


def _gdn_prep_kernel(cur_ref, prev_ref, ab_ref, cw_ref, alog_ref, dtb_ref,
                     qkv_ref, cols_ref, gct_ref, *, tiles_per_seq):
    tm = cur_ref.shape[0]
    i = pl.program_id(0)
    halo_on = (i % tiles_per_seq) != 0
    n_blk = cur_ref.shape[1] // LANES
    q_scale = GDN_D ** -0.5
    for cb in range(n_blk):
        cs = slice(cb * LANES, (cb + 1) * LANES)
        cur = cur_ref[:, cs].astype(F32)
        halo = prev_ref[:, cs].astype(F32)[8:16, :]
        halo = jnp.where(halo_on, halo, 0.0)
        xe = jnp.concatenate([halo, cur], axis=0)
        w = cw_ref[:, cs]
        y = w[0:1, :] * xe[5:5 + tm, :]
        for j in range(1, GDN_CONV):
            y = y + w[j:j + 1, :] * xe[5 + j:5 + j + tm, :]
        y = y * jax.nn.sigmoid(y)
        if cb < 2 * GDN_HEADS:
            ss = jnp.sum(y * y, axis=-1, keepdims=True)
            y = y * lax.rsqrt(ss + NORM_EPS)
            if cb < GDN_HEADS:
                y = y * q_scale
        qkv_ref[:, cs] = y.astype(BF16)

    ab = ab_ref[...]
    g = -jnp.exp(alog_ref[...]) * jax.nn.softplus(ab + dtb_ref[...])
    row = lax.broadcasted_iota(jnp.int32, (tm, tm), 0)
    col = lax.broadcasted_iota(jnp.int32, (tm, tm), 1)
    tri = ((col <= row) & ((row // CHUNK) == (col // CHUNK))).astype(BF16)
    g_hi, g_mid, g_lo = _split3(g)
    gc = _dot(tri, g_hi) + _dot(tri, g_mid) + _dot(tri, g_lo)
    lane = lax.broadcasted_iota(jnp.int32, (tm, LANES), 1)
    cols_ref[...] = jnp.where(lane < GDN_HEADS, gc, jax.nn.sigmoid(ab))
    for c in range(tm // CHUNK):
        blk = gc[c * CHUNK:(c + 1) * CHUNK, :]
        blk = jnp.concatenate([blk, jnp.zeros_like(blk)], axis=0)
        gct_ref[c] = blk.T[0:GDN_HEADS, 0:CHUNK]


def _gdn_prep(p, ab, conv_w, alog_row, dtb_row, seq, tm=256):
    t = p.shape[0]
    cw = 3 * GDN_HEADS * GDN_D
    tiles_per_seq = seq // tm
    kern = functools.partial(_gdn_prep_kernel, tiles_per_seq=tiles_per_seq)
    return pl.pallas_call(
        kern,
        grid=(t // tm,),
        in_specs=[
            pl.BlockSpec((tm, cw), lambda i: (i, 0)),
            pl.BlockSpec((16, cw), lambda i: (jnp.maximum(i * (tm // 16) - 1, 0), 0)),
            pl.BlockSpec((tm, LANES), lambda i: (i, 0)),
            pl.BlockSpec((GDN_CONV, cw), lambda i: (0, 0)),
            pl.BlockSpec((1, LANES), lambda i: (0, 0)),
            pl.BlockSpec((1, LANES), lambda i: (0, 0)),
        ],
        out_specs=[
            pl.BlockSpec((tm, cw), lambda i: (i, 0)),
            pl.BlockSpec((tm, LANES), lambda i: (i, 0)),
            pl.BlockSpec((tm // CHUNK, GDN_HEADS, CHUNK), lambda i: (i, 0, 0)),
        ],
        out_shape=[
            jax.ShapeDtypeStruct((t, cw), BF16),
            jax.ShapeDtypeStruct((t, LANES), F32),
            jax.ShapeDtypeStruct((t // CHUNK, GDN_HEADS, CHUNK), F32),
        ],
        compiler_params=_cp(("parallel",)),
        name="gdn_prep",
    )(p, p, ab, conv_w, alog_row, dtb_row)


def _gdn_chunk_kernel(q_ref, k_ref, v_ref, z_ref, cols_ref, gct_ref, nw_ref, o_ref, s_ref):
    c = CHUNK
    nb = q_ref.shape[0]
    units = [(b, h) for b in range(nb) for h in range(GDN_HEADS)]

    @pl.when(pl.program_id(0) == 0)
    def _():
        s_ref[...] = jnp.zeros_like(s_ref)

    ri = lax.broadcasted_iota(jnp.int32, (c, c), 0)
    ci = lax.broadcasted_iota(jnp.int32, (c, c), 1)
    incl = ri >= ci
    strict = ri > ci
    eye = (ri == ci).astype(F32)
    nw = nw_ref[...]

    cols, e_g, e_kd, e_last, gct = [], [], [], [], []
    for b in range(nb):
        cb = cols_ref[b]
        last = cb[c - 1:c, :]
        cols.append(cb)
        e_g.append(jnp.exp(cb))
        e_kd.append(jnp.exp(last - cb))
        e_last.append(jnp.exp(last))
        gct.append(gct_ref[b, 0])

    kq, kb_l, kf_l = [], [], []
    for b, h in units:
        hs = slice(h * GDN_D, (h + 1) * GDN_D)
        k = k_ref[b, :, hs]
        kf = k.astype(F32)
        kb = kf * cols[b][:, GDN_HEADS + h:GDN_HEADS + h + 1]
        kq.append(_dot_nt(jnp.concatenate([kb.astype(BF16), q_ref[b, :, hs]], axis=0), k))
        kb_l.append(kb)
        kf_l.append(kf)

    a_l, qk_l = [], []
    for i, (b, h) in enumerate(units):
        dec = jnp.exp(jnp.minimum(cols[b][:, h:h + 1] - gct[b][h:h + 1, :], 0.0))
        a_l.append(jnp.where(strict, -kq[i][0:c, :] * dec, 0.0))
        qk_l.append(jnp.where(incl, kq[i][c:2 * c, :] * dec, 0.0).astype(BF16))

    tinv = [eye + a for a in a_l]
    pw = a_l
    for _ in range(5):
        pwb = [x.astype(BF16) for x in pw]
        pw = [_dot(x, x) for x in pwb]
        tinv = [t + _dot(t.astype(BF16), x.astype(BF16)) for t, x in zip(tinv, pw)]

    uw = []
    for i, (b, h) in enumerate(units):
        hs = slice(h * GDN_D, (h + 1) * GDN_D)
        beta = cols[b][:, GDN_HEADS + h:GDN_HEADS + h + 1]
        rhs = jnp.concatenate([v_ref[b, :, hs].astype(F32) * beta,
                               kb_l[i] * e_g[b][:, h:h + 1]], axis=1).astype(BF16)
        uw.append(_dot(tinv[i].astype(BF16), rhs))

    r_l = []
    for i, (b, h) in enumerate(units):
        hs = slice(h * GDN_D, (h + 1) * GDN_D)
        qd = (q_ref[b, :, hs].astype(F32) * e_g[b][:, h:h + 1]).astype(BF16)
        lhs = jnp.concatenate([uw[i][:, GDN_D:2 * GDN_D].astype(BF16), qd], axis=0)
        r_l.append(_dot(lhs, s_ref[b * GDN_HEADS + h].astype(BF16)))

    for i, (b, h) in enumerate(units):
        hs = slice(h * GDN_D, (h + 1) * GDN_D)
        v_new = (uw[i][:, 0:GDN_D] - r_l[i][0:c, :]).astype(BF16)
        o = r_l[i][c:2 * c, :] + _dot(qk_l[i], v_new)
        kd = (kf_l[i] * e_kd[b][:, h:h + 1]).astype(BF16)
        u = b * GDN_HEADS + h
        s_ref[u] = s_ref[u] * e_last[b][:, h:h + 1] + _dot_tn(kd, v_new)
        z = z_ref[b, :, hs].astype(F32)
        o_ref[b, :, hs] = (_rms(o, nw) * (z * jax.nn.sigmoid(z))).astype(BF16)


def _gdn_chunk(qkvn, p, cols, gct, norm_w, batch, seq):
    nc = seq // CHUNK
    hw = GDN_HEADS * GDN_D
    qkvn3 = qkvn.reshape(batch, seq, qkvn.shape[1])
    p3 = p.reshape(batch, seq, p.shape[1])
    cols3 = cols.reshape(batch, seq, LANES)
    gct4 = gct.reshape(batch, nc, GDN_HEADS, CHUNK)
    tile = lambda col: pl.BlockSpec((batch, CHUNK, hw), lambda c: (0, c, col))
    out = pl.pallas_call(
        _gdn_chunk_kernel,
        grid=(nc,),
        in_specs=[
            tile(0), tile(1), tile(2),
            tile(3),
            pl.BlockSpec((batch, CHUNK, LANES), lambda c: (0, c, 0)),
            pl.BlockSpec((batch, 1, GDN_HEADS, CHUNK), lambda c: (0, c, 0, 0)),
            pl.BlockSpec((1, GDN_D), lambda c: (0, 0)),
        ],
        out_specs=pl.BlockSpec((batch, CHUNK, hw), lambda c: (0, c, 0)),
        out_shape=jax.ShapeDtypeStruct((batch, seq, hw), BF16),
        scratch_shapes=[pltpu.VMEM((batch * GDN_HEADS, GDN_D, GDN_D), F32)],
        compiler_params=_cp(("arbitrary",)),
        name="gdn_chunk",
    )(qkvn3, qkvn3, qkvn3, p3, cols3, gct4, norm_w)
    return out.reshape(batch * seq, hw)


def _rope(x, cos, sin_signed):
    lane = lax.broadcasted_iota(jnp.int32, x.shape, 1)
    fwd = pltpu.roll(x, LANES - MLA_ROPE // 2, 1)
    bwd = pltpu.roll(x, MLA_ROPE // 2, 1)
    rot = jnp.where(lane < MLA_ROPE // 2, fwd, bwd)
    return x * cos + rot * sin_signed


def _mla_prep_kernel(cq_ref, ckv_ref, kr_ref, cos_ref, sin_ref, cost_ref, sint_ref, qnw_ref, kvnw_ref,
                     wqt_ref, wkn_ref, wvt_ref, qt_ref, kn_ref, kro_ref, vt_ref):
    cos = cos_ref[...]
    sin = sin_ref[...]
    cos_t = cost_ref[...]
    sin_t = sint_ref[...]
    cq = _rms(cq_ref[...].astype(F32), qnw_ref[...]).astype(BF16)
    hd = 2 * LANES
    half = MLA_ROPE // 2
    scale = (MLA_NOPE + MLA_ROPE) ** -0.5 * LOG2_E
    for h in range(MLA_HEADS):
        qh = _dot_nt(wqt_ref[h * hd:(h + 1) * hd, :], cq) * scale
        lo = qh[MLA_NOPE:MLA_NOPE + half, :]
        hi = qh[MLA_NOPE + half:MLA_NOPE + MLA_ROPE, :]
        qt_ref[h * hd:h * hd + MLA_NOPE, :] = qh[0:MLA_NOPE, :].astype(BF16)
        qt_ref[h * hd + MLA_NOPE:h * hd + MLA_NOPE + half, :] = (lo * cos_t - hi * sin_t).astype(BF16)
        qt_ref[h * hd + MLA_NOPE + half:h * hd + MLA_NOPE + MLA_ROPE, :] = (hi * cos_t + lo * sin_t).astype(BF16)
        qt_ref[h * hd + MLA_NOPE + MLA_ROPE:(h + 1) * hd, :] = qh[MLA_NOPE + MLA_ROPE:hd, :].astype(BF16)
    kvl = _rms(ckv_ref[...].astype(F32), kvnw_ref[...]).astype(BF16)
    kn_ref[...] = _dot(kvl, wkn_ref[...]).astype(BF16)
    vt_ref[...] = _dot_nt(wvt_ref[...], kvl).astype(BF16)
    kro_ref[...] = _rope(kr_ref[...].astype(F32), cos, sin).astype(BF16)


def _mla_prep(p, tables, qnw, kvnw, wqt, wkn, wvt, seq, tm=512):
    cos_row, sin_row, cos_col, sin_col = tables
    t = p.shape[0]
    tiles_per_seq = seq // tm
    hw = MLA_HEADS * MLA_NOPE
    half = MLA_ROPE // 2
    cq_blk = 6144 // MLA_Q_LORA
    ckv_blk = 6656 // MLA_KV_LORA
    kr_blk = 6912 // LANES
    return pl.pallas_call(
        _mla_prep_kernel,
        grid=(t // tm,),
        in_specs=[
            pl.BlockSpec((tm, MLA_Q_LORA), lambda i: (i, cq_blk)),
            pl.BlockSpec((tm, MLA_KV_LORA), lambda i: (i, ckv_blk)),
            pl.BlockSpec((tm, LANES), lambda i: (i, kr_blk)),
            pl.BlockSpec((tm, LANES), lambda i: (i % tiles_per_seq, 0)),
            pl.BlockSpec((tm, LANES), lambda i: (i % tiles_per_seq, 0)),
            pl.BlockSpec((half, tm), lambda i: (0, i % tiles_per_seq)),
            pl.BlockSpec((half, tm), lambda i: (0, i % tiles_per_seq)),
            pl.BlockSpec((1, MLA_Q_LORA), lambda i: (0, 0)),
            pl.BlockSpec((1, MLA_KV_LORA), lambda i: (0, 0)),
            pl.BlockSpec((2 * hw, MLA_Q_LORA), lambda i: (0, 0)),
            pl.BlockSpec((MLA_KV_LORA, hw), lambda i: (0, 0)),
            pl.BlockSpec((hw, MLA_KV_LORA), lambda i: (0, 0)),
        ],
        out_specs=[
            pl.BlockSpec((2 * hw, tm), lambda i: (0, i)),
            pl.BlockSpec((tm, hw), lambda i: (i, 0)),
            pl.BlockSpec((tm, LANES), lambda i: (i, 0)),
            pl.BlockSpec((hw, tm), lambda i: (0, i)),
        ],
        out_shape=[
            jax.ShapeDtypeStruct((2 * hw, t), BF16),
            jax.ShapeDtypeStruct((t, hw), BF16),
            jax.ShapeDtypeStruct((t, LANES), BF16),
            jax.ShapeDtypeStruct((hw, t), BF16),
        ],
        compiler_params=_cp(("parallel",)),
        name="mla_prep",
    )(p, p, p, cos_row, sin_row, cos_col, sin_col, qnw, kvnw, wqt, wkn, wvt)


ATTN_HEADS_PER_STEP = 8
ATTN_SUM_ROWS = 16


def _mla_attn_kernel(qt_ref, kt_ref, q_ref, kn_ref, kr_ref, vt_ref, o_ref, m_ref, acc_ref):
    qi = qt_ref[pl.program_id(2)]
    ki = kt_ref[pl.program_id(2)]
    tq = q_ref.shape[1]
    tk = kn_ref.shape[0]
    hd = 2 * LANES

    @pl.when(ki == 0)
    def _():
        m_ref[...] = jnp.full_like(m_ref, NEG_BIG)
        acc_ref[...] = jnp.zeros_like(acc_ref)

    def step(masked):
        kr = kr_ref[...]
        ones = jnp.ones((ATTN_SUM_ROWS, tk), BF16)

        def scores(h):
            k = jnp.concatenate([kn_ref[:, h * MLA_NOPE:(h + 1) * MLA_NOPE], kr], axis=1)
            s = _dot(k, q_ref[h * hd:(h + 1) * hd, :])
            if masked:
                ck = lax.broadcasted_iota(jnp.int32, (tk, tq), 0) // CHUNK
                cq = lax.broadcasted_iota(jnp.int32, (tk, tq), 1) // CHUNK
                s = jnp.where(ck <= cq, s, NEG_BIG)
            return s

        def update(h, s):
            m_prev = m_ref[h]
            m_new = jnp.maximum(m_prev, jnp.max(s, axis=0, keepdims=True))
            alpha = jnp.exp2(m_prev - m_new)
            p = jnp.exp2((s - m_new).astype(BF16))
            v_ext = jnp.concatenate([vt_ref[h * MLA_V:(h + 1) * MLA_V, :], ones], axis=0)
            acc_ref[h] = alpha * acc_ref[h] + _dot(v_ext, p)
            m_ref[h] = m_new

        s_prev = scores(0)
        for h in range(1, ATTN_HEADS_PER_STEP):
            s_next = scores(h)
            update(h - 1, s_prev)
            s_prev = s_next
        update(ATTN_HEADS_PER_STEP - 1, s_prev)

    @pl.when(ki < qi)
    def _():
        step(False)

    @pl.when(ki == qi)
    def _():
        step(True)
        for h in range(ATTN_HEADS_PER_STEP):
            acc = acc_ref[h]
            o = acc[0:MLA_V, :] / acc[MLA_V:MLA_V + 1, :]
            o_ref[:, h * MLA_V:(h + 1) * MLA_V] = o.T.astype(BF16)


def _mla_attn(qt_all, kn, kr, vt, batch, seq, tq=512):
    t = kn.shape[0]
    nq = seq // tq
    hps = ATTN_HEADS_PER_STEP
    pairs = [(qi, ki) for qi in range(nq) for ki in range(qi + 1)]
    qt = jnp.asarray(np.array([pr[0] for pr in pairs], np.int32))
    kt = jnp.asarray(np.array([pr[1] for pr in pairs], np.int32))
    return pl.pallas_call(
        _mla_attn_kernel,
        grid_spec=pltpu.PrefetchScalarGridSpec(
            num_scalar_prefetch=2,
            grid=(batch, MLA_HEADS // hps, len(pairs)),
            in_specs=[
                pl.BlockSpec((hps * 2 * LANES, tq), lambda b, h, pr, qt, kt: (h, b * nq + qt[pr])),
                pl.BlockSpec((tq, hps * MLA_NOPE), lambda b, h, pr, qt, kt: (b * nq + kt[pr], h)),
                pl.BlockSpec((tq, LANES), lambda b, h, pr, qt, kt: (b * nq + kt[pr], 0)),
                pl.BlockSpec((hps * MLA_V, tq), lambda b, h, pr, qt, kt: (h, b * nq + kt[pr])),
            ],
            out_specs=pl.BlockSpec((tq, hps * MLA_V), lambda b, h, pr, qt, kt: (b * nq + qt[pr], h)),
            scratch_shapes=[
                pltpu.VMEM((hps, 1, tq), F32),
                pltpu.VMEM((hps, MLA_V + ATTN_SUM_ROWS, tq), F32),
            ],
        ),
        out_shape=jax.ShapeDtypeStruct((t, MLA_HEADS * MLA_V), BF16),
        compiler_params=_cp(("parallel", "parallel", "arbitrary")),
        name="mla_attn",
    )(qt, kt, qt_all, kn, kr, vt)


def _mix_out_kernel(x_ref, oa_ref, ob_ref, ga_ref, gb_ref, wga_ref, wmo_ref, wout_ref, nw_ref,
                    wr_hi_ref, wr_lo_ref, br_ref, x1_ref, h2_ref, sel_ref, cw_ref):
    ya = _dot(oa_ref[...], wga_ref[...])
    yb = _dot(ob_ref[...], wmo_ref[...])
    merged = (jax.nn.sigmoid(ga_ref[...].astype(F32)) * ya
              + jax.nn.sigmoid(gb_ref[...].astype(F32)) * yb)
    x1 = x_ref[...] + _dot(merged.astype(BF16), wout_ref[...])
    x1_ref[...] = x1
    h2 = _rms(x1, nw_ref[...])
    h_hi = h2.astype(BF16)
    h2_ref[...] = h_hi

    h_lo = (h2 - h_hi.astype(F32)).astype(BF16)
    logits = (_dot(h_hi, wr_hi_ref[...]) + _dot(h_hi, wr_lo_ref[...]) + _dot(h_lo, wr_hi_ref[...])
              + br_ref[...])
    lane = lax.broadcasted_iota(jnp.int32, logits.shape, 1)
    work = jnp.where(lane < N_EXPERTS, logits, -jnp.inf)
    sel = jnp.zeros(logits.shape, F32)
    cw = jnp.zeros(logits.shape, F32)
    top = None
    denom = None
    for kk in range(TOP_K):
        mx = jnp.max(work, axis=-1, keepdims=True)
        am = jnp.min(jnp.where(work == mx, lane, LANES), axis=-1, keepdims=True)
        hit = lane == am
        if kk == 0:
            top = mx
            e = jnp.ones_like(mx)
            denom = e
        else:
            e = jnp.exp(mx - top)
            denom = denom + e
        sel = jnp.where(hit, 1.0, sel)
        cw = jnp.where(hit, e, cw)
        work = jnp.where(hit, -jnp.inf, work)
    sel_ref[...] = sel.astype(BF16)
    cw_ref[...] = cw / denom


def _mix_out(x2, oa, ob, p, wga, wmo, wout, nw, wr_hi, wr_lo, br, tm=256):
    t, d = x2.shape
    full = lambda i: (0, 0)
    return pl.pallas_call(
        _mix_out_kernel,
        grid=(t // tm,),
        in_specs=[
            pl.BlockSpec((tm, d), lambda i: (i, 0)),
            pl.BlockSpec((tm, d), lambda i: (i, 0)),
            pl.BlockSpec((tm, d), lambda i: (i, 0)),
            pl.BlockSpec((tm, d), lambda i: (i, 4)),
            pl.BlockSpec((tm, d), lambda i: (i, 5)),
            pl.BlockSpec((d, d), full),
            pl.BlockSpec((d, d), full),
            pl.BlockSpec((d, d), full),
            pl.BlockSpec((1, d), full),
            pl.BlockSpec((d, LANES), full),
            pl.BlockSpec((d, LANES), full),
            pl.BlockSpec((1, LANES), full),
        ],
        out_specs=[
            pl.BlockSpec((tm, d), lambda i: (i, 0)),
            pl.BlockSpec((tm, d), lambda i: (i, 0)),
            pl.BlockSpec((tm, LANES), lambda i: (i, 0)),
            pl.BlockSpec((tm, LANES), lambda i: (i, 0)),
        ],
        out_shape=[
            jax.ShapeDtypeStruct((t, d), F32),
            jax.ShapeDtypeStruct((t, d), BF16),
            jax.ShapeDtypeStruct((t, LANES), BF16),
            jax.ShapeDtypeStruct((t, LANES), F32),
        ],
        compiler_params=_cp(("parallel",)),
        name="mix_out",
    )(x2, oa, ob, p, p, wga, wmo, wout, nw, wr_hi, wr_lo, br)


def _route_pos_kernel(sel_ref, lpos_ref, keyt_ref, offs_ref, cnt_ref, tot_ref, carry_ref):
    tm = sel_ref.shape[0]
    i = pl.program_id(0)

    @pl.when(i == 0)
    def _():
        carry_ref[...] = jnp.zeros_like(carry_ref)

    sel = sel_ref[...]
    row = lax.broadcasted_iota(jnp.int32, (tm, tm), 0)
    col = lax.broadcasted_iota(jnp.int32, (tm, tm), 1)
    before = (col < row).astype(BF16)
    lpos_ref[...] = _dot(before, sel)
    pos_t = _dot_tn(sel, (row < col).astype(BF16))
    sel_t = _dot_tn(sel, (row == col).astype(BF16))
    keyt_ref[0] = jnp.where(sel_t > 0.5, pos_t, -1.0)[0:N_EXPERTS, :]

    n = jnp.sum(sel.astype(F32), axis=0, keepdims=True)
    carry = carry_ref[0:1, :]
    offs_ref[0] = carry.astype(jnp.int32)
    cnt_ref[0] = n.astype(jnp.int32)
    total = carry + jnp.ceil(n * (1.0 / SEG_ALIGN)) * SEG_ALIGN
    carry_ref[...] = jnp.broadcast_to(total, carry_ref.shape)
    tot_ref[...] = jnp.broadcast_to(total, tot_ref.shape).astype(jnp.int32)


def _route_pos(sel):
    t = sel.shape[0]
    tm = ROUTE_TILE
    nt = t // tm
    return pl.pallas_call(
        _route_pos_kernel,
        grid=(nt,),
        in_specs=[pl.BlockSpec((tm, LANES), lambda i: (i, 0))],
        out_specs=[
            pl.BlockSpec((tm, LANES), lambda i: (i, 0)),
            pl.BlockSpec((1, N_EXPERTS, tm), lambda i: (i, 0, 0)),
            pl.BlockSpec((1, 1, LANES), lambda i: (i, 0, 0)),
            pl.BlockSpec((1, 1, LANES), lambda i: (i, 0, 0)),
            pl.BlockSpec((8, LANES), lambda i: (0, 0)),
        ],
        out_shape=[
            jax.ShapeDtypeStruct((t, LANES), F32),
            jax.ShapeDtypeStruct((nt, N_EXPERTS, tm), F32),
            jax.ShapeDtypeStruct((nt, 1, LANES), jnp.int32),
            jax.ShapeDtypeStruct((nt, 1, LANES), jnp.int32),
            jax.ShapeDtypeStruct((8, LANES), jnp.int32),
        ],
        scratch_shapes=[pltpu.VMEM((8, LANES), F32)],
        compiler_params=_cp(("arbitrary",)),
        name="route_pos",
    )(sel)


def _rows(ref, start, n):
    return ref.at[pl.ds(pl.multiple_of(start, n), n)]


def _seg_windows(cnt_ref, base):
    longest = lax.fori_loop(0, N_EXPERTS, lambda e, m: jnp.maximum(m, cnt_ref[base + e]), 0)
    return lax.shift_right_logical(longest + (SEG_WINDOW - 1), SEG_WINDOW.bit_length() - 1)


def _seg_copies(cnt_ref, seg_ref, base, win, e_lo, e_hi, make_copy, wait):
    first = win * SEG_WINDOW

    def per_expert(e, carry):
        rows = jnp.minimum(jnp.maximum(cnt_ref[base + e] - first, 0), SEG_WINDOW)
        n_chunks = lax.shift_right_logical(rows + (SEG_ALIGN - 1), SEG_ALIGN.bit_length() - 1)
        slot0 = seg_ref[base + e] + first
        stage0 = e * SEG_WINDOW

        def per_chunk(c, carry2):
            cp = make_copy(stage0 + c * SEG_ALIGN, slot0 + c * SEG_ALIGN)
            if wait:
                cp.wait()
            else:
                cp.start()
            return carry2

        return lax.fori_loop(0, n_chunks, per_chunk, carry)

    lax.fori_loop(e_lo, e_hi, per_expert, 0)


def _dispatch_kernel(seg_ref, cnt_ref, fill_lo_ref, fill_hi_ref, h_ref, keyt_ref, xs_ref,
                     stage_ref, zero_ref, sem):
    i = pl.program_id(0)
    tm = h_ref.shape[0]
    base = i * N_EXPERTS

    def zero_fills(wait):
        def act(cp):
            if wait:
                cp.wait()
            else:
                cp.start()

        def fill(c):
            return pltpu.make_async_copy(zero_ref.at[pl.ds(0, SEG_ALIGN)], _rows(xs_ref, c * SEG_ALIGN, SEG_ALIGN),
                                         sem.at[2])

        def per_expert(e, carry):
            lo = lax.shift_right_logical(fill_lo_ref[e], SEG_ALIGN.bit_length() - 1)
            hi = lax.shift_right_logical(fill_hi_ref[e], SEG_ALIGN.bit_length() - 1)
            return lax.fori_loop(lo, hi, lambda c, a: (act(fill(c)), a)[1], carry)

        lax.fori_loop(0, N_EXPERTS, per_expert, 0)

        def fill_tail(c):
            return pltpu.make_async_copy(zero_ref, _rows(xs_ref, c * ZERO_ROWS, ZERO_ROWS), sem.at[2])

        lo = lax.shift_right_logical(fill_hi_ref[N_EXPERTS - 1], ZERO_ROWS.bit_length() - 1)
        hi = xs_ref.shape[0] // ZERO_ROWS
        lax.fori_loop(lo, hi, lambda c, a: (act(fill_tail(c)), a)[1], 0)

    @pl.when(i == 0)
    def _():
        zero_ref[...] = jnp.zeros_like(zero_ref)
        zero_fills(wait=False)

    buf = i % 2

    def copier(b):
        def make_copy(stage_row, slot):
            return pltpu.make_async_copy(_rows(stage_ref.at[b], stage_row, SEG_ALIGN),
                                         _rows(xs_ref, slot, SEG_ALIGN), sem.at[b])
        return make_copy

    half = N_EXPERTS // 2
    j = lax.broadcasted_iota(jnp.int32, (SEG_WINDOW, tm), 0).astype(F32)

    def build_and_send(win):
        key = keyt_ref[0] - jnp.asarray(win * SEG_WINDOW, F32)
        for hf in range(2):
            pick = jnp.concatenate([(key[e:e + 1, :] == j).astype(BF16)
                                    for e in range(hf * half, (hf + 1) * half)], axis=0)
            stage_ref[buf, hf * half * SEG_WINDOW:(hf + 1) * half * SEG_WINDOW, :] = _dot(pick, h_ref[...])
            _seg_copies(cnt_ref, seg_ref, base, win, hf * half, (hf + 1) * half, copier(buf), wait=False)

    n_win = _seg_windows(cnt_ref, base)
    build_and_send(0)

    @pl.when(i > 0)
    def _():
        prev = base - N_EXPERTS
        last = jnp.maximum(_seg_windows(cnt_ref, prev) - 1, 0)
        _seg_copies(cnt_ref, seg_ref, prev, last, 0, N_EXPERTS, copier(1 - buf), wait=True)

    def more(win, carry):
        _seg_copies(cnt_ref, seg_ref, base, win - 1, 0, N_EXPERTS, copier(buf), wait=True)
        build_and_send(win)
        return carry

    lax.fori_loop(1, n_win, more, 0)

    @pl.when(i == pl.num_programs(0) - 1)
    def _():
        _seg_copies(cnt_ref, seg_ref, base, jnp.maximum(n_win - 1, 0), 0, N_EXPERTS, copier(buf), wait=True)
        zero_fills(wait=True)


def _dispatch(seg, cnt, fill_lo, fill_hi, h2, keyt, n_pad):
    t, d = h2.shape
    tm = ROUTE_TILE
    return pl.pallas_call(
        _dispatch_kernel,
        grid_spec=pltpu.PrefetchScalarGridSpec(
            num_scalar_prefetch=4,
            grid=(t // tm,),
            in_specs=[
                pl.BlockSpec((tm, d), lambda i, *_: (i, 0)),
                pl.BlockSpec((1, N_EXPERTS, tm), lambda i, *_: (i, 0, 0)),
            ],
            out_specs=pl.BlockSpec(memory_space=pl.ANY),
            scratch_shapes=[pltpu.VMEM((2, N_EXPERTS * SEG_WINDOW, d), F32),
                            pltpu.VMEM((ZERO_ROWS, d), F32),
                            pltpu.SemaphoreType.DMA((3,))],
        ),
        out_shape=jax.ShapeDtypeStruct((n_pad, d), F32),
        compiler_params=_cp(("arbitrary",)),
        name="dispatch",
    )(seg, cnt, fill_lo, fill_hi, h2, keyt)


def _experts_kernel(be_ref, nv_ref, xs_ref, wgu_ref, wd_ref, bg_ref, bu_ref, bd_ref, ys_ref,
                    wg_s, wu_s, wd_s):
    j = pl.program_id(0)
    grp = 2 * LANES
    prev = be_ref[jnp.maximum(j - 1, 0)]

    @pl.when((j == 0) | (be_ref[j] != prev))
    def _():
        r = lax.broadcasted_iota(jnp.int32, (grp, grp), 0)
        c = lax.broadcasted_iota(jnp.int32, (grp, grp), 1)
        src = jnp.where(c < LANES, 2 * c, 2 * (c - LANES) + 1)
        pick = (r == src).astype(BF16)
        for g in range(wgu_ref.shape[2] // grp):
            y = _dot(wgu_ref[0, :, g * grp:(g + 1) * grp].astype(BF16), pick)
            wg_s[:, g * LANES:(g + 1) * LANES] = y[:, 0:LANES].astype(BF16)
            wu_s[:, g * LANES:(g + 1) * LANES] = y[:, LANES:grp].astype(BF16)
        wd_s[...] = wd_ref[0].astype(BF16)

    @pl.when(j < nv_ref[0])
    def _():
        x = xs_ref[...].astype(BF16)
        g = _dot(x, wg_s[...]) + bg_ref[0]
        u = _dot(x, wu_s[...]) + bu_ref[0]
        gate = jnp.minimum(g, SWIGLU_LIMIT)
        up = jnp.clip(u, -SWIGLU_LIMIT, SWIGLU_LIMIT)
        act = (up + 1.0) * (gate * jax.nn.sigmoid(gate * SWIGLU_ALPHA))
        ys_ref[...] = _dot(act.astype(BF16), wd_s[...]) + bd_ref[0]

    @pl.when(j >= nv_ref[0])
    def _():
        ys_ref[...] = jnp.zeros_like(ys_ref)


def _experts(block_e, n_valid, xs, wgu, wd, bg, bu, bd):
    n_pad = xs.shape[0]
    de, d = wd.shape[1:]
    blk = (MOE_ROWS, d)
    n_blocks = n_pad // MOE_ROWS
    xrow = lambda j, be, nv: (jnp.minimum(j, nv[0] - 1), 0)
    wsel = lambda j, be, nv: (be[j], 0, 0)
    return pl.pallas_call(
        _experts_kernel,
        grid_spec=pltpu.PrefetchScalarGridSpec(
            num_scalar_prefetch=2,
            grid=(n_blocks,),
            in_specs=[
                pl.BlockSpec(blk, xrow),
                pl.BlockSpec((1, d, 2 * de), wsel),
                pl.BlockSpec((1, de, d), wsel),
                pl.BlockSpec((1, 1, de), wsel),
                pl.BlockSpec((1, 1, de), wsel),
                pl.BlockSpec((1, 1, d), wsel),
            ],
            out_specs=pl.BlockSpec(blk, lambda j, be, nv: (j, 0)),
            scratch_shapes=[
                pltpu.VMEM((d, de), BF16),
                pltpu.VMEM((d, de), BF16),
                pltpu.VMEM((de, d), BF16),
            ],
        ),
        out_shape=jax.ShapeDtypeStruct(xs.shape, F32),
        compiler_params=pltpu.CompilerParams(dimension_semantics=("arbitrary",),
                                             vmem_limit_bytes=EXPERTS_VMEM_LIMIT),
        name="experts",
    )(block_e, n_valid, xs, wgu, wd, bg, bu, bd)


def _combine_kernel(seg_ref, cnt_ref, x1_ref, lpos_ref, cw_ref, nw_ref, ys_ref, o_ref, stage_ref, sem,
                    *, final_norm):
    i = pl.program_id(0)
    tm, d = x1_ref.shape
    base = i * N_EXPERTS

    buf = i % 2

    def gather(tile_base, win, b, wait):
        for e in range(N_EXPERTS):
            slot = seg_ref[tile_base + e] + win * SEG_WINDOW
            cp = pltpu.make_async_copy(ys_ref.at[pl.ds(pl.multiple_of(slot, SEG_ALIGN), SEG_WINDOW)],
                                       stage_ref.at[b, pl.ds(e * SEG_WINDOW, SEG_WINDOW)], sem.at[b])
            if wait:
                cp.wait()
            else:
                cp.start()

    @pl.when(i == 0)
    def _():
        gather(base, 0, buf, wait=False)

    @pl.when(i + 1 < pl.num_programs(0))
    def _():
        gather(base + N_EXPERTS, 0, 1 - buf, wait=False)

    n_stage = N_EXPERTS * SEG_WINDOW
    owner = lax.broadcasted_iota(jnp.int32, (LANES, n_stage), 1) // SEG_WINDOW
    expand = (owner == lax.broadcasted_iota(jnp.int32, (LANES, n_stage), 0)).astype(BF16)
    j = (lax.broadcasted_iota(jnp.int32, (tm, n_stage), 1) % SEG_WINDOW).astype(F32)
    cw_wide = _dot(cw_ref[...].astype(BF16), expand)

    def window(win, y):
        rank = (lpos_ref[...] - jnp.asarray(win * SEG_WINDOW, F32)).astype(BF16)
        take = jnp.where(_dot(rank, expand) == j, cw_wide, 0.0).astype(BF16)
        gather(base, win, buf, wait=True)
        return y + _dot(take, stage_ref[buf].astype(BF16))

    def more(win, y):
        gather(base, win, buf, wait=False)
        return window(win, y)

    y = window(0, jnp.zeros((tm, d), F32))
    y = lax.fori_loop(1, _seg_windows(cnt_ref, base), more, y)
    out = x1_ref[...] + y
    if final_norm:
        out = _rms(out, nw_ref[...])
    o_ref[...] = out


def _combine(seg, cnt, x1, lpos, cw, nw, ys, final_norm):
    t, d = x1.shape
    tm = ROUTE_TILE
    kern = functools.partial(_combine_kernel, final_norm=final_norm)
    return pl.pallas_call(
        kern,
        grid_spec=pltpu.PrefetchScalarGridSpec(
            num_scalar_prefetch=2,
            grid=(t // tm,),
            in_specs=[
                pl.BlockSpec((tm, d), lambda i, *_: (i, 0)),
                pl.BlockSpec((tm, LANES), lambda i, *_: (i, 0)),
                pl.BlockSpec((tm, LANES), lambda i, *_: (i, 0)),
                pl.BlockSpec((1, d), lambda i, *_: (0, 0)),
                pl.BlockSpec(memory_space=pl.ANY),
            ],
            out_specs=pl.BlockSpec((tm, d), lambda i, *_: (i, 0)),
            scratch_shapes=[pltpu.VMEM((2, N_EXPERTS * SEG_WINDOW, d), F32), pltpu.SemaphoreType.DMA((2,))],
        ),
        out_shape=jax.ShapeDtypeStruct((t, d), F32),
        compiler_params=_cp(("arbitrary",)),
        name="combine",
    )(seg, cnt, x1, lpos, cw, nw, ys)


def _rope_tables(seq):
    half = MLA_ROPE // 2
    inv = 1.0 / (ROPE_THETA ** (jnp.arange(0, MLA_ROPE, 2, dtype=F32) / MLA_ROPE))
    ang = jnp.arange(seq, dtype=F32)[:, None] * inv[None, :]
    cos, sin = jnp.cos(ang), jnp.sin(ang)
    zeros = jnp.zeros((seq, LANES - MLA_ROPE), F32)
    cos_row = jnp.concatenate([cos, cos, zeros], axis=-1)
    sin_row = jnp.concatenate([-sin, sin, zeros], axis=-1)
    del half
    return cos_row, sin_row, cos.T, sin.T


def _pad_cols(a, width):
    return jnp.pad(a, ((0, 0), (0, width - a.shape[1])))


def _layer(x2, batch, seq, final_norm_w, final_norm, rope_tables,
           norm_mix_w, w_in, gdn_conv_w, gdn_a_log, gdn_dt_bias, gdn_norm_w, w_gdn_o,
           mla_q_norm_w, w_mla_q_b, mla_kv_norm_w, w_mla_kv_b, w_mla_o, w_out,
           norm_ffn_w, w_router, b_router, w_gate_up, b_gate_up, w_down, b_down):
    t, d = x2.shape
    qk_w = GDN_HEADS * GDN_D
    o_b = 4 * qk_w
    o_a = o_b + GDN_HEADS
    o_cq = o_a + GDN_HEADS
    o_ckv = o_cq + MLA_Q_LORA
    o_kr = o_ckv + MLA_KV_LORA
    o_ga = o_kr + MLA_ROPE
    o_gb = o_ga + d
    w_p = jnp.concatenate([
        w_in[:, 0:o_b], w_in[:, o_ga:o_gb + d], w_in[:, o_cq:o_ckv], w_in[:, o_ckv:o_kr],
        _pad_cols(w_in[:, o_kr:o_ga], 2 * LANES)], axis=1).astype(BF16)
    w_ab = _pad_cols(jnp.concatenate([w_in[:, o_a:o_cq], w_in[:, o_b:o_a]], axis=1), LANES).astype(BF16)

    p, ab = _in_proj(x2, norm_mix_w[None, :], w_p, w_ab)

    alog_row = _pad_cols(gdn_a_log[None, :].astype(F32), LANES)
    dtb_row = _pad_cols(gdn_dt_bias[None, :].astype(F32), LANES)
    qkvn, cols, gct = _gdn_prep(p, ab, gdn_conv_w.astype(F32), alog_row, dtb_row, seq)
    o_gdn = _gdn_chunk(qkvn, p, cols, gct, gdn_norm_w[None, :].astype(F32), batch, seq)

    hd = MLA_NOPE + MLA_ROPE
    wq = w_mla_q_b.reshape(MLA_Q_LORA, MLA_HEADS, hd)
    wqt = jnp.pad(wq, ((0, 0), (0, 0), (0, 2 * LANES - hd))).reshape(MLA_Q_LORA, MLA_HEADS * 2 * LANES).T
    wkv = w_mla_kv_b.reshape(MLA_KV_LORA, MLA_HEADS, MLA_NOPE + MLA_V)
    wkn = wkv[:, :, :MLA_NOPE].reshape(MLA_KV_LORA, -1)
    wvt = wkv[:, :, MLA_NOPE:].reshape(MLA_KV_LORA, -1).T
    qt, kn, kr, vt = _mla_prep(p, rope_tables, mla_q_norm_w[None, :].astype(F32),
                               mla_kv_norm_w[None, :].astype(F32), wqt.astype(BF16), wkn.astype(BF16),
                               wvt.astype(BF16), seq)
    o_mla = _mla_attn(qt, kn, kr, vt, batch, seq)

    wr = _pad_cols(w_router.astype(F32), LANES)
    wr_hi = wr.astype(BF16)
    wr_lo = (wr - wr_hi.astype(F32)).astype(BF16)
    br = _pad_cols(b_router[None, :].astype(F32), LANES)
    x1, h2, sel, cw = _mix_out(x2, o_gdn, o_mla, p, w_gdn_o.astype(BF16), w_mla_o.astype(BF16),
                               w_out.astype(BF16), norm_ffn_w[None, :].astype(F32), wr_hi, wr_lo, br)

    lpos, keyt, offs, cnt, tot = _route_pos(sel)
    n_tiles = t // ROUTE_TILE
    used = tot[0, :N_EXPERTS]
    padded = (used + MOE_ROWS - 1) // MOE_ROWS * MOE_ROWS
    pad_end = jnp.cumsum(padded)
    pad_start = pad_end - padded
    seg = (pad_start[None, :] + offs[:, 0, :N_EXPERTS]).astype(jnp.int32).reshape(-1)
    cnt = cnt[:, 0, :N_EXPERTS].reshape(-1)
    worst_used = t * TOP_K + n_tiles * N_EXPERTS * (SEG_ALIGN - 1)
    n_pad = -(-worst_used // MOE_ROWS) * MOE_ROWS + (N_EXPERTS + 1) * MOE_ROWS
    n_blocks = n_pad // MOE_ROWS
    blk_start = jnp.arange(n_blocks, dtype=jnp.int32) * MOE_ROWS
    block_e = jnp.minimum(jnp.sum((pad_end[None, :] <= blk_start[:, None]).astype(jnp.int32), axis=1),
                          N_EXPERTS - 1).astype(jnp.int32)
    n_valid = (pad_end[-1:] // MOE_ROWS).astype(jnp.int32)

    xs = _dispatch(seg, cnt, (pad_start + used).astype(jnp.int32), pad_end.astype(jnp.int32), h2, keyt, n_pad)
    bg = b_gate_up[:, None, 0::2].astype(F32)
    bu = b_gate_up[:, None, 1::2].astype(F32)
    ys = _experts(block_e, n_valid, xs, w_gate_up, w_down, bg, bu, b_down[:, None, :].astype(F32))
    return _combine(seg, cnt, x1, lpos, cw, final_norm_w[None, :].astype(F32), ys, final_norm)


def kernel(x, norm_mix_w, w_in, gdn_conv_w, gdn_a_log, gdn_dt_bias, gdn_norm_w, w_gdn_o, mla_q_norm_w, w_mla_q_b, mla_kv_norm_w, w_mla_kv_b, w_mla_o, w_out, norm_ffn_w, w_router, b_router, w_gate_up, b_gate_up, w_down, b_down, norm_final_w):
    batch, seq, d = x.shape
    depth = w_in.shape[0]
    rope_tables = _rope_tables(seq)
    x2 = x.reshape(batch * seq, d)
    for layer in range(depth):
        x2 = _layer(x2, batch, seq, norm_final_w, layer == depth - 1, rope_tables,
                    norm_mix_w[layer], w_in[layer], gdn_conv_w[layer], gdn_a_log[layer],
                    gdn_dt_bias[layer], gdn_norm_w[layer], w_gdn_o[layer], mla_q_norm_w[layer],
                    w_mla_q_b[layer], mla_kv_norm_w[layer], w_mla_kv_b[layer], w_mla_o[layer],
                    w_out[layer], norm_ffn_w[layer], w_router[layer], b_router[layer],
                    w_gate_up[layer], b_gate_up[layer], w_down[layer], b_down[layer])
    return x2.reshape(batch, seq, d)
```

```python
import functools

import jax
import jax.numpy as jnp
import numpy as np
from jax import lax
from jax.experimental import pallas as pl
from jax.experimental.pallas import tpu as pltpu

F32 = jnp.float32
BF16 = jnp.bfloat16

CHUNK = 64
NORM_EPS = 1e-6
GDN_HEADS = 8
GDN_D = 128
GDN_CONV = 4
MLA_HEADS = 8
MLA_Q_LORA = 512
MLA_KV_LORA = 256
MLA_NOPE = 128
MLA_ROPE = 64
MLA_V = 128
ROPE_THETA = 10000.0
N_EXPERTS = 32
TOP_K = 4
SWIGLU_LIMIT = 7.0
SWIGLU_ALPHA = 1.702

LANES = 128
MOE_ROWS = 512
ROUTE_TILE = 256
SEG_ALIGN = 8
SEG_WINDOW = 64
ZERO_ROWS = 64
VMEM_LIMIT = 48 * 1024 * 1024
EXPERTS_VMEM_LIMIT = 56 * 1024 * 1024

NEG_BIG = -1e30
LOG2_E = 1.4426950408889634


def _cp(sem):
    return pltpu.CompilerParams(dimension_semantics=sem, vmem_limit_bytes=VMEM_LIMIT)


def _dot(a, b):
    return jnp.dot(a, b, preferred_element_type=F32)


def _dot_nt(a, b):
    return lax.dot_general(a, b, (((1,), (1,)), ((), ())), preferred_element_type=F32)


def _dot_tn(a, b):
    return lax.dot_general(a, b, (((0,), (0,)), ((), ())), preferred_element_type=F32)


def _split3(x):
    hi = x.astype(BF16)
    r = x - hi.astype(F32)
    mid = r.astype(BF16)
    lo = (r - mid.astype(F32)).astype(BF16)
    return hi, mid, lo


def _rms(x, w):
    ms = jnp.mean(x * x, axis=-1, keepdims=True)
    return x * lax.rsqrt(ms + NORM_EPS) * w


def _in_proj_kernel(x_ref, nw_ref, w_ref, wab_ref, p_ref, ab_ref, h_ref):
    @pl.when(pl.program_id(1) == 0)
    def _():
        hb = _rms(x_ref[...], nw_ref[...]).astype(BF16)
        h_ref[...] = hb
        ab_ref[...] = _dot(hb, wab_ref[...])

    p_ref[...] = _dot(h_ref[...], w_ref[...]).astype(BF16)


def _in_proj(x2, norm_w, w_p, w_ab, tm=1024, tn=3584):
    t, d = x2.shape
    n = w_p.shape[1]
    return pl.pallas_call(
        _in_proj_kernel,
        grid=(t // tm, n // tn),
        in_specs=[
            pl.BlockSpec((tm, d), lambda i, j: (i, 0)),
            pl.BlockSpec((1, d), lambda i, j: (0, 0)),
            pl.BlockSpec((d, tn), lambda i, j: (0, j)),
            pl.BlockSpec((d, LANES), lambda i, j: (0, 0)),
        ],
        out_specs=[
            pl.BlockSpec((tm, tn), lambda i, j: (i, j)),
            pl.BlockSpec((tm, LANES), lambda i, j: (i, 0)),
        ],
        out_shape=[
            jax.ShapeDtypeStruct((t, n), BF16),
            jax.ShapeDtypeStruct((t, LANES), F32),
        ],
        scratch_shapes=[pltpu.VMEM((tm, d), BF16)],
        compiler_params=_cp(("parallel", "arbitrary")),
        name="in_proj",
    )(x2, norm_w, w_p, w_ab)


def _gdn_prep_kernel(cur_ref, prev_ref, ab_ref, cw_ref, alog_ref, dtb_ref,
                     qkv_ref, cols_ref, gct_ref, *, tiles_per_seq):
    tm = cur_ref.shape[0]
    i = pl.program_id(0)
    halo_on = (i % tiles_per_seq) != 0
    n_blk = cur_ref.shape[1] // LANES
    q_scale = GDN_D ** -0.5
    for cb in range(n_blk):
        cs = slice(cb * LANES, (cb + 1) * LANES)
        cur = cur_ref[:, cs].astype(F32)
        halo = prev_ref[:, cs].astype(F32)[8:16, :]
        halo = jnp.where(halo_on, halo, 0.0)
        xe = jnp.concatenate([halo, cur], axis=0)
        w = cw_ref[:, cs]
        y = w[0:1, :] * xe[5:5 + tm, :]
        for j in range(1, GDN_CONV):
            y = y + w[j:j + 1, :] * xe[5 + j:5 + j + tm, :]
        y = y * jax.nn.sigmoid(y)
        if cb < 2 * GDN_HEADS:
            ss = jnp.sum(y * y, axis=-1, keepdims=True)
            y = y * lax.rsqrt(ss + NORM_EPS)
            if cb < GDN_HEADS:
                y = y * q_scale
        qkv_ref[:, cs] = y.astype(BF16)

    ab = ab_ref[...]
    g = -jnp.exp(alog_ref[...]) * jax.nn.softplus(ab + dtb_ref[...])
    row = lax.broadcasted_iota(jnp.int32, (tm, tm), 0)
    col = lax.broadcasted_iota(jnp.int32, (tm, tm), 1)
    tri = ((col <= row) & ((row // CHUNK) == (col // CHUNK))).astype(BF16)
    g_hi, g_mid, g_lo = _split3(g)
    gc = _dot(tri, g_hi) + _dot(tri, g_mid) + _dot(tri, g_lo)
    lane = lax.broadcasted_iota(jnp.int32, (tm, LANES), 1)
    cols_ref[...] = jnp.where(lane < GDN_HEADS, gc, jax.nn.sigmoid(ab))
    for c in range(tm // CHUNK):
        blk = gc[c * CHUNK:(c + 1) * CHUNK, :]
        blk = jnp.concatenate([blk, jnp.zeros_like(blk)], axis=0)
        gct_ref[c] = blk.T[0:GDN_HEADS, 0:CHUNK]


def _gdn_prep(p, ab, conv_w, alog_row, dtb_row, seq, tm=256):
    t = p.shape[0]
    cw = 3 * GDN_HEADS * GDN_D
    tiles_per_seq = seq // tm
    kern = functools.partial(_gdn_prep_kernel, tiles_per_seq=tiles_per_seq)
    return pl.pallas_call(
        kern,
        grid=(t // tm,),
        in_specs=[
            pl.BlockSpec((tm, cw), lambda i: (i, 0)),
            pl.BlockSpec((16, cw), lambda i: (jnp.maximum(i * (tm // 16) - 1, 0), 0)),
            pl.BlockSpec((tm, LANES), lambda i: (i, 0)),
            pl.BlockSpec((GDN_CONV, cw), lambda i: (0, 0)),
            pl.BlockSpec((1, LANES), lambda i: (0, 0)),
            pl.BlockSpec((1, LANES), lambda i: (0, 0)),
        ],
        out_specs=[
            pl.BlockSpec((tm, cw), lambda i: (i, 0)),
            pl.BlockSpec((tm, LANES), lambda i: (i, 0)),
            pl.BlockSpec((tm // CHUNK, GDN_HEADS, CHUNK), lambda i: (i, 0, 0)),
        ],
        out_shape=[
            jax.ShapeDtypeStruct((t, cw), BF16),
            jax.ShapeDtypeStruct((t, LANES), F32),
            jax.ShapeDtypeStruct((t // CHUNK, GDN_HEADS, CHUNK), F32),
        ],
        compiler_params=_cp(("parallel",)),
        name="gdn_prep",
    )(p, p, ab, conv_w, alog_row, dtb_row)


def _gdn_chunk_kernel(q_ref, k_ref, v_ref, z_ref, cols_ref, gct_ref, nw_ref, o_ref, s_ref):
    c = CHUNK
    nb = q_ref.shape[0]
    units = [(b, h) for b in range(nb) for h in range(GDN_HEADS)]

    @pl.when(pl.program_id(0) == 0)
    def _():
        s_ref[...] = jnp.zeros_like(s_ref)

    ri = lax.broadcasted_iota(jnp.int32, (c, c), 0)
    ci = lax.broadcasted_iota(jnp.int32, (c, c), 1)
    incl = ri >= ci
    strict = ri > ci
    eye = (ri == ci).astype(F32)
    nw = nw_ref[...]

    cols, e_g, e_kd, e_last, gct = [], [], [], [], []
    for b in range(nb):
        cb = cols_ref[b]
        last = cb[c - 1:c, :]
        cols.append(cb)
        e_g.append(jnp.exp(cb))
        e_kd.append(jnp.exp(last - cb))
        e_last.append(jnp.exp(last))
        gct.append(gct_ref[b, 0])

    kq, kb_l, kf_l = [], [], []
    for b, h in units:
        hs = slice(h * GDN_D, (h + 1) * GDN_D)
        k = k_ref[b, :, hs]
        kf = k.astype(F32)
        kb = kf * cols[b][:, GDN_HEADS + h:GDN_HEADS + h + 1]
        kq.append(_dot_nt(jnp.concatenate([kb.astype(BF16), q_ref[b, :, hs]], axis=0), k))
        kb_l.append(kb)
        kf_l.append(kf)

    a_l, qk_l = [], []
    for i, (b, h) in enumerate(units):
        dec = jnp.exp(jnp.minimum(cols[b][:, h:h + 1] - gct[b][h:h + 1, :], 0.0))
        a_l.append(jnp.where(strict, -kq[i][0:c, :] * dec, 0.0))
        qk_l.append(jnp.where(incl, kq[i][c:2 * c, :] * dec, 0.0).astype(BF16))

    tinv = [eye + a for a in a_l]
    pw = a_l
    for _ in range(5):
        pwb = [x.astype(BF16) for x in pw]
        pw = [_dot(x, x) for x in pwb]
        tinv = [t + _dot(t.astype(BF16), x.astype(BF16)) for t, x in zip(tinv, pw)]

    uw = []
    for i, (b, h) in enumerate(units):
        hs = slice(h * GDN_D, (h + 1) * GDN_D)
        beta = cols[b][:, GDN_HEADS + h:GDN_HEADS + h + 1]
        rhs = jnp.concatenate([v_ref[b, :, hs].astype(F32) * beta,
                               kb_l[i] * e_g[b][:, h:h + 1]], axis=1).astype(BF16)
        uw.append(_dot(tinv[i].astype(BF16), rhs))

    r_l = []
    for i, (b, h) in enumerate(units):
        hs = slice(h * GDN_D, (h + 1) * GDN_D)
        qd = (q_ref[b, :, hs].astype(F32) * e_g[b][:, h:h + 1]).astype(BF16)
        lhs = jnp.concatenate([uw[i][:, GDN_D:2 * GDN_D].astype(BF16), qd], axis=0)
        r_l.append(_dot(lhs, s_ref[b * GDN_HEADS + h].astype(BF16)))

    for i, (b, h) in enumerate(units):
        hs = slice(h * GDN_D, (h + 1) * GDN_D)
        v_new = (uw[i][:, 0:GDN_D] - r_l[i][0:c, :]).astype(BF16)
        o = r_l[i][c:2 * c, :] + _dot(qk_l[i], v_new)
        kd = (kf_l[i] * e_kd[b][:, h:h + 1]).astype(BF16)
        u = b * GDN_HEADS + h
        s_ref[u] = s_ref[u] * e_last[b][:, h:h + 1] + _dot_tn(kd, v_new)
        z = z_ref[b, :, hs].astype(F32)
        o_ref[b, :, hs] = (_rms(o, nw) * (z * jax.nn.sigmoid(z))).astype(BF16)


def _gdn_chunk(qkvn, p, cols, gct, norm_w, batch, seq):
    nc = seq // CHUNK
    hw = GDN_HEADS * GDN_D
    qkvn3 = qkvn.reshape(batch, seq, qkvn.shape[1])
    p3 = p.reshape(batch, seq, p.shape[1])
    cols3 = cols.reshape(batch, seq, LANES)
    gct4 = gct.reshape(batch, nc, GDN_HEADS, CHUNK)
    tile = lambda col: pl.BlockSpec((batch, CHUNK, hw), lambda c: (0, c, col))
    out = pl.pallas_call(
        _gdn_chunk_kernel,
        grid=(nc,),
        in_specs=[
            tile(0), tile(1), tile(2),
            tile(3),
            pl.BlockSpec((batch, CHUNK, LANES), lambda c: (0, c, 0)),
            pl.BlockSpec((batch, 1, GDN_HEADS, CHUNK), lambda c: (0, c, 0, 0)),
            pl.BlockSpec((1, GDN_D), lambda c: (0, 0)),
        ],
        out_specs=pl.BlockSpec((batch, CHUNK, hw), lambda c: (0, c, 0)),
        out_shape=jax.ShapeDtypeStruct((batch, seq, hw), BF16),
        scratch_shapes=[pltpu.VMEM((batch * GDN_HEADS, GDN_D, GDN_D), F32)],
        compiler_params=_cp(("arbitrary",)),
        name="gdn_chunk",
    )(qkvn3, qkvn3, qkvn3, p3, cols3, gct4, norm_w)
    return out.reshape(batch * seq, hw)


def _rope(x, cos, sin_signed):
    lane = lax.broadcasted_iota(jnp.int32, x.shape, 1)
    fwd = pltpu.roll(x, LANES - MLA_ROPE // 2, 1)
    bwd = pltpu.roll(x, MLA_ROPE // 2, 1)
    rot = jnp.where(lane < MLA_ROPE // 2, fwd, bwd)
    return x * cos + rot * sin_signed


def _mla_prep_kernel(cq_ref, ckv_ref, kr_ref, cos_ref, sin_ref, cost_ref, sint_ref, qnw_ref, kvnw_ref,
                     wqt_ref, wkn_ref, wvt_ref, qt_ref, kn_ref, kro_ref, vt_ref):
    cos = cos_ref[...]
    sin = sin_ref[...]
    cos_t = cost_ref[...]
    sin_t = sint_ref[...]
    cq = _rms(cq_ref[...].astype(F32), qnw_ref[...]).astype(BF16)
    hd = 2 * LANES
    half = MLA_ROPE // 2
    scale = (MLA_NOPE + MLA_ROPE) ** -0.5 * LOG2_E
    for h in range(MLA_HEADS):
        qh = _dot_nt(wqt_ref[h * hd:(h + 1) * hd, :], cq) * scale
        lo = qh[MLA_NOPE:MLA_NOPE + half, :]
        hi = qh[MLA_NOPE + half:MLA_NOPE + MLA_ROPE, :]
        qt_ref[h * hd:h * hd + MLA_NOPE, :] = qh[0:MLA_NOPE, :].astype(BF16)
        qt_ref[h * hd + MLA_NOPE:h * hd + MLA_NOPE + half, :] = (lo * cos_t - hi * sin_t).astype(BF16)
        qt_ref[h * hd + MLA_NOPE + half:h * hd + MLA_NOPE + MLA_ROPE, :] = (hi * cos_t + lo * sin_t).astype(BF16)
        qt_ref[h * hd + MLA_NOPE + MLA_ROPE:(h + 1) * hd, :] = qh[MLA_NOPE + MLA_ROPE:hd, :].astype(BF16)
    kvl = _rms(ckv_ref[...].astype(F32), kvnw_ref[...]).astype(BF16)
    kn_ref[...] = _dot(kvl, wkn_ref[...]).astype(BF16)
    vt_ref[...] = _dot_nt(wvt_ref[...], kvl).astype(BF16)
    kro_ref[...] = _rope(kr_ref[...].astype(F32), cos, sin).astype(BF16)


def _mla_prep(p, tables, qnw, kvnw, wqt, wkn, wvt, seq, tm=512):
    cos_row, sin_row, cos_col, sin_col = tables
    t = p.shape[0]
    tiles_per_seq = seq // tm
    hw = MLA_HEADS * MLA_NOPE
    half = MLA_ROPE // 2
    cq_blk = 6144 // MLA_Q_LORA
    ckv_blk = 6656 // MLA_KV_LORA
    kr_blk = 6912 // LANES
    return pl.pallas_call(
        _mla_prep_kernel,
        grid=(t // tm,),
        in_specs=[
            pl.BlockSpec((tm, MLA_Q_LORA), lambda i: (i, cq_blk)),
            pl.BlockSpec((tm, MLA_KV_LORA), lambda i: (i, ckv_blk)),
            pl.BlockSpec((tm, LANES), lambda i: (i, kr_blk)),
            pl.BlockSpec((tm, LANES), lambda i: (i % tiles_per_seq, 0)),
            pl.BlockSpec((tm, LANES), lambda i: (i % tiles_per_seq, 0)),
            pl.BlockSpec((half, tm), lambda i: (0, i % tiles_per_seq)),
            pl.BlockSpec((half, tm), lambda i: (0, i % tiles_per_seq)),
            pl.BlockSpec((1, MLA_Q_LORA), lambda i: (0, 0)),
            pl.BlockSpec((1, MLA_KV_LORA), lambda i: (0, 0)),
            pl.BlockSpec((2 * hw, MLA_Q_LORA), lambda i: (0, 0)),
            pl.BlockSpec((MLA_KV_LORA, hw), lambda i: (0, 0)),
            pl.BlockSpec((hw, MLA_KV_LORA), lambda i: (0, 0)),
        ],
        out_specs=[
            pl.BlockSpec((2 * hw, tm), lambda i: (0, i)),
            pl.BlockSpec((tm, hw), lambda i: (i, 0)),
            pl.BlockSpec((tm, LANES), lambda i: (i, 0)),
            pl.BlockSpec((hw, tm), lambda i: (0, i)),
        ],
        out_shape=[
            jax.ShapeDtypeStruct((2 * hw, t), BF16),
            jax.ShapeDtypeStruct((t, hw), BF16),
            jax.ShapeDtypeStruct((t, LANES), BF16),
            jax.ShapeDtypeStruct((hw, t), BF16),
        ],
        compiler_params=_cp(("parallel",)),
        name="mla_prep",
    )(p, p, p, cos_row, sin_row, cos_col, sin_col, qnw, kvnw, wqt, wkn, wvt)


ATTN_HEADS_PER_STEP = 8
ATTN_SUM_ROWS = 16


def _mla_attn_kernel(qt_ref, kt_ref, q_ref, kn_ref, kr_ref, vt_ref, o_ref, m_ref, acc_ref):
    qi = qt_ref[pl.program_id(2)]
    ki = kt_ref[pl.program_id(2)]
    tq = q_ref.shape[1]
    tk = kn_ref.shape[0]
    hd = 2 * LANES

    @pl.when(ki == 0)
    def _():
        m_ref[...] = jnp.full_like(m_ref, NEG_BIG)
        acc_ref[...] = jnp.zeros_like(acc_ref)

    def step(masked):
        kr = kr_ref[...]
        ones = jnp.ones((ATTN_SUM_ROWS, tk), BF16)

        def scores(h):
            k = jnp.concatenate([kn_ref[:, h * MLA_NOPE:(h + 1) * MLA_NOPE], kr], axis=1)
            s = _dot(k, q_ref[h * hd:(h + 1) * hd, :])
            if masked:
                ck = lax.broadcasted_iota(jnp.int32, (tk, tq), 0) // CHUNK
                cq = lax.broadcasted_iota(jnp.int32, (tk, tq), 1) // CHUNK
                s = jnp.where(ck <= cq, s, NEG_BIG)
            return s

        def update(h, s):
            m_prev = m_ref[h]
            m_new = jnp.maximum(m_prev, jnp.max(s, axis=0, keepdims=True))
            alpha = jnp.exp2(m_prev - m_new)
            p = jnp.exp2((s - m_new).astype(BF16))
            v_ext = jnp.concatenate([vt_ref[h * MLA_V:(h + 1) * MLA_V, :], ones], axis=0)
            acc_ref[h] = alpha * acc_ref[h] + _dot(v_ext, p)
            m_ref[h] = m_new

        s_prev = scores(0)
        for h in range(1, ATTN_HEADS_PER_STEP):
            s_next = scores(h)
            update(h - 1, s_prev)
            s_prev = s_next
        update(ATTN_HEADS_PER_STEP - 1, s_prev)

    @pl.when(ki < qi)
    def _():
        step(False)

    @pl.when(ki == qi)
    def _():
        step(True)
        for h in range(ATTN_HEADS_PER_STEP):
            acc = acc_ref[h]
            o = acc[0:MLA_V, :] / acc[MLA_V:MLA_V + 1, :]
            o_ref[:, h * MLA_V:(h + 1) * MLA_V] = o.T.astype(BF16)


def _mla_attn(qt_all, kn, kr, vt, batch, seq, tq=512):
    t = kn.shape[0]
    nq = seq // tq
    hps = ATTN_HEADS_PER_STEP
    pairs = [(qi, ki) for qi in range(nq) for ki in range(qi + 1)]
    qt = jnp.asarray(np.array([pr[0] for pr in pairs], np.int32))
    kt = jnp.asarray(np.array([pr[1] for pr in pairs], np.int32))
    return pl.pallas_call(
        _mla_attn_kernel,
        grid_spec=pltpu.PrefetchScalarGridSpec(
            num_scalar_prefetch=2,
            grid=(batch, MLA_HEADS // hps, len(pairs)),
            in_specs=[
                pl.BlockSpec((hps * 2 * LANES, tq), lambda b, h, pr, qt, kt: (h, b * nq + qt[pr])),
                pl.BlockSpec((tq, hps * MLA_NOPE), lambda b, h, pr, qt, kt: (b * nq + kt[pr], h)),
                pl.BlockSpec((tq, LANES), lambda b, h, pr, qt, kt: (b * nq + kt[pr], 0)),
                pl.BlockSpec((hps * MLA_V, tq), lambda b, h, pr, qt, kt: (h, b * nq + kt[pr])),
            ],
            out_specs=pl.BlockSpec((tq, hps * MLA_V), lambda b, h, pr, qt, kt: (b * nq + qt[pr], h)),
            scratch_shapes=[
                pltpu.VMEM((hps, 1, tq), F32),
                pltpu.VMEM((hps, MLA_V + ATTN_SUM_ROWS, tq), F32),
            ],
        ),
        out_shape=jax.ShapeDtypeStruct((t, MLA_HEADS * MLA_V), BF16),
        compiler_params=_cp(("parallel", "parallel", "arbitrary")),
        name="mla_attn",
    )(qt, kt, qt_all, kn, kr, vt)


def _mix_out_kernel(x_ref, oa_ref, ob_ref, ga_ref, gb_ref, wga_ref, wmo_ref, wout_ref, nw_ref,
                    wr_hi_ref, wr_lo_ref, br_ref, x1_ref, h2_ref, sel_ref, cw_ref):
    ya = _dot(oa_ref[...], wga_ref[...])
    yb = _dot(ob_ref[...], wmo_ref[...])
    merged = (jax.nn.sigmoid(ga_ref[...].astype(F32)) * ya
              + jax.nn.sigmoid(gb_ref[...].astype(F32)) * yb)
    x1 = x_ref[...] + _dot(merged.astype(BF16), wout_ref[...])
    x1_ref[...] = x1
    h2 = _rms(x1, nw_ref[...])
    h_hi = h2.astype(BF16)
    h2_ref[...] = h_hi

    h_lo = (h2 - h_hi.astype(F32)).astype(BF16)
    logits = (_dot(h_hi, wr_hi_ref[...]) + _dot(h_hi, wr_lo_ref[...]) + _dot(h_lo, wr_hi_ref[...])
              + br_ref[...])
    lane = lax.broadcasted_iota(jnp.int32, logits.shape, 1)
    work = jnp.where(lane < N_EXPERTS, logits, -jnp.inf)
    sel = jnp.zeros(logits.shape, F32)
    cw = jnp.zeros(logits.shape, F32)
    top = None
    denom = None
    for kk in range(TOP_K):
        mx = jnp.max(work, axis=-1, keepdims=True)
        am = jnp.min(jnp.where(work == mx, lane, LANES), axis=-1, keepdims=True)
        hit = lane == am
        if kk == 0:
            top = mx
            e = jnp.ones_like(mx)
            denom = e
        else:
            e = jnp.exp(mx - top)
            denom = denom + e
        sel = jnp.where(hit, 1.0, sel)
        cw = jnp.where(hit, e, cw)
        work = jnp.where(hit, -jnp.inf, work)
    sel_ref[...] = sel.astype(BF16)
    cw_ref[...] = cw / denom


def _mix_out(x2, oa, ob, p, wga, wmo, wout, nw, wr_hi, wr_lo, br, tm=256):
    t, d = x2.shape
    full = lambda i: (0, 0)
    return pl.pallas_call(
        _mix_out_kernel,
        grid=(t // tm,),
        in_specs=[
            pl.BlockSpec((tm, d), lambda i: (i, 0)),
            pl.BlockSpec((tm, d), lambda i: (i, 0)),
            pl.BlockSpec((tm, d), lambda i: (i, 0)),
            pl.BlockSpec((tm, d), lambda i: (i, 4)),
            pl.BlockSpec((tm, d), lambda i: (i, 5)),
            pl.BlockSpec((d, d), full),
            pl.BlockSpec((d, d), full),
            pl.BlockSpec((d, d), full),
            pl.BlockSpec((1, d), full),
            pl.BlockSpec((d, LANES), full),
            pl.BlockSpec((d, LANES), full),
            pl.BlockSpec((1, LANES), full),
        ],
        out_specs=[
            pl.BlockSpec((tm, d), lambda i: (i, 0)),
            pl.BlockSpec((tm, d), lambda i: (i, 0)),
            pl.BlockSpec((tm, LANES), lambda i: (i, 0)),
            pl.BlockSpec((tm, LANES), lambda i: (i, 0)),
        ],
        out_shape=[
            jax.ShapeDtypeStruct((t, d), F32),
            jax.ShapeDtypeStruct((t, d), BF16),
            jax.ShapeDtypeStruct((t, LANES), BF16),
            jax.ShapeDtypeStruct((t, LANES), F32),
        ],
        compiler_params=_cp(("parallel",)),
        name="mix_out",
    )(x2, oa, ob, p, p, wga, wmo, wout, nw, wr_hi, wr_lo, br)


def _route_pos_kernel(sel_ref, lpos_ref, keyt_ref, offs_ref, cnt_ref, tot_ref, carry_ref):
    tm = sel_ref.shape[0]
    i = pl.program_id(0)

    @pl.when(i == 0)
    def _():
        carry_ref[...] = jnp.zeros_like(carry_ref)

    sel = sel_ref[...]
    row = lax.broadcasted_iota(jnp.int32, (tm, tm), 0)
    col = lax.broadcasted_iota(jnp.int32, (tm, tm), 1)
    before = (col < row).astype(BF16)
    lpos_ref[...] = _dot(before, sel)
    pos_t = _dot_tn(sel, (row < col).astype(BF16))
    sel_t = _dot_tn(sel, (row == col).astype(BF16))
    keyt_ref[0] = jnp.where(sel_t > 0.5, pos_t, -1.0)[0:N_EXPERTS, :]

    n = jnp.sum(sel.astype(F32), axis=0, keepdims=True)
    carry = carry_ref[0:1, :]
    offs_ref[0] = carry.astype(jnp.int32)
    cnt_ref[0] = n.astype(jnp.int32)
    total = carry + jnp.ceil(n * (1.0 / SEG_ALIGN)) * SEG_ALIGN
    carry_ref[...] = jnp.broadcast_to(total, carry_ref.shape)
    tot_ref[...] = jnp.broadcast_to(total, tot_ref.shape).astype(jnp.int32)


def _route_pos(sel):
    t = sel.shape[0]
    tm = ROUTE_TILE
    nt = t // tm
    return pl.pallas_call(
        _route_pos_kernel,
        grid=(nt,),
        in_specs=[pl.BlockSpec((tm, LANES), lambda i: (i, 0))],
        out_specs=[
            pl.BlockSpec((tm, LANES), lambda i: (i, 0)),
            pl.BlockSpec((1, N_EXPERTS, tm), lambda i: (i, 0, 0)),
            pl.BlockSpec((1, 1, LANES), lambda i: (i, 0, 0)),
            pl.BlockSpec((1, 1, LANES), lambda i: (i, 0, 0)),
            pl.BlockSpec((8, LANES), lambda i: (0, 0)),
        ],
        out_shape=[
            jax.ShapeDtypeStruct((t, LANES), F32),
            jax.ShapeDtypeStruct((nt, N_EXPERTS, tm), F32),
            jax.ShapeDtypeStruct((nt, 1, LANES), jnp.int32),
            jax.ShapeDtypeStruct((nt, 1, LANES), jnp.int32),
            jax.ShapeDtypeStruct((8, LANES), jnp.int32),
        ],
        scratch_shapes=[pltpu.VMEM((8, LANES), F32)],
        compiler_params=_cp(("arbitrary",)),
        name="route_pos",
    )(sel)


def _rows(ref, start, n):
    return ref.at[pl.ds(pl.multiple_of(start, n), n)]


def _pack_pairs(x):
    half = x.shape[1] // 2
    hi = lax.bitcast_convert_type(x[:, :half], jnp.uint32)
    lo = lax.bitcast_convert_type(x[:, half:], jnp.uint32)
    return hi | lax.shift_right_logical(lo, jnp.uint32(16))


def _unpack_pairs(u):
    hi = lax.bitcast_convert_type(u & jnp.uint32(0xFFFF0000), F32).astype(BF16)
    lo = lax.bitcast_convert_type(lax.shift_left(u, jnp.uint32(16)), F32).astype(BF16)
    return hi, lo


def _seg_windows(cnt_ref, base):
    longest = lax.fori_loop(0, N_EXPERTS, lambda e, m: jnp.maximum(m, cnt_ref[base + e]), 0)
    return lax.shift_right_logical(longest + (SEG_WINDOW - 1), SEG_WINDOW.bit_length() - 1)


def _dispatch_kernel(seg_ref, cnt_ref, fill_lo_ref, fill_hi_ref, h_ref, keyt_ref, xs_ref,
                     stage_ref, zero_ref, sem):
    i = pl.program_id(0)
    tm = h_ref.shape[0]
    base = i * N_EXPERTS

    def zero_fills(wait):
        def act(cp):
            if wait:
                cp.wait()
            else:
                cp.start()

        def fill(c):
            return pltpu.make_async_copy(zero_ref.at[pl.ds(0, SEG_ALIGN)], _rows(xs_ref, c * SEG_ALIGN, SEG_ALIGN),
                                         sem.at[2])

        def per_expert(e, carry):
            lo = lax.shift_right_logical(fill_lo_ref[e], SEG_ALIGN.bit_length() - 1)
            hi = lax.shift_right_logical(fill_hi_ref[e], SEG_ALIGN.bit_length() - 1)
            return lax.fori_loop(lo, hi, lambda c, a: (act(fill(c)), a)[1], carry)

        lax.fori_loop(0, N_EXPERTS, per_expert, 0)

        def fill_tail(c):
            return pltpu.make_async_copy(zero_ref, _rows(xs_ref, c * ZERO_ROWS, ZERO_ROWS), sem.at[2])

        lo = lax.shift_right_logical(fill_hi_ref[N_EXPERTS - 1], ZERO_ROWS.bit_length() - 1)
        hi = xs_ref.shape[0] // ZERO_ROWS
        lax.fori_loop(lo, hi, lambda c, a: (act(fill_tail(c)), a)[1], 0)

    @pl.when(i == 0)
    def _():
        zero_ref[...] = jnp.zeros_like(zero_ref)
        zero_fills(wait=False)
        zero_fills(wait=True)

    buf = i % 2

    def send(tile_base, win, b, wait, only_live=False):
        for e in range(N_EXPERTS):
            def go(e=e):
                slot = seg_ref[tile_base + e] + win * SEG_WINDOW
                cp = pltpu.make_async_copy(stage_ref.at[b, pl.ds(e * SEG_WINDOW, SEG_WINDOW)],
                                           xs_ref.at[pl.ds(pl.multiple_of(slot, SEG_ALIGN), SEG_WINDOW)],
                                           sem.at[b])
                if wait:
                    cp.wait()
                else:
                    cp.start()

            if only_live:
                pl.when(cnt_ref[tile_base + e] > win * SEG_WINDOW)(go)
            else:
                go()

    half = N_EXPERTS // 2
    j = lax.broadcasted_iota(jnp.int32, (SEG_WINDOW, tm), 0).astype(F32)

    def build(win, b):
        key = keyt_ref[0] - jnp.asarray(win * SEG_WINDOW, F32)
        for hf in range(2):
            pick = jnp.concatenate([(key[e:e + 1, :] == j).astype(BF16)
                                    for e in range(hf * half, (hf + 1) * half)], axis=0)
            stage_ref[b, hf * half * SEG_WINDOW:(hf + 1) * half * SEG_WINDOW, :] = _pack_pairs(_dot(pick, h_ref[...]))

    n_win = _seg_windows(cnt_ref, base)
    build(0, buf)

    @pl.when(i > 0)
    def _():
        send(base - N_EXPERTS, 0, 1 - buf, wait=True)

    send(base, 0, buf, wait=False)

    def more(win, carry):
        build(win, 1 - buf)
        send(base, win, 1 - buf, wait=False, only_live=True)
        send(base, win, 1 - buf, wait=True, only_live=True)
        return carry

    lax.fori_loop(1, n_win, more, 0)

    @pl.when(i == pl.num_programs(0) - 1)
    def _():
        send(base, 0, buf, wait=True)


def _dispatch(seg, cnt, fill_lo, fill_hi, h2, keyt, n_pad):
    t, d = h2.shape
    tm = ROUTE_TILE
    return pl.pallas_call(
        _dispatch_kernel,
        grid_spec=pltpu.PrefetchScalarGridSpec(
            num_scalar_prefetch=4,
            grid=(t // tm,),
            in_specs=[
                pl.BlockSpec((tm, d), lambda i, *_: (i, 0)),
                pl.BlockSpec((1, N_EXPERTS, tm), lambda i, *_: (i, 0, 0)),
            ],
            out_specs=pl.BlockSpec(memory_space=pl.ANY),
            scratch_shapes=[pltpu.VMEM((2, N_EXPERTS * SEG_WINDOW, d // 2), jnp.uint32),
                            pltpu.VMEM((ZERO_ROWS, d // 2), jnp.uint32),
                            pltpu.SemaphoreType.DMA((3,))],
        ),
        out_shape=jax.ShapeDtypeStruct((n_pad, d // 2), jnp.uint32),
        compiler_params=_cp(("arbitrary",)),
        name="dispatch",
    )(seg, cnt, fill_lo, fill_hi, h2, keyt)


def _experts_kernel(be_ref, nv_ref, xs_ref, wgu_ref, wd_ref, bg_ref, bu_ref, bd_ref, ys_ref,
                    wg_s, wu_s, wd_s):
    j = pl.program_id(0)
    grp = 2 * LANES
    prev = be_ref[jnp.maximum(j - 1, 0)]

    @pl.when((j == 0) | (be_ref[j] != prev))
    def _():
        r = lax.broadcasted_iota(jnp.int32, (grp, grp), 0)
        c = lax.broadcasted_iota(jnp.int32, (grp, grp), 1)
        src = jnp.where(c < LANES, 2 * c, 2 * (c - LANES) + 1)
        pick = (r == src).astype(BF16)
        for g in range(wgu_ref.shape[2] // grp):
            y = _dot(wgu_ref[0, :, g * grp:(g + 1) * grp].astype(BF16), pick)
            wg_s[:, g * LANES:(g + 1) * LANES] = y[:, 0:LANES].astype(BF16)
            wu_s[:, g * LANES:(g + 1) * LANES] = y[:, LANES:grp].astype(BF16)
        wd_s[...] = wd_ref[0].astype(BF16)

    @pl.when(j < nv_ref[0])
    def _():
        x = jnp.concatenate(_unpack_pairs(xs_ref[...]), axis=1)
        g = _dot(x, wg_s[...]) + bg_ref[0]
        u = _dot(x, wu_s[...]) + bu_ref[0]
        gate = jnp.minimum(g, SWIGLU_LIMIT)
        up = jnp.clip(u, -SWIGLU_LIMIT, SWIGLU_LIMIT)
        act = (up + 1.0) * (gate * jax.nn.sigmoid(gate * SWIGLU_ALPHA))
        y = _dot(act.astype(BF16), wd_s[...]) + bd_ref[0]
        ys_ref[...] = _pack_pairs(y.astype(BF16).astype(F32))

    @pl.when(j >= nv_ref[0])
    def _():
        ys_ref[...] = jnp.zeros_like(ys_ref)


def _experts(block_e, n_valid, xs, wgu, wd, bg, bu, bd):
    n_pad = xs.shape[0]
    de, d = wd.shape[1:]
    blk = (MOE_ROWS, xs.shape[1])
    n_blocks = n_pad // MOE_ROWS
    xrow = lambda j, be, nv: (jnp.minimum(j, nv[0] - 1), 0)
    wsel = lambda j, be, nv: (be[j], 0, 0)
    return pl.pallas_call(
        _experts_kernel,
        grid_spec=pltpu.PrefetchScalarGridSpec(
            num_scalar_prefetch=2,
            grid=(n_blocks,),
            in_specs=[
                pl.BlockSpec(blk, xrow),
                pl.BlockSpec((1, d, 2 * de), wsel),
                pl.BlockSpec((1, de, d), wsel),
                pl.BlockSpec((1, 1, de), wsel),
                pl.BlockSpec((1, 1, de), wsel),
                pl.BlockSpec((1, 1, d), wsel),
            ],
            out_specs=pl.BlockSpec(blk, lambda j, be, nv: (j, 0)),
            scratch_shapes=[
                pltpu.VMEM((d, de), BF16),
                pltpu.VMEM((d, de), BF16),
                pltpu.VMEM((de, d), BF16),
            ],
        ),
        out_shape=jax.ShapeDtypeStruct(xs.shape, xs.dtype),
        compiler_params=pltpu.CompilerParams(dimension_semantics=("arbitrary",),
                                             vmem_limit_bytes=EXPERTS_VMEM_LIMIT),
        name="experts",
    )(block_e, n_valid, xs, wgu, wd, bg, bu, bd)


def _combine_kernel(seg_ref, cnt_ref, x1_ref, lpos_ref, cw_ref, nw_ref, ys_ref, o_ref, stage_ref, sem,
                    *, final_norm):
    i = pl.program_id(0)
    tm, d = x1_ref.shape
    base = i * N_EXPERTS

    buf = i % 2

    def gather(tile_base, win, b, wait):
        for e in range(N_EXPERTS):
            slot = seg_ref[tile_base + e] + win * SEG_WINDOW
            cp = pltpu.make_async_copy(ys_ref.at[pl.ds(pl.multiple_of(slot, SEG_ALIGN), SEG_WINDOW)],
                                       stage_ref.at[b, pl.ds(e * SEG_WINDOW, SEG_WINDOW)], sem.at[b])
            if wait:
                cp.wait()
            else:
                cp.start()

    @pl.when(i == 0)
    def _():
        gather(base, 0, buf, wait=False)

    @pl.when(i + 1 < pl.num_programs(0))
    def _():
        gather(base + N_EXPERTS, 0, 1 - buf, wait=False)

    n_stage = N_EXPERTS * SEG_WINDOW
    owner = lax.broadcasted_iota(jnp.int32, (LANES, n_stage), 1) // SEG_WINDOW
    expand = (owner == lax.broadcasted_iota(jnp.int32, (LANES, n_stage), 0)).astype(BF16)
    j = (lax.broadcasted_iota(jnp.int32, (tm, n_stage), 1) % SEG_WINDOW).astype(F32)
    cw_wide = _dot(cw_ref[...].astype(BF16), expand)

    def window(win, y):
        rank = (lpos_ref[...] - jnp.asarray(win * SEG_WINDOW, F32)).astype(BF16)
        take = jnp.where(_dot(rank, expand) == j, cw_wide, 0.0).astype(BF16)
        gather(base, win, buf, wait=True)
        hi, lo = _unpack_pairs(stage_ref[buf])
        return y + jnp.concatenate([_dot(take, hi), _dot(take, lo)], axis=1)

    def more(win, y):
        gather(base, win, buf, wait=False)
        return window(win, y)

    y = window(0, jnp.zeros((tm, d), F32))
    y = lax.fori_loop(1, _seg_windows(cnt_ref, base), more, y)
    out = x1_ref[...] + y
    if final_norm:
        out = _rms(out, nw_ref[...])
    o_ref[...] = out


def _combine(seg, cnt, x1, lpos, cw, nw, ys, final_norm):
    t, d = x1.shape
    tm = ROUTE_TILE
    kern = functools.partial(_combine_kernel, final_norm=final_norm)
    return pl.pallas_call(
        kern,
        grid_spec=pltpu.PrefetchScalarGridSpec(
            num_scalar_prefetch=2,
            grid=(t // tm,),
            in_specs=[
                pl.BlockSpec((tm, d), lambda i, *_: (i, 0)),
                pl.BlockSpec((tm, LANES), lambda i, *_: (i, 0)),
                pl.BlockSpec((tm, LANES), lambda i, *_: (i, 0)),
                pl.BlockSpec((1, d), lambda i, *_: (0, 0)),
                pl.BlockSpec(memory_space=pl.ANY),
            ],
            out_specs=pl.BlockSpec((tm, d), lambda i, *_: (i, 0)),
            scratch_shapes=[pltpu.VMEM((2, N_EXPERTS * SEG_WINDOW, d // 2), jnp.uint32),
                            pltpu.SemaphoreType.DMA((2,))],
        ),
        out_shape=jax.ShapeDtypeStruct((t, d), F32),
        compiler_params=_cp(("arbitrary",)),
        name="combine",
    )(seg, cnt, x1, lpos, cw, nw, ys)


def _rope_tables(seq):
    half = MLA_ROPE // 2
    inv = 1.0 / (ROPE_THETA ** (jnp.arange(0, MLA_ROPE, 2, dtype=F32) / MLA_ROPE))
    ang = jnp.arange(seq, dtype=F32)[:, None] * inv[None, :]
    cos, sin = jnp.cos(ang), jnp.sin(ang)
    zeros = jnp.zeros((seq, LANES - MLA_ROPE), F32)
    cos_row = jnp.concatenate([cos, cos, zeros], axis=-1)
    sin_row = jnp.concatenate([-sin, sin, zeros], axis=-1)
    del half
    return cos_row, sin_row, cos.T, sin.T


def _pad_cols(a, width):
    return jnp.pad(a, ((0, 0), (0, width - a.shape[1])))


def _layer(x2, batch, seq, final_norm_w, final_norm, rope_tables,
           norm_mix_w, w_in, gdn_conv_w, gdn_a_log, gdn_dt_bias, gdn_norm_w, w_gdn_o,
           mla_q_norm_w, w_mla_q_b, mla_kv_norm_w, w_mla_kv_b, w_mla_o, w_out,
           norm_ffn_w, w_router, b_router, w_gate_up, b_gate_up, w_down, b_down):
    t, d = x2.shape
    qk_w = GDN_HEADS * GDN_D
    o_b = 4 * qk_w
    o_a = o_b + GDN_HEADS
    o_cq = o_a + GDN_HEADS
    o_ckv = o_cq + MLA_Q_LORA
    o_kr = o_ckv + MLA_KV_LORA
    o_ga = o_kr + MLA_ROPE
    o_gb = o_ga + d
    w_p = jnp.concatenate([
        w_in[:, 0:o_b], w_in[:, o_ga:o_gb + d], w_in[:, o_cq:o_ckv], w_in[:, o_ckv:o_kr],
        _pad_cols(w_in[:, o_kr:o_ga], 2 * LANES)], axis=1).astype(BF16)
    w_ab = _pad_cols(jnp.concatenate([w_in[:, o_a:o_cq], w_in[:, o_b:o_a]], axis=1), LANES).astype(BF16)

    p, ab = _in_proj(x2, norm_mix_w[None, :], w_p, w_ab)

    alog_row = _pad_cols(gdn_a_log[None, :].astype(F32), LANES)
    dtb_row = _pad_cols(gdn_dt_bias[None, :].astype(F32), LANES)
    qkvn, cols, gct = _gdn_prep(p, ab, gdn_conv_w.astype(F32), alog_row, dtb_row, seq)
    o_gdn = _gdn_chunk(qkvn, p, cols, gct, gdn_norm_w[None, :].astype(F32), batch, seq)

    hd = MLA_NOPE + MLA_ROPE
    wq = w_mla_q_b.reshape(MLA_Q_LORA, MLA_HEADS, hd)
    wqt = jnp.pad(wq, ((0, 0), (0, 0), (0, 2 * LANES - hd))).reshape(MLA_Q_LORA, MLA_HEADS * 2 * LANES).T
    wkv = w_mla_kv_b.reshape(MLA_KV_LORA, MLA_HEADS, MLA_NOPE + MLA_V)
    wkn = wkv[:, :, :MLA_NOPE].reshape(MLA_KV_LORA, -1)
    wvt = wkv[:, :, MLA_NOPE:].reshape(MLA_KV_LORA, -1).T
    qt, kn, kr, vt = _mla_prep(p, rope_tables, mla_q_norm_w[None, :].astype(F32),
                               mla_kv_norm_w[None, :].astype(F32), wqt.astype(BF16), wkn.astype(BF16),
                               wvt.astype(BF16), seq)
    o_mla = _mla_attn(qt, kn, kr, vt, batch, seq)

    wr = _pad_cols(w_router.astype(F32), LANES)
    wr_hi = wr.astype(BF16)
    wr_lo = (wr - wr_hi.astype(F32)).astype(BF16)
    br = _pad_cols(b_router[None, :].astype(F32), LANES)
    x1, h2, sel, cw = _mix_out(x2, o_gdn, o_mla, p, w_gdn_o.astype(BF16), w_mla_o.astype(BF16),
                               w_out.astype(BF16), norm_ffn_w[None, :].astype(F32), wr_hi, wr_lo, br)

    lpos, keyt, offs, cnt, tot = _route_pos(sel)
    n_tiles = t // ROUTE_TILE
    used = tot[0, :N_EXPERTS]
    padded = (used + SEG_WINDOW + MOE_ROWS - 1) // MOE_ROWS * MOE_ROWS
    pad_end = jnp.cumsum(padded)
    pad_start = pad_end - padded
    seg = (pad_start[None, :] + offs[:, 0, :N_EXPERTS]).astype(jnp.int32).reshape(-1)
    cnt = cnt[:, 0, :N_EXPERTS].reshape(-1)
    worst_used = t * TOP_K + n_tiles * N_EXPERTS * (SEG_ALIGN - 1) + N_EXPERTS * SEG_WINDOW
    n_pad = -(-worst_used // MOE_ROWS) * MOE_ROWS + (N_EXPERTS + 1) * MOE_ROWS
    n_blocks = n_pad // MOE_ROWS
    blk_start = jnp.arange(n_blocks, dtype=jnp.int32) * MOE_ROWS
    block_e = jnp.minimum(jnp.sum((pad_end[None, :] <= blk_start[:, None]).astype(jnp.int32), axis=1),
                          N_EXPERTS - 1).astype(jnp.int32)
    n_valid = (pad_end[-1:] // MOE_ROWS).astype(jnp.int32)

    xs = _dispatch(seg, cnt, (pad_start + used).astype(jnp.int32), pad_end.astype(jnp.int32), h2, keyt, n_pad)
    bg = b_gate_up[:, None, 0::2].astype(F32)
    bu = b_gate_up[:, None, 1::2].astype(F32)
    ys = _experts(block_e, n_valid, xs, w_gate_up, w_down, bg, bu, b_down[:, None, :].astype(F32))
    return _combine(seg, cnt, x1, lpos, cw, final_norm_w[None, :].astype(F32), ys, final_norm)


def kernel(x, norm_mix_w, w_in, gdn_conv_w, gdn_a_log, gdn_dt_bias, gdn_norm_w, w_gdn_o, mla_q_norm_w, w_mla_q_b, mla_kv_norm_w, w_mla_kv_b, w_mla_o, w_out, norm_ffn_w, w_router, b_router, w_gate_up, b_gate_up, w_down, b_down, norm_final_w):
    batch, seq, d = x.shape
    depth = w_in.shape[0]
    rope_tables = _rope_tables(seq)
    x2 = x.reshape(batch * seq, d)
    for layer in range(depth):
        x2 = _layer(x2, batch, seq, norm_final_w, layer == depth - 1, rope_tables,
                    norm_mix_w[layer], w_in[layer], gdn_conv_w[layer], gdn_a_log[layer],
                    gdn_dt_bias[layer], gdn_norm_w[layer], w_gdn_o[layer], mla_q_norm_w[layer],
                    w_mla_q_b[layer], mla_kv_norm_w[layer], w_mla_kv_b[layer], w_mla_o[layer],
                    w_out[layer], norm_ffn_w[layer], w_router[layer], b_router[layer],
                    w_gate_up[layer], b_gate_up[layer], w_down[layer], b_down[layer])
    return x2.reshape(batch, seq, d)
```

```python
import functools

import jax
import jax.numpy as jnp
import numpy as np
from jax import lax
from jax.experimental import pallas as pl
from jax.experimental.pallas import tpu as pltpu

F32 = jnp.float32
BF16 = jnp.bfloat16

CHUNK = 64
NORM_EPS = 1e-6
GDN_HEADS = 8
GDN_D = 128
GDN_CONV = 4
MLA_HEADS = 8
MLA_Q_LORA = 512
MLA_KV_LORA = 256
MLA_NOPE = 128
MLA_ROPE = 64
MLA_V = 128
ROPE_THETA = 10000.0
N_EXPERTS = 32
TOP_K = 4
SWIGLU_LIMIT = 7.0
SWIGLU_ALPHA = 1.702

LANES = 128
MOE_ROWS = 512
ROUTE_TILE = 256
SEG_ALIGN = 8
SEG_WINDOW = 64
ZERO_ROWS = 64
VMEM_LIMIT = 48 * 1024 * 1024
EXPERTS_VMEM_LIMIT = 56 * 1024 * 1024

NEG_BIG = -1e30
LOG2_E = 1.4426950408889634


def _cp(sem):
    return pltpu.CompilerParams(dimension_semantics=sem, vmem_limit_bytes=VMEM_LIMIT)


def _dot(a, b):
    return jnp.dot(a, b, preferred_element_type=F32)


def _dot_nt(a, b):
    return lax.dot_general(a, b, (((1,), (1,)), ((), ())), preferred_element_type=F32)


def _dot_tn(a, b):
    return lax.dot_general(a, b, (((0,), (0,)), ((), ())), preferred_element_type=F32)


def _split3(x):
    hi = x.astype(BF16)
    r = x - hi.astype(F32)
    mid = r.astype(BF16)
    lo = (r - mid.astype(F32)).astype(BF16)
    return hi, mid, lo


def _rms(x, w):
    ms = jnp.mean(x * x, axis=-1, keepdims=True)
    return x * lax.rsqrt(ms + NORM_EPS) * w


def _in_proj_kernel(x_ref, nw_ref, w_ref, wab_ref, p_ref, ab_ref, h_ref):
    @pl.when(pl.program_id(1) == 0)
    def _():
        hb = _rms(x_ref[...], nw_ref[...]).astype(BF16)
        h_ref[...] = hb
        ab_ref[...] = _dot(hb, wab_ref[...])

    p_ref[...] = _dot(h_ref[...], w_ref[...]).astype(BF16)


def _in_proj(x2, norm_w, w_p, w_ab, tm=1024, tn=3584):
    t, d = x2.shape
    n = w_p.shape[1]
    return pl.pallas_call(
        _in_proj_kernel,
        grid=(t // tm, n // tn),
        in_specs=[
            pl.BlockSpec((tm, d), lambda i, j: (i, 0)),
            pl.BlockSpec((1, d), lambda i, j: (0, 0)),
            pl.BlockSpec((d, tn), lambda i, j: (0, j)),
            pl.BlockSpec((d, LANES), lambda i, j: (0, 0)),
        ],
        out_specs=[
            pl.BlockSpec((tm, tn), lambda i, j: (i, j)),
            pl.BlockSpec((tm, LANES), lambda i, j: (i, 0)),
        ],
        out_shape=[
            jax.ShapeDtypeStruct((t, n), BF16),
            jax.ShapeDtypeStruct((t, LANES), F32),
        ],
        scratch_shapes=[pltpu.VMEM((tm, d), BF16)],
        compiler_params=_cp(("parallel", "arbitrary")),
        name="in_proj",
    )(x2, norm_w, w_p, w_ab)


def _gdn_prep_kernel(cur_ref, prev_ref, ab_ref, cw_ref, alog_ref, dtb_ref,
                     qkv_ref, cols_ref, gct_ref, *, tiles_per_seq):
    tm = cur_ref.shape[0]
    i = pl.program_id(0)
    halo_on = (i % tiles_per_seq) != 0
    q_scale = GDN_D ** -0.5
    grp = 2 * LANES
    row = lax.broadcasted_iota(jnp.int32, (tm, tm), 0)
    col = lax.broadcasted_iota(jnp.int32, (tm, tm), 1)
    shift = [(col == row - s).astype(BF16) for s in range(1, GDN_CONV)]
    for cg in range(cur_ref.shape[1] // grp):
        gs = slice(cg * grp, (cg + 1) * grp)
        cur_b = cur_ref[:, gs]
        cur = cur_b.astype(F32)
        w = cw_ref[:, gs]
        y = w[GDN_CONV - 1:GDN_CONV, :] * cur
        for s in range(1, GDN_CONV):
            y = y + w[GDN_CONV - 1 - s:GDN_CONV - s, :] * _dot(shift[s - 1], cur_b)
        halo = jnp.where(halo_on, prev_ref[:, gs].astype(F32)[8:16, :], 0.0)
        xe = jnp.concatenate([halo, cur[0:8, :]], axis=0)
        head = w[0:1, :] * xe[5:13, :]
        for j in range(1, GDN_CONV):
            head = head + w[j:j + 1, :] * xe[5 + j:13 + j, :]
        y = jnp.concatenate([head, y[8:, :]], axis=0)
        hy = 0.5 * y
        y = hy + hy * jnp.tanh(hy)
        for half in range(2):
            cb = 2 * cg + half
            yh = y[:, half * LANES:(half + 1) * LANES]
            if cb < 2 * GDN_HEADS:
                ss = jnp.sum(yh * yh, axis=-1, keepdims=True)
                yh = yh * lax.rsqrt(ss + NORM_EPS)
                if cb < GDN_HEADS:
                    yh = yh * q_scale
            qkv_ref[:, cb * LANES:(cb + 1) * LANES] = yh.astype(BF16)

    ab = ab_ref[...]
    g = -jnp.exp(alog_ref[...]) * jax.nn.softplus(ab + dtb_ref[...])
    row = lax.broadcasted_iota(jnp.int32, (tm, tm), 0)
    col = lax.broadcasted_iota(jnp.int32, (tm, tm), 1)
    tri = ((col <= row) & ((row // CHUNK) == (col // CHUNK))).astype(BF16)
    g_hi, g_mid, g_lo = _split3(g)
    gc = _dot(tri, g_hi) + _dot(tri, g_mid) + _dot(tri, g_lo)
    lane = lax.broadcasted_iota(jnp.int32, (tm, LANES), 1)
    cols_ref[...] = jnp.where(lane < GDN_HEADS, gc, jax.nn.sigmoid(ab))
    for c in range(tm // CHUNK):
        blk = gc[c * CHUNK:(c + 1) * CHUNK, :]
        blk = jnp.concatenate([blk, jnp.zeros_like(blk)], axis=0)
        gct_ref[c] = blk.T[0:GDN_HEADS, 0:CHUNK]


def _gdn_prep(p, ab, conv_w, alog_row, dtb_row, seq, tm=256):
    t = p.shape[0]
    cw = 3 * GDN_HEADS * GDN_D
    tiles_per_seq = seq // tm
    kern = functools.partial(_gdn_prep_kernel, tiles_per_seq=tiles_per_seq)
    return pl.pallas_call(
        kern,
        grid=(t // tm,),
        in_specs=[
            pl.BlockSpec((tm, cw), lambda i: (i, 0)),
            pl.BlockSpec((16, cw), lambda i: (jnp.maximum(i * (tm // 16) - 1, 0), 0)),
            pl.BlockSpec((tm, LANES), lambda i: (i, 0)),
            pl.BlockSpec((GDN_CONV, cw), lambda i: (0, 0)),
            pl.BlockSpec((1, LANES), lambda i: (0, 0)),
            pl.BlockSpec((1, LANES), lambda i: (0, 0)),
        ],
        out_specs=[
            pl.BlockSpec((tm, cw), lambda i: (i, 0)),
            pl.BlockSpec((tm, LANES), lambda i: (i, 0)),
            pl.BlockSpec((tm // CHUNK, GDN_HEADS, CHUNK), lambda i: (i, 0, 0)),
        ],
        out_shape=[
            jax.ShapeDtypeStruct((t, cw), BF16),
            jax.ShapeDtypeStruct((t, LANES), F32),
            jax.ShapeDtypeStruct((t // CHUNK, GDN_HEADS, CHUNK), F32),
        ],
        compiler_params=_cp(("parallel",)),
        name="gdn_prep",
    )(p, p, ab, conv_w, alog_row, dtb_row)


def _gdn_chunk_kernel(q_ref, k_ref, v_ref, z_ref, cols_ref, gct_ref, nw_ref, o_ref, s_ref):
    c = CHUNK
    nb = q_ref.shape[0]
    units = [(b, h) for b in range(nb) for h in range(GDN_HEADS)]

    @pl.when(pl.program_id(0) == 0)
    def _():
        s_ref[...] = jnp.zeros_like(s_ref)

    ri = lax.broadcasted_iota(jnp.int32, (c, c), 0)
    ci = lax.broadcasted_iota(jnp.int32, (c, c), 1)
    incl = ri >= ci
    strict = ri > ci
    eye = (ri == ci).astype(F32)
    nw = nw_ref[...]

    cols, e_g, e_kd, e_last, gct = [], [], [], [], []
    for b in range(nb):
        cb = cols_ref[b]
        last = cb[c - 1:c, :]
        cols.append(cb)
        e_g.append(jnp.exp(cb))
        e_kd.append(jnp.exp(last - cb))
        e_last.append(jnp.exp(last))
        gct.append(gct_ref[b, 0])

    kq, kb_l, kf_l = [], [], []
    for b, h in units:
        hs = slice(h * GDN_D, (h + 1) * GDN_D)
        k = k_ref[b, :, hs]
        kf = k.astype(F32)
        kb = kf * cols[b][:, GDN_HEADS + h:GDN_HEADS + h + 1]
        kq.append(_dot_nt(jnp.concatenate([kb.astype(BF16), q_ref[b, :, hs]], axis=0), k))
        kb_l.append(kb)
        kf_l.append(kf)

    a_l, qk_l = [], []
    for i, (b, h) in enumerate(units):
        dec = jnp.exp(jnp.minimum(cols[b][:, h:h + 1] - gct[b][h:h + 1, :], 0.0))
        a_l.append(jnp.where(strict, -kq[i][0:c, :] * dec, 0.0))
        qk_l.append(jnp.where(incl, kq[i][c:2 * c, :] * dec, 0.0).astype(BF16))

    tinv = [eye + a for a in a_l]
    pw = a_l
    for _ in range(5):
        pwb = [x.astype(BF16) for x in pw]
        pw = [_dot(x, x) for x in pwb]
        tinv = [t + _dot(t.astype(BF16), x.astype(BF16)) for t, x in zip(tinv, pw)]

    uw = []
    for i, (b, h) in enumerate(units):
        hs = slice(h * GDN_D, (h + 1) * GDN_D)
        beta = cols[b][:, GDN_HEADS + h:GDN_HEADS + h + 1]
        rhs = jnp.concatenate([v_ref[b, :, hs].astype(F32) * beta,
                               kb_l[i] * e_g[b][:, h:h + 1]], axis=1).astype(BF16)
        uw.append(_dot(tinv[i].astype(BF16), rhs))

    r_l = []
    for i, (b, h) in enumerate(units):
        hs = slice(h * GDN_D, (h + 1) * GDN_D)
        qd = (q_ref[b, :, hs].astype(F32) * e_g[b][:, h:h + 1]).astype(BF16)
        lhs = jnp.concatenate([uw[i][:, GDN_D:2 * GDN_D].astype(BF16), qd], axis=0)
        r_l.append(_dot(lhs, s_ref[b * GDN_HEADS + h].astype(BF16)))

    for i, (b, h) in enumerate(units):
        hs = slice(h * GDN_D, (h + 1) * GDN_D)
        v_new = (uw[i][:, 0:GDN_D] - r_l[i][0:c, :]).astype(BF16)
        o = r_l[i][c:2 * c, :] + _dot(qk_l[i], v_new)
        kd = (kf_l[i] * e_kd[b][:, h:h + 1]).astype(BF16)
        u = b * GDN_HEADS + h
        s_ref[u] = s_ref[u] * e_last[b][:, h:h + 1] + _dot_tn(kd, v_new)
        z = z_ref[b, :, hs].astype(F32)
        o_ref[b, :, hs] = (_rms(o, nw) * (z * jax.nn.sigmoid(z))).astype(BF16)


def _gdn_chunk(qkvn, p, cols, gct, norm_w, batch, seq):
    nc = seq // CHUNK
    hw = GDN_HEADS * GDN_D
    qkvn3 = qkvn.reshape(batch, seq, qkvn.shape[1])
    p3 = p.reshape(batch, seq, p.shape[1])
    cols3 = cols.reshape(batch, seq, LANES)
    gct4 = gct.reshape(batch, nc, GDN_HEADS, CHUNK)
    tile = lambda col: pl.BlockSpec((batch, CHUNK, hw), lambda c: (0, c, col))
    out = pl.pallas_call(
        _gdn_chunk_kernel,
        grid=(nc,),
        in_specs=[
            tile(0), tile(1), tile(2),
            tile(3),
            pl.BlockSpec((batch, CHUNK, LANES), lambda c: (0, c, 0)),
            pl.BlockSpec((batch, 1, GDN_HEADS, CHUNK), lambda c: (0, c, 0, 0)),
            pl.BlockSpec((1, GDN_D), lambda c: (0, 0)),
        ],
        out_specs=pl.BlockSpec((batch, CHUNK, hw), lambda c: (0, c, 0)),
        out_shape=jax.ShapeDtypeStruct((batch, seq, hw), BF16),
        scratch_shapes=[pltpu.VMEM((batch * GDN_HEADS, GDN_D, GDN_D), F32)],
        compiler_params=_cp(("arbitrary",)),
        name="gdn_chunk",
    )(qkvn3, qkvn3, qkvn3, p3, cols3, gct4, norm_w)
    return out.reshape(batch * seq, hw)


def _rope(x, cos, sin_signed):
    lane = lax.broadcasted_iota(jnp.int32, x.shape, 1)
    fwd = pltpu.roll(x, LANES - MLA_ROPE // 2, 1)
    bwd = pltpu.roll(x, MLA_ROPE // 2, 1)
    rot = jnp.where(lane < MLA_ROPE // 2, fwd, bwd)
    return x * cos + rot * sin_signed


def _mla_prep_kernel(cq_ref, ckv_ref, kr_ref, cos_ref, sin_ref, cost_ref, sint_ref, qnw_ref, kvnw_ref,
                     wqt_ref, wkn_ref, wvt_ref, qt_ref, kn_ref, kro_ref, vt_ref):
    cos = cos_ref[...]
    sin = sin_ref[...]
    cos_t = cost_ref[...]
    sin_t = sint_ref[...]
    cq = _rms(cq_ref[...].astype(F32), qnw_ref[...]).astype(BF16)
    hd = 2 * LANES
    half = MLA_ROPE // 2
    scale = (MLA_NOPE + MLA_ROPE) ** -0.5 * LOG2_E
    for h in range(MLA_HEADS):
        qh = _dot_nt(wqt_ref[h * hd:(h + 1) * hd, :], cq) * scale
        lo = qh[MLA_NOPE:MLA_NOPE + half, :]
        hi = qh[MLA_NOPE + half:MLA_NOPE + MLA_ROPE, :]
        qt_ref[h * hd:h * hd + MLA_NOPE, :] = qh[0:MLA_NOPE, :].astype(BF16)
        qt_ref[h * hd + MLA_NOPE:h * hd + MLA_NOPE + half, :] = (lo * cos_t - hi * sin_t).astype(BF16)
        qt_ref[h * hd + MLA_NOPE + half:h * hd + MLA_NOPE + MLA_ROPE, :] = (hi * cos_t + lo * sin_t).astype(BF16)
        qt_ref[h * hd + MLA_NOPE + MLA_ROPE:(h + 1) * hd, :] = qh[MLA_NOPE + MLA_ROPE:hd, :].astype(BF16)
    kvl = _rms(ckv_ref[...].astype(F32), kvnw_ref[...]).astype(BF16)
    kn_ref[...] = _dot(kvl, wkn_ref[...]).astype(BF16)
    vt_ref[...] = _dot_nt(wvt_ref[...], kvl).astype(BF16)
    kro_ref[...] = _rope(kr_ref[...].astype(F32), cos, sin).astype(BF16)


def _mla_prep(p, tables, qnw, kvnw, wqt, wkn, wvt, seq, tm=512):
    cos_row, sin_row, cos_col, sin_col = tables
    t = p.shape[0]
    tiles_per_seq = seq // tm
    hw = MLA_HEADS * MLA_NOPE
    half = MLA_ROPE // 2
    cq_blk = 6144 // MLA_Q_LORA
    ckv_blk = 6656 // MLA_KV_LORA
    kr_blk = 6912 // LANES
    return pl.pallas_call(
        _mla_prep_kernel,
        grid=(t // tm,),
        in_specs=[
            pl.BlockSpec((tm, MLA_Q_LORA), lambda i: (i, cq_blk)),
            pl.BlockSpec((tm, MLA_KV_LORA), lambda i: (i, ckv_blk)),
            pl.BlockSpec((tm, LANES), lambda i: (i, kr_blk)),
            pl.BlockSpec((tm, LANES), lambda i: (i % tiles_per_seq, 0)),
            pl.BlockSpec((tm, LANES), lambda i: (i % tiles_per_seq, 0)),
            pl.BlockSpec((half, tm), lambda i: (0, i % tiles_per_seq)),
            pl.BlockSpec((half, tm), lambda i: (0, i % tiles_per_seq)),
            pl.BlockSpec((1, MLA_Q_LORA), lambda i: (0, 0)),
            pl.BlockSpec((1, MLA_KV_LORA), lambda i: (0, 0)),
            pl.BlockSpec((2 * hw, MLA_Q_LORA), lambda i: (0, 0)),
            pl.BlockSpec((MLA_KV_LORA, hw), lambda i: (0, 0)),
            pl.BlockSpec((hw, MLA_KV_LORA), lambda i: (0, 0)),
        ],
        out_specs=[
            pl.BlockSpec((2 * hw, tm), lambda i: (0, i)),
            pl.BlockSpec((tm, hw), lambda i: (i, 0)),
            pl.BlockSpec((tm, LANES), lambda i: (i, 0)),
            pl.BlockSpec((hw, tm), lambda i: (0, i)),
        ],
        out_shape=[
            jax.ShapeDtypeStruct((2 * hw, t), BF16),
            jax.ShapeDtypeStruct((t, hw), BF16),
            jax.ShapeDtypeStruct((t, LANES), BF16),
            jax.ShapeDtypeStruct((hw, t), BF16),
        ],
        compiler_params=_cp(("parallel",)),
        name="mla_prep",
    )(p, p, p, cos_row, sin_row, cos_col, sin_col, qnw, kvnw, wqt, wkn, wvt)


ATTN_HEADS_PER_STEP = 8
ATTN_SUM_ROWS = 16


def _mla_attn_kernel(qt_ref, kt_ref, q_ref, kn_ref, kr_ref, vt_ref, o_ref, m_ref, acc_ref):
    qi = qt_ref[pl.program_id(2)]
    ki = kt_ref[pl.program_id(2)]
    tq = q_ref.shape[1]
    tk = kn_ref.shape[0]
    hd = 2 * LANES

    @pl.when(ki == 0)
    def _():
        m_ref[...] = jnp.full_like(m_ref, NEG_BIG)
        acc_ref[...] = jnp.zeros_like(acc_ref)

    def step(masked):
        kr = kr_ref[...]
        ones = jnp.ones((ATTN_SUM_ROWS, tk), BF16)

        def scores(h):
            k = jnp.concatenate([kn_ref[:, h * MLA_NOPE:(h + 1) * MLA_NOPE], kr], axis=1)
            s = _dot(k, q_ref[h * hd:(h + 1) * hd, :])
            if masked:
                ck = lax.broadcasted_iota(jnp.int32, (tk, tq), 0) // CHUNK
                cq = lax.broadcasted_iota(jnp.int32, (tk, tq), 1) // CHUNK
                s = jnp.where(ck <= cq, s, NEG_BIG)
            return s

        def update(h, s):
            m_prev = m_ref[h]
            m_new = jnp.maximum(m_prev, jnp.max(s, axis=0, keepdims=True))
            alpha = jnp.exp2(m_prev - m_new)
            p = jnp.exp2((s - m_new).astype(BF16))
            v_ext = jnp.concatenate([vt_ref[h * MLA_V:(h + 1) * MLA_V, :], ones], axis=0)
            acc_ref[h] = alpha * acc_ref[h] + _dot(v_ext, p)
            m_ref[h] = m_new

        s_prev = scores(0)
        for h in range(1, ATTN_HEADS_PER_STEP):
            s_next = scores(h)
            update(h - 1, s_prev)
            s_prev = s_next
        update(ATTN_HEADS_PER_STEP - 1, s_prev)

    @pl.when(ki < qi)
    def _():
        step(False)

    @pl.when(ki == qi)
    def _():
        step(True)
        for h in range(ATTN_HEADS_PER_STEP):
            acc = acc_ref[h]
            o = acc[0:MLA_V, :] / acc[MLA_V:MLA_V + 1, :]
            o_ref[:, h * MLA_V:(h + 1) * MLA_V] = o.T.astype(BF16)


def _mla_attn(qt_all, kn, kr, vt, batch, seq, tq=512):
    t = kn.shape[0]
    nq = seq // tq
    hps = ATTN_HEADS_PER_STEP
    pairs = [(qi, ki) for qi in range(nq) for ki in range(qi + 1)]
    qt = jnp.asarray(np.array([pr[0] for pr in pairs], np.int32))
    kt = jnp.asarray(np.array([pr[1] for pr in pairs], np.int32))
    return pl.pallas_call(
        _mla_attn_kernel,
        grid_spec=pltpu.PrefetchScalarGridSpec(
            num_scalar_prefetch=2,
            grid=(batch, MLA_HEADS // hps, len(pairs)),
            in_specs=[
                pl.BlockSpec((hps * 2 * LANES, tq), lambda b, h, pr, qt, kt: (h, b * nq + qt[pr])),
                pl.BlockSpec((tq, hps * MLA_NOPE), lambda b, h, pr, qt, kt: (b * nq + kt[pr], h)),
                pl.BlockSpec((tq, LANES), lambda b, h, pr, qt, kt: (b * nq + kt[pr], 0)),
                pl.BlockSpec((hps * MLA_V, tq), lambda b, h, pr, qt, kt: (h, b * nq + kt[pr])),
            ],
            out_specs=pl.BlockSpec((tq, hps * MLA_V), lambda b, h, pr, qt, kt: (b * nq + qt[pr], h)),
            scratch_shapes=[
                pltpu.VMEM((hps, 1, tq), F32),
                pltpu.VMEM((hps, MLA_V + ATTN_SUM_ROWS, tq), F32),
            ],
        ),
        out_shape=jax.ShapeDtypeStruct((t, MLA_HEADS * MLA_V), BF16),
        compiler_params=_cp(("parallel", "parallel", "arbitrary")),
        name="mla_attn",
    )(qt, kt, qt_all, kn, kr, vt)


def _mix_out_kernel(x_ref, oa_ref, ob_ref, ga_ref, gb_ref, wga_ref, wmo_ref, wout_ref, nw_ref,
                    wr_hi_ref, wr_lo_ref, br_ref, x1_ref, h2_ref, selt_ref, cwt_ref):
    ya = _dot(oa_ref[...], wga_ref[...])
    yb = _dot(ob_ref[...], wmo_ref[...])
    merged = (jax.nn.sigmoid(ga_ref[...].astype(F32)) * ya
              + jax.nn.sigmoid(gb_ref[...].astype(F32)) * yb)
    x1 = x_ref[...] + _dot(merged.astype(BF16), wout_ref[...])
    x1_ref[...] = x1
    h2 = _rms(x1, nw_ref[...])
    h_hi = h2.astype(BF16)
    h2_ref[...] = h_hi

    h_lo = (h2 - h_hi.astype(F32)).astype(BF16)
    logits = (_dot_nt(wr_hi_ref[...], h_hi) + _dot_nt(wr_lo_ref[...], h_hi) + _dot_nt(wr_hi_ref[...], h_lo)
              + br_ref[...])[0:N_EXPERTS, :]
    expert = lax.broadcasted_iota(jnp.int32, logits.shape, 0)
    work = logits
    sel = jnp.zeros(logits.shape, F32)
    cw = jnp.zeros(logits.shape, F32)
    top = None
    denom = None
    for kk in range(TOP_K):
        mx = jnp.max(work, axis=0, keepdims=True)
        am = jnp.min(jnp.where(work == mx, expert, N_EXPERTS), axis=0, keepdims=True)
        hit = expert == am
        if kk == 0:
            top = mx
            e = jnp.ones_like(mx)
            denom = e
        else:
            e = jnp.exp(mx - top)
            denom = denom + e
        sel = jnp.where(hit, 1.0, sel)
        cw = jnp.where(hit, e, cw)
        work = jnp.where(hit, -jnp.inf, work)
    selt_ref[...] = sel.astype(BF16)
    cwt_ref[...] = cw / denom


def _mix_out(x2, oa, ob, p, wga, wmo, wout, nw, wr_hi, wr_lo, br):
    t, d = x2.shape
    tm = br.shape[1]
    full = lambda i: (0, 0)
    return pl.pallas_call(
        _mix_out_kernel,
        grid=(t // tm,),
        in_specs=[
            pl.BlockSpec((tm, d), lambda i: (i, 0)),
            pl.BlockSpec((tm, d), lambda i: (i, 0)),
            pl.BlockSpec((tm, d), lambda i: (i, 0)),
            pl.BlockSpec((tm, d), lambda i: (i, 4)),
            pl.BlockSpec((tm, d), lambda i: (i, 5)),
            pl.BlockSpec((d, d), full),
            pl.BlockSpec((d, d), full),
            pl.BlockSpec((d, d), full),
            pl.BlockSpec((1, d), full),
            pl.BlockSpec((LANES, d), full),
            pl.BlockSpec((LANES, d), full),
            pl.BlockSpec((LANES, tm), full),
        ],
        out_specs=[
            pl.BlockSpec((tm, d), lambda i: (i, 0)),
            pl.BlockSpec((tm, d), lambda i: (i, 0)),
            pl.BlockSpec((N_EXPERTS, tm), lambda i: (0, i)),
            pl.BlockSpec((N_EXPERTS, tm), lambda i: (0, i)),
        ],
        out_shape=[
            jax.ShapeDtypeStruct((t, d), F32),
            jax.ShapeDtypeStruct((t, d), BF16),
            jax.ShapeDtypeStruct((N_EXPERTS, t), BF16),
            jax.ShapeDtypeStruct((N_EXPERTS, t), F32),
        ],
        compiler_params=_cp(("parallel",)),
        name="mix_out",
    )(x2, oa, ob, p, p, wga, wmo, wout, nw, wr_hi, wr_lo, br)


def _route_pos_kernel(selt_ref, lpos_ref, keyt_ref, offs_ref, cnt_ref, tot_ref, carry_ref):
    tm = selt_ref.shape[1]
    i = pl.program_id(0)

    @pl.when(i == 0)
    def _():
        carry_ref[...] = jnp.zeros_like(carry_ref)

    sel_t = selt_ref[...]
    sel_rows = jnp.concatenate([sel_t, jnp.zeros((LANES - N_EXPERTS, tm), BF16)], axis=0)
    row = lax.broadcasted_iota(jnp.int32, (tm, tm), 0)
    col = lax.broadcasted_iota(jnp.int32, (tm, tm), 1)
    lpos_ref[...] = _dot_nt((col < row).astype(BF16), sel_rows)
    pos_t = _dot(sel_t, (row < col).astype(BF16))
    keyt_ref[0] = jnp.where(sel_t > 0, pos_t, -1.0)

    n = _dot_nt(jnp.ones((8, tm), BF16), sel_rows)[0:1, :]
    carry = carry_ref[0:1, :]
    offs_ref[0] = carry.astype(jnp.int32)
    cnt_ref[0] = n.astype(jnp.int32)
    total = carry + jnp.ceil(n * (1.0 / SEG_ALIGN)) * SEG_ALIGN
    carry_ref[...] = jnp.broadcast_to(total, carry_ref.shape)
    tot_ref[...] = jnp.broadcast_to(total, tot_ref.shape).astype(jnp.int32)


def _route_pos(sel_t):
    t = sel_t.shape[1]
    tm = ROUTE_TILE
    nt = t // tm
    return pl.pallas_call(
        _route_pos_kernel,
        grid=(nt,),
        in_specs=[pl.BlockSpec((N_EXPERTS, tm), lambda i: (0, i))],
        out_specs=[
            pl.BlockSpec((tm, LANES), lambda i: (i, 0)),
            pl.BlockSpec((1, N_EXPERTS, tm), lambda i: (i, 0, 0)),
            pl.BlockSpec((1, 1, LANES), lambda i: (i, 0, 0)),
            pl.BlockSpec((1, 1, LANES), lambda i: (i, 0, 0)),
            pl.BlockSpec((8, LANES), lambda i: (0, 0)),
        ],
        out_shape=[
            jax.ShapeDtypeStruct((t, LANES), F32),
            jax.ShapeDtypeStruct((nt, N_EXPERTS, tm), F32),
            jax.ShapeDtypeStruct((nt, 1, LANES), jnp.int32),
            jax.ShapeDtypeStruct((nt, 1, LANES), jnp.int32),
            jax.ShapeDtypeStruct((8, LANES), jnp.int32),
        ],
        scratch_shapes=[pltpu.VMEM((8, LANES), F32)],
        compiler_params=_cp(("arbitrary",)),
        name="route_pos",
    )(sel_t)


def _rows(ref, start, n):
    return ref.at[pl.ds(pl.multiple_of(start, n), n)]


def _pack_pairs(x):
    half = x.shape[1] // 2
    hi = lax.bitcast_convert_type(x[:, :half], jnp.uint32)
    lo = lax.bitcast_convert_type(x[:, half:], jnp.uint32)
    return hi | lax.shift_right_logical(lo, jnp.uint32(16))


def _unpack_pairs(u):
    hi = lax.bitcast_convert_type(u & jnp.uint32(0xFFFF0000), F32).astype(BF16)
    lo = lax.bitcast_convert_type(lax.shift_left(u, jnp.uint32(16)), F32).astype(BF16)
    return hi, lo


def _seg_windows(cnt_ref, base):
    longest = lax.fori_loop(0, N_EXPERTS, lambda e, m: jnp.maximum(m, cnt_ref[base + e]), 0)
    return lax.shift_right_logical(longest + (SEG_WINDOW - 1), SEG_WINDOW.bit_length() - 1)


def _dispatch_kernel(seg_ref, cnt_ref, fill_lo_ref, fill_hi_ref, h_ref, keyt_ref, xs_ref,
                     stage_ref, zero_ref, sem):
    i = pl.program_id(0)
    tm = h_ref.shape[0]
    base = i * N_EXPERTS

    def zero_fills(wait):
        def act(cp):
            if wait:
                cp.wait()
            else:
                cp.start()

        def fill(c):
            return pltpu.make_async_copy(zero_ref.at[pl.ds(0, SEG_ALIGN)], _rows(xs_ref, c * SEG_ALIGN, SEG_ALIGN),
                                         sem.at[2])

        def per_expert(e, carry):
            lo = lax.shift_right_logical(fill_lo_ref[e], SEG_ALIGN.bit_length() - 1)
            hi = lax.shift_right_logical(fill_hi_ref[e], SEG_ALIGN.bit_length() - 1)
            return lax.fori_loop(lo, hi, lambda c, a: (act(fill(c)), a)[1], carry)

        lax.fori_loop(0, N_EXPERTS, per_expert, 0)

        def fill_tail(c):
            return pltpu.make_async_copy(zero_ref, _rows(xs_ref, c * ZERO_ROWS, ZERO_ROWS), sem.at[2])

        lo = lax.shift_right_logical(fill_hi_ref[N_EXPERTS - 1], ZERO_ROWS.bit_length() - 1)
        hi = xs_ref.shape[0] // ZERO_ROWS
        lax.fori_loop(lo, hi, lambda c, a: (act(fill_tail(c)), a)[1], 0)

    @pl.when(i == 0)
    def _():
        zero_ref[...] = jnp.zeros_like(zero_ref)
        zero_fills(wait=False)
        zero_fills(wait=True)

    buf = i % 2

    def send(tile_base, win, b, wait, only_live=False):
        for e in range(N_EXPERTS):
            def go(e=e):
                slot = seg_ref[tile_base + e] + win * SEG_WINDOW
                cp = pltpu.make_async_copy(stage_ref.at[b, pl.ds(e * SEG_WINDOW, SEG_WINDOW)],
                                           xs_ref.at[pl.ds(pl.multiple_of(slot, SEG_ALIGN), SEG_WINDOW)],
                                           sem.at[b])
                if wait:
                    cp.wait()
                else:
                    cp.start()

            if only_live:
                pl.when(cnt_ref[tile_base + e] > win * SEG_WINDOW)(go)
            else:
                go()

    half = N_EXPERTS // 2
    j = lax.broadcasted_iota(jnp.int32, (SEG_WINDOW, tm), 0).astype(F32)

    def build(win, b):
        key = keyt_ref[0] - jnp.asarray(win * SEG_WINDOW, F32)
        for hf in range(2):
            pick = jnp.concatenate([(key[e:e + 1, :] == j).astype(BF16)
                                    for e in range(hf * half, (hf + 1) * half)], axis=0)
            stage_ref[b, hf * half * SEG_WINDOW:(hf + 1) * half * SEG_WINDOW, :] = _pack_pairs(_dot(pick, h_ref[...]))

    n_win = _seg_windows(cnt_ref, base)
    build(0, buf)

    @pl.when(i > 0)
    def _():
        send(base - N_EXPERTS, 0, 1 - buf, wait=True)

    send(base, 0, buf, wait=False)

    def more(win, carry):
        build(win, 1 - buf)
        send(base, win, 1 - buf, wait=False, only_live=True)
        send(base, win, 1 - buf, wait=True, only_live=True)
        return carry

    lax.fori_loop(1, n_win, more, 0)

    @pl.when(i == pl.num_programs(0) - 1)
    def _():
        send(base, 0, buf, wait=True)


def _dispatch(seg, cnt, fill_lo, fill_hi, h2, keyt, n_pad):
    t, d = h2.shape
    tm = ROUTE_TILE
    return pl.pallas_call(
        _dispatch_kernel,
        grid_spec=pltpu.PrefetchScalarGridSpec(
            num_scalar_prefetch=4,
            grid=(t // tm,),
            in_specs=[
                pl.BlockSpec((tm, d), lambda i, *_: (i, 0)),
                pl.BlockSpec((1, N_EXPERTS, tm), lambda i, *_: (i, 0, 0)),
            ],
            out_specs=pl.BlockSpec(memory_space=pl.ANY),
            scratch_shapes=[pltpu.VMEM((2, N_EXPERTS * SEG_WINDOW, d // 2), jnp.uint32),
                            pltpu.VMEM((ZERO_ROWS, d // 2), jnp.uint32),
                            pltpu.SemaphoreType.DMA((3,))],
        ),
        out_shape=jax.ShapeDtypeStruct((n_pad, d // 2), jnp.uint32),
        compiler_params=_cp(("arbitrary",)),
        name="dispatch",
    )(seg, cnt, fill_lo, fill_hi, h2, keyt)


def _experts_kernel(be_ref, nv_ref, xs_ref, wgu_ref, wd_ref, bg_ref, bu_ref, bd_ref, ys_ref,
                    wg_s, wu_s, wd_s):
    j = pl.program_id(0)
    grp = 2 * LANES
    prev = be_ref[jnp.maximum(j - 1, 0)]

    @pl.when((j == 0) | (be_ref[j] != prev))
    def _():
        r = lax.broadcasted_iota(jnp.int32, (grp, grp), 0)
        c = lax.broadcasted_iota(jnp.int32, (grp, grp), 1)
        src = jnp.where(c < LANES, 2 * c, 2 * (c - LANES) + 1)
        pick = (r == src).astype(BF16)
        for g in range(wgu_ref.shape[2] // grp):
            y = _dot(wgu_ref[0, :, g * grp:(g + 1) * grp].astype(BF16), pick)
            wg_s[:, g * LANES:(g + 1) * LANES] = y[:, 0:LANES].astype(BF16)
            wu_s[:, g * LANES:(g + 1) * LANES] = y[:, LANES:grp].astype(BF16)
        wd_s[...] = wd_ref[0].astype(BF16)

    @pl.when(j < nv_ref[0])
    def _():
        x = jnp.concatenate(_unpack_pairs(xs_ref[...]), axis=1)
        g = _dot(x, wg_s[...]) + bg_ref[0]
        u = _dot(x, wu_s[...]) + bu_ref[0]
        gate = jnp.minimum(g, SWIGLU_LIMIT)
        up = jnp.clip(u, -SWIGLU_LIMIT, SWIGLU_LIMIT)
        act = (up + 1.0) * (gate * jax.nn.sigmoid(gate * SWIGLU_ALPHA))
        y = _dot(act.astype(BF16), wd_s[...]) + bd_ref[0]
        ys_ref[...] = _pack_pairs(y.astype(BF16).astype(F32))

    @pl.when(j >= nv_ref[0])
    def _():
        ys_ref[...] = jnp.zeros_like(ys_ref)


def _experts(block_e, n_valid, xs, wgu, wd, bg, bu, bd):
    n_pad = xs.shape[0]
    de, d = wd.shape[1:]
    blk = (MOE_ROWS, xs.shape[1])
    n_blocks = n_pad // MOE_ROWS
    xrow = lambda j, be, nv: (jnp.minimum(j, nv[0] - 1), 0)
    wsel = lambda j, be, nv: (be[j], 0, 0)
    return pl.pallas_call(
        _experts_kernel,
        grid_spec=pltpu.PrefetchScalarGridSpec(
            num_scalar_prefetch=2,
            grid=(n_blocks,),
            in_specs=[
                pl.BlockSpec(blk, xrow),
                pl.BlockSpec((1, d, 2 * de), wsel),
                pl.BlockSpec((1, de, d), wsel),
                pl.BlockSpec((1, 1, de), wsel),
                pl.BlockSpec((1, 1, de), wsel),
                pl.BlockSpec((1, 1, d), wsel),
            ],
            out_specs=pl.BlockSpec(blk, lambda j, be, nv: (j, 0)),
            scratch_shapes=[
                pltpu.VMEM((d, de), BF16),
                pltpu.VMEM((d, de), BF16),
                pltpu.VMEM((de, d), BF16),
            ],
        ),
        out_shape=jax.ShapeDtypeStruct(xs.shape, xs.dtype),
        compiler_params=pltpu.CompilerParams(dimension_semantics=("arbitrary",),
                                             vmem_limit_bytes=EXPERTS_VMEM_LIMIT),
        name="experts",
    )(block_e, n_valid, xs, wgu, wd, bg, bu, bd)


def _combine_kernel(seg_ref, cnt_ref, x1_ref, lpos_ref, cw_ref, nw_ref, ys_ref, o_ref, stage_ref, sem,
                    *, final_norm):
    i = pl.program_id(0)
    tm, d = x1_ref.shape
    base = i * N_EXPERTS

    buf = i % 2

    def gather(tile_base, win, b, wait):
        for e in range(N_EXPERTS):
            slot = seg_ref[tile_base + e] + win * SEG_WINDOW
            cp = pltpu.make_async_copy(ys_ref.at[pl.ds(pl.multiple_of(slot, SEG_ALIGN), SEG_WINDOW)],
                                       stage_ref.at[b, pl.ds(e * SEG_WINDOW, SEG_WINDOW)], sem.at[b])
            if wait:
                cp.wait()
            else:
                cp.start()

    @pl.when(i == 0)
    def _():
        gather(base, 0, buf, wait=False)

    @pl.when(i + 1 < pl.num_programs(0))
    def _():
        gather(base + N_EXPERTS, 0, 1 - buf, wait=False)

    n_stage = N_EXPERTS * SEG_WINDOW
    owner = lax.broadcasted_iota(jnp.int32, (LANES, n_stage), 1) // SEG_WINDOW
    expand = (owner == lax.broadcasted_iota(jnp.int32, (LANES, n_stage), 0)).astype(BF16)
    j = (lax.broadcasted_iota(jnp.int32, (tm, n_stage), 1) % SEG_WINDOW).astype(F32)
    cw_rows = jnp.concatenate([cw_ref[...].astype(BF16), jnp.zeros((LANES - N_EXPERTS, tm), BF16)], axis=0)
    cw_wide = _dot_tn(cw_rows, expand)

    def window(win, y):
        rank = (lpos_ref[...] - jnp.asarray(win * SEG_WINDOW, F32)).astype(BF16)
        take = jnp.where(_dot(rank, expand) == j, cw_wide, 0.0).astype(BF16)
        gather(base, win, buf, wait=True)
        hi, lo = _unpack_pairs(stage_ref[buf])
        return y + jnp.concatenate([_dot(take, hi), _dot(take, lo)], axis=1)

    def more(win, y):
        gather(base, win, buf, wait=False)
        return window(win, y)

    y = window(0, jnp.zeros((tm, d), F32))
    y = lax.fori_loop(1, _seg_windows(cnt_ref, base), more, y)
    out = x1_ref[...] + y
    if final_norm:
        out = _rms(out, nw_ref[...])
    o_ref[...] = out


def _combine(seg, cnt, x1, lpos, cw, nw, ys, final_norm):
    t, d = x1.shape
    tm = ROUTE_TILE
    kern = functools.partial(_combine_kernel, final_norm=final_norm)
    return pl.pallas_call(
        kern,
        grid_spec=pltpu.PrefetchScalarGridSpec(
            num_scalar_prefetch=2,
            grid=(t // tm,),
            in_specs=[
                pl.BlockSpec((tm, d), lambda i, *_: (i, 0)),
                pl.BlockSpec((tm, LANES), lambda i, *_: (i, 0)),
                pl.BlockSpec((N_EXPERTS, tm), lambda i, *_: (0, i)),
                pl.BlockSpec((1, d), lambda i, *_: (0, 0)),
                pl.BlockSpec(memory_space=pl.ANY),
            ],
            out_specs=pl.BlockSpec((tm, d), lambda i, *_: (i, 0)),
            scratch_shapes=[pltpu.VMEM((2, N_EXPERTS * SEG_WINDOW, d // 2), jnp.uint32),
                            pltpu.SemaphoreType.DMA((2,))],
        ),
        out_shape=jax.ShapeDtypeStruct((t, d), F32),
        compiler_params=_cp(("arbitrary",)),
        name="combine",
    )(seg, cnt, x1, lpos, cw, nw, ys)


def _rope_tables(seq):
    half = MLA_ROPE // 2
    inv = 1.0 / (ROPE_THETA ** (jnp.arange(0, MLA_ROPE, 2, dtype=F32) / MLA_ROPE))
    ang = jnp.arange(seq, dtype=F32)[:, None] * inv[None, :]
    cos, sin = jnp.cos(ang), jnp.sin(ang)
    zeros = jnp.zeros((seq, LANES - MLA_ROPE), F32)
    cos_row = jnp.concatenate([cos, cos, zeros], axis=-1)
    sin_row = jnp.concatenate([-sin, sin, zeros], axis=-1)
    del half
    return cos_row, sin_row, cos.T, sin.T


def _pad_cols(a, width):
    return jnp.pad(a, ((0, 0), (0, width - a.shape[1])))


def _layer(x2, batch, seq, final_norm_w, final_norm, rope_tables,
           norm_mix_w, w_in, gdn_conv_w, gdn_a_log, gdn_dt_bias, gdn_norm_w, w_gdn_o,
           mla_q_norm_w, w_mla_q_b, mla_kv_norm_w, w_mla_kv_b, w_mla_o, w_out,
           norm_ffn_w, w_router, b_router, w_gate_up, b_gate_up, w_down, b_down):
    t, d = x2.shape
    qk_w = GDN_HEADS * GDN_D
    o_b = 4 * qk_w
    o_a = o_b + GDN_HEADS
    o_cq = o_a + GDN_HEADS
    o_ckv = o_cq + MLA_Q_LORA
    o_kr = o_ckv + MLA_KV_LORA
    o_ga = o_kr + MLA_ROPE
    o_gb = o_ga + d
    w_p = jnp.concatenate([
        w_in[:, 0:o_b], w_in[:, o_ga:o_gb + d], w_in[:, o_cq:o_ckv], w_in[:, o_ckv:o_kr],
        _pad_cols(w_in[:, o_kr:o_ga], 2 * LANES)], axis=1).astype(BF16)
    w_ab = _pad_cols(jnp.concatenate([w_in[:, o_a:o_cq], w_in[:, o_b:o_a]], axis=1), LANES).astype(BF16)

    p, ab = _in_proj(x2, norm_mix_w[None, :], w_p, w_ab)

    alog_row = _pad_cols(gdn_a_log[None, :].astype(F32), LANES)
    dtb_row = _pad_cols(gdn_dt_bias[None, :].astype(F32), LANES)
    qkvn, cols, gct = _gdn_prep(p, ab, gdn_conv_w.astype(F32), alog_row, dtb_row, seq)
    o_gdn = _gdn_chunk(qkvn, p, cols, gct, gdn_norm_w[None, :].astype(F32), batch, seq)

    hd = MLA_NOPE + MLA_ROPE
    wq = w_mla_q_b.reshape(MLA_Q_LORA, MLA_HEADS, hd)
    wqt = jnp.pad(wq, ((0, 0), (0, 0), (0, 2 * LANES - hd))).reshape(MLA_Q_LORA, MLA_HEADS * 2 * LANES).T
    wkv = w_mla_kv_b.reshape(MLA_KV_LORA, MLA_HEADS, MLA_NOPE + MLA_V)
    wkn = wkv[:, :, :MLA_NOPE].reshape(MLA_KV_LORA, -1)
    wvt = wkv[:, :, MLA_NOPE:].reshape(MLA_KV_LORA, -1).T
    qt, kn, kr, vt = _mla_prep(p, rope_tables, mla_q_norm_w[None, :].astype(F32),
                               mla_kv_norm_w[None, :].astype(F32), wqt.astype(BF16), wkn.astype(BF16),
                               wvt.astype(BF16), seq)
    o_mla = _mla_attn(qt, kn, kr, vt, batch, seq)

    wr = _pad_cols(w_router.astype(F32), LANES).T
    wr_hi = wr.astype(BF16)
    wr_lo = (wr - wr_hi.astype(F32)).astype(BF16)
    br = jnp.broadcast_to(_pad_cols(b_router[None, :].astype(F32), LANES).T, (LANES, ROUTE_TILE))
    x1, h2, sel, cw = _mix_out(x2, o_gdn, o_mla, p, w_gdn_o.astype(BF16), w_mla_o.astype(BF16),
                               w_out.astype(BF16), norm_ffn_w[None, :].astype(F32), wr_hi, wr_lo, br)

    lpos, keyt, offs, cnt, tot = _route_pos(sel)
    n_tiles = t // ROUTE_TILE
    used = tot[0, :N_EXPERTS]
    padded = (used + SEG_WINDOW + MOE_ROWS - 1) // MOE_ROWS * MOE_ROWS
    pad_end = jnp.cumsum(padded)
    pad_start = pad_end - padded
    seg = (pad_start[None, :] + offs[:, 0, :N_EXPERTS]).astype(jnp.int32).reshape(-1)
    cnt = cnt[:, 0, :N_EXPERTS].reshape(-1)
    worst_used = t * TOP_K + n_tiles * N_EXPERTS * (SEG_ALIGN - 1) + N_EXPERTS * SEG_WINDOW
    n_pad = -(-worst_used // MOE_ROWS) * MOE_ROWS + (N_EXPERTS + 1) * MOE_ROWS
    n_blocks = n_pad // MOE_ROWS
    blk_start = jnp.arange(n_blocks, dtype=jnp.int32) * MOE_ROWS
    block_e = jnp.minimum(jnp.sum((pad_end[None, :] <= blk_start[:, None]).astype(jnp.int32), axis=1),
                          N_EXPERTS - 1).astype(jnp.int32)
    n_valid = (pad_end[-1:] // MOE_ROWS).astype(jnp.int32)

    xs = _dispatch(seg, cnt, (pad_start + used).astype(jnp.int32), pad_end.astype(jnp.int32), h2, keyt, n_pad)
    bg = b_gate_up[:, None, 0::2].astype(F32)
    bu = b_gate_up[:, None, 1::2].astype(F32)
    ys = _experts(block_e, n_valid, xs, w_gate_up, w_down, bg, bu, b_down[:, None, :].astype(F32))
    return _combine(seg, cnt, x1, lpos, cw, final_norm_w[None, :].astype(F32), ys, final_norm)


def kernel(x, norm_mix_w, w_in, gdn_conv_w, gdn_a_log, gdn_dt_bias, gdn_norm_w, w_gdn_o, mla_q_norm_w, w_mla_q_b, mla_kv_norm_w, w_mla_kv_b, w_mla_o, w_out, norm_ffn_w, w_router, b_router, w_gate_up, b_gate_up, w_down, b_down, norm_final_w):
    batch, seq, d = x.shape
    depth = w_in.shape[0]
    rope_tables = _rope_tables(seq)
    x2 = x.reshape(batch * seq, d)
    for layer in range(depth):
        x2 = _layer(x2, batch, seq, norm_final_w, layer == depth - 1, rope_tables,
                    norm_mix_w[layer], w_in[layer], gdn_conv_w[layer], gdn_a_log[layer],
                    gdn_dt_bias[layer], gdn_norm_w[layer], w_gdn_o[layer], mla_q_norm_w[layer],
                    w_mla_q_b[layer], mla_kv_norm_w[layer], w_mla_kv_b[layer], w_mla_o[layer],
                    w_out[layer], norm_ffn_w[layer], w_router[layer], b_router[layer],
                    w_gate_up[layer], b_gate_up[layer], w_down[layer], b_down[layer])
    return x2.reshape(batch, seq, d)
```

```python
import functools

import jax
import jax.numpy as jnp
import numpy as np
from jax import lax
from jax.experimental import pallas as pl
from jax.experimental.pallas import tpu as pltpu

F32 = jnp.float32
BF16 = jnp.bfloat16

CHUNK = 64
NORM_EPS = 1e-6
GDN_HEADS = 8
GDN_D = 128
GDN_CONV = 4
MLA_HEADS = 8
MLA_Q_LORA = 512
MLA_KV_LORA = 256
MLA_NOPE = 128
MLA_ROPE = 64
MLA_V = 128
ROPE_THETA = 10000.0
N_EXPERTS = 32
TOP_K = 4
SWIGLU_LIMIT = 7.0
SWIGLU_ALPHA = 1.702

LANES = 128
MOE_ROWS = 512
ROUTE_TILE = 256
SEG_ALIGN = 8
SEG_WINDOW = 64
ZERO_ROWS = 64
VMEM_LIMIT = 48 * 1024 * 1024
EXPERTS_VMEM_LIMIT = 56 * 1024 * 1024

NEG_BIG = -1e30
LOG2_E = 1.4426950408889634


def _cp(sem):
    return pltpu.CompilerParams(dimension_semantics=sem, vmem_limit_bytes=VMEM_LIMIT)


def _dot(a, b):
    return jnp.dot(a, b, preferred_element_type=F32)


def _dot_nt(a, b):
    return lax.dot_general(a, b, (((1,), (1,)), ((), ())), preferred_element_type=F32)


def _dot_tn(a, b):
    return lax.dot_general(a, b, (((0,), (0,)), ((), ())), preferred_element_type=F32)


def _split3(x):
    hi = x.astype(BF16)
    r = x - hi.astype(F32)
    mid = r.astype(BF16)
    lo = (r - mid.astype(F32)).astype(BF16)
    return hi, mid, lo


def _rms(x, w):
    ms = jnp.mean(x * x, axis=-1, keepdims=True)
    return x * lax.rsqrt(ms + NORM_EPS) * w


def _in_proj_kernel(x_ref, nw_ref, w_ref, wab_ref, p_ref, ab_ref, h_ref):
    @pl.when(pl.program_id(1) == 0)
    def _():
        hb = _rms(x_ref[...], nw_ref[...]).astype(BF16)
        h_ref[...] = hb
        ab_ref[...] = _dot(hb, wab_ref[...])

    p_ref[...] = _dot(h_ref[...], w_ref[...]).astype(BF16)


def _in_proj(x2, norm_w, w_p, w_ab, tm=1024, tn=3584):
    t, d = x2.shape
    n = w_p.shape[1]
    return pl.pallas_call(
        _in_proj_kernel,
        grid=(t // tm, n // tn),
        in_specs=[
            pl.BlockSpec((tm, d), lambda i, j: (i, 0)),
            pl.BlockSpec((1, d), lambda i, j: (0, 0)),
            pl.BlockSpec((d, tn), lambda i, j: (0, j)),
            pl.BlockSpec((d, LANES), lambda i, j: (0, 0)),
        ],
        out_specs=[
            pl.BlockSpec((tm, tn), lambda i, j: (i, j)),
            pl.BlockSpec((tm, LANES), lambda i, j: (i, 0)),
        ],
        out_shape=[
            jax.ShapeDtypeStruct((t, n), BF16),
            jax.ShapeDtypeStruct((t, LANES), F32),
        ],
        scratch_shapes=[pltpu.VMEM((tm, d), BF16)],
        compiler_params=_cp(("parallel", "arbitrary")),
        name="in_proj",
    )(x2, norm_w, w_p, w_ab)


def _gdn_prep_kernel(cur_ref, prev_ref, ab_ref, cw_ref, alog_ref, dtb_ref,
                     qkv_ref, cols_ref, gct_ref, *, tiles_per_seq):
    tm = cur_ref.shape[0]
    i = pl.program_id(0)
    halo_on = (i % tiles_per_seq) != 0
    q_scale = GDN_D ** -0.5
    grp = 2 * LANES
    row = lax.broadcasted_iota(jnp.int32, (tm, tm), 0)
    col = lax.broadcasted_iota(jnp.int32, (tm, tm), 1)
    shift = [(col == row - s).astype(BF16) for s in range(1, GDN_CONV)]
    for cg in range(cur_ref.shape[1] // grp):
        gs = slice(cg * grp, (cg + 1) * grp)
        cur_b = cur_ref[:, gs]
        cur = cur_b.astype(F32)
        w = cw_ref[:, gs]
        y = w[GDN_CONV - 1:GDN_CONV, :] * cur
        for s in range(1, GDN_CONV):
            y = y + w[GDN_CONV - 1 - s:GDN_CONV - s, :] * _dot(shift[s - 1], cur_b)
        halo = jnp.where(halo_on, prev_ref[:, gs].astype(F32)[8:16, :], 0.0)
        xe = jnp.concatenate([halo, cur[0:8, :]], axis=0)
        head = w[0:1, :] * xe[5:13, :]
        for j in range(1, GDN_CONV):
            head = head + w[j:j + 1, :] * xe[5 + j:13 + j, :]
        y = jnp.concatenate([head, y[8:, :]], axis=0)
        hy = 0.5 * y
        y = hy + hy * jnp.tanh(hy)
        for half in range(2):
            cb = 2 * cg + half
            yh = y[:, half * LANES:(half + 1) * LANES]
            if cb < 2 * GDN_HEADS:
                ss = jnp.sum(yh * yh, axis=-1, keepdims=True)
                yh = yh * lax.rsqrt(ss + NORM_EPS)
                if cb < GDN_HEADS:
                    yh = yh * q_scale
            qkv_ref[:, cb * LANES:(cb + 1) * LANES] = yh.astype(BF16)

    ab = ab_ref[...]
    g = -jnp.exp(alog_ref[...]) * jax.nn.softplus(ab + dtb_ref[...])
    row = lax.broadcasted_iota(jnp.int32, (tm, tm), 0)
    col = lax.broadcasted_iota(jnp.int32, (tm, tm), 1)
    tri = ((col <= row) & ((row // CHUNK) == (col // CHUNK))).astype(BF16)
    g_hi, g_mid, g_lo = _split3(g)
    gc = _dot(tri, g_hi) + _dot(tri, g_mid) + _dot(tri, g_lo)
    lane = lax.broadcasted_iota(jnp.int32, (tm, LANES), 1)
    cols_ref[...] = jnp.where(lane < GDN_HEADS, gc, jax.nn.sigmoid(ab))
    for c in range(tm // CHUNK):
        blk = gc[c * CHUNK:(c + 1) * CHUNK, :]
        blk = jnp.concatenate([blk, jnp.zeros_like(blk)], axis=0)
        gct_ref[c] = blk.T[0:GDN_HEADS, 0:CHUNK]


def _gdn_prep(p, ab, conv_w, alog_row, dtb_row, seq, tm=256):
    t = p.shape[0]
    cw = 3 * GDN_HEADS * GDN_D
    tiles_per_seq = seq // tm
    kern = functools.partial(_gdn_prep_kernel, tiles_per_seq=tiles_per_seq)
    return pl.pallas_call(
        kern,
        grid=(t // tm,),
        in_specs=[
            pl.BlockSpec((tm, cw), lambda i: (i, 0)),
            pl.BlockSpec((16, cw), lambda i: (jnp.maximum(i * (tm // 16) - 1, 0), 0)),
            pl.BlockSpec((tm, LANES), lambda i: (i, 0)),
            pl.BlockSpec((GDN_CONV, cw), lambda i: (0, 0)),
            pl.BlockSpec((1, LANES), lambda i: (0, 0)),
            pl.BlockSpec((1, LANES), lambda i: (0, 0)),
        ],
        out_specs=[
            pl.BlockSpec((tm, cw), lambda i: (i, 0)),
            pl.BlockSpec((tm, LANES), lambda i: (i, 0)),
            pl.BlockSpec((tm // CHUNK, GDN_HEADS, CHUNK), lambda i: (i, 0, 0)),
        ],
        out_shape=[
            jax.ShapeDtypeStruct((t, cw), BF16),
            jax.ShapeDtypeStruct((t, LANES), F32),
            jax.ShapeDtypeStruct((t // CHUNK, GDN_HEADS, CHUNK), F32),
        ],
        compiler_params=_cp(("parallel",)),
        name="gdn_prep",
    )(p, p, ab, conv_w, alog_row, dtb_row)


GDN_CHUNKS_PER_STEP = 2


def _gdn_chunk_kernel(q_ref, k_ref, v_ref, z_ref, cols_ref, gct_ref, nw_ref, o_ref, s_ref):
    c = CHUNK
    nb = q_ref.shape[0]
    chains = [(b, h) for b in range(nb) for h in range(GDN_HEADS)]
    units = [(g, b, h) for g in range(GDN_CHUNKS_PER_STEP) for b, h in chains]

    @pl.when(pl.program_id(0) == 0)
    def _():
        s_ref[...] = jnp.zeros_like(s_ref)

    ri = lax.broadcasted_iota(jnp.int32, (c, c), 0)
    ci = lax.broadcasted_iota(jnp.int32, (c, c), 1)
    incl = ri >= ci
    strict = ri > ci
    eye = (ri == ci).astype(F32)
    nw = nw_ref[...]

    def rows(g):
        return slice(g * c, (g + 1) * c)

    def head(h):
        return slice(h * GDN_D, (h + 1) * GDN_D)

    cols, e_g, e_kd, e_last, gct = {}, {}, {}, {}, {}
    for g in range(GDN_CHUNKS_PER_STEP):
        for b in range(nb):
            cb = cols_ref[b, rows(g), :]
            last = cb[c - 1:c, :]
            cols[g, b] = cb
            e_g[g, b] = jnp.exp(cb)
            e_kd[g, b] = jnp.exp(last - cb)
            e_last[g, b] = jnp.exp(last)
            gct[g, b] = gct_ref[b, g]

    kq, kb_l, kf_l = [], [], []
    for g, b, h in units:
        k = k_ref[b, rows(g), head(h)]
        kf = k.astype(F32)
        kb = kf * cols[g, b][:, GDN_HEADS + h:GDN_HEADS + h + 1]
        kq.append(_dot_nt(jnp.concatenate([kb.astype(BF16), q_ref[b, rows(g), head(h)]], axis=0), k))
        kb_l.append(kb)
        kf_l.append(kf)

    a_l, qk_l = [], []
    for i, (g, b, h) in enumerate(units):
        dec = jnp.exp(jnp.minimum(cols[g, b][:, h:h + 1] - gct[g, b][h:h + 1, :], 0.0))
        a_l.append(jnp.where(strict, -kq[i][0:c, :] * dec, 0.0))
        qk_l.append(jnp.where(incl, kq[i][c:2 * c, :] * dec, 0.0).astype(BF16))

    tinv = [eye + a for a in a_l]
    pw = a_l
    for _ in range(5):
        pwb = [x.astype(BF16) for x in pw]
        pw = [_dot(x, x) for x in pwb]
        tinv = [t + _dot(t.astype(BF16), x.astype(BF16)) for t, x in zip(tinv, pw)]

    u_l, lhs_l, kd_l = [], [], []
    for i, (g, b, h) in enumerate(units):
        beta = cols[g, b][:, GDN_HEADS + h:GDN_HEADS + h + 1]
        eg = e_g[g, b][:, h:h + 1]
        rhs = jnp.concatenate([v_ref[b, rows(g), head(h)].astype(F32) * beta, kb_l[i] * eg],
                              axis=1).astype(BF16)
        uw = _dot(tinv[i].astype(BF16), rhs)
        qd = (q_ref[b, rows(g), head(h)].astype(F32) * eg).astype(BF16)
        u_l.append(uw[:, 0:GDN_D])
        lhs_l.append(jnp.concatenate([uw[:, GDN_D:2 * GDN_D].astype(BF16), qd], axis=0))
        kd_l.append((kf_l[i] * e_kd[g, b][:, h:h + 1]).astype(BF16))

    for g in range(GDN_CHUNKS_PER_STEP):
        first = g * len(chains)
        r_l = [_dot(lhs_l[first + n], s_ref[n].astype(BF16)) for n in range(len(chains))]
        for n, (b, h) in enumerate(chains):
            i = first + n
            v_new = (u_l[i] - r_l[n][0:c, :]).astype(BF16)
            o = r_l[n][c:2 * c, :] + _dot(qk_l[i], v_new)
            s_ref[n] = s_ref[n] * e_last[g, b][:, h:h + 1] + _dot_tn(kd_l[i], v_new)
            z = z_ref[b, rows(g), head(h)].astype(F32)
            o_ref[b, rows(g), head(h)] = (_rms(o, nw) * (z * jax.nn.sigmoid(z))).astype(BF16)


def _gdn_chunk(qkvn, p, cols, gct, norm_w, batch, seq):
    nc = seq // CHUNK
    hw = GDN_HEADS * GDN_D
    g = GDN_CHUNKS_PER_STEP
    qkvn3 = qkvn.reshape(batch, seq, qkvn.shape[1])
    p3 = p.reshape(batch, seq, p.shape[1])
    cols3 = cols.reshape(batch, seq, LANES)
    gct4 = gct.reshape(batch, nc, GDN_HEADS, CHUNK)
    tile = lambda col: pl.BlockSpec((batch, g * CHUNK, hw), lambda c: (0, c, col))
    out = pl.pallas_call(
        _gdn_chunk_kernel,
        grid=(nc // g,),
        in_specs=[
            tile(0), tile(1), tile(2),
            tile(3),
            pl.BlockSpec((batch, g * CHUNK, LANES), lambda c: (0, c, 0)),
            pl.BlockSpec((batch, g, GDN_HEADS, CHUNK), lambda c: (0, c, 0, 0)),
            pl.BlockSpec((1, GDN_D), lambda c: (0, 0)),
        ],
        out_specs=pl.BlockSpec((batch, g * CHUNK, hw), lambda c: (0, c, 0)),
        out_shape=jax.ShapeDtypeStruct((batch, seq, hw), BF16),
        scratch_shapes=[pltpu.VMEM((batch * GDN_HEADS, GDN_D, GDN_D), F32)],
        compiler_params=_cp(("arbitrary",)),
        name="gdn_chunk",
    )(qkvn3, qkvn3, qkvn3, p3, cols3, gct4, norm_w)
    return out.reshape(batch * seq, hw)


def _rope(x, cos, sin_signed):
    lane = lax.broadcasted_iota(jnp.int32, x.shape, 1)
    fwd = pltpu.roll(x, LANES - MLA_ROPE // 2, 1)
    bwd = pltpu.roll(x, MLA_ROPE // 2, 1)
    rot = jnp.where(lane < MLA_ROPE // 2, fwd, bwd)
    return x * cos + rot * sin_signed


def _mla_prep_kernel(cq_ref, ckv_ref, kr_ref, cos_ref, sin_ref, cost_ref, sint_ref, qnw_ref, kvnw_ref,
                     wqt_ref, wkn_ref, wvt_ref, qt_ref, kn_ref, kro_ref, vt_ref):
    cos = cos_ref[...]
    sin = sin_ref[...]
    cos_t = cost_ref[...]
    sin_t = sint_ref[...]
    cq = _rms(cq_ref[...].astype(F32), qnw_ref[...]).astype(BF16)
    hd = 2 * LANES
    half = MLA_ROPE // 2
    scale = (MLA_NOPE + MLA_ROPE) ** -0.5 * LOG2_E
    for h in range(MLA_HEADS):
        qh = _dot_nt(wqt_ref[h * hd:(h + 1) * hd, :], cq) * scale
        lo = qh[MLA_NOPE:MLA_NOPE + half, :]
        hi = qh[MLA_NOPE + half:MLA_NOPE + MLA_ROPE, :]
        qt_ref[h * hd:h * hd + MLA_NOPE, :] = qh[0:MLA_NOPE, :].astype(BF16)
        qt_ref[h * hd + MLA_NOPE:h * hd + MLA_NOPE + half, :] = (lo * cos_t - hi * sin_t).astype(BF16)
        qt_ref[h * hd + MLA_NOPE + half:h * hd + MLA_NOPE + MLA_ROPE, :] = (hi * cos_t + lo * sin_t).astype(BF16)
        qt_ref[h * hd + MLA_NOPE + MLA_ROPE:(h + 1) * hd, :] = qh[MLA_NOPE + MLA_ROPE:hd, :].astype(BF16)
    kvl = _rms(ckv_ref[...].astype(F32), kvnw_ref[...]).astype(BF16)
    kn_ref[...] = _dot(kvl, wkn_ref[...]).astype(BF16)
    vt_ref[...] = _dot_nt(wvt_ref[...], kvl).astype(BF16)
    kro_ref[...] = _rope(kr_ref[...].astype(F32), cos, sin).astype(BF16)


def _mla_prep(p, tables, qnw, kvnw, wqt, wkn, wvt, seq, tm=512):
    cos_row, sin_row, cos_col, sin_col = tables
    t = p.shape[0]
    tiles_per_seq = seq // tm
    hw = MLA_HEADS * MLA_NOPE
    half = MLA_ROPE // 2
    cq_blk = 6144 // MLA_Q_LORA
    ckv_blk = 6656 // MLA_KV_LORA
    kr_blk = 6912 // LANES
    return pl.pallas_call(
        _mla_prep_kernel,
        grid=(t // tm,),
        in_specs=[
            pl.BlockSpec((tm, MLA_Q_LORA), lambda i: (i, cq_blk)),
            pl.BlockSpec((tm, MLA_KV_LORA), lambda i: (i, ckv_blk)),
            pl.BlockSpec((tm, LANES), lambda i: (i, kr_blk)),
            pl.BlockSpec((tm, LANES), lambda i: (i % tiles_per_seq, 0)),
            pl.BlockSpec((tm, LANES), lambda i: (i % tiles_per_seq, 0)),
            pl.BlockSpec((half, tm), lambda i: (0, i % tiles_per_seq)),
            pl.BlockSpec((half, tm), lambda i: (0, i % tiles_per_seq)),
            pl.BlockSpec((1, MLA_Q_LORA), lambda i: (0, 0)),
            pl.BlockSpec((1, MLA_KV_LORA), lambda i: (0, 0)),
            pl.BlockSpec((2 * hw, MLA_Q_LORA), lambda i: (0, 0)),
            pl.BlockSpec((MLA_KV_LORA, hw), lambda i: (0, 0)),
            pl.BlockSpec((hw, MLA_KV_LORA), lambda i: (0, 0)),
        ],
        out_specs=[
            pl.BlockSpec((2 * hw, tm), lambda i: (0, i)),
            pl.BlockSpec((tm, hw), lambda i: (i, 0)),
            pl.BlockSpec((tm, LANES), lambda i: (i, 0)),
            pl.BlockSpec((hw, tm), lambda i: (0, i)),
        ],
        out_shape=[
            jax.ShapeDtypeStruct((2 * hw, t), BF16),
            jax.ShapeDtypeStruct((t, hw), BF16),
            jax.ShapeDtypeStruct((t, LANES), BF16),
            jax.ShapeDtypeStruct((hw, t), BF16),
        ],
        compiler_params=_cp(("parallel",)),
        name="mla_prep",
    )(p, p, p, cos_row, sin_row, cos_col, sin_col, qnw, kvnw, wqt, wkn, wvt)


ATTN_HEADS_PER_STEP = 8
ATTN_SUM_ROWS = 16


def _mla_attn_kernel(qt_ref, kt_ref, q_ref, kn_ref, kr_ref, vt_ref, o_ref, m_ref, acc_ref):
    qi = qt_ref[pl.program_id(2)]
    ki = kt_ref[pl.program_id(2)]
    tq = q_ref.shape[1]
    tk = kn_ref.shape[0]
    hd = 2 * LANES

    @pl.when(ki == 0)
    def _():
        m_ref[...] = jnp.full_like(m_ref, NEG_BIG)
        acc_ref[...] = jnp.zeros_like(acc_ref)

    def step(masked):
        kr = kr_ref[...]
        ones = jnp.ones((ATTN_SUM_ROWS, tk), BF16)

        def scores(h):
            k = jnp.concatenate([kn_ref[:, h * MLA_NOPE:(h + 1) * MLA_NOPE], kr], axis=1)
            s = _dot(k, q_ref[h * hd:(h + 1) * hd, :])
            if masked:
                ck = lax.broadcasted_iota(jnp.int32, (tk, tq), 0) // CHUNK
                cq = lax.broadcasted_iota(jnp.int32, (tk, tq), 1) // CHUNK
                s = jnp.where(ck <= cq, s, NEG_BIG)
            return s

        def update(h, s):
            m_prev = m_ref[h]
            m_new = jnp.maximum(m_prev, jnp.max(s, axis=0, keepdims=True))
            alpha = jnp.exp2(m_prev - m_new)
            p = jnp.exp2((s - m_new).astype(BF16))
            v_ext = jnp.concatenate([vt_ref[h * MLA_V:(h + 1) * MLA_V, :], ones], axis=0)
            acc_ref[h] = alpha * acc_ref[h] + _dot(v_ext, p)
            m_ref[h] = m_new

        s_prev = scores(0)
        for h in range(1, ATTN_HEADS_PER_STEP):
            s_next = scores(h)
            update(h - 1, s_prev)
            s_prev = s_next
        update(ATTN_HEADS_PER_STEP - 1, s_prev)

    @pl.when(ki < qi)
    def _():
        step(False)

    @pl.when(ki == qi)
    def _():
        step(True)
        for h in range(ATTN_HEADS_PER_STEP):
            acc = acc_ref[h]
            o = acc[0:MLA_V, :] / acc[MLA_V:MLA_V + 1, :]
            o_ref[:, h * MLA_V:(h + 1) * MLA_V] = o.T.astype(BF16)


def _mla_attn(qt_all, kn, kr, vt, batch, seq, tq=512):
    t = kn.shape[0]
    nq = seq // tq
    hps = ATTN_HEADS_PER_STEP
    pairs = [(qi, ki) for qi in range(nq) for ki in range(qi + 1)]
    qt = jnp.asarray(np.array([pr[0] for pr in pairs], np.int32))
    kt = jnp.asarray(np.array([pr[1] for pr in pairs], np.int32))
    return pl.pallas_call(
        _mla_attn_kernel,
        grid_spec=pltpu.PrefetchScalarGridSpec(
            num_scalar_prefetch=2,
            grid=(batch, MLA_HEADS // hps, len(pairs)),
            in_specs=[
                pl.BlockSpec((hps * 2 * LANES, tq), lambda b, h, pr, qt, kt: (h, b * nq + qt[pr])),
                pl.BlockSpec((tq, hps * MLA_NOPE), lambda b, h, pr, qt, kt: (b * nq + kt[pr], h)),
                pl.BlockSpec((tq, LANES), lambda b, h, pr, qt, kt: (b * nq + kt[pr], 0)),
                pl.BlockSpec((hps * MLA_V, tq), lambda b, h, pr, qt, kt: (h, b * nq + kt[pr])),
            ],
            out_specs=pl.BlockSpec((tq, hps * MLA_V), lambda b, h, pr, qt, kt: (b * nq + qt[pr], h)),
            scratch_shapes=[
                pltpu.VMEM((hps, 1, tq), F32),
                pltpu.VMEM((hps, MLA_V + ATTN_SUM_ROWS, tq), F32),
            ],
        ),
        out_shape=jax.ShapeDtypeStruct((t, MLA_HEADS * MLA_V), BF16),
        compiler_params=_cp(("parallel", "parallel", "arbitrary")),
        name="mla_attn",
    )(qt, kt, qt_all, kn, kr, vt)


def _mix_out_kernel(x_ref, oa_ref, ob_ref, ga_ref, gb_ref, wga_ref, wmo_ref, wout_ref, nw_ref,
                    wr_hi_ref, wr_lo_ref, br_ref, x1_ref, h2_ref, selt_ref, cwt_ref):
    ya = _dot(oa_ref[...], wga_ref[...])
    yb = _dot(ob_ref[...], wmo_ref[...])
    merged = (jax.nn.sigmoid(ga_ref[...].astype(F32)) * ya
              + jax.nn.sigmoid(gb_ref[...].astype(F32)) * yb)
    x1 = x_ref[...] + _dot(merged.astype(BF16), wout_ref[...])
    x1_ref[...] = x1
    h2 = _rms(x1, nw_ref[...])
    h_hi = h2.astype(BF16)
    h2_ref[...] = h_hi

    h_lo = (h2 - h_hi.astype(F32)).astype(BF16)
    logits = (_dot_nt(wr_hi_ref[...], h_hi) + _dot_nt(wr_lo_ref[...], h_hi) + _dot_nt(wr_hi_ref[...], h_lo)
              + br_ref[...])[0:N_EXPERTS, :]
    expert = lax.broadcasted_iota(jnp.int32, logits.shape, 0)
    work = logits
    sel = jnp.zeros(logits.shape, F32)
    cw = jnp.zeros(logits.shape, F32)
    top = None
    denom = None
    for kk in range(TOP_K):
        mx = jnp.max(work, axis=0, keepdims=True)
        am = jnp.min(jnp.where(work == mx, expert, N_EXPERTS), axis=0, keepdims=True)
        hit = expert == am
        if kk == 0:
            top = mx
            e = jnp.ones_like(mx)
            denom = e
        else:
            e = jnp.exp(mx - top)
            denom = denom + e
        sel = jnp.where(hit, 1.0, sel)
        cw = jnp.where(hit, e, cw)
        work = jnp.where(hit, -jnp.inf, work)
    selt_ref[...] = sel.astype(BF16)
    cwt_ref[...] = cw / denom


def _mix_out(x2, oa, ob, p, wga, wmo, wout, nw, wr_hi, wr_lo, br):
    t, d = x2.shape
    tm = br.shape[1]
    full = lambda i: (0, 0)
    return pl.pallas_call(
        _mix_out_kernel,
        grid=(t // tm,),
        in_specs=[
            pl.BlockSpec((tm, d), lambda i: (i, 0)),
            pl.BlockSpec((tm, d), lambda i: (i, 0)),
            pl.BlockSpec((tm, d), lambda i: (i, 0)),
            pl.BlockSpec((tm, d), lambda i: (i, 4)),
            pl.BlockSpec((tm, d), lambda i: (i, 5)),
            pl.BlockSpec((d, d), full),
            pl.BlockSpec((d, d), full),
            pl.BlockSpec((d, d), full),
            pl.BlockSpec((1, d), full),
            pl.BlockSpec((LANES, d), full),
            pl.BlockSpec((LANES, d), full),
            pl.BlockSpec((LANES, tm), full),
        ],
        out_specs=[
            pl.BlockSpec((tm, d), lambda i: (i, 0)),
            pl.BlockSpec((tm, d), lambda i: (i, 0)),
            pl.BlockSpec((N_EXPERTS, tm), lambda i: (0, i)),
            pl.BlockSpec((N_EXPERTS, tm), lambda i: (0, i)),
        ],
        out_shape=[
            jax.ShapeDtypeStruct((t, d), F32),
            jax.ShapeDtypeStruct((t, d), BF16),
            jax.ShapeDtypeStruct((N_EXPERTS, t), BF16),
            jax.ShapeDtypeStruct((N_EXPERTS, t), F32),
        ],
        compiler_params=_cp(("parallel",)),
        name="mix_out",
    )(x2, oa, ob, p, p, wga, wmo, wout, nw, wr_hi, wr_lo, br)


def _route_pos_kernel(selt_ref, lpos_ref, keyt_ref, offs_ref, cnt_ref, tot_ref, carry_ref):
    tm = selt_ref.shape[1]
    i = pl.program_id(0)

    @pl.when(i == 0)
    def _():
        carry_ref[...] = jnp.zeros_like(carry_ref)

    sel_t = selt_ref[...]
    sel_rows = jnp.concatenate([sel_t, jnp.zeros((LANES - N_EXPERTS, tm), BF16)], axis=0)
    row = lax.broadcasted_iota(jnp.int32, (tm, tm), 0)
    col = lax.broadcasted_iota(jnp.int32, (tm, tm), 1)
    lpos_ref[...] = _dot_nt((col < row).astype(BF16), sel_rows)
    pos_t = _dot(sel_t, (row < col).astype(BF16))
    keyt_ref[0] = jnp.where(sel_t > 0, pos_t, -1.0)

    n = _dot_nt(jnp.ones((8, tm), BF16), sel_rows)[0:1, :]
    carry = carry_ref[0:1, :]
    offs_ref[0] = carry.astype(jnp.int32)
    cnt_ref[0] = n.astype(jnp.int32)
    total = carry + jnp.ceil(n * (1.0 / SEG_ALIGN)) * SEG_ALIGN
    carry_ref[...] = jnp.broadcast_to(total, carry_ref.shape)
    tot_ref[...] = jnp.broadcast_to(total, tot_ref.shape).astype(jnp.int32)


def _route_pos(sel_t):
    t = sel_t.shape[1]
    tm = ROUTE_TILE
    nt = t // tm
    return pl.pallas_call(
        _route_pos_kernel,
        grid=(nt,),
        in_specs=[pl.BlockSpec((N_EXPERTS, tm), lambda i: (0, i))],
        out_specs=[
            pl.BlockSpec((tm, LANES), lambda i: (i, 0)),
            pl.BlockSpec((1, N_EXPERTS, tm), lambda i: (i, 0, 0)),
            pl.BlockSpec((1, 1, LANES), lambda i: (i, 0, 0)),
            pl.BlockSpec((1, 1, LANES), lambda i: (i, 0, 0)),
            pl.BlockSpec((8, LANES), lambda i: (0, 0)),
        ],
        out_shape=[
            jax.ShapeDtypeStruct((t, LANES), F32),
            jax.ShapeDtypeStruct((nt, N_EXPERTS, tm), F32),
            jax.ShapeDtypeStruct((nt, 1, LANES), jnp.int32),
            jax.ShapeDtypeStruct((nt, 1, LANES), jnp.int32),
            jax.ShapeDtypeStruct((8, LANES), jnp.int32),
        ],
        scratch_shapes=[pltpu.VMEM((8, LANES), F32)],
        compiler_params=_cp(("arbitrary",)),
        name="route_pos",
    )(sel_t)


def _rows(ref, start, n):
    return ref.at[pl.ds(pl.multiple_of(start, n), n)]


def _pack_pairs(x):
    half = x.shape[1] // 2
    hi = lax.bitcast_convert_type(x[:, :half], jnp.uint32)
    lo = lax.bitcast_convert_type(x[:, half:], jnp.uint32)
    return hi | lax.shift_right_logical(lo, jnp.uint32(16))


def _unpack_pairs(u):
    hi = lax.bitcast_convert_type(u & jnp.uint32(0xFFFF0000), F32).astype(BF16)
    lo = lax.bitcast_convert_type(lax.shift_left(u, jnp.uint32(16)), F32).astype(BF16)
    return hi, lo


def _seg_windows(cnt_ref, base):
    longest = lax.fori_loop(0, N_EXPERTS, lambda e, m: jnp.maximum(m, cnt_ref[base + e]), 0)
    return lax.shift_right_logical(longest + (SEG_WINDOW - 1), SEG_WINDOW.bit_length() - 1)


def _dispatch_kernel(seg_ref, cnt_ref, fill_lo_ref, fill_hi_ref, h_ref, keyt_ref, xs_ref,
                     stage_ref, zero_ref, sem):
    i = pl.program_id(0)
    tm = h_ref.shape[0]
    base = i * N_EXPERTS

    def zero_fills(wait):
        def act(cp):
            if wait:
                cp.wait()
            else:
                cp.start()

        def fill(c):
            return pltpu.make_async_copy(zero_ref.at[pl.ds(0, SEG_ALIGN)], _rows(xs_ref, c * SEG_ALIGN, SEG_ALIGN),
                                         sem.at[2])

        def per_expert(e, carry):
            lo = lax.shift_right_logical(fill_lo_ref[e], SEG_ALIGN.bit_length() - 1)
            hi = lax.shift_right_logical(fill_hi_ref[e], SEG_ALIGN.bit_length() - 1)
            return lax.fori_loop(lo, hi, lambda c, a: (act(fill(c)), a)[1], carry)

        lax.fori_loop(0, N_EXPERTS, per_expert, 0)

        def fill_tail(c):
            return pltpu.make_async_copy(zero_ref, _rows(xs_ref, c * ZERO_ROWS, ZERO_ROWS), sem.at[2])

        lo = lax.shift_right_logical(fill_hi_ref[N_EXPERTS - 1], ZERO_ROWS.bit_length() - 1)
        hi = xs_ref.shape[0] // ZERO_ROWS
        lax.fori_loop(lo, hi, lambda c, a: (act(fill_tail(c)), a)[1], 0)

    @pl.when(i == 0)
    def _():
        zero_ref[...] = jnp.zeros_like(zero_ref)
        zero_fills(wait=False)
        zero_fills(wait=True)

    buf = i % 2

    def send(tile_base, win, b, wait, only_live=False):
        for e in range(N_EXPERTS):
            def go(e=e):
                slot = seg_ref[tile_base + e] + win * SEG_WINDOW
                cp = pltpu.make_async_copy(stage_ref.at[b, pl.ds(e * SEG_WINDOW, SEG_WINDOW)],
                                           xs_ref.at[pl.ds(pl.multiple_of(slot, SEG_ALIGN), SEG_WINDOW)],
                                           sem.at[b])
                if wait:
                    cp.wait()
                else:
                    cp.start()

            if only_live:
                pl.when(cnt_ref[tile_base + e] > win * SEG_WINDOW)(go)
            else:
                go()

    half = N_EXPERTS // 2
    j = lax.broadcasted_iota(jnp.int32, (SEG_WINDOW, tm), 0).astype(F32)

    def build(win, b):
        key = keyt_ref[0] - jnp.asarray(win * SEG_WINDOW, F32)
        for hf in range(2):
            pick = jnp.concatenate([(key[e:e + 1, :] == j).astype(BF16)
                                    for e in range(hf * half, (hf + 1) * half)], axis=0)
            stage_ref[b, hf * half * SEG_WINDOW:(hf + 1) * half * SEG_WINDOW, :] = _pack_pairs(_dot(pick, h_ref[...]))

    n_win = _seg_windows(cnt_ref, base)
    build(0, buf)

    @pl.when(i > 0)
    def _():
        send(base - N_EXPERTS, 0, 1 - buf, wait=True)

    send(base, 0, buf, wait=False)

    def more(win, carry):
        build(win, 1 - buf)
        send(base, win, 1 - buf, wait=False, only_live=True)
        send(base, win, 1 - buf, wait=True, only_live=True)
        return carry

    lax.fori_loop(1, n_win, more, 0)

    @pl.when(i == pl.num_programs(0) - 1)
    def _():
        send(base, 0, buf, wait=True)


def _dispatch(seg, cnt, fill_lo, fill_hi, h2, keyt, n_pad):
    t, d = h2.shape
    tm = ROUTE_TILE
    return pl.pallas_call(
        _dispatch_kernel,
        grid_spec=pltpu.PrefetchScalarGridSpec(
            num_scalar_prefetch=4,
            grid=(t // tm,),
            in_specs=[
                pl.BlockSpec((tm, d), lambda i, *_: (i, 0)),
                pl.BlockSpec((1, N_EXPERTS, tm), lambda i, *_: (i, 0, 0)),
            ],
            out_specs=pl.BlockSpec(memory_space=pl.ANY),
            scratch_shapes=[pltpu.VMEM((2, N_EXPERTS * SEG_WINDOW, d // 2), jnp.uint32),
                            pltpu.VMEM((ZERO_ROWS, d // 2), jnp.uint32),
                            pltpu.SemaphoreType.DMA((3,))],
        ),
        out_shape=jax.ShapeDtypeStruct((n_pad, d // 2), jnp.uint32),
        compiler_params=_cp(("arbitrary",)),
        name="dispatch",
    )(seg, cnt, fill_lo, fill_hi, h2, keyt)


def _experts_kernel(be_ref, nv_ref, xs_ref, wgu_ref, wd_ref, bg_ref, bu_ref, bd_ref, ys_ref,
                    wg_s, wu_s, wd_s):
    j = pl.program_id(0)
    grp = 2 * LANES
    prev = be_ref[jnp.maximum(j - 1, 0)]

    @pl.when((j == 0) | (be_ref[j] != prev))
    def _():
        r = lax.broadcasted_iota(jnp.int32, (grp, grp), 0)
        c = lax.broadcasted_iota(jnp.int32, (grp, grp), 1)
        src = jnp.where(c < LANES, 2 * c, 2 * (c - LANES) + 1)
        pick = (r == src).astype(BF16)
        for g in range(wgu_ref.shape[2] // grp):
            y = _dot(wgu_ref[0, :, g * grp:(g + 1) * grp].astype(BF16), pick)
            wg_s[:, g * LANES:(g + 1) * LANES] = y[:, 0:LANES].astype(BF16)
            wu_s[:, g * LANES:(g + 1) * LANES] = y[:, LANES:grp].astype(BF16)
        wd_s[...] = wd_ref[0].astype(BF16)

    @pl.when(j < nv_ref[0])
    def _():
        x = jnp.concatenate(_unpack_pairs(xs_ref[...]), axis=1)
        g = _dot(x, wg_s[...]) + bg_ref[0]
        u = _dot(x, wu_s[...]) + bu_ref[0]
        gate = jnp.minimum(g, SWIGLU_LIMIT)
        up = jnp.clip(u, -SWIGLU_LIMIT, SWIGLU_LIMIT)
        act = (up + 1.0) * (gate * jax.nn.sigmoid(gate * SWIGLU_ALPHA))
        y = _dot(act.astype(BF16), wd_s[...]) + bd_ref[0]
        ys_ref[...] = _pack_pairs(y.astype(BF16).astype(F32))

    @pl.when(j >= nv_ref[0])
    def _():
        ys_ref[...] = jnp.zeros_like(ys_ref)


def _experts(block_e, n_valid, xs, wgu, wd, bg, bu, bd):
    n_pad = xs.shape[0]
    de, d = wd.shape[1:]
    blk = (MOE_ROWS, xs.shape[1])
    n_blocks = n_pad // MOE_ROWS
    xrow = lambda j, be, nv: (jnp.minimum(j, nv[0] - 1), 0)
    wsel = lambda j, be, nv: (be[j], 0, 0)
    return pl.pallas_call(
        _experts_kernel,
        grid_spec=pltpu.PrefetchScalarGridSpec(
            num_scalar_prefetch=2,
            grid=(n_blocks,),
            in_specs=[
                pl.BlockSpec(blk, xrow),
                pl.BlockSpec((1, d, 2 * de), wsel),
                pl.BlockSpec((1, de, d), wsel),
                pl.BlockSpec((1, 1, de), wsel),
                pl.BlockSpec((1, 1, de), wsel),
                pl.BlockSpec((1, 1, d), wsel),
            ],
            out_specs=pl.BlockSpec(blk, lambda j, be, nv: (j, 0)),
            scratch_shapes=[
                pltpu.VMEM((d, de), BF16),
                pltpu.VMEM((d, de), BF16),
                pltpu.VMEM((de, d), BF16),
            ],
        ),
        out_shape=jax.ShapeDtypeStruct(xs.shape, xs.dtype),
        compiler_params=pltpu.CompilerParams(dimension_semantics=("arbitrary",),
                                             vmem_limit_bytes=EXPERTS_VMEM_LIMIT),
        name="experts",
    )(block_e, n_valid, xs, wgu, wd, bg, bu, bd)


def _combine_kernel(seg_ref, cnt_ref, x1_ref, lpos_ref, cw_ref, nw_ref, ys_ref, o_ref, stage_ref, sem,
                    *, final_norm):
    i = pl.program_id(0)
    tm, d = x1_ref.shape
    base = i * N_EXPERTS

    buf = i % 2

    def gather(tile_base, win, b, wait):
        for e in range(N_EXPERTS):
            slot = seg_ref[tile_base + e] + win * SEG_WINDOW
            cp = pltpu.make_async_copy(ys_ref.at[pl.ds(pl.multiple_of(slot, SEG_ALIGN), SEG_WINDOW)],
                                       stage_ref.at[b, pl.ds(e * SEG_WINDOW, SEG_WINDOW)], sem.at[b])
            if wait:
                cp.wait()
            else:
                cp.start()

    @pl.when(i == 0)
    def _():
        gather(base, 0, buf, wait=False)

    @pl.when(i + 1 < pl.num_programs(0))
    def _():
        gather(base + N_EXPERTS, 0, 1 - buf, wait=False)

    n_stage = N_EXPERTS * SEG_WINDOW
    owner = lax.broadcasted_iota(jnp.int32, (LANES, n_stage), 1) // SEG_WINDOW
    expand = (owner == lax.broadcasted_iota(jnp.int32, (LANES, n_stage), 0)).astype(BF16)
    j = (lax.broadcasted_iota(jnp.int32, (tm, n_stage), 1) % SEG_WINDOW).astype(F32)
    cw_rows = jnp.concatenate([cw_ref[...].astype(BF16), jnp.zeros((LANES - N_EXPERTS, tm), BF16)], axis=0)
    cw_wide = _dot_tn(cw_rows, expand)

    def window(win, y):
        rank = (lpos_ref[...] - jnp.asarray(win * SEG_WINDOW, F32)).astype(BF16)
        take = jnp.where(_dot(rank, expand) == j, cw_wide, 0.0).astype(BF16)
        gather(base, win, buf, wait=True)
        hi, lo = _unpack_pairs(stage_ref[buf])
        return y + jnp.concatenate([_dot(take, hi), _dot(take, lo)], axis=1)

    def more(win, y):
        gather(base, win, buf, wait=False)
        return window(win, y)

    y = window(0, jnp.zeros((tm, d), F32))
    y = lax.fori_loop(1, _seg_windows(cnt_ref, base), more, y)
    out = x1_ref[...] + y
    if final_norm:
        out = _rms(out, nw_ref[...])
    o_ref[...] = out


def _combine(seg, cnt, x1, lpos, cw, nw, ys, final_norm):
    t, d = x1.shape
    tm = ROUTE_TILE
    kern = functools.partial(_combine_kernel, final_norm=final_norm)
    return pl.pallas_call(
        kern,
        grid_spec=pltpu.PrefetchScalarGridSpec(
            num_scalar_prefetch=2,
            grid=(t // tm,),
            in_specs=[
                pl.BlockSpec((tm, d), lambda i, *_: (i, 0)),
                pl.BlockSpec((tm, LANES), lambda i, *_: (i, 0)),
                pl.BlockSpec((N_EXPERTS, tm), lambda i, *_: (0, i)),
                pl.BlockSpec((1, d), lambda i, *_: (0, 0)),
                pl.BlockSpec(memory_space=pl.ANY),
            ],
            out_specs=pl.BlockSpec((tm, d), lambda i, *_: (i, 0)),
            scratch_shapes=[pltpu.VMEM((2, N_EXPERTS * SEG_WINDOW, d // 2), jnp.uint32),
                            pltpu.SemaphoreType.DMA((2,))],
        ),
        out_shape=jax.ShapeDtypeStruct((t, d), F32),
        compiler_params=_cp(("arbitrary",)),
        name="combine",
    )(seg, cnt, x1, lpos, cw, nw, ys)


def _rope_tables(seq):
    half = MLA_ROPE // 2
    inv = 1.0 / (ROPE_THETA ** (jnp.arange(0, MLA_ROPE, 2, dtype=F32) / MLA_ROPE))
    ang = jnp.arange(seq, dtype=F32)[:, None] * inv[None, :]
    cos, sin = jnp.cos(ang), jnp.sin(ang)
    zeros = jnp.zeros((seq, LANES - MLA_ROPE), F32)
    cos_row = jnp.concatenate([cos, cos, zeros], axis=-1)
    sin_row = jnp.concatenate([-sin, sin, zeros], axis=-1)
    del half
    return cos_row, sin_row, cos.T, sin.T


def _pad_cols(a, width):
    return jnp.pad(a, ((0, 0), (0, width - a.shape[1])))


def _layer(x2, batch, seq, final_norm_w, final_norm, rope_tables,
           norm_mix_w, w_in, gdn_conv_w, gdn_a_log, gdn_dt_bias, gdn_norm_w, w_gdn_o,
           mla_q_norm_w, w_mla_q_b, mla_kv_norm_w, w_mla_kv_b, w_mla_o, w_out,
           norm_ffn_w, w_router, b_router, w_gate_up, b_gate_up, w_down, b_down):
    t, d = x2.shape
    qk_w = GDN_HEADS * GDN_D
    o_b = 4 * qk_w
    o_a = o_b + GDN_HEADS
    o_cq = o_a + GDN_HEADS
    o_ckv = o_cq + MLA_Q_LORA
    o_kr = o_ckv + MLA_KV_LORA
    o_ga = o_kr + MLA_ROPE
    o_gb = o_ga + d
    w_p = jnp.concatenate([
        w_in[:, 0:o_b], w_in[:, o_ga:o_gb + d], w_in[:, o_cq:o_ckv], w_in[:, o_ckv:o_kr],
        _pad_cols(w_in[:, o_kr:o_ga], 2 * LANES)], axis=1).astype(BF16)
    w_ab = _pad_cols(jnp.concatenate([w_in[:, o_a:o_cq], w_in[:, o_b:o_a]], axis=1), LANES).astype(BF16)

    p, ab = _in_proj(x2, norm_mix_w[None, :], w_p, w_ab)

    alog_row = _pad_cols(gdn_a_log[None, :].astype(F32), LANES)
    dtb_row = _pad_cols(gdn_dt_bias[None, :].astype(F32), LANES)
    qkvn, cols, gct = _gdn_prep(p, ab, gdn_conv_w.astype(F32), alog_row, dtb_row, seq)
    o_gdn = _gdn_chunk(qkvn, p, cols, gct, gdn_norm_w[None, :].astype(F32), batch, seq)

    hd = MLA_NOPE + MLA_ROPE
    wq = w_mla_q_b.reshape(MLA_Q_LORA, MLA_HEADS, hd)
    wqt = jnp.pad(wq, ((0, 0), (0, 0), (0, 2 * LANES - hd))).reshape(MLA_Q_LORA, MLA_HEADS * 2 * LANES).T
    wkv = w_mla_kv_b.reshape(MLA_KV_LORA, MLA_HEADS, MLA_NOPE + MLA_V)
    wkn = wkv[:, :, :MLA_NOPE].reshape(MLA_KV_LORA, -1)
    wvt = wkv[:, :, MLA_NOPE:].reshape(MLA_KV_LORA, -1).T
    qt, kn, kr, vt = _mla_prep(p, rope_tables, mla_q_norm_w[None, :].astype(F32),
                               mla_kv_norm_w[None, :].astype(F32), wqt.astype(BF16), wkn.astype(BF16),
                               wvt.astype(BF16), seq)
    o_mla = _mla_attn(qt, kn, kr, vt, batch, seq)

    wr = _pad_cols(w_router.astype(F32), LANES).T
    wr_hi = wr.astype(BF16)
    wr_lo = (wr - wr_hi.astype(F32)).astype(BF16)
    br = jnp.broadcast_to(_pad_cols(b_router[None, :].astype(F32), LANES).T, (LANES, ROUTE_TILE))
    x1, h2, sel, cw = _mix_out(x2, o_gdn, o_mla, p, w_gdn_o.astype(BF16), w_mla_o.astype(BF16),
                               w_out.astype(BF16), norm_ffn_w[None, :].astype(F32), wr_hi, wr_lo, br)

    lpos, keyt, offs, cnt, tot = _route_pos(sel)
    n_tiles = t // ROUTE_TILE
    used = tot[0, :N_EXPERTS]
    padded = (used + SEG_WINDOW + MOE_ROWS - 1) // MOE_ROWS * MOE_ROWS
    pad_end = jnp.cumsum(padded)
    pad_start = pad_end - padded
    seg = (pad_start[None, :] + offs[:, 0, :N_EXPERTS]).astype(jnp.int32).reshape(-1)
    cnt = cnt[:, 0, :N_EXPERTS].reshape(-1)
    worst_used = t * TOP_K + n_tiles * N_EXPERTS * (SEG_ALIGN - 1) + N_EXPERTS * SEG_WINDOW
    n_pad = -(-worst_used // MOE_ROWS) * MOE_ROWS + (N_EXPERTS + 1) * MOE_ROWS
    n_blocks = n_pad // MOE_ROWS
    blk_start = jnp.arange(n_blocks, dtype=jnp.int32) * MOE_ROWS
    block_e = jnp.minimum(jnp.sum((pad_end[None, :] <= blk_start[:, None]).astype(jnp.int32), axis=1),
                          N_EXPERTS - 1).astype(jnp.int32)
    n_valid = (pad_end[-1:] // MOE_ROWS).astype(jnp.int32)

    xs = _dispatch(seg, cnt, (pad_start + used).astype(jnp.int32), pad_end.astype(jnp.int32), h2, keyt, n_pad)
    bg = b_gate_up[:, None, 0::2].astype(F32)
    bu = b_gate_up[:, None, 1::2].astype(F32)
    ys = _experts(block_e, n_valid, xs, w_gate_up, w_down, bg, bu, b_down[:, None, :].astype(F32))
    return _combine(seg, cnt, x1, lpos, cw, final_norm_w[None, :].astype(F32), ys, final_norm)


def kernel(x, norm_mix_w, w_in, gdn_conv_w, gdn_a_log, gdn_dt_bias, gdn_norm_w, w_gdn_o, mla_q_norm_w, w_mla_q_b, mla_kv_norm_w, w_mla_kv_b, w_mla_o, w_out, norm_ffn_w, w_router, b_router, w_gate_up, b_gate_up, w_down, b_down, norm_final_w):
    batch, seq, d = x.shape
    depth = w_in.shape[0]
    rope_tables = _rope_tables(seq)
    x2 = x.reshape(batch * seq, d)
    for layer in range(depth):
        x2 = _layer(x2, batch, seq, norm_final_w, layer == depth - 1, rope_tables,
                    norm_mix_w[layer], w_in[layer], gdn_conv_w[layer], gdn_a_log[layer],
                    gdn_dt_bias[layer], gdn_norm_w[layer], w_gdn_o[layer], mla_q_norm_w[layer],
                    w_mla_q_b[layer], mla_kv_norm_w[layer], w_mla_kv_b[layer], w_mla_o[layer],
                    w_out[layer], norm_ffn_w[layer], w_router[layer], b_router[layer],
                    w_gate_up[layer], b_gate_up[layer], w_down[layer], b_down[layer])
    return x2.reshape(batch, seq, d)
```

```python
import functools

import jax
import jax.numpy as jnp
import numpy as np
from jax import lax
from jax.experimental import pallas as pl
from jax.experimental.pallas import tpu as pltpu

F32 = jnp.float32
BF16 = jnp.bfloat16

CHUNK = 64
NORM_EPS = 1e-6
GDN_HEADS = 8
GDN_D = 128
GDN_CONV = 4
MLA_HEADS = 8
MLA_Q_LORA = 512
MLA_KV_LORA = 256
MLA_NOPE = 128
MLA_ROPE = 64
MLA_V = 128
ROPE_THETA = 10000.0
N_EXPERTS = 32
TOP_K = 4
SWIGLU_LIMIT = 7.0
SWIGLU_ALPHA = 1.702

LANES = 128
MOE_ROWS = 512
ROUTE_TILE = 256
SEG_ALIGN = 8
SEG_WINDOW = 64
ZERO_ROWS = 64
VMEM_LIMIT = 48 * 1024 * 1024
EXPERTS_VMEM_LIMIT = 56 * 1024 * 1024

NEG_BIG = -1e30
LOG2_E = 1.4426950408889634


def _cp(sem):
    return pltpu.CompilerParams(dimension_semantics=sem, vmem_limit_bytes=VMEM_LIMIT)


def _dot(a, b):
    return jnp.dot(a, b, preferred_element_type=F32)


def _dot_nt(a, b):
    return lax.dot_general(a, b, (((1,), (1,)), ((), ())), preferred_element_type=F32)


def _dot_tn(a, b):
    return lax.dot_general(a, b, (((0,), (0,)), ((), ())), preferred_element_type=F32)


def _split3(x):
    hi = x.astype(BF16)
    r = x - hi.astype(F32)
    mid = r.astype(BF16)
    lo = (r - mid.astype(F32)).astype(BF16)
    return hi, mid, lo


def _rms(x, w):
    ms = jnp.mean(x * x, axis=-1, keepdims=True)
    return x * lax.rsqrt(ms + NORM_EPS) * w


def _in_proj_kernel(x_ref, nw_ref, w_ref, wab_ref, p_ref, ab_ref, h_ref):
    @pl.when(pl.program_id(1) == 0)
    def _():
        hb = _rms(x_ref[...], nw_ref[...]).astype(BF16)
        h_ref[...] = hb
        ab_ref[...] = _dot(hb, wab_ref[...])

    p_ref[...] = _dot(h_ref[...], w_ref[...]).astype(BF16)


def _in_proj(x2, norm_w, w_p, w_ab, tm=1024, tn=3584):
    t, d = x2.shape
    n = w_p.shape[1]
    return pl.pallas_call(
        _in_proj_kernel,
        grid=(t // tm, n // tn),
        in_specs=[
            pl.BlockSpec((tm, d), lambda i, j: (i, 0)),
            pl.BlockSpec((1, d), lambda i, j: (0, 0)),
            pl.BlockSpec((d, tn), lambda i, j: (0, j)),
            pl.BlockSpec((d, LANES), lambda i, j: (0, 0)),
        ],
        out_specs=[
            pl.BlockSpec((tm, tn), lambda i, j: (i, j)),
            pl.BlockSpec((tm, LANES), lambda i, j: (i, 0)),
        ],
        out_shape=[
            jax.ShapeDtypeStruct((t, n), BF16),
            jax.ShapeDtypeStruct((t, LANES), F32),
        ],
        scratch_shapes=[pltpu.VMEM((tm, d), BF16)],
        compiler_params=_cp(("parallel", "arbitrary")),
        name="in_proj",
    )(x2, norm_w, w_p, w_ab)


def _gdn_prep_kernel(cur_ref, prev_ref, ab_ref, cw_ref, alog_ref, dtb_ref,
                     qkv_ref, cols_ref, gct_ref, *, tiles_per_seq):
    tm = cur_ref.shape[0]
    i = pl.program_id(0)
    halo_on = (i % tiles_per_seq) != 0
    q_scale = GDN_D ** -0.5
    grp = 2 * LANES
    row = lax.broadcasted_iota(jnp.int32, (tm, tm), 0)
    col = lax.broadcasted_iota(jnp.int32, (tm, tm), 1)
    shift = [(col == row - s).astype(BF16) for s in range(1, GDN_CONV)]
    for cg in range(cur_ref.shape[1] // grp):
        gs = slice(cg * grp, (cg + 1) * grp)
        cur_b = cur_ref[:, gs]
        cur = cur_b.astype(F32)
        w = cw_ref[:, gs]
        y = w[GDN_CONV - 1:GDN_CONV, :] * cur
        for s in range(1, GDN_CONV):
            y = y + w[GDN_CONV - 1 - s:GDN_CONV - s, :] * _dot(shift[s - 1], cur_b)
        halo = jnp.where(halo_on, prev_ref[:, gs].astype(F32)[8:16, :], 0.0)
        xe = jnp.concatenate([halo, cur[0:8, :]], axis=0)
        head = w[0:1, :] * xe[5:13, :]
        for j in range(1, GDN_CONV):
            head = head + w[j:j + 1, :] * xe[5 + j:13 + j, :]
        y = jnp.concatenate([head, y[8:, :]], axis=0)
        hy = 0.5 * y
        y = hy + hy * jnp.tanh(hy)
        for half in range(2):
            cb = 2 * cg + half
            yh = y[:, half * LANES:(half + 1) * LANES]
            if cb < 2 * GDN_HEADS:
                ss = jnp.sum(yh * yh, axis=-1, keepdims=True)
                yh = yh * lax.rsqrt(ss + NORM_EPS)
                if cb < GDN_HEADS:
                    yh = yh * q_scale
            qkv_ref[:, cb * LANES:(cb + 1) * LANES] = yh.astype(BF16)

    ab = ab_ref[...]
    g = -jnp.exp(alog_ref[...]) * jax.nn.softplus(ab + dtb_ref[...])
    row = lax.broadcasted_iota(jnp.int32, (tm, tm), 0)
    col = lax.broadcasted_iota(jnp.int32, (tm, tm), 1)
    tri = ((col <= row) & ((row // CHUNK) == (col // CHUNK))).astype(BF16)
    g_hi, g_mid, g_lo = _split3(g)
    gc = _dot(tri, g_hi) + _dot(tri, g_mid) + _dot(tri, g_lo)
    lane = lax.broadcasted_iota(jnp.int32, (tm, LANES), 1)
    cols_ref[...] = jnp.where(lane < GDN_HEADS, gc, jax.nn.sigmoid(ab))
    for c in range(tm // CHUNK):
        blk = gc[c * CHUNK:(c + 1) * CHUNK, :]
        blk = jnp.concatenate([blk, jnp.zeros_like(blk)], axis=0)
        gct_ref[c] = blk.T[0:GDN_HEADS, 0:CHUNK]


def _gdn_prep(p, ab, conv_w, alog_row, dtb_row, seq, tm=256):
    t = p.shape[0]
    cw = 3 * GDN_HEADS * GDN_D
    tiles_per_seq = seq // tm
    kern = functools.partial(_gdn_prep_kernel, tiles_per_seq=tiles_per_seq)
    return pl.pallas_call(
        kern,
        grid=(t // tm,),
        in_specs=[
            pl.BlockSpec((tm, cw), lambda i: (i, 0)),
            pl.BlockSpec((16, cw), lambda i: (jnp.maximum(i * (tm // 16) - 1, 0), 0)),
            pl.BlockSpec((tm, LANES), lambda i: (i, 0)),
            pl.BlockSpec((GDN_CONV, cw), lambda i: (0, 0)),
            pl.BlockSpec((1, LANES), lambda i: (0, 0)),
            pl.BlockSpec((1, LANES), lambda i: (0, 0)),
        ],
        out_specs=[
            pl.BlockSpec((tm, cw), lambda i: (i, 0)),
            pl.BlockSpec((tm, LANES), lambda i: (i, 0)),
            pl.BlockSpec((tm // CHUNK, GDN_HEADS, CHUNK), lambda i: (i, 0, 0)),
        ],
        out_shape=[
            jax.ShapeDtypeStruct((t, cw), BF16),
            jax.ShapeDtypeStruct((t, LANES), F32),
            jax.ShapeDtypeStruct((t // CHUNK, GDN_HEADS, CHUNK), F32),
        ],
        compiler_params=_cp(("parallel",)),
        name="gdn_prep",
    )(p, p, ab, conv_w, alog_row, dtb_row)


GDN_CHUNKS_PER_STEP = 2


def _gdn_chunk_kernel(q_ref, k_ref, v_ref, z_ref, cols_ref, gct_ref, nw_ref, o_ref, s_ref):
    c = CHUNK
    nb = q_ref.shape[0]
    chains = [(b, h) for b in range(nb) for h in range(GDN_HEADS)]
    units = [(g, b, h) for g in range(GDN_CHUNKS_PER_STEP) for b, h in chains]

    @pl.when(pl.program_id(0) == 0)
    def _():
        s_ref[...] = jnp.zeros_like(s_ref)

    ri = lax.broadcasted_iota(jnp.int32, (c, c), 0)
    ci = lax.broadcasted_iota(jnp.int32, (c, c), 1)
    incl = ri >= ci
    strict = ri > ci
    eye = (ri == ci).astype(F32)
    nw = nw_ref[...]

    def rows(g):
        return slice(g * c, (g + 1) * c)

    def head(h):
        return slice(h * GDN_D, (h + 1) * GDN_D)

    cols, e_g, e_kd, e_last, gct = {}, {}, {}, {}, {}
    for g in range(GDN_CHUNKS_PER_STEP):
        for b in range(nb):
            cb = cols_ref[b, rows(g), :]
            last = cb[c - 1:c, :]
            cols[g, b] = cb
            e_g[g, b] = jnp.exp(cb)
            e_kd[g, b] = jnp.exp(last - cb)
            e_last[g, b] = jnp.exp(last)
            gct[g, b] = gct_ref[b, g]

    kq, kb_l, kf_l = [], [], []
    for g, b, h in units:
        k = k_ref[b, rows(g), head(h)]
        kf = k.astype(F32)
        kb = kf * cols[g, b][:, GDN_HEADS + h:GDN_HEADS + h + 1]
        kq.append(_dot_nt(jnp.concatenate([kb.astype(BF16), q_ref[b, rows(g), head(h)]], axis=0), k))
        kb_l.append(kb)
        kf_l.append(kf)

    a_l, qk_l = [], []
    for i, (g, b, h) in enumerate(units):
        dec = jnp.exp(jnp.minimum(cols[g, b][:, h:h + 1] - gct[g, b][h:h + 1, :], 0.0))
        a_l.append(jnp.where(strict, -kq[i][0:c, :] * dec, 0.0))
        qk_l.append(jnp.where(incl, kq[i][c:2 * c, :] * dec, 0.0).astype(BF16))

    tinv = [eye + a for a in a_l]
    pw = a_l
    for _ in range(5):
        pwb = [x.astype(BF16) for x in pw]
        pw = [_dot(x, x) for x in pwb]
        tinv = [t + _dot(t.astype(BF16), x.astype(BF16)) for t, x in zip(tinv, pw)]

    u_l, lhs_l, kd_l = [], [], []
    for i, (g, b, h) in enumerate(units):
        beta = cols[g, b][:, GDN_HEADS + h:GDN_HEADS + h + 1]
        eg = e_g[g, b][:, h:h + 1]
        rhs = jnp.concatenate([v_ref[b, rows(g), head(h)].astype(F32) * beta, kb_l[i] * eg],
                              axis=1).astype(BF16)
        uw = _dot(tinv[i].astype(BF16), rhs)
        qd = (q_ref[b, rows(g), head(h)].astype(F32) * eg).astype(BF16)
        u_l.append(uw[:, 0:GDN_D])
        lhs_l.append(jnp.concatenate([uw[:, GDN_D:2 * GDN_D].astype(BF16), qd], axis=0))
        kd_l.append((kf_l[i] * e_kd[g, b][:, h:h + 1]).astype(BF16))

    for g in range(GDN_CHUNKS_PER_STEP):
        first = g * len(chains)
        r_l = [_dot(lhs_l[first + n], s_ref[n].astype(BF16)) for n in range(len(chains))]
        for n, (b, h) in enumerate(chains):
            i = first + n
            v_new = (u_l[i] - r_l[n][0:c, :]).astype(BF16)
            o = r_l[n][c:2 * c, :] + _dot(qk_l[i], v_new)
            s_ref[n] = s_ref[n] * e_last[g, b][:, h:h + 1] + _dot_tn(kd_l[i], v_new)
            z = z_ref[b, rows(g), head(h)].astype(F32)
            o_ref[b, rows(g), head(h)] = (_rms(o, nw) * (z * jax.nn.sigmoid(z))).astype(BF16)


def _gdn_chunk(qkvn, p, cols, gct, norm_w, batch, seq):
    nc = seq // CHUNK
    hw = GDN_HEADS * GDN_D
    g = GDN_CHUNKS_PER_STEP
    qkvn3 = qkvn.reshape(batch, seq, qkvn.shape[1])
    p3 = p.reshape(batch, seq, p.shape[1])
    cols3 = cols.reshape(batch, seq, LANES)
    gct4 = gct.reshape(batch, nc, GDN_HEADS, CHUNK)
    tile = lambda col: pl.BlockSpec((batch, g * CHUNK, hw), lambda c: (0, c, col))
    out = pl.pallas_call(
        _gdn_chunk_kernel,
        grid=(nc // g,),
        in_specs=[
            tile(0), tile(1), tile(2),
            tile(3),
            pl.BlockSpec((batch, g * CHUNK, LANES), lambda c: (0, c, 0)),
            pl.BlockSpec((batch, g, GDN_HEADS, CHUNK), lambda c: (0, c, 0, 0)),
            pl.BlockSpec((1, GDN_D), lambda c: (0, 0)),
        ],
        out_specs=pl.BlockSpec((batch, g * CHUNK, hw), lambda c: (0, c, 0)),
        out_shape=jax.ShapeDtypeStruct((batch, seq, hw), BF16),
        scratch_shapes=[pltpu.VMEM((batch * GDN_HEADS, GDN_D, GDN_D), F32)],
        compiler_params=_cp(("arbitrary",)),
        name="gdn_chunk",
    )(qkvn3, qkvn3, qkvn3, p3, cols3, gct4, norm_w)
    return out.reshape(batch * seq, hw)


def _rope(x, cos, sin_signed):
    lane = lax.broadcasted_iota(jnp.int32, x.shape, 1)
    fwd = pltpu.roll(x, LANES - MLA_ROPE // 2, 1)
    bwd = pltpu.roll(x, MLA_ROPE // 2, 1)
    rot = jnp.where(lane < MLA_ROPE // 2, fwd, bwd)
    return x * cos + rot * sin_signed


def _mla_prep_kernel(cq_ref, ckv_ref, kr_ref, cos_ref, sin_ref, cost_ref, sint_ref, qnw_ref, kvnw_ref,
                     wqt_ref, wkn_ref, wvt_ref, qt_ref, kn_ref, kro_ref, vt_ref):
    cos = cos_ref[...]
    sin = sin_ref[...]
    cos_t = cost_ref[...]
    sin_t = sint_ref[...]
    cq = _rms(cq_ref[...].astype(F32), qnw_ref[...]).astype(BF16)
    hd = 2 * LANES
    half = MLA_ROPE // 2
    scale = (MLA_NOPE + MLA_ROPE) ** -0.5 * LOG2_E
    for h in range(MLA_HEADS):
        qh = _dot_nt(wqt_ref[h * hd:(h + 1) * hd, :], cq) * scale
        lo = qh[MLA_NOPE:MLA_NOPE + half, :]
        hi = qh[MLA_NOPE + half:MLA_NOPE + MLA_ROPE, :]
        qt_ref[0, h * hd:h * hd + MLA_NOPE, :] = qh[0:MLA_NOPE, :].astype(BF16)
        qt_ref[0, h * hd + MLA_NOPE:h * hd + MLA_NOPE + half, :] = (lo * cos_t - hi * sin_t).astype(BF16)
        qt_ref[0, h * hd + MLA_NOPE + half:h * hd + MLA_NOPE + MLA_ROPE, :] = (hi * cos_t + lo * sin_t).astype(BF16)
        qt_ref[0, h * hd + MLA_NOPE + MLA_ROPE:(h + 1) * hd, :] = qh[MLA_NOPE + MLA_ROPE:hd, :].astype(BF16)
    kvl = _rms(ckv_ref[...].astype(F32), kvnw_ref[...]).astype(BF16)
    kn_ref[...] = _dot(kvl, wkn_ref[...]).astype(BF16)
    vt_ref[0] = _dot_nt(wvt_ref[...], kvl).astype(BF16)
    kro_ref[...] = _rope(kr_ref[...].astype(F32), cos, sin).astype(BF16)


def _mla_prep(p, tables, qnw, kvnw, wqt, wkn, wvt, seq, tm=512):
    cos_row, sin_row, cos_col, sin_col = tables
    t = p.shape[0]
    tiles_per_seq = seq // tm
    hw = MLA_HEADS * MLA_NOPE
    half = MLA_ROPE // 2
    cq_blk = 6144 // MLA_Q_LORA
    ckv_blk = 6656 // MLA_KV_LORA
    kr_blk = 6912 // LANES
    return pl.pallas_call(
        _mla_prep_kernel,
        grid=(t // tm,),
        in_specs=[
            pl.BlockSpec((tm, MLA_Q_LORA), lambda i: (i, cq_blk)),
            pl.BlockSpec((tm, MLA_KV_LORA), lambda i: (i, ckv_blk)),
            pl.BlockSpec((tm, LANES), lambda i: (i, kr_blk)),
            pl.BlockSpec((tm, LANES), lambda i: (i % tiles_per_seq, 0)),
            pl.BlockSpec((tm, LANES), lambda i: (i % tiles_per_seq, 0)),
            pl.BlockSpec((half, tm), lambda i: (0, i % tiles_per_seq)),
            pl.BlockSpec((half, tm), lambda i: (0, i % tiles_per_seq)),
            pl.BlockSpec((1, MLA_Q_LORA), lambda i: (0, 0)),
            pl.BlockSpec((1, MLA_KV_LORA), lambda i: (0, 0)),
            pl.BlockSpec((2 * hw, MLA_Q_LORA), lambda i: (0, 0)),
            pl.BlockSpec((MLA_KV_LORA, hw), lambda i: (0, 0)),
            pl.BlockSpec((hw, MLA_KV_LORA), lambda i: (0, 0)),
        ],
        out_specs=[
            pl.BlockSpec((1, 2 * hw, tm), lambda i: (i, 0, 0)),
            pl.BlockSpec((tm, hw), lambda i: (i, 0)),
            pl.BlockSpec((tm, LANES), lambda i: (i, 0)),
            pl.BlockSpec((1, hw, tm), lambda i: (i, 0, 0)),
        ],
        out_shape=[
            jax.ShapeDtypeStruct((t // tm, 2 * hw, tm), BF16),
            jax.ShapeDtypeStruct((t, hw), BF16),
            jax.ShapeDtypeStruct((t, LANES), BF16),
            jax.ShapeDtypeStruct((t // tm, hw, tm), BF16),
        ],
        compiler_params=_cp(("parallel",)),
        name="mla_prep",
    )(p, p, p, cos_row, sin_row, cos_col, sin_col, qnw, kvnw, wqt, wkn, wvt)


ATTN_HEADS_PER_STEP = 8
ATTN_SUM_ROWS = 16


def _mla_attn_kernel(qt_ref, kt_ref, q_ref, kn_ref, kr_ref, vt_ref, o_ref, m_ref, acc_ref):
    qi = qt_ref[pl.program_id(2)]
    ki = kt_ref[pl.program_id(2)]
    tq = q_ref.shape[2]
    tk = kn_ref.shape[0]
    hd = 2 * LANES
    q0 = qi * tq
    k0 = ki * tk
    last_k = (q0 + tq) // tk - 1

    @pl.when(ki == 0)
    def _():
        m_ref[...] = jnp.full_like(m_ref, NEG_BIG)
        acc_ref[...] = jnp.zeros_like(acc_ref)

    def step(masked):
        kr = kr_ref[...]
        ones = jnp.ones((ATTN_SUM_ROWS, tk), BF16)
        if masked:
            ck = lax.broadcasted_iota(jnp.int32, (tk, tq), 0) // CHUNK
            cq = lax.broadcasted_iota(jnp.int32, (tk, tq), 1) // CHUNK
            visible = (ck <= cq) if tq == tk else ((ck - cq) <= (q0 - k0) // CHUNK)

        def scores(h):
            k = jnp.concatenate([kn_ref[:, h * MLA_NOPE:(h + 1) * MLA_NOPE], kr], axis=1)
            s = _dot(k, q_ref[0, h * hd:(h + 1) * hd, :])
            if masked:
                s = jnp.where(visible, s, NEG_BIG)
            return s

        def update(h, s):
            m_prev = m_ref[h]
            m_new = jnp.maximum(m_prev, jnp.max(s, axis=0, keepdims=True))
            alpha = jnp.exp2(m_prev - m_new)
            p = jnp.exp2((s - m_new).astype(BF16))
            v_ext = jnp.concatenate([vt_ref[0, h * MLA_V:(h + 1) * MLA_V, :], ones], axis=0)
            acc_ref[h] = alpha * acc_ref[h] + _dot(v_ext, p)
            m_ref[h] = m_new

        s_prev = scores(0)
        for h in range(1, ATTN_HEADS_PER_STEP):
            s_next = scores(h)
            update(h - 1, s_prev)
            s_prev = s_next
        update(ATTN_HEADS_PER_STEP - 1, s_prev)

    @pl.when(k0 + tk <= q0)
    def _():
        step(False)

    def finish():
        for h in range(ATTN_HEADS_PER_STEP):
            acc = acc_ref[h]
            o = acc[0:MLA_V, :] / acc[MLA_V:MLA_V + 1, :]
            o_ref[:, h * MLA_V:(h + 1) * MLA_V] = o.T.astype(BF16)

    @pl.when(k0 + tk > q0)
    def _():
        step(True)
        if tq == tk:
            finish()
        else:
            pl.when(ki == last_k)(finish)


def _mla_attn(qt_all, kn, kr, vt, batch, seq):
    t = kn.shape[0]
    tq = qt_all.shape[2]
    tk = vt.shape[2]
    nq = seq // tq
    nk = seq // tk
    hps = ATTN_HEADS_PER_STEP
    pairs = [(qi, ki) for qi in range(nq) for ki in range((qi + 1) * tq // tk)]
    qt = jnp.asarray(np.array([pr[0] for pr in pairs], np.int32))
    kt = jnp.asarray(np.array([pr[1] for pr in pairs], np.int32))
    return pl.pallas_call(
        _mla_attn_kernel,
        grid_spec=pltpu.PrefetchScalarGridSpec(
            num_scalar_prefetch=2,
            grid=(batch, MLA_HEADS // hps, len(pairs)),
            in_specs=[
                pl.BlockSpec((1, hps * 2 * LANES, tq), lambda b, h, pr, qt, kt: (b * nq + qt[pr], h, 0)),
                pl.BlockSpec((tk, hps * MLA_NOPE), lambda b, h, pr, qt, kt: (b * nk + kt[pr], h)),
                pl.BlockSpec((tk, LANES), lambda b, h, pr, qt, kt: (b * nk + kt[pr], 0)),
                pl.BlockSpec((1, hps * MLA_V, tk), lambda b, h, pr, qt, kt: (b * nk + kt[pr], h, 0)),
            ],
            out_specs=pl.BlockSpec((tq, hps * MLA_V), lambda b, h, pr, qt, kt: (b * nq + qt[pr], h)),
            scratch_shapes=[
                pltpu.VMEM((hps, 1, tq), F32),
                pltpu.VMEM((hps, MLA_V + ATTN_SUM_ROWS, tq), F32),
            ],
        ),
        out_shape=jax.ShapeDtypeStruct((t, MLA_HEADS * MLA_V), BF16),
        compiler_params=_cp(("parallel", "parallel", "arbitrary")),
        name="mla_attn",
    )(qt, kt, qt_all, kn, kr, vt)


def _mix_out_kernel(x_ref, oa_ref, ob_ref, ga_ref, gb_ref, wga_ref, wmo_ref, wout_ref, nw_ref,
                    wr_hi_ref, wr_lo_ref, br_ref, x1_ref, h2_ref, selt_ref, cwt_ref):
    ya = _dot(oa_ref[...], wga_ref[...])
    yb = _dot(ob_ref[...], wmo_ref[...])
    merged = (jax.nn.sigmoid(ga_ref[...].astype(F32)) * ya
              + jax.nn.sigmoid(gb_ref[...].astype(F32)) * yb)
    x1 = x_ref[...] + _dot(merged.astype(BF16), wout_ref[...])
    x1_ref[...] = x1
    h2 = _rms(x1, nw_ref[...])
    h_hi = h2.astype(BF16)
    h2_ref[...] = h_hi

    h_lo = (h2 - h_hi.astype(F32)).astype(BF16)
    logits = (_dot_nt(wr_hi_ref[...], h_hi) + _dot_nt(wr_lo_ref[...], h_hi) + _dot_nt(wr_hi_ref[...], h_lo)
              + br_ref[...])[0:N_EXPERTS, :]
    expert = lax.broadcasted_iota(jnp.int32, logits.shape, 0)
    work = logits
    sel = jnp.zeros(logits.shape, F32)
    cw = jnp.zeros(logits.shape, F32)
    top = None
    denom = None
    for kk in range(TOP_K):
        mx = jnp.max(work, axis=0, keepdims=True)
        am = jnp.min(jnp.where(work == mx, expert, N_EXPERTS), axis=0, keepdims=True)
        hit = expert == am
        if kk == 0:
            top = mx
            e = jnp.ones_like(mx)
            denom = e
        else:
            e = jnp.exp(mx - top)
            denom = denom + e
        sel = jnp.where(hit, 1.0, sel)
        cw = jnp.where(hit, e, cw)
        work = jnp.where(hit, -jnp.inf, work)
    selt_ref[...] = sel.astype(BF16)
    cwt_ref[...] = cw / denom


def _mix_out(x2, oa, ob, p, wga, wmo, wout, nw, wr_hi, wr_lo, br):
    t, d = x2.shape
    tm = br.shape[1]
    full = lambda i: (0, 0)
    return pl.pallas_call(
        _mix_out_kernel,
        grid=(t // tm,),
        in_specs=[
            pl.BlockSpec((tm, d), lambda i: (i, 0)),
            pl.BlockSpec((tm, d), lambda i: (i, 0)),
            pl.BlockSpec((tm, d), lambda i: (i, 0)),
            pl.BlockSpec((tm, d), lambda i: (i, 4)),
            pl.BlockSpec((tm, d), lambda i: (i, 5)),
            pl.BlockSpec((d, d), full),
            pl.BlockSpec((d, d), full),
            pl.BlockSpec((d, d), full),
            pl.BlockSpec((1, d), full),
            pl.BlockSpec((LANES, d), full),
            pl.BlockSpec((LANES, d), full),
            pl.BlockSpec((LANES, tm), full),
        ],
        out_specs=[
            pl.BlockSpec((tm, d), lambda i: (i, 0)),
            pl.BlockSpec((tm, d), lambda i: (i, 0)),
            pl.BlockSpec((N_EXPERTS, tm), lambda i: (0, i)),
            pl.BlockSpec((N_EXPERTS, tm), lambda i: (0, i)),
        ],
        out_shape=[
            jax.ShapeDtypeStruct((t, d), F32),
            jax.ShapeDtypeStruct((t, d), BF16),
            jax.ShapeDtypeStruct((N_EXPERTS, t), BF16),
            jax.ShapeDtypeStruct((N_EXPERTS, t), F32),
        ],
        compiler_params=_cp(("parallel",)),
        name="mix_out",
    )(x2, oa, ob, p, p, wga, wmo, wout, nw, wr_hi, wr_lo, br)


def _route_pos_kernel(selt_ref, lpos_ref, keyt_ref, offs_ref, cnt_ref, tot_ref, carry_ref):
    tm = selt_ref.shape[1]
    i = pl.program_id(0)

    @pl.when(i == 0)
    def _():
        carry_ref[...] = jnp.zeros_like(carry_ref)

    sel_t = selt_ref[...]
    sel_rows = jnp.concatenate([sel_t, jnp.zeros((LANES - N_EXPERTS, tm), BF16)], axis=0)
    row = lax.broadcasted_iota(jnp.int32, (tm, tm), 0)
    col = lax.broadcasted_iota(jnp.int32, (tm, tm), 1)
    lpos_ref[...] = _dot_nt((col < row).astype(BF16), sel_rows)
    pos_t = _dot(sel_t, (row < col).astype(BF16))
    keyt_ref[0] = jnp.where(sel_t > 0, pos_t, -1.0)

    n = _dot_nt(jnp.ones((8, tm), BF16), sel_rows)[0:1, :]
    carry = carry_ref[0:1, :]
    offs_ref[0] = carry.astype(jnp.int32)
    cnt_ref[0] = n.astype(jnp.int32)
    total = carry + jnp.ceil(n * (1.0 / SEG_ALIGN)) * SEG_ALIGN
    carry_ref[...] = jnp.broadcast_to(total, carry_ref.shape)
    tot_ref[...] = jnp.broadcast_to(total, tot_ref.shape).astype(jnp.int32)


def _route_pos(sel_t):
    t = sel_t.shape[1]
    tm = ROUTE_TILE
    nt = t // tm
    return pl.pallas_call(
        _route_pos_kernel,
        grid=(nt,),
        in_specs=[pl.BlockSpec((N_EXPERTS, tm), lambda i: (0, i))],
        out_specs=[
            pl.BlockSpec((tm, LANES), lambda i: (i, 0)),
            pl.BlockSpec((1, N_EXPERTS, tm), lambda i: (i, 0, 0)),
            pl.BlockSpec((1, 1, LANES), lambda i: (i, 0, 0)),
            pl.BlockSpec((1, 1, LANES), lambda i: (i, 0, 0)),
            pl.BlockSpec((8, LANES), lambda i: (0, 0)),
        ],
        out_shape=[
            jax.ShapeDtypeStruct((t, LANES), F32),
            jax.ShapeDtypeStruct((nt, N_EXPERTS, tm), F32),
            jax.ShapeDtypeStruct((nt, 1, LANES), jnp.int32),
            jax.ShapeDtypeStruct((nt, 1, LANES), jnp.int32),
            jax.ShapeDtypeStruct((8, LANES), jnp.int32),
        ],
        scratch_shapes=[pltpu.VMEM((8, LANES), F32)],
        compiler_params=_cp(("arbitrary",)),
        name="route_pos",
    )(sel_t)


def _rows(ref, start, n):
    return ref.at[pl.ds(pl.multiple_of(start, n), n)]


def _pack_pairs(x):
    half = x.shape[1] // 2
    hi = lax.bitcast_convert_type(x[:, :half], jnp.uint32)
    lo = lax.bitcast_convert_type(x[:, half:], jnp.uint32)
    return hi | lax.shift_right_logical(lo, jnp.uint32(16))


def _unpack_pairs(u):
    hi = lax.bitcast_convert_type(u & jnp.uint32(0xFFFF0000), F32).astype(BF16)
    lo = lax.bitcast_convert_type(lax.shift_left(u, jnp.uint32(16)), F32).astype(BF16)
    return hi, lo


def _seg_windows(cnt_ref, base):
    longest = lax.fori_loop(0, N_EXPERTS, lambda e, m: jnp.maximum(m, cnt_ref[base + e]), 0)
    return lax.shift_right_logical(longest + (SEG_WINDOW - 1), SEG_WINDOW.bit_length() - 1)


def _dispatch_kernel(seg_ref, cnt_ref, fill_lo_ref, fill_hi_ref, h_ref, keyt_ref, xs_ref,
                     stage_ref, zero_ref, sem):
    i = pl.program_id(0)
    tm = h_ref.shape[0]
    base = i * N_EXPERTS

    def zero_fills(wait):
        def act(cp):
            if wait:
                cp.wait()
            else:
                cp.start()

        def fill(c):
            return pltpu.make_async_copy(zero_ref.at[pl.ds(0, SEG_ALIGN)], _rows(xs_ref, c * SEG_ALIGN, SEG_ALIGN),
                                         sem.at[2])

        def per_expert(e, carry):
            lo = lax.shift_right_logical(fill_lo_ref[e], SEG_ALIGN.bit_length() - 1)
            hi = lax.shift_right_logical(fill_hi_ref[e], SEG_ALIGN.bit_length() - 1)
            return lax.fori_loop(lo, hi, lambda c, a: (act(fill(c)), a)[1], carry)

        lax.fori_loop(0, N_EXPERTS, per_expert, 0)

        def fill_tail(c):
            return pltpu.make_async_copy(zero_ref, _rows(xs_ref, c * ZERO_ROWS, ZERO_ROWS), sem.at[2])

        lo = lax.shift_right_logical(fill_hi_ref[N_EXPERTS - 1], ZERO_ROWS.bit_length() - 1)
        hi = xs_ref.shape[0] // ZERO_ROWS
        lax.fori_loop(lo, hi, lambda c, a: (act(fill_tail(c)), a)[1], 0)

    @pl.when(i == 0)
    def _():
        zero_ref[...] = jnp.zeros_like(zero_ref)
        zero_fills(wait=False)
        zero_fills(wait=True)

    buf = i % 2

    def send(tile_base, win, b, wait, only_live=False):
        for e in range(N_EXPERTS):
            def go(e=e):
                slot = seg_ref[tile_base + e] + win * SEG_WINDOW
                cp = pltpu.make_async_copy(stage_ref.at[b, pl.ds(e * SEG_WINDOW, SEG_WINDOW)],
                                           xs_ref.at[pl.ds(pl.multiple_of(slot, SEG_ALIGN), SEG_WINDOW)],
                                           sem.at[b])
                if wait:
                    cp.wait()
                else:
                    cp.start()

            if only_live:
                pl.when(cnt_ref[tile_base + e] > win * SEG_WINDOW)(go)
            else:
                go()

    half = N_EXPERTS // 2
    j = lax.broadcasted_iota(jnp.int32, (SEG_WINDOW, tm), 0).astype(F32)

    def build(win, b):
        key = keyt_ref[0] - jnp.asarray(win * SEG_WINDOW, F32)
        for hf in range(2):
            pick = jnp.concatenate([(key[e:e + 1, :] == j).astype(BF16)
                                    for e in range(hf * half, (hf + 1) * half)], axis=0)
            stage_ref[b, hf * half * SEG_WINDOW:(hf + 1) * half * SEG_WINDOW, :] = _pack_pairs(_dot(pick, h_ref[...]))

    n_win = _seg_windows(cnt_ref, base)
    build(0, buf)

    @pl.when(i > 0)
    def _():
        send(base - N_EXPERTS, 0, 1 - buf, wait=True)

    send(base, 0, buf, wait=False)

    def more(win, carry):
        build(win, 1 - buf)
        send(base, win, 1 - buf, wait=False, only_live=True)
        send(base, win, 1 - buf, wait=True, only_live=True)
        return carry

    lax.fori_loop(1, n_win, more, 0)

    @pl.when(i == pl.num_programs(0) - 1)
    def _():
        send(base, 0, buf, wait=True)


def _dispatch(seg, cnt, fill_lo, fill_hi, h2, keyt, n_pad):
    t, d = h2.shape
    tm = ROUTE_TILE
    return pl.pallas_call(
        _dispatch_kernel,
        grid_spec=pltpu.PrefetchScalarGridSpec(
            num_scalar_prefetch=4,
            grid=(t // tm,),
            in_specs=[
                pl.BlockSpec((tm, d), lambda i, *_: (i, 0)),
                pl.BlockSpec((1, N_EXPERTS, tm), lambda i, *_: (i, 0, 0)),
            ],
            out_specs=pl.BlockSpec(memory_space=pl.ANY),
            scratch_shapes=[pltpu.VMEM((2, N_EXPERTS * SEG_WINDOW, d // 2), jnp.uint32),
                            pltpu.VMEM((ZERO_ROWS, d // 2), jnp.uint32),
                            pltpu.SemaphoreType.DMA((3,))],
        ),
        out_shape=jax.ShapeDtypeStruct((n_pad, d // 2), jnp.uint32),
        compiler_params=_cp(("arbitrary",)),
        name="dispatch",
    )(seg, cnt, fill_lo, fill_hi, h2, keyt)


def _experts_kernel(be_ref, nv_ref, xs_ref, wgu_ref, wd_ref, bg_ref, bu_ref, bd_ref, ys_ref,
                    wg_s, wu_s, wd_s):
    j = pl.program_id(0)
    grp = 2 * LANES
    prev = be_ref[jnp.maximum(j - 1, 0)]

    @pl.when((j == 0) | (be_ref[j] != prev))
    def _():
        r = lax.broadcasted_iota(jnp.int32, (grp, grp), 0)
        c = lax.broadcasted_iota(jnp.int32, (grp, grp), 1)
        src = jnp.where(c < LANES, 2 * c, 2 * (c - LANES) + 1)
        pick = (r == src).astype(BF16)
        for g in range(wgu_ref.shape[2] // grp):
            y = _dot(wgu_ref[0, :, g * grp:(g + 1) * grp].astype(BF16), pick)
            wg_s[:, g * LANES:(g + 1) * LANES] = y[:, 0:LANES].astype(BF16)
            wu_s[:, g * LANES:(g + 1) * LANES] = y[:, LANES:grp].astype(BF16)
        wd_s[...] = wd_ref[0].astype(BF16)

    @pl.when(j < nv_ref[0])
    def _():
        x = jnp.concatenate(_unpack_pairs(xs_ref[...]), axis=1)
        g = _dot(x, wg_s[...]) + bg_ref[0]
        u = _dot(x, wu_s[...]) + bu_ref[0]
        gate = jnp.minimum(g, SWIGLU_LIMIT)
        up = jnp.clip(u, -SWIGLU_LIMIT, SWIGLU_LIMIT)
        act = (up + 1.0) * (gate * jax.nn.sigmoid(gate * SWIGLU_ALPHA))
        y = _dot(act.astype(BF16), wd_s[...]) + bd_ref[0]
        ys_ref[...] = _pack_pairs(y.astype(BF16).astype(F32))

    @pl.when(j >= nv_ref[0])
    def _():
        ys_ref[...] = jnp.zeros_like(ys_ref)


def _experts(block_e, n_valid, xs, wgu, wd, bg, bu, bd):
    n_pad = xs.shape[0]
    de, d = wd.shape[1:]
    blk = (MOE_ROWS, xs.shape[1])
    n_blocks = n_pad // MOE_ROWS
    xrow = lambda j, be, nv: (jnp.minimum(j, nv[0] - 1), 0)
    wsel = lambda j, be, nv: (be[j], 0, 0)
    return pl.pallas_call(
        _experts_kernel,
        grid_spec=pltpu.PrefetchScalarGridSpec(
            num_scalar_prefetch=2,
            grid=(n_blocks,),
            in_specs=[
                pl.BlockSpec(blk, xrow),
                pl.BlockSpec((1, d, 2 * de), wsel),
                pl.BlockSpec((1, de, d), wsel),
                pl.BlockSpec((1, 1, de), wsel),
                pl.BlockSpec((1, 1, de), wsel),
                pl.BlockSpec((1, 1, d), wsel),
            ],
            out_specs=pl.BlockSpec(blk, lambda j, be, nv: (j, 0)),
            scratch_shapes=[
                pltpu.VMEM((d, de), BF16),
                pltpu.VMEM((d, de), BF16),
                pltpu.VMEM((de, d), BF16),
            ],
        ),
        out_shape=jax.ShapeDtypeStruct(xs.shape, xs.dtype),
        compiler_params=pltpu.CompilerParams(dimension_semantics=("arbitrary",),
                                             vmem_limit_bytes=EXPERTS_VMEM_LIMIT),
        name="experts",
    )(block_e, n_valid, xs, wgu, wd, bg, bu, bd)


def _combine_kernel(seg_ref, cnt_ref, x1_ref, lpos_ref, cw_ref, nw_ref, ys_ref, o_ref, stage_ref, sem,
                    *, final_norm):
    i = pl.program_id(0)
    tm, d = x1_ref.shape
    base = i * N_EXPERTS

    buf = i % 2

    def gather(tile_base, win, b, wait):
        for e in range(N_EXPERTS):
            slot = seg_ref[tile_base + e] + win * SEG_WINDOW
            cp = pltpu.make_async_copy(ys_ref.at[pl.ds(pl.multiple_of(slot, SEG_ALIGN), SEG_WINDOW)],
                                       stage_ref.at[b, pl.ds(e * SEG_WINDOW, SEG_WINDOW)], sem.at[b])
            if wait:
                cp.wait()
            else:
                cp.start()

    @pl.when(i == 0)
    def _():
        gather(base, 0, buf, wait=False)

    @pl.when(i + 1 < pl.num_programs(0))
    def _():
        gather(base + N_EXPERTS, 0, 1 - buf, wait=False)

    n_stage = N_EXPERTS * SEG_WINDOW
    owner = lax.broadcasted_iota(jnp.int32, (LANES, n_stage), 1) // SEG_WINDOW
    expand = (owner == lax.broadcasted_iota(jnp.int32, (LANES, n_stage), 0)).astype(BF16)
    j = (lax.broadcasted_iota(jnp.int32, (tm, n_stage), 1) % SEG_WINDOW).astype(F32)
    cw_rows = jnp.concatenate([cw_ref[...].astype(BF16), jnp.zeros((LANES - N_EXPERTS, tm), BF16)], axis=0)
    cw_wide = _dot_tn(cw_rows, expand)

    def window(win, y):
        rank = (lpos_ref[...] - jnp.asarray(win * SEG_WINDOW, F32)).astype(BF16)
        take = jnp.where(_dot(rank, expand) == j, cw_wide, 0.0).astype(BF16)
        gather(base, win, buf, wait=True)
        hi, lo = _unpack_pairs(stage_ref[buf])
        return y + jnp.concatenate([_dot(take, hi), _dot(take, lo)], axis=1)

    def more(win, y):
        gather(base, win, buf, wait=False)
        return window(win, y)

    y = window(0, jnp.zeros((tm, d), F32))
    y = lax.fori_loop(1, _seg_windows(cnt_ref, base), more, y)
    out = x1_ref[...] + y
    if final_norm:
        out = _rms(out, nw_ref[...])
    o_ref[...] = out


def _combine(seg, cnt, x1, lpos, cw, nw, ys, final_norm):
    t, d = x1.shape
    tm = ROUTE_TILE
    kern = functools.partial(_combine_kernel, final_norm=final_norm)
    return pl.pallas_call(
        kern,
        grid_spec=pltpu.PrefetchScalarGridSpec(
            num_scalar_prefetch=2,
            grid=(t // tm,),
            in_specs=[
                pl.BlockSpec((tm, d), lambda i, *_: (i, 0)),
                pl.BlockSpec((tm, LANES), lambda i, *_: (i, 0)),
                pl.BlockSpec((N_EXPERTS, tm), lambda i, *_: (0, i)),
                pl.BlockSpec((1, d), lambda i, *_: (0, 0)),
                pl.BlockSpec(memory_space=pl.ANY),
            ],
            out_specs=pl.BlockSpec((tm, d), lambda i, *_: (i, 0)),
            scratch_shapes=[pltpu.VMEM((2, N_EXPERTS * SEG_WINDOW, d // 2), jnp.uint32),
                            pltpu.SemaphoreType.DMA((2,))],
        ),
        out_shape=jax.ShapeDtypeStruct((t, d), F32),
        compiler_params=_cp(("arbitrary",)),
        name="combine",
    )(seg, cnt, x1, lpos, cw, nw, ys)


def _rope_tables(seq):
    half = MLA_ROPE // 2
    inv = 1.0 / (ROPE_THETA ** (jnp.arange(0, MLA_ROPE, 2, dtype=F32) / MLA_ROPE))
    ang = jnp.arange(seq, dtype=F32)[:, None] * inv[None, :]
    cos, sin = jnp.cos(ang), jnp.sin(ang)
    zeros = jnp.zeros((seq, LANES - MLA_ROPE), F32)
    cos_row = jnp.concatenate([cos, cos, zeros], axis=-1)
    sin_row = jnp.concatenate([-sin, sin, zeros], axis=-1)
    del half
    return cos_row, sin_row, cos.T, sin.T


def _pad_cols(a, width):
    return jnp.pad(a, ((0, 0), (0, width - a.shape[1])))


def _layer(x2, batch, seq, final_norm_w, final_norm, rope_tables,
           norm_mix_w, w_in, gdn_conv_w, gdn_a_log, gdn_dt_bias, gdn_norm_w, w_gdn_o,
           mla_q_norm_w, w_mla_q_b, mla_kv_norm_w, w_mla_kv_b, w_mla_o, w_out,
           norm_ffn_w, w_router, b_router, w_gate_up, b_gate_up, w_down, b_down):
    t, d = x2.shape
    qk_w = GDN_HEADS * GDN_D
    o_b = 4 * qk_w
    o_a = o_b + GDN_HEADS
    o_cq = o_a + GDN_HEADS
    o_ckv = o_cq + MLA_Q_LORA
    o_kr = o_ckv + MLA_KV_LORA
    o_ga = o_kr + MLA_ROPE
    o_gb = o_ga + d
    w_p = jnp.concatenate([
        w_in[:, 0:o_b], w_in[:, o_ga:o_gb + d], w_in[:, o_cq:o_ckv], w_in[:, o_ckv:o_kr],
        _pad_cols(w_in[:, o_kr:o_ga], 2 * LANES)], axis=1).astype(BF16)
    w_ab = _pad_cols(jnp.concatenate([w_in[:, o_a:o_cq], w_in[:, o_b:o_a]], axis=1), LANES).astype(BF16)

    p, ab = _in_proj(x2, norm_mix_w[None, :], w_p, w_ab)

    alog_row = _pad_cols(gdn_a_log[None, :].astype(F32), LANES)
    dtb_row = _pad_cols(gdn_dt_bias[None, :].astype(F32), LANES)
    qkvn, cols, gct = _gdn_prep(p, ab, gdn_conv_w.astype(F32), alog_row, dtb_row, seq)
    o_gdn = _gdn_chunk(qkvn, p, cols, gct, gdn_norm_w[None, :].astype(F32), batch, seq)

    hd = MLA_NOPE + MLA_ROPE
    wq = w_mla_q_b.reshape(MLA_Q_LORA, MLA_HEADS, hd)
    wqt = jnp.pad(wq, ((0, 0), (0, 0), (0, 2 * LANES - hd))).reshape(MLA_Q_LORA, MLA_HEADS * 2 * LANES).T
    wkv = w_mla_kv_b.reshape(MLA_KV_LORA, MLA_HEADS, MLA_NOPE + MLA_V)
    wkn = wkv[:, :, :MLA_NOPE].reshape(MLA_KV_LORA, -1)
    wvt = wkv[:, :, MLA_NOPE:].reshape(MLA_KV_LORA, -1).T
    qt, kn, kr, vt = _mla_prep(p, rope_tables, mla_q_norm_w[None, :].astype(F32),
                               mla_kv_norm_w[None, :].astype(F32), wqt.astype(BF16), wkn.astype(BF16),
                               wvt.astype(BF16), seq)
    o_mla = _mla_attn(qt, kn, kr, vt, batch, seq)

    wr = _pad_cols(w_router.astype(F32), LANES).T
    wr_hi = wr.astype(BF16)
    wr_lo = (wr - wr_hi.astype(F32)).astype(BF16)
    br = jnp.broadcast_to(_pad_cols(b_router[None, :].astype(F32), LANES).T, (LANES, ROUTE_TILE))
    x1, h2, sel, cw = _mix_out(x2, o_gdn, o_mla, p, w_gdn_o.astype(BF16), w_mla_o.astype(BF16),
                               w_out.astype(BF16), norm_ffn_w[None, :].astype(F32), wr_hi, wr_lo, br)

    lpos, keyt, offs, cnt, tot = _route_pos(sel)
    n_tiles = t // ROUTE_TILE
    used = tot[0, :N_EXPERTS]
    padded = (used + SEG_WINDOW + MOE_ROWS - 1) // MOE_ROWS * MOE_ROWS
    pad_end = jnp.cumsum(padded)
    pad_start = pad_end - padded
    seg = (pad_start[None, :] + offs[:, 0, :N_EXPERTS]).astype(jnp.int32).reshape(-1)
    cnt = cnt[:, 0, :N_EXPERTS].reshape(-1)
    worst_used = t * TOP_K + n_tiles * N_EXPERTS * (SEG_ALIGN - 1) + N_EXPERTS * SEG_WINDOW
    n_pad = -(-worst_used // MOE_ROWS) * MOE_ROWS + (N_EXPERTS + 1) * MOE_ROWS
    n_blocks = n_pad // MOE_ROWS
    blk_start = jnp.arange(n_blocks, dtype=jnp.int32) * MOE_ROWS
    block_e = jnp.minimum(jnp.sum((pad_end[None, :] <= blk_start[:, None]).astype(jnp.int32), axis=1),
                          N_EXPERTS - 1).astype(jnp.int32)
    n_valid = (pad_end[-1:] // MOE_ROWS).astype(jnp.int32)

    xs = _dispatch(seg, cnt, (pad_start + used).astype(jnp.int32), pad_end.astype(jnp.int32), h2, keyt, n_pad)
    bg = b_gate_up[:, None, 0::2].astype(F32)
    bu = b_gate_up[:, None, 1::2].astype(F32)
    ys = _experts(block_e, n_valid, xs, w_gate_up, w_down, bg, bu, b_down[:, None, :].astype(F32))
    return _combine(seg, cnt, x1, lpos, cw, final_norm_w[None, :].astype(F32), ys, final_norm)


def kernel(x, norm_mix_w, w_in, gdn_conv_w, gdn_a_log, gdn_dt_bias, gdn_norm_w, w_gdn_o, mla_q_norm_w, w_mla_q_b, mla_kv_norm_w, w_mla_kv_b, w_mla_o, w_out, norm_ffn_w, w_router, b_router, w_gate_up, b_gate_up, w_down, b_down, norm_final_w):
    batch, seq, d = x.shape
    depth = w_in.shape[0]
    rope_tables = _rope_tables(seq)
    x2 = x.reshape(batch * seq, d)
    for layer in range(depth):
        x2 = _layer(x2, batch, seq, norm_final_w, layer == depth - 1, rope_tables,
                    norm_mix_w[layer], w_in[layer], gdn_conv_w[layer], gdn_a_log[layer],
                    gdn_dt_bias[layer], gdn_norm_w[layer], w_gdn_o[layer], mla_q_norm_w[layer],
                    w_mla_q_b[layer], mla_kv_norm_w[layer], w_mla_kv_b[layer], w_mla_o[layer],
                    w_out[layer], norm_ffn_w[layer], w_router[layer], b_router[layer],
                    w_gate_up[layer], b_gate_up[layer], w_down[layer], b_down[layer])
    return x2.reshape(batch, seq, d)
```

```python
import functools

import jax
import jax.numpy as jnp
import numpy as np
from jax import lax
from jax.experimental import pallas as pl
from jax.experimental.pallas import tpu as pltpu

F32 = jnp.float32
BF16 = jnp.bfloat16

CHUNK = 64
NORM_EPS = 1e-6
GDN_HEADS = 8
GDN_D = 128
GDN_CONV = 4
MLA_HEADS = 8
MLA_Q_LORA = 512
MLA_KV_LORA = 256
MLA_NOPE = 128
MLA_ROPE = 64
MLA_V = 128
ROPE_THETA = 10000.0
N_EXPERTS = 32
TOP_K = 4
SWIGLU_LIMIT = 7.0
SWIGLU_ALPHA = 1.702

LANES = 128
MOE_ROWS = 512
ROUTE_TILE = 256
SEG_ALIGN = 8
SEG_WINDOW = 64
ZERO_ROWS = 64
VMEM_LIMIT = 48 * 1024 * 1024
EXPERTS_VMEM_LIMIT = 56 * 1024 * 1024

NEG_BIG = -1e30
LOG2_E = 1.4426950408889634


def _cp(sem):
    return pltpu.CompilerParams(dimension_semantics=sem, vmem_limit_bytes=VMEM_LIMIT)


def _dot(a, b):
    return jnp.dot(a, b, preferred_element_type=F32)


def _dot_nt(a, b):
    return lax.dot_general(a, b, (((1,), (1,)), ((), ())), preferred_element_type=F32)


def _dot_tn(a, b):
    return lax.dot_general(a, b, (((0,), (0,)), ((), ())), preferred_element_type=F32)


def _split3(x):
    hi = x.astype(BF16)
    r = x - hi.astype(F32)
    mid = r.astype(BF16)
    lo = (r - mid.astype(F32)).astype(BF16)
    return hi, mid, lo


def _rms(x, w):
    ms = jnp.mean(x * x, axis=-1, keepdims=True)
    return x * lax.rsqrt(ms + NORM_EPS) * w


def _in_proj_kernel(x_ref, nw_ref, w_ref, wab_ref, p_ref, ab_ref, h_ref):
    @pl.when(pl.program_id(1) == 0)
    def _():
        hb = _rms(x_ref[...], nw_ref[...]).astype(BF16)
        h_ref[...] = hb
        ab_ref[...] = _dot(hb, wab_ref[...])

    p_ref[...] = _dot(h_ref[...], w_ref[...]).astype(BF16)


def _in_proj(x2, norm_w, w_p, w_ab, tm=1024, tn=3584):
    t, d = x2.shape
    n = w_p.shape[1]
    return pl.pallas_call(
        _in_proj_kernel,
        grid=(t // tm, n // tn),
        in_specs=[
            pl.BlockSpec((tm, d), lambda i, j: (i, 0)),
            pl.BlockSpec((1, d), lambda i, j: (0, 0)),
            pl.BlockSpec((d, tn), lambda i, j: (0, j)),
            pl.BlockSpec((d, LANES), lambda i, j: (0, 0)),
        ],
        out_specs=[
            pl.BlockSpec((tm, tn), lambda i, j: (i, j)),
            pl.BlockSpec((tm, LANES), lambda i, j: (i, 0)),
        ],
        out_shape=[
            jax.ShapeDtypeStruct((t, n), BF16),
            jax.ShapeDtypeStruct((t, LANES), F32),
        ],
        scratch_shapes=[pltpu.VMEM((tm, d), BF16)],
        compiler_params=_cp(("parallel", "arbitrary")),
        name="in_proj",
    )(x2, norm_w, w_p, w_ab)


def _gdn_prep_kernel(cur_ref, prev_ref, ab_ref, cw_ref, alog_ref, dtb_ref,
                     qkv_ref, cols_ref, gct_ref, *, tiles_per_seq):
    tm = cur_ref.shape[0]
    i = pl.program_id(0)
    halo_on = (i % tiles_per_seq) != 0
    q_scale = GDN_D ** -0.5
    grp = 2 * LANES
    row = lax.broadcasted_iota(jnp.int32, (tm, tm), 0)
    col = lax.broadcasted_iota(jnp.int32, (tm, tm), 1)
    shift = [(col == row - s).astype(BF16) for s in range(1, GDN_CONV)]
    for cg in range(cur_ref.shape[1] // grp):
        gs = slice(cg * grp, (cg + 1) * grp)
        cur_b = cur_ref[:, gs]
        cur = cur_b.astype(F32)
        w = cw_ref[:, gs]
        y = w[GDN_CONV - 1:GDN_CONV, :] * cur
        for s in range(1, GDN_CONV):
            y = y + w[GDN_CONV - 1 - s:GDN_CONV - s, :] * _dot(shift[s - 1], cur_b)
        halo = jnp.where(halo_on, prev_ref[:, gs].astype(F32)[8:16, :], 0.0)
        xe = jnp.concatenate([halo, cur[0:8, :]], axis=0)
        head = w[0:1, :] * xe[5:13, :]
        for j in range(1, GDN_CONV):
            head = head + w[j:j + 1, :] * xe[5 + j:13 + j, :]
        y = jnp.concatenate([head, y[8:, :]], axis=0)
        hy = 0.5 * y
        y = hy + hy * jnp.tanh(hy)
        for half in range(2):
            cb = 2 * cg + half
            yh = y[:, half * LANES:(half + 1) * LANES]
            if cb < 2 * GDN_HEADS:
                ss = jnp.sum(yh * yh, axis=-1, keepdims=True)
                yh = yh * lax.rsqrt(ss + NORM_EPS)
                if cb < GDN_HEADS:
                    yh = yh * q_scale
            qkv_ref[:, cb * LANES:(cb + 1) * LANES] = yh.astype(BF16)

    ab = ab_ref[...]
    g = -jnp.exp(alog_ref[...]) * jax.nn.softplus(ab + dtb_ref[...])
    row = lax.broadcasted_iota(jnp.int32, (tm, tm), 0)
    col = lax.broadcasted_iota(jnp.int32, (tm, tm), 1)
    tri = ((col <= row) & ((row // CHUNK) == (col // CHUNK))).astype(BF16)
    g_hi, g_mid, g_lo = _split3(g)
    gc = _dot(tri, g_hi) + _dot(tri, g_mid) + _dot(tri, g_lo)
    lane = lax.broadcasted_iota(jnp.int32, (tm, LANES), 1)
    cols_ref[...] = jnp.where(lane < GDN_HEADS, gc, jax.nn.sigmoid(ab))
    for c in range(tm // CHUNK):
        blk = gc[c * CHUNK:(c + 1) * CHUNK, :]
        blk = jnp.concatenate([blk, jnp.zeros_like(blk)], axis=0)
        gct_ref[c] = blk.T[0:GDN_HEADS, 0:CHUNK]


def _gdn_prep(p, ab, conv_w, alog_row, dtb_row, seq, tm=256):
    t = p.shape[0]
    cw = 3 * GDN_HEADS * GDN_D
    tiles_per_seq = seq // tm
    kern = functools.partial(_gdn_prep_kernel, tiles_per_seq=tiles_per_seq)
    return pl.pallas_call(
        kern,
        grid=(t // tm,),
        in_specs=[
            pl.BlockSpec((tm, cw), lambda i: (i, 0)),
            pl.BlockSpec((16, cw), lambda i: (jnp.maximum(i * (tm // 16) - 1, 0), 0)),
            pl.BlockSpec((tm, LANES), lambda i: (i, 0)),
            pl.BlockSpec((GDN_CONV, cw), lambda i: (0, 0)),
            pl.BlockSpec((1, LANES), lambda i: (0, 0)),
            pl.BlockSpec((1, LANES), lambda i: (0, 0)),
        ],
        out_specs=[
            pl.BlockSpec((tm, cw), lambda i: (i, 0)),
            pl.BlockSpec((tm, LANES), lambda i: (i, 0)),
            pl.BlockSpec((tm // CHUNK, GDN_HEADS, CHUNK), lambda i: (i, 0, 0)),
        ],
        out_shape=[
            jax.ShapeDtypeStruct((t, cw), BF16),
            jax.ShapeDtypeStruct((t, LANES), F32),
            jax.ShapeDtypeStruct((t // CHUNK, GDN_HEADS, CHUNK), F32),
        ],
        compiler_params=_cp(("parallel",)),
        name="gdn_prep",
    )(p, p, ab, conv_w, alog_row, dtb_row)


GDN_CHUNKS_PER_STEP = 2


def _gdn_chunk_kernel(q_ref, k_ref, v_ref, z_ref, cols_ref, gct_ref, nw_ref, o_ref, s_ref):
    c = CHUNK
    nb = q_ref.shape[0]
    chains = [(b, h) for b in range(nb) for h in range(GDN_HEADS)]
    units = [(g, b, h) for g in range(GDN_CHUNKS_PER_STEP) for b, h in chains]

    @pl.when(pl.program_id(0) == 0)
    def _():
        s_ref[...] = jnp.zeros_like(s_ref)

    ri = lax.broadcasted_iota(jnp.int32, (c, c), 0)
    ci = lax.broadcasted_iota(jnp.int32, (c, c), 1)
    incl = ri >= ci
    strict = ri > ci
    eye = (ri == ci).astype(F32)
    nw = nw_ref[...]

    def rows(g):
        return slice(g * c, (g + 1) * c)

    def head(h):
        return slice(h * GDN_D, (h + 1) * GDN_D)

    cols, e_g, e_kd, e_last, gct = {}, {}, {}, {}, {}
    for g in range(GDN_CHUNKS_PER_STEP):
        for b in range(nb):
            cb = cols_ref[b, rows(g), :]
            last = cb[c - 1:c, :]
            cols[g, b] = cb
            e_g[g, b] = jnp.exp(cb)
            e_kd[g, b] = jnp.exp(last - cb)
            e_last[g, b] = jnp.exp(last)
            gct[g, b] = gct_ref[b, g]

    kq, kb_l, kf_l = [], [], []
    for g, b, h in units:
        k = k_ref[b, rows(g), head(h)]
        kf = k.astype(F32)
        kb = kf * cols[g, b][:, GDN_HEADS + h:GDN_HEADS + h + 1]
        kq.append(_dot_nt(jnp.concatenate([kb.astype(BF16), q_ref[b, rows(g), head(h)]], axis=0), k))
        kb_l.append(kb)
        kf_l.append(kf)

    a_l, qk_l = [], []
    for i, (g, b, h) in enumerate(units):
        dec = jnp.exp(jnp.minimum(cols[g, b][:, h:h + 1] - gct[g, b][h:h + 1, :], 0.0))
        a_l.append(jnp.where(strict, -kq[i][0:c, :] * dec, 0.0))
        qk_l.append(jnp.where(incl, kq[i][c:2 * c, :] * dec, 0.0).astype(BF16))

    tinv = [eye + a for a in a_l]
    pw = a_l
    for _ in range(5):
        pwb = [x.astype(BF16) for x in pw]
        pw = [_dot(x, x) for x in pwb]
        tinv = [t + _dot(t.astype(BF16), x.astype(BF16)) for t, x in zip(tinv, pw)]

    u_l, lhs_l, kd_l = [], [], []
    for i, (g, b, h) in enumerate(units):
        beta = cols[g, b][:, GDN_HEADS + h:GDN_HEADS + h + 1]
        eg = e_g[g, b][:, h:h + 1]
        rhs = jnp.concatenate([v_ref[b, rows(g), head(h)].astype(F32) * beta, kb_l[i] * eg],
                              axis=1).astype(BF16)
        uw = _dot(tinv[i].astype(BF16), rhs)
        qd = (q_ref[b, rows(g), head(h)].astype(F32) * eg).astype(BF16)
        u_l.append(uw[:, 0:GDN_D])
        lhs_l.append(jnp.concatenate([uw[:, GDN_D:2 * GDN_D].astype(BF16), qd], axis=0))
        kd_l.append((kf_l[i] * e_kd[g, b][:, h:h + 1]).astype(BF16))

    for g in range(GDN_CHUNKS_PER_STEP):
        first = g * len(chains)
        r_l = [_dot(lhs_l[first + n], s_ref[n].astype(BF16)) for n in range(len(chains))]
        for n, (b, h) in enumerate(chains):
            i = first + n
            v_new = (u_l[i] - r_l[n][0:c, :]).astype(BF16)
            o = r_l[n][c:2 * c, :] + _dot(qk_l[i], v_new)
            s_ref[n] = s_ref[n] * e_last[g, b][:, h:h + 1] + _dot_tn(kd_l[i], v_new)
            z = z_ref[b, rows(g), head(h)].astype(F32)
            o_ref[b, rows(g), head(h)] = (_rms(o, nw) * (z * jax.nn.sigmoid(z))).astype(BF16)


def _gdn_chunk(qkvn, p, cols, gct, norm_w, batch, seq):
    nc = seq // CHUNK
    hw = GDN_HEADS * GDN_D
    g = GDN_CHUNKS_PER_STEP
    qkvn3 = qkvn.reshape(batch, seq, qkvn.shape[1])
    p3 = p.reshape(batch, seq, p.shape[1])
    cols3 = cols.reshape(batch, seq, LANES)
    gct4 = gct.reshape(batch, nc, GDN_HEADS, CHUNK)
    tile = lambda col: pl.BlockSpec((batch, g * CHUNK, hw), lambda c: (0, c, col))
    out = pl.pallas_call(
        _gdn_chunk_kernel,
        grid=(nc // g,),
        in_specs=[
            tile(0), tile(1), tile(2),
            tile(3),
            pl.BlockSpec((batch, g * CHUNK, LANES), lambda c: (0, c, 0)),
            pl.BlockSpec((batch, g, GDN_HEADS, CHUNK), lambda c: (0, c, 0, 0)),
            pl.BlockSpec((1, GDN_D), lambda c: (0, 0)),
        ],
        out_specs=pl.BlockSpec((batch, g * CHUNK, hw), lambda c: (0, c, 0)),
        out_shape=jax.ShapeDtypeStruct((batch, seq, hw), BF16),
        scratch_shapes=[pltpu.VMEM((batch * GDN_HEADS, GDN_D, GDN_D), F32)],
        compiler_params=_cp(("arbitrary",)),
        name="gdn_chunk",
    )(qkvn3, qkvn3, qkvn3, p3, cols3, gct4, norm_w)
    return out.reshape(batch * seq, hw)


def _rope(x, cos, sin_signed):
    lane = lax.broadcasted_iota(jnp.int32, x.shape, 1)
    fwd = pltpu.roll(x, LANES - MLA_ROPE // 2, 1)
    bwd = pltpu.roll(x, MLA_ROPE // 2, 1)
    rot = jnp.where(lane < MLA_ROPE // 2, fwd, bwd)
    return x * cos + rot * sin_signed


def _mla_prep_kernel(cq_ref, ckv_ref, kr_ref, cos_ref, sin_ref, cost_ref, sint_ref, qnw_ref, kvnw_ref,
                     wqt_ref, wkn_ref, wvt_ref, qt_ref, kn_ref, kro_ref, vt_ref):
    cos = cos_ref[...]
    sin = sin_ref[...]
    cos_t = cost_ref[...]
    sin_t = sint_ref[...]
    cq = _rms(cq_ref[...].astype(F32), qnw_ref[...]).astype(BF16)
    hd = 2 * LANES
    half = MLA_ROPE // 2
    scale = (MLA_NOPE + MLA_ROPE) ** -0.5 * LOG2_E
    for h in range(MLA_HEADS):
        qh = _dot_nt(wqt_ref[h * hd:(h + 1) * hd, :], cq) * scale
        lo = qh[MLA_NOPE:MLA_NOPE + half, :]
        hi = qh[MLA_NOPE + half:MLA_NOPE + MLA_ROPE, :]
        qt_ref[0, h * hd:h * hd + MLA_NOPE, :] = qh[0:MLA_NOPE, :].astype(BF16)
        qt_ref[0, h * hd + MLA_NOPE:h * hd + MLA_NOPE + half, :] = (lo * cos_t - hi * sin_t).astype(BF16)
        qt_ref[0, h * hd + MLA_NOPE + half:h * hd + MLA_NOPE + MLA_ROPE, :] = (hi * cos_t + lo * sin_t).astype(BF16)
        qt_ref[0, h * hd + MLA_NOPE + MLA_ROPE:(h + 1) * hd, :] = qh[MLA_NOPE + MLA_ROPE:hd, :].astype(BF16)
    kvl = _rms(ckv_ref[...].astype(F32), kvnw_ref[...]).astype(BF16)
    kn_ref[...] = _dot(kvl, wkn_ref[...]).astype(BF16)
    vt_ref[0] = _dot_nt(wvt_ref[...], kvl).astype(BF16)
    kro_ref[...] = _rope(kr_ref[...].astype(F32), cos, sin).astype(BF16)


def _mla_prep(p, tables, qnw, kvnw, wqt, wkn, wvt, seq, tm=512):
    cos_row, sin_row, cos_col, sin_col = tables
    t = p.shape[0]
    tiles_per_seq = seq // tm
    hw = MLA_HEADS * MLA_NOPE
    half = MLA_ROPE // 2
    cq_blk = 6144 // MLA_Q_LORA
    ckv_blk = 6656 // MLA_KV_LORA
    kr_blk = 6912 // LANES
    return pl.pallas_call(
        _mla_prep_kernel,
        grid=(t // tm,),
        in_specs=[
            pl.BlockSpec((tm, MLA_Q_LORA), lambda i: (i, cq_blk)),
            pl.BlockSpec((tm, MLA_KV_LORA), lambda i: (i, ckv_blk)),
            pl.BlockSpec((tm, LANES), lambda i: (i, kr_blk)),
            pl.BlockSpec((tm, LANES), lambda i: (i % tiles_per_seq, 0)),
            pl.BlockSpec((tm, LANES), lambda i: (i % tiles_per_seq, 0)),
            pl.BlockSpec((half, tm), lambda i: (0, i % tiles_per_seq)),
            pl.BlockSpec((half, tm), lambda i: (0, i % tiles_per_seq)),
            pl.BlockSpec((1, MLA_Q_LORA), lambda i: (0, 0)),
            pl.BlockSpec((1, MLA_KV_LORA), lambda i: (0, 0)),
            pl.BlockSpec((2 * hw, MLA_Q_LORA), lambda i: (0, 0)),
            pl.BlockSpec((MLA_KV_LORA, hw), lambda i: (0, 0)),
            pl.BlockSpec((hw, MLA_KV_LORA), lambda i: (0, 0)),
        ],
        out_specs=[
            pl.BlockSpec((1, 2 * hw, tm), lambda i: (i, 0, 0)),
            pl.BlockSpec((tm, hw), lambda i: (i, 0)),
            pl.BlockSpec((tm, LANES), lambda i: (i, 0)),
            pl.BlockSpec((1, hw, tm), lambda i: (i, 0, 0)),
        ],
        out_shape=[
            jax.ShapeDtypeStruct((t // tm, 2 * hw, tm), BF16),
            jax.ShapeDtypeStruct((t, hw), BF16),
            jax.ShapeDtypeStruct((t, LANES), BF16),
            jax.ShapeDtypeStruct((t // tm, hw, tm), BF16),
        ],
        compiler_params=_cp(("parallel",)),
        name="mla_prep",
    )(p, p, p, cos_row, sin_row, cos_col, sin_col, qnw, kvnw, wqt, wkn, wvt)


ATTN_HEADS_PER_STEP = 8
ATTN_SUM_ROWS = 16


def _mla_attn_kernel(qt_ref, kt_ref, q_ref, kn_ref, kr_ref, vt_ref, o_ref, m_ref, acc_ref):
    qi = qt_ref[pl.program_id(2)]
    ki = kt_ref[pl.program_id(2)]
    tq = q_ref.shape[2]
    tk = kn_ref.shape[0]
    hd = 2 * LANES
    q0 = qi * tq
    k0 = ki * tk
    last_k = (q0 + tq) // tk - 1

    @pl.when(ki == 0)
    def _():
        m_ref[...] = jnp.full_like(m_ref, NEG_BIG)
        acc_ref[...] = jnp.zeros_like(acc_ref)

    def step(masked):
        kr = kr_ref[...]
        ones = jnp.ones((ATTN_SUM_ROWS, tk), BF16)
        if masked:
            ck = lax.broadcasted_iota(jnp.int32, (tk, tq), 0) // CHUNK
            cq = lax.broadcasted_iota(jnp.int32, (tk, tq), 1) // CHUNK
            visible = (ck <= cq) if tq == tk else ((ck - cq) <= (q0 - k0) // CHUNK)

        def scores(h):
            k = jnp.concatenate([kn_ref[:, h * MLA_NOPE:(h + 1) * MLA_NOPE], kr], axis=1)
            s = _dot(k, q_ref[0, h * hd:(h + 1) * hd, :])
            if masked:
                s = jnp.where(visible, s, NEG_BIG)
            return s

        def update(h, s):
            m_prev = m_ref[h]
            m_new = jnp.maximum(m_prev, jnp.max(s, axis=0, keepdims=True))
            alpha = jnp.exp2(m_prev - m_new)
            p = jnp.exp2((s - m_new).astype(BF16))
            v_ext = jnp.concatenate([vt_ref[0, h * MLA_V:(h + 1) * MLA_V, :], ones], axis=0)
            acc_ref[h] = alpha * acc_ref[h] + _dot(v_ext, p)
            m_ref[h] = m_new

        s_prev = scores(0)
        for h in range(1, ATTN_HEADS_PER_STEP):
            s_next = scores(h)
            update(h - 1, s_prev)
            s_prev = s_next
        update(ATTN_HEADS_PER_STEP - 1, s_prev)

    @pl.when(k0 + tk <= q0)
    def _():
        step(False)

    def finish():
        for h in range(ATTN_HEADS_PER_STEP):
            acc = acc_ref[h]
            o = acc[0:MLA_V, :] / acc[MLA_V:MLA_V + 1, :]
            o_ref[:, h * MLA_V:(h + 1) * MLA_V] = o.T.astype(BF16)

    @pl.when(k0 + tk > q0)
    def _():
        step(True)
        if tq == tk:
            finish()
        else:
            pl.when(ki == last_k)(finish)


def _mla_attn(qt_all, kn, kr, vt, batch, seq):
    t = kn.shape[0]
    tq = qt_all.shape[2]
    tk = vt.shape[2]
    nq = seq // tq
    nk = seq // tk
    hps = ATTN_HEADS_PER_STEP
    pairs = [(qi, ki) for qi in range(nq) for ki in range((qi + 1) * tq // tk)]
    qt = jnp.asarray(np.array([pr[0] for pr in pairs], np.int32))
    kt = jnp.asarray(np.array([pr[1] for pr in pairs], np.int32))
    return pl.pallas_call(
        _mla_attn_kernel,
        grid_spec=pltpu.PrefetchScalarGridSpec(
            num_scalar_prefetch=2,
            grid=(batch, MLA_HEADS // hps, len(pairs)),
            in_specs=[
                pl.BlockSpec((1, hps * 2 * LANES, tq), lambda b, h, pr, qt, kt: (b * nq + qt[pr], h, 0)),
                pl.BlockSpec((tk, hps * MLA_NOPE), lambda b, h, pr, qt, kt: (b * nk + kt[pr], h)),
                pl.BlockSpec((tk, LANES), lambda b, h, pr, qt, kt: (b * nk + kt[pr], 0)),
                pl.BlockSpec((1, hps * MLA_V, tk), lambda b, h, pr, qt, kt: (b * nk + kt[pr], h, 0)),
            ],
            out_specs=pl.BlockSpec((tq, hps * MLA_V), lambda b, h, pr, qt, kt: (b * nq + qt[pr], h)),
            scratch_shapes=[
                pltpu.VMEM((hps, 1, tq), F32),
                pltpu.VMEM((hps, MLA_V + ATTN_SUM_ROWS, tq), F32),
            ],
        ),
        out_shape=jax.ShapeDtypeStruct((t, MLA_HEADS * MLA_V), BF16),
        compiler_params=_cp(("parallel", "parallel", "arbitrary")),
        name="mla_attn",
    )(qt, kt, qt_all, kn, kr, vt)


MIX_SUBTILES = 2


def _mix_out_kernel(x_ref, oa_ref, ob_ref, ga_ref, gb_ref, wga_ref, wmo_ref, wout_ref, nw_ref,
                    wr_hi_ref, wr_lo_ref, br_ref, x1_ref, h2_ref, selt_ref, cwt_ref):
    sub = br_ref.shape[1]

    def mix(r):
        ya = _dot(oa_ref[r, :], wga_ref[...])
        yb = _dot(ob_ref[r, :], wmo_ref[...])
        merged = (jax.nn.sigmoid(ga_ref[r, :].astype(F32)) * ya
                  + jax.nn.sigmoid(gb_ref[r, :].astype(F32)) * yb)
        x1 = x_ref[r, :] + _dot(merged.astype(BF16), wout_ref[...])
        x1_ref[r, :] = x1
        h2 = _rms(x1, nw_ref[...])
        h_hi = h2.astype(BF16)
        h2_ref[r, :] = h_hi
        h_lo = (h2 - h_hi.astype(F32)).astype(BF16)
        return (_dot_nt(wr_hi_ref[...], h_hi) + _dot_nt(wr_lo_ref[...], h_hi) + _dot_nt(wr_hi_ref[...], h_lo)
                + br_ref[...])[0:N_EXPERTS, :]

    def route(logits, r):
        expert = lax.broadcasted_iota(jnp.int32, logits.shape, 0)
        work = logits
        sel = jnp.zeros(logits.shape, F32)
        cw = jnp.zeros(logits.shape, F32)
        top = None
        denom = None
        for kk in range(TOP_K):
            mx = jnp.max(work, axis=0, keepdims=True)
            am = jnp.min(jnp.where(work == mx, expert, N_EXPERTS), axis=0, keepdims=True)
            hit = expert == am
            if kk == 0:
                top = mx
                e = jnp.ones_like(mx)
                denom = e
            else:
                e = jnp.exp(mx - top)
                denom = denom + e
            sel = jnp.where(hit, 1.0, sel)
            cw = jnp.where(hit, e, cw)
            work = jnp.where(hit, -jnp.inf, work)
        selt_ref[:, r] = sel.astype(BF16)
        cwt_ref[:, r] = cw / denom

    subs = [slice(n * sub, (n + 1) * sub) for n in range(x_ref.shape[0] // sub)]
    logits = [mix(r) for r in subs]
    for lg, r in zip(logits, subs):
        route(lg, r)


def _mix_out(x2, oa, ob, p, wga, wmo, wout, nw, wr_hi, wr_lo, br):
    t, d = x2.shape
    tm = MIX_SUBTILES * br.shape[1]
    full = lambda i: (0, 0)
    return pl.pallas_call(
        _mix_out_kernel,
        grid=(t // tm,),
        in_specs=[
            pl.BlockSpec((tm, d), lambda i: (i, 0)),
            pl.BlockSpec((tm, d), lambda i: (i, 0)),
            pl.BlockSpec((tm, d), lambda i: (i, 0)),
            pl.BlockSpec((tm, d), lambda i: (i, 4)),
            pl.BlockSpec((tm, d), lambda i: (i, 5)),
            pl.BlockSpec((d, d), full),
            pl.BlockSpec((d, d), full),
            pl.BlockSpec((d, d), full),
            pl.BlockSpec((1, d), full),
            pl.BlockSpec((LANES, d), full),
            pl.BlockSpec((LANES, d), full),
            pl.BlockSpec(br.shape, full),
        ],
        out_specs=[
            pl.BlockSpec((tm, d), lambda i: (i, 0)),
            pl.BlockSpec((tm, d), lambda i: (i, 0)),
            pl.BlockSpec((N_EXPERTS, tm), lambda i: (0, i)),
            pl.BlockSpec((N_EXPERTS, tm), lambda i: (0, i)),
        ],
        out_shape=[
            jax.ShapeDtypeStruct((t, d), F32),
            jax.ShapeDtypeStruct((t, d), BF16),
            jax.ShapeDtypeStruct((N_EXPERTS, t), BF16),
            jax.ShapeDtypeStruct((N_EXPERTS, t), F32),
        ],
        compiler_params=_cp(("parallel",)),
        name="mix_out",
    )(x2, oa, ob, p, p, wga, wmo, wout, nw, wr_hi, wr_lo, br)


def _route_pos_kernel(selt_ref, lpos_ref, keyt_ref, offs_ref, cnt_ref, tot_ref, carry_ref):
    tm = selt_ref.shape[1]
    i = pl.program_id(0)

    @pl.when(i == 0)
    def _():
        carry_ref[...] = jnp.zeros_like(carry_ref)

    sel_t = selt_ref[...]
    sel_rows = jnp.concatenate([sel_t, jnp.zeros((LANES - N_EXPERTS, tm), BF16)], axis=0)
    row = lax.broadcasted_iota(jnp.int32, (tm, tm), 0)
    col = lax.broadcasted_iota(jnp.int32, (tm, tm), 1)
    lpos_ref[...] = _dot_nt((col < row).astype(BF16), sel_rows)
    pos_t = _dot(sel_t, (row < col).astype(BF16))
    keyt_ref[0] = jnp.where(sel_t > 0, pos_t, -1.0)

    n = _dot_nt(jnp.ones((8, tm), BF16), sel_rows)[0:1, :]
    carry = carry_ref[0:1, :]
    offs_ref[0] = carry.astype(jnp.int32)
    cnt_ref[0] = n.astype(jnp.int32)
    total = carry + jnp.ceil(n * (1.0 / SEG_ALIGN)) * SEG_ALIGN
    carry_ref[...] = jnp.broadcast_to(total, carry_ref.shape)
    tot_ref[...] = jnp.broadcast_to(total, tot_ref.shape).astype(jnp.int32)


def _route_pos(sel_t):
    t = sel_t.shape[1]
    tm = ROUTE_TILE
    nt = t // tm
    return pl.pallas_call(
        _route_pos_kernel,
        grid=(nt,),
        in_specs=[pl.BlockSpec((N_EXPERTS, tm), lambda i: (0, i))],
        out_specs=[
            pl.BlockSpec((tm, LANES), lambda i: (i, 0)),
            pl.BlockSpec((1, N_EXPERTS, tm), lambda i: (i, 0, 0)),
            pl.BlockSpec((1, 1, LANES), lambda i: (i, 0, 0)),
            pl.BlockSpec((1, 1, LANES), lambda i: (i, 0, 0)),
            pl.BlockSpec((8, LANES), lambda i: (0, 0)),
        ],
        out_shape=[
            jax.ShapeDtypeStruct((t, LANES), F32),
            jax.ShapeDtypeStruct((nt, N_EXPERTS, tm), F32),
            jax.ShapeDtypeStruct((nt, 1, LANES), jnp.int32),
            jax.ShapeDtypeStruct((nt, 1, LANES), jnp.int32),
            jax.ShapeDtypeStruct((8, LANES), jnp.int32),
        ],
        scratch_shapes=[pltpu.VMEM((8, LANES), F32)],
        compiler_params=_cp(("arbitrary",)),
        name="route_pos",
    )(sel_t)


def _rows(ref, start, n):
    return ref.at[pl.ds(pl.multiple_of(start, n), n)]


def _pack_pairs(x):
    half = x.shape[1] // 2
    hi = lax.bitcast_convert_type(x[:, :half], jnp.uint32)
    lo = lax.bitcast_convert_type(x[:, half:], jnp.uint32)
    return hi | lax.shift_right_logical(lo, jnp.uint32(16))


def _unpack_pairs(u):
    hi = lax.bitcast_convert_type(u & jnp.uint32(0xFFFF0000), F32).astype(BF16)
    lo = lax.bitcast_convert_type(lax.shift_left(u, jnp.uint32(16)), F32).astype(BF16)
    return hi, lo


def _seg_windows(cnt_ref, base):
    longest = lax.fori_loop(0, N_EXPERTS, lambda e, m: jnp.maximum(m, cnt_ref[base + e]), 0)
    return lax.shift_right_logical(longest + (SEG_WINDOW - 1), SEG_WINDOW.bit_length() - 1)


def _dispatch_kernel(seg_ref, cnt_ref, fill_lo_ref, fill_hi_ref, h_ref, keyt_ref, xs_ref,
                     stage_ref, zero_ref, sem):
    i = pl.program_id(0)
    tm = h_ref.shape[0]
    base = i * N_EXPERTS

    def zero_fills(wait):
        def act(cp):
            if wait:
                cp.wait()
            else:
                cp.start()

        def fill(c):
            return pltpu.make_async_copy(zero_ref.at[pl.ds(0, SEG_ALIGN)], _rows(xs_ref, c * SEG_ALIGN, SEG_ALIGN),
                                         sem.at[2])

        def per_expert(e, carry):
            lo = lax.shift_right_logical(fill_lo_ref[e], SEG_ALIGN.bit_length() - 1)
            hi = lax.shift_right_logical(fill_hi_ref[e], SEG_ALIGN.bit_length() - 1)
            return lax.fori_loop(lo, hi, lambda c, a: (act(fill(c)), a)[1], carry)

        lax.fori_loop(0, N_EXPERTS, per_expert, 0)

        def fill_tail(c):
            return pltpu.make_async_copy(zero_ref, _rows(xs_ref, c * ZERO_ROWS, ZERO_ROWS), sem.at[2])

        lo = lax.shift_right_logical(fill_hi_ref[N_EXPERTS - 1], ZERO_ROWS.bit_length() - 1)
        hi = xs_ref.shape[0] // ZERO_ROWS
        lax.fori_loop(lo, hi, lambda c, a: (act(fill_tail(c)), a)[1], 0)

    @pl.when(i == 0)
    def _():
        zero_ref[...] = jnp.zeros_like(zero_ref)
        zero_fills(wait=False)
        zero_fills(wait=True)

    buf = i % 2

    def send(tile_base, win, b, wait, only_live=False):
        for e in range(N_EXPERTS):
            def go(e=e):
                slot = seg_ref[tile_base + e] + win * SEG_WINDOW
                cp = pltpu.make_async_copy(stage_ref.at[b, pl.ds(e * SEG_WINDOW, SEG_WINDOW)],
                                           xs_ref.at[pl.ds(pl.multiple_of(slot, SEG_ALIGN), SEG_WINDOW)],
                                           sem.at[b])
                if wait:
                    cp.wait()
                else:
                    cp.start()

            if only_live:
                pl.when(cnt_ref[tile_base + e] > win * SEG_WINDOW)(go)
            else:
                go()

    half = N_EXPERTS // 2
    j = lax.broadcasted_iota(jnp.int32, (SEG_WINDOW, tm), 0).astype(F32)

    def build(win, b):
        key = keyt_ref[0] - jnp.asarray(win * SEG_WINDOW, F32)
        for hf in range(2):
            pick = jnp.concatenate([(key[e:e + 1, :] == j).astype(BF16)
                                    for e in range(hf * half, (hf + 1) * half)], axis=0)
            stage_ref[b, hf * half * SEG_WINDOW:(hf + 1) * half * SEG_WINDOW, :] = _pack_pairs(_dot(pick, h_ref[...]))

    n_win = _seg_windows(cnt_ref, base)
    build(0, buf)

    @pl.when(i > 0)
    def _():
        send(base - N_EXPERTS, 0, 1 - buf, wait=True)

    send(base, 0, buf, wait=False)

    def more(win, carry):
        build(win, 1 - buf)
        send(base, win, 1 - buf, wait=False, only_live=True)
        send(base, win, 1 - buf, wait=True, only_live=True)
        return carry

    lax.fori_loop(1, n_win, more, 0)

    @pl.when(i == pl.num_programs(0) - 1)
    def _():
        send(base, 0, buf, wait=True)


def _dispatch(seg, cnt, fill_lo, fill_hi, h2, keyt, n_pad):
    t, d = h2.shape
    tm = ROUTE_TILE
    return pl.pallas_call(
        _dispatch_kernel,
        grid_spec=pltpu.PrefetchScalarGridSpec(
            num_scalar_prefetch=4,
            grid=(t // tm,),
            in_specs=[
                pl.BlockSpec((tm, d), lambda i, *_: (i, 0)),
                pl.BlockSpec((1, N_EXPERTS, tm), lambda i, *_: (i, 0, 0)),
            ],
            out_specs=pl.BlockSpec(memory_space=pl.ANY),
            scratch_shapes=[pltpu.VMEM((2, N_EXPERTS * SEG_WINDOW, d // 2), jnp.uint32),
                            pltpu.VMEM((ZERO_ROWS, d // 2), jnp.uint32),
                            pltpu.SemaphoreType.DMA((3,))],
        ),
        out_shape=jax.ShapeDtypeStruct((n_pad, d // 2), jnp.uint32),
        compiler_params=_cp(("arbitrary",)),
        name="dispatch",
    )(seg, cnt, fill_lo, fill_hi, h2, keyt)


def _experts_kernel(be_ref, nv_ref, xs_ref, wgu_ref, wd_ref, bg_ref, bu_ref, bd_ref, ys_ref,
                    wg_s, wu_s, wd_s):
    j = pl.program_id(0)
    grp = 2 * LANES
    prev = be_ref[jnp.maximum(j - 1, 0)]

    @pl.when((j == 0) | (be_ref[j] != prev))
    def _():
        r = lax.broadcasted_iota(jnp.int32, (grp, grp), 0)
        c = lax.broadcasted_iota(jnp.int32, (grp, grp), 1)
        src = jnp.where(c < LANES, 2 * c, 2 * (c - LANES) + 1)
        pick = (r == src).astype(BF16)
        for g in range(wgu_ref.shape[2] // grp):
            y = _dot(wgu_ref[0, :, g * grp:(g + 1) * grp].astype(BF16), pick)
            wg_s[:, g * LANES:(g + 1) * LANES] = y[:, 0:LANES].astype(BF16)
            wu_s[:, g * LANES:(g + 1) * LANES] = y[:, LANES:grp].astype(BF16)
        wd_s[...] = wd_ref[0].astype(BF16)

    @pl.when(j < nv_ref[0])
    def _():
        x = jnp.concatenate(_unpack_pairs(xs_ref[...]), axis=1)
        g = _dot(x, wg_s[...]) + bg_ref[0]
        u = _dot(x, wu_s[...]) + bu_ref[0]
        gate = jnp.minimum(g, SWIGLU_LIMIT)
        up = jnp.clip(u, -SWIGLU_LIMIT, SWIGLU_LIMIT)
        act = (up + 1.0) * (gate * jax.nn.sigmoid(gate * SWIGLU_ALPHA))
        y = _dot(act.astype(BF16), wd_s[...]) + bd_ref[0]
        ys_ref[...] = _pack_pairs(y.astype(BF16).astype(F32))

    @pl.when(j >= nv_ref[0])
    def _():
        ys_ref[...] = jnp.zeros_like(ys_ref)


def _experts(block_e, n_valid, xs, wgu, wd, bg, bu, bd):
    n_pad = xs.shape[0]
    de, d = wd.shape[1:]
    blk = (MOE_ROWS, xs.shape[1])
    n_blocks = n_pad // MOE_ROWS
    xrow = lambda j, be, nv: (jnp.minimum(j, nv[0] - 1), 0)
    wsel = lambda j, be, nv: (be[j], 0, 0)
    return pl.pallas_call(
        _experts_kernel,
        grid_spec=pltpu.PrefetchScalarGridSpec(
            num_scalar_prefetch=2,
            grid=(n_blocks,),
            in_specs=[
                pl.BlockSpec(blk, xrow),
                pl.BlockSpec((1, d, 2 * de), wsel),
                pl.BlockSpec((1, de, d), wsel),
                pl.BlockSpec((1, 1, de), wsel),
                pl.BlockSpec((1, 1, de), wsel),
                pl.BlockSpec((1, 1, d), wsel),
            ],
            out_specs=pl.BlockSpec(blk, lambda j, be, nv: (j, 0)),
            scratch_shapes=[
                pltpu.VMEM((d, de), BF16),
                pltpu.VMEM((d, de), BF16),
                pltpu.VMEM((de, d), BF16),
            ],
        ),
        out_shape=jax.ShapeDtypeStruct(xs.shape, xs.dtype),
        compiler_params=pltpu.CompilerParams(dimension_semantics=("arbitrary",),
                                             vmem_limit_bytes=EXPERTS_VMEM_LIMIT),
        name="experts",
    )(block_e, n_valid, xs, wgu, wd, bg, bu, bd)


def _combine_kernel(seg_ref, cnt_ref, x1_ref, lpos_ref, cw_ref, nw_ref, ys_ref, o_ref, stage_ref, sem,
                    *, final_norm):
    i = pl.program_id(0)
    tm, d = x1_ref.shape
    base = i * N_EXPERTS

    buf = i % 2

    def gather(tile_base, win, b, wait):
        for e in range(N_EXPERTS):
            slot = seg_ref[tile_base + e] + win * SEG_WINDOW
            cp = pltpu.make_async_copy(ys_ref.at[pl.ds(pl.multiple_of(slot, SEG_ALIGN), SEG_WINDOW)],
                                       stage_ref.at[b, pl.ds(e * SEG_WINDOW, SEG_WINDOW)], sem.at[b])
            if wait:
                cp.wait()
            else:
                cp.start()

    @pl.when(i == 0)
    def _():
        gather(base, 0, buf, wait=False)

    @pl.when(i + 1 < pl.num_programs(0))
    def _():
        gather(base + N_EXPERTS, 0, 1 - buf, wait=False)

    n_stage = N_EXPERTS * SEG_WINDOW
    owner = lax.broadcasted_iota(jnp.int32, (LANES, n_stage), 1) // SEG_WINDOW
    expand = (owner == lax.broadcasted_iota(jnp.int32, (LANES, n_stage), 0)).astype(BF16)
    j = (lax.broadcasted_iota(jnp.int32, (tm, n_stage), 1) % SEG_WINDOW).astype(F32)
    cw_rows = jnp.concatenate([cw_ref[...].astype(BF16), jnp.zeros((LANES - N_EXPERTS, tm), BF16)], axis=0)
    cw_wide = _dot_tn(cw_rows, expand)

    def window(win, y):
        rank = (lpos_ref[...] - jnp.asarray(win * SEG_WINDOW, F32)).astype(BF16)
        take = jnp.where(_dot(rank, expand) == j, cw_wide, 0.0).astype(BF16)
        gather(base, win, buf, wait=True)
        hi, lo = _unpack_pairs(stage_ref[buf])
        return y + jnp.concatenate([_dot(take, hi), _dot(take, lo)], axis=1)

    def more(win, y):
        gather(base, win, buf, wait=False)
        return window(win, y)

    y = window(0, jnp.zeros((tm, d), F32))
    y = lax.fori_loop(1, _seg_windows(cnt_ref, base), more, y)
    out = x1_ref[...] + y
    if final_norm:
        out = _rms(out, nw_ref[...])
    o_ref[...] = out


def _combine(seg, cnt, x1, lpos, cw, nw, ys, final_norm):
    t, d = x1.shape
    tm = ROUTE_TILE
    kern = functools.partial(_combine_kernel, final_norm=final_norm)
    return pl.pallas_call(
        kern,
        grid_spec=pltpu.PrefetchScalarGridSpec(
            num_scalar_prefetch=2,
            grid=(t // tm,),
            in_specs=[
                pl.BlockSpec((tm, d), lambda i, *_: (i, 0)),
                pl.BlockSpec((tm, LANES), lambda i, *_: (i, 0)),
                pl.BlockSpec((N_EXPERTS, tm), lambda i, *_: (0, i)),
                pl.BlockSpec((1, d), lambda i, *_: (0, 0)),
                pl.BlockSpec(memory_space=pl.ANY),
            ],
            out_specs=pl.BlockSpec((tm, d), lambda i, *_: (i, 0)),
            scratch_shapes=[pltpu.VMEM((2, N_EXPERTS * SEG_WINDOW, d // 2), jnp.uint32),
                            pltpu.SemaphoreType.DMA((2,))],
        ),
        out_shape=jax.ShapeDtypeStruct((t, d), F32),
        compiler_params=_cp(("arbitrary",)),
        name="combine",
    )(seg, cnt, x1, lpos, cw, nw, ys)


def _rope_tables(seq):
    half = MLA_ROPE // 2
    inv = 1.0 / (ROPE_THETA ** (jnp.arange(0, MLA_ROPE, 2, dtype=F32) / MLA_ROPE))
    ang = jnp.arange(seq, dtype=F32)[:, None] * inv[None, :]
    cos, sin = jnp.cos(ang), jnp.sin(ang)
    zeros = jnp.zeros((seq, LANES - MLA_ROPE), F32)
    cos_row = jnp.concatenate([cos, cos, zeros], axis=-1)
    sin_row = jnp.concatenate([-sin, sin, zeros], axis=-1)
    del half
    return cos_row, sin_row, cos.T, sin.T


def _pad_cols(a, width):
    return jnp.pad(a, ((0, 0), (0, width - a.shape[1])))


def _layer(x2, batch, seq, final_norm_w, final_norm, rope_tables,
           norm_mix_w, w_in, gdn_conv_w, gdn_a_log, gdn_dt_bias, gdn_norm_w, w_gdn_o,
           mla_q_norm_w, w_mla_q_b, mla_kv_norm_w, w_mla_kv_b, w_mla_o, w_out,
           norm_ffn_w, w_router, b_router, w_gate_up, b_gate_up, w_down, b_down):
    t, d = x2.shape
    qk_w = GDN_HEADS * GDN_D
    o_b = 4 * qk_w
    o_a = o_b + GDN_HEADS
    o_cq = o_a + GDN_HEADS
    o_ckv = o_cq + MLA_Q_LORA
    o_kr = o_ckv + MLA_KV_LORA
    o_ga = o_kr + MLA_ROPE
    o_gb = o_ga + d
    w_p = jnp.concatenate([
        w_in[:, 0:o_b], w_in[:, o_ga:o_gb + d], w_in[:, o_cq:o_ckv], w_in[:, o_ckv:o_kr],
        _pad_cols(w_in[:, o_kr:o_ga], 2 * LANES)], axis=1).astype(BF16)
    w_ab = _pad_cols(jnp.concatenate([w_in[:, o_a:o_cq], w_in[:, o_b:o_a]], axis=1), LANES).astype(BF16)

    p, ab = _in_proj(x2, norm_mix_w[None, :], w_p, w_ab)

    alog_row = _pad_cols(gdn_a_log[None, :].astype(F32), LANES)
    dtb_row = _pad_cols(gdn_dt_bias[None, :].astype(F32), LANES)
    qkvn, cols, gct = _gdn_prep(p, ab, gdn_conv_w.astype(F32), alog_row, dtb_row, seq)
    o_gdn = _gdn_chunk(qkvn, p, cols, gct, gdn_norm_w[None, :].astype(F32), batch, seq)

    hd = MLA_NOPE + MLA_ROPE
    wq = w_mla_q_b.reshape(MLA_Q_LORA, MLA_HEADS, hd)
    wqt = jnp.pad(wq, ((0, 0), (0, 0), (0, 2 * LANES - hd))).reshape(MLA_Q_LORA, MLA_HEADS * 2 * LANES).T
    wkv = w_mla_kv_b.reshape(MLA_KV_LORA, MLA_HEADS, MLA_NOPE + MLA_V)
    wkn = wkv[:, :, :MLA_NOPE].reshape(MLA_KV_LORA, -1)
    wvt = wkv[:, :, MLA_NOPE:].reshape(MLA_KV_LORA, -1).T
    qt, kn, kr, vt = _mla_prep(p, rope_tables, mla_q_norm_w[None, :].astype(F32),
                               mla_kv_norm_w[None, :].astype(F32), wqt.astype(BF16), wkn.astype(BF16),
                               wvt.astype(BF16), seq)
    o_mla = _mla_attn(qt, kn, kr, vt, batch, seq)

    wr = _pad_cols(w_router.astype(F32), LANES).T
    wr_hi = wr.astype(BF16)
    wr_lo = (wr - wr_hi.astype(F32)).astype(BF16)
    br = jnp.broadcast_to(_pad_cols(b_router[None, :].astype(F32), LANES).T, (LANES, ROUTE_TILE))
    x1, h2, sel, cw = _mix_out(x2, o_gdn, o_mla, p, w_gdn_o.astype(BF16), w_mla_o.astype(BF16),
                               w_out.astype(BF16), norm_ffn_w[None, :].astype(F32), wr_hi, wr_lo, br)

    lpos, keyt, offs, cnt, tot = _route_pos(sel)
    n_tiles = t // ROUTE_TILE
    used = tot[0, :N_EXPERTS]
    padded = (used + SEG_WINDOW + MOE_ROWS - 1) // MOE_ROWS * MOE_ROWS
    pad_end = jnp.cumsum(padded)
    pad_start = pad_end - padded
    seg = (pad_start[None, :] + offs[:, 0, :N_EXPERTS]).astype(jnp.int32).reshape(-1)
    cnt = cnt[:, 0, :N_EXPERTS].reshape(-1)
    worst_used = t * TOP_K + n_tiles * N_EXPERTS * (SEG_ALIGN - 1) + N_EXPERTS * SEG_WINDOW
    n_pad = -(-worst_used // MOE_ROWS) * MOE_ROWS + (N_EXPERTS + 1) * MOE_ROWS
    n_blocks = n_pad // MOE_ROWS
    blk_start = jnp.arange(n_blocks, dtype=jnp.int32) * MOE_ROWS
    block_e = jnp.minimum(jnp.sum((pad_end[None, :] <= blk_start[:, None]).astype(jnp.int32), axis=1),
                          N_EXPERTS - 1).astype(jnp.int32)
    n_valid = (pad_end[-1:] // MOE_ROWS).astype(jnp.int32)

    xs = _dispatch(seg, cnt, (pad_start + used).astype(jnp.int32), pad_end.astype(jnp.int32), h2, keyt, n_pad)
    bg = b_gate_up[:, None, 0::2].astype(F32)
    bu = b_gate_up[:, None, 1::2].astype(F32)
    ys = _experts(block_e, n_valid, xs, w_gate_up, w_down, bg, bu, b_down[:, None, :].astype(F32))
    return _combine(seg, cnt, x1, lpos, cw, final_norm_w[None, :].astype(F32), ys, final_norm)


def kernel(x, norm_mix_w, w_in, gdn_conv_w, gdn_a_log, gdn_dt_bias, gdn_norm_w, w_gdn_o, mla_q_norm_w, w_mla_q_b, mla_kv_norm_w, w_mla_kv_b, w_mla_o, w_out, norm_ffn_w, w_router, b_router, w_gate_up, b_gate_up, w_down, b_down, norm_final_w):
    batch, seq, d = x.shape
    depth = w_in.shape[0]
    rope_tables = _rope_tables(seq)
    x2 = x.reshape(batch * seq, d)
    for layer in range(depth):
        x2 = _layer(x2, batch, seq, norm_final_w, layer == depth - 1, rope_tables,
                    norm_mix_w[layer], w_in[layer], gdn_conv_w[layer], gdn_a_log[layer],
                    gdn_dt_bias[layer], gdn_norm_w[layer], w_gdn_o[layer], mla_q_norm_w[layer],
                    w_mla_q_b[layer], mla_kv_norm_w[layer], w_mla_kv_b[layer], w_mla_o[layer],
                    w_out[layer], norm_ffn_w[layer], w_router[layer], b_router[layer],
                    w_gate_up[layer], b_gate_up[layer], w_down[layer], b_down[layer])
    return x2.reshape(batch, seq, d)
```

```python
import functools

import jax
import jax.numpy as jnp
import numpy as np
from jax import lax
from jax.experimental import pallas as pl
from jax.experimental.pallas import tpu as pltpu

F32 = jnp.float32
BF16 = jnp.bfloat16

CHUNK = 64
NORM_EPS = 1e-6
GDN_HEADS = 8
GDN_D = 128
GDN_CONV = 4
MLA_HEADS = 8
MLA_Q_LORA = 512
MLA_KV_LORA = 256
MLA_NOPE = 128
MLA_ROPE = 64
MLA_V = 128
ROPE_THETA = 10000.0
N_EXPERTS = 32
TOP_K = 4
SWIGLU_LIMIT = 7.0
SWIGLU_ALPHA = 1.702

LANES = 128
MOE_ROWS = 512
ROUTE_TILE = 256
SEG_ALIGN = 8
SEG_WINDOW = 48
ZERO_ROWS = 64
VMEM_LIMIT = 48 * 1024 * 1024
EXPERTS_VMEM_LIMIT = 56 * 1024 * 1024

NEG_BIG = -1e30
LOG2_E = 1.4426950408889634


def _cp(sem):
    return pltpu.CompilerParams(dimension_semantics=sem, vmem_limit_bytes=VMEM_LIMIT)


def _dot(a, b):
    return jnp.dot(a, b, preferred_element_type=F32)


def _dot_nt(a, b):
    return lax.dot_general(a, b, (((1,), (1,)), ((), ())), preferred_element_type=F32)


def _dot_tn(a, b):
    return lax.dot_general(a, b, (((0,), (0,)), ((), ())), preferred_element_type=F32)


def _split3(x):
    hi = x.astype(BF16)
    r = x - hi.astype(F32)
    mid = r.astype(BF16)
    lo = (r - mid.astype(F32)).astype(BF16)
    return hi, mid, lo


def _rms(x, w):
    ms = jnp.mean(x * x, axis=-1, keepdims=True)
    return x * lax.rsqrt(ms + NORM_EPS) * w


def _in_proj_kernel(x_ref, nw_ref, w_ref, wab_ref, p_ref, ab_ref, h_ref):
    @pl.when(pl.program_id(1) == 0)
    def _():
        hb = _rms(x_ref[...], nw_ref[...]).astype(BF16)
        h_ref[...] = hb
        ab_ref[...] = _dot(hb, wab_ref[...])

    p_ref[...] = _dot(h_ref[...], w_ref[...]).astype(BF16)


def _in_proj(x2, norm_w, w_p, w_ab, tm=1024, tn=3584):
    t, d = x2.shape
    n = w_p.shape[1]
    return pl.pallas_call(
        _in_proj_kernel,
        grid=(t // tm, n // tn),
        in_specs=[
            pl.BlockSpec((tm, d), lambda i, j: (i, 0)),
            pl.BlockSpec((1, d), lambda i, j: (0, 0)),
            pl.BlockSpec((d, tn), lambda i, j: (0, j)),
            pl.BlockSpec((d, LANES), lambda i, j: (0, 0)),
        ],
        out_specs=[
            pl.BlockSpec((tm, tn), lambda i, j: (i, j)),
            pl.BlockSpec((tm, LANES), lambda i, j: (i, 0)),
        ],
        out_shape=[
            jax.ShapeDtypeStruct((t, n), BF16),
            jax.ShapeDtypeStruct((t, LANES), F32),
        ],
        scratch_shapes=[pltpu.VMEM((tm, d), BF16)],
        compiler_params=_cp(("parallel", "arbitrary")),
        name="in_proj",
    )(x2, norm_w, w_p, w_ab)


def _gdn_prep_kernel(cur_ref, prev_ref, ab_ref, cw_ref, alog_ref, dtb_ref,
                     qkv_ref, cols_ref, gct_ref, *, tiles_per_seq):
    tm = cur_ref.shape[0]
    i = pl.program_id(0)
    halo_on = (i % tiles_per_seq) != 0
    q_scale = GDN_D ** -0.5
    grp = 2 * LANES
    row = lax.broadcasted_iota(jnp.int32, (tm, tm), 0)
    col = lax.broadcasted_iota(jnp.int32, (tm, tm), 1)
    shift = [(col == row - s).astype(BF16) for s in range(1, GDN_CONV)]
    for cg in range(cur_ref.shape[1] // grp):
        gs = slice(cg * grp, (cg + 1) * grp)
        cur_b = cur_ref[:, gs]
        cur = cur_b.astype(F32)
        w = cw_ref[:, gs]
        y = w[GDN_CONV - 1:GDN_CONV, :] * cur
        for s in range(1, GDN_CONV):
            y = y + w[GDN_CONV - 1 - s:GDN_CONV - s, :] * _dot(shift[s - 1], cur_b)
        halo = jnp.where(halo_on, prev_ref[:, gs].astype(F32)[8:16, :], 0.0)
        xe = jnp.concatenate([halo, cur[0:8, :]], axis=0)
        head = w[0:1, :] * xe[5:13, :]
        for j in range(1, GDN_CONV):
            head = head + w[j:j + 1, :] * xe[5 + j:13 + j, :]
        y = jnp.concatenate([head, y[8:, :]], axis=0)
        hy = 0.5 * y
        y = hy + hy * jnp.tanh(hy)
        for half in range(2):
            cb = 2 * cg + half
            yh = y[:, half * LANES:(half + 1) * LANES]
            if cb < 2 * GDN_HEADS:
                ss = jnp.sum(yh * yh, axis=-1, keepdims=True)
                yh = yh * lax.rsqrt(ss + NORM_EPS)
                if cb < GDN_HEADS:
                    yh = yh * q_scale
            qkv_ref[:, cb * LANES:(cb + 1) * LANES] = yh.astype(BF16)

    ab = ab_ref[...]
    g = -jnp.exp(alog_ref[...]) * jax.nn.softplus(ab + dtb_ref[...])
    row = lax.broadcasted_iota(jnp.int32, (tm, tm), 0)
    col = lax.broadcasted_iota(jnp.int32, (tm, tm), 1)
    tri = ((col <= row) & ((row // CHUNK) == (col // CHUNK))).astype(BF16)
    g_hi, g_mid, g_lo = _split3(g)
    gc = _dot(tri, g_hi) + _dot(tri, g_mid) + _dot(tri, g_lo)
    lane = lax.broadcasted_iota(jnp.int32, (tm, LANES), 1)
    cols_ref[...] = jnp.where(lane < GDN_HEADS, gc, jax.nn.sigmoid(ab))
    for c in range(tm // CHUNK):
        blk = gc[c * CHUNK:(c + 1) * CHUNK, :]
        blk = jnp.concatenate([blk, jnp.zeros_like(blk)], axis=0)
        gct_ref[c] = blk.T[0:GDN_HEADS, 0:CHUNK]


def _gdn_prep(p, ab, conv_w, alog_row, dtb_row, seq, tm=256):
    t = p.shape[0]
    cw = 3 * GDN_HEADS * GDN_D
    tiles_per_seq = seq // tm
    kern = functools.partial(_gdn_prep_kernel, tiles_per_seq=tiles_per_seq)
    return pl.pallas_call(
        kern,
        grid=(t // tm,),
        in_specs=[
            pl.BlockSpec((tm, cw), lambda i: (i, 0)),
            pl.BlockSpec((16, cw), lambda i: (jnp.maximum(i * (tm // 16) - 1, 0), 0)),
            pl.BlockSpec((tm, LANES), lambda i: (i, 0)),
            pl.BlockSpec((GDN_CONV, cw), lambda i: (0, 0)),
            pl.BlockSpec((1, LANES), lambda i: (0, 0)),
            pl.BlockSpec((1, LANES), lambda i: (0, 0)),
        ],
        out_specs=[
            pl.BlockSpec((tm, cw), lambda i: (i, 0)),
            pl.BlockSpec((tm, LANES), lambda i: (i, 0)),
            pl.BlockSpec((tm // CHUNK, GDN_HEADS, CHUNK), lambda i: (i, 0, 0)),
        ],
        out_shape=[
            jax.ShapeDtypeStruct((t, cw), BF16),
            jax.ShapeDtypeStruct((t, LANES), F32),
            jax.ShapeDtypeStruct((t // CHUNK, GDN_HEADS, CHUNK), F32),
        ],
        compiler_params=_cp(("parallel",)),
        name="gdn_prep",
    )(p, p, ab, conv_w, alog_row, dtb_row)


GDN_CHUNKS_PER_STEP = 2


def _gdn_chunk_kernel(q_ref, k_ref, v_ref, z_ref, cols_ref, gct_ref, nw_ref, o_ref, s_ref):
    c = CHUNK
    nb = q_ref.shape[0]
    chains = [(b, h) for b in range(nb) for h in range(GDN_HEADS)]
    units = [(g, b, h) for g in range(GDN_CHUNKS_PER_STEP) for b, h in chains]

    @pl.when(pl.program_id(0) == 0)
    def _():
        s_ref[...] = jnp.zeros_like(s_ref)

    ri = lax.broadcasted_iota(jnp.int32, (c, c), 0)
    ci = lax.broadcasted_iota(jnp.int32, (c, c), 1)
    incl = ri >= ci
    strict = ri > ci
    eye = (ri == ci).astype(F32)
    nw = nw_ref[...]

    def rows(g):
        return slice(g * c, (g + 1) * c)

    def head(h):
        return slice(h * GDN_D, (h + 1) * GDN_D)

    cols, e_g, e_kd, e_last, gct = {}, {}, {}, {}, {}
    for g in range(GDN_CHUNKS_PER_STEP):
        for b in range(nb):
            cb = cols_ref[b, rows(g), :]
            last = cb[c - 1:c, :]
            cols[g, b] = cb
            e_g[g, b] = jnp.exp(cb)
            e_kd[g, b] = jnp.exp(last - cb)
            e_last[g, b] = jnp.exp(last)
            gct[g, b] = gct_ref[b, g]

    kq, kb_l, kf_l = [], [], []
    for g, b, h in units:
        k = k_ref[b, rows(g), head(h)]
        kf = k.astype(F32)
        kb = kf * cols[g, b][:, GDN_HEADS + h:GDN_HEADS + h + 1]
        kq.append(_dot_nt(jnp.concatenate([kb.astype(BF16), q_ref[b, rows(g), head(h)]], axis=0), k))
        kb_l.append(kb)
        kf_l.append(kf)

    a_l, qk_l = [], []
    for i, (g, b, h) in enumerate(units):
        dec = jnp.exp(jnp.minimum(cols[g, b][:, h:h + 1] - gct[g, b][h:h + 1, :], 0.0))
        a_l.append(jnp.where(strict, -kq[i][0:c, :] * dec, 0.0))
        qk_l.append(jnp.where(incl, kq[i][c:2 * c, :] * dec, 0.0).astype(BF16))

    tinv = [eye + a for a in a_l]
    pw = a_l
    for _ in range(5):
        pwb = [x.astype(BF16) for x in pw]
        pw = [_dot(x, x) for x in pwb]
        tinv = [t + _dot(t.astype(BF16), x.astype(BF16)) for t, x in zip(tinv, pw)]

    u_l, lhs_l, kd_l = [], [], []
    for i, (g, b, h) in enumerate(units):
        beta = cols[g, b][:, GDN_HEADS + h:GDN_HEADS + h + 1]
        eg = e_g[g, b][:, h:h + 1]
        rhs = jnp.concatenate([v_ref[b, rows(g), head(h)].astype(F32) * beta, kb_l[i] * eg],
                              axis=1).astype(BF16)
        uw = _dot(tinv[i].astype(BF16), rhs)
        qd = (q_ref[b, rows(g), head(h)].astype(F32) * eg).astype(BF16)
        u_l.append(uw[:, 0:GDN_D])
        lhs_l.append(jnp.concatenate([uw[:, GDN_D:2 * GDN_D].astype(BF16), qd], axis=0))
        kd_l.append((kf_l[i] * e_kd[g, b][:, h:h + 1]).astype(BF16))

    for g in range(GDN_CHUNKS_PER_STEP):
        first = g * len(chains)
        r_l = [_dot(lhs_l[first + n], s_ref[n].astype(BF16)) for n in range(len(chains))]
        for n, (b, h) in enumerate(chains):
            i = first + n
            v_new = (u_l[i] - r_l[n][0:c, :]).astype(BF16)
            o = r_l[n][c:2 * c, :] + _dot(qk_l[i], v_new)
            s_ref[n] = s_ref[n] * e_last[g, b][:, h:h + 1] + _dot_tn(kd_l[i], v_new)
            z = z_ref[b, rows(g), head(h)].astype(F32)
            o_ref[b, rows(g), head(h)] = (_rms(o, nw) * (z * jax.nn.sigmoid(z))).astype(BF16)


def _gdn_chunk(qkvn, p, cols, gct, norm_w, batch, seq):
    nc = seq // CHUNK
    hw = GDN_HEADS * GDN_D
    g = GDN_CHUNKS_PER_STEP
    qkvn3 = qkvn.reshape(batch, seq, qkvn.shape[1])
    p3 = p.reshape(batch, seq, p.shape[1])
    cols3 = cols.reshape(batch, seq, LANES)
    gct4 = gct.reshape(batch, nc, GDN_HEADS, CHUNK)
    tile = lambda col: pl.BlockSpec((batch, g * CHUNK, hw), lambda c: (0, c, col))
    out = pl.pallas_call(
        _gdn_chunk_kernel,
        grid=(nc // g,),
        in_specs=[
            tile(0), tile(1), tile(2),
            tile(3),
            pl.BlockSpec((batch, g * CHUNK, LANES), lambda c: (0, c, 0)),
            pl.BlockSpec((batch, g, GDN_HEADS, CHUNK), lambda c: (0, c, 0, 0)),
            pl.BlockSpec((1, GDN_D), lambda c: (0, 0)),
        ],
        out_specs=pl.BlockSpec((batch, g * CHUNK, hw), lambda c: (0, c, 0)),
        out_shape=jax.ShapeDtypeStruct((batch, seq, hw), BF16),
        scratch_shapes=[pltpu.VMEM((batch * GDN_HEADS, GDN_D, GDN_D), F32)],
        compiler_params=_cp(("arbitrary",)),
        name="gdn_chunk",
    )(qkvn3, qkvn3, qkvn3, p3, cols3, gct4, norm_w)
    return out.reshape(batch * seq, hw)


def _rope(x, cos, sin_signed):
    lane = lax.broadcasted_iota(jnp.int32, x.shape, 1)
    fwd = pltpu.roll(x, LANES - MLA_ROPE // 2, 1)
    bwd = pltpu.roll(x, MLA_ROPE // 2, 1)
    rot = jnp.where(lane < MLA_ROPE // 2, fwd, bwd)
    return x * cos + rot * sin_signed


def _mla_prep_kernel(cq_ref, ckv_ref, kr_ref, cos_ref, sin_ref, cost_ref, sint_ref, qnw_ref, kvnw_ref,
                     wqt_ref, wkn_ref, wvt_ref, qt_ref, kn_ref, kro_ref, vt_ref):
    cos = cos_ref[...]
    sin = sin_ref[...]
    cos_t = cost_ref[...]
    sin_t = sint_ref[...]
    cq = _rms(cq_ref[...].astype(F32), qnw_ref[...]).astype(BF16)
    hd = 2 * LANES
    half = MLA_ROPE // 2
    scale = (MLA_NOPE + MLA_ROPE) ** -0.5 * LOG2_E
    for h in range(MLA_HEADS):
        qh = _dot_nt(wqt_ref[h * hd:(h + 1) * hd, :], cq) * scale
        lo = qh[MLA_NOPE:MLA_NOPE + half, :]
        hi = qh[MLA_NOPE + half:MLA_NOPE + MLA_ROPE, :]
        qt_ref[0, h * hd:h * hd + MLA_NOPE, :] = qh[0:MLA_NOPE, :].astype(BF16)
        qt_ref[0, h * hd + MLA_NOPE:h * hd + MLA_NOPE + half, :] = (lo * cos_t - hi * sin_t).astype(BF16)
        qt_ref[0, h * hd + MLA_NOPE + half:h * hd + MLA_NOPE + MLA_ROPE, :] = (hi * cos_t + lo * sin_t).astype(BF16)
        qt_ref[0, h * hd + MLA_NOPE + MLA_ROPE:(h + 1) * hd, :] = qh[MLA_NOPE + MLA_ROPE:hd, :].astype(BF16)
    kvl = _rms(ckv_ref[...].astype(F32), kvnw_ref[...]).astype(BF16)
    kn_ref[...] = _dot(kvl, wkn_ref[...]).astype(BF16)
    vt_ref[0] = _dot_nt(wvt_ref[...], kvl).astype(BF16)
    kro_ref[...] = _rope(kr_ref[...].astype(F32), cos, sin).astype(BF16)


def _mla_prep(p, tables, qnw, kvnw, wqt, wkn, wvt, seq, tm=512):
    cos_row, sin_row, cos_col, sin_col = tables
    t = p.shape[0]
    tiles_per_seq = seq // tm
    hw = MLA_HEADS * MLA_NOPE
    half = MLA_ROPE // 2
    cq_blk = 6144 // MLA_Q_LORA
    ckv_blk = 6656 // MLA_KV_LORA
    kr_blk = 6912 // LANES
    return pl.pallas_call(
        _mla_prep_kernel,
        grid=(t // tm,),
        in_specs=[
            pl.BlockSpec((tm, MLA_Q_LORA), lambda i: (i, cq_blk)),
            pl.BlockSpec((tm, MLA_KV_LORA), lambda i: (i, ckv_blk)),
            pl.BlockSpec((tm, LANES), lambda i: (i, kr_blk)),
            pl.BlockSpec((tm, LANES), lambda i: (i % tiles_per_seq, 0)),
            pl.BlockSpec((tm, LANES), lambda i: (i % tiles_per_seq, 0)),
            pl.BlockSpec((half, tm), lambda i: (0, i % tiles_per_seq)),
            pl.BlockSpec((half, tm), lambda i: (0, i % tiles_per_seq)),
            pl.BlockSpec((1, MLA_Q_LORA), lambda i: (0, 0)),
            pl.BlockSpec((1, MLA_KV_LORA), lambda i: (0, 0)),
            pl.BlockSpec((2 * hw, MLA_Q_LORA), lambda i: (0, 0)),
            pl.BlockSpec((MLA_KV_LORA, hw), lambda i: (0, 0)),
            pl.BlockSpec((hw, MLA_KV_LORA), lambda i: (0, 0)),
        ],
        out_specs=[
            pl.BlockSpec((1, 2 * hw, tm), lambda i: (i, 0, 0)),
            pl.BlockSpec((tm, hw), lambda i: (i, 0)),
            pl.BlockSpec((tm, LANES), lambda i: (i, 0)),
            pl.BlockSpec((1, hw, tm), lambda i: (i, 0, 0)),
        ],
        out_shape=[
            jax.ShapeDtypeStruct((t // tm, 2 * hw, tm), BF16),
            jax.ShapeDtypeStruct((t, hw), BF16),
            jax.ShapeDtypeStruct((t, LANES), BF16),
            jax.ShapeDtypeStruct((t // tm, hw, tm), BF16),
        ],
        compiler_params=_cp(("parallel",)),
        name="mla_prep",
    )(p, p, p, cos_row, sin_row, cos_col, sin_col, qnw, kvnw, wqt, wkn, wvt)


ATTN_HEADS_PER_STEP = 8
ATTN_SUM_ROWS = 16


def _mla_attn_kernel(qt_ref, kt_ref, q_ref, kn_ref, kr_ref, vt_ref, o_ref, m_ref, acc_ref):
    qi = qt_ref[pl.program_id(2)]
    ki = kt_ref[pl.program_id(2)]
    tq = q_ref.shape[2]
    tk = kn_ref.shape[0]
    hd = 2 * LANES
    q0 = qi * tq
    k0 = ki * tk
    last_k = (q0 + tq) // tk - 1

    @pl.when(ki == 0)
    def _():
        m_ref[...] = jnp.full_like(m_ref, NEG_BIG)
        acc_ref[...] = jnp.zeros_like(acc_ref)

    def step(masked):
        kr = kr_ref[...]
        ones = jnp.ones((ATTN_SUM_ROWS, tk), BF16)
        if masked:
            ck = lax.broadcasted_iota(jnp.int32, (tk, tq), 0) // CHUNK
            cq = lax.broadcasted_iota(jnp.int32, (tk, tq), 1) // CHUNK
            visible = (ck <= cq) if tq == tk else ((ck - cq) <= (q0 - k0) // CHUNK)

        def scores(h):
            k = jnp.concatenate([kn_ref[:, h * MLA_NOPE:(h + 1) * MLA_NOPE], kr], axis=1)
            s = _dot(k, q_ref[0, h * hd:(h + 1) * hd, :])
            if masked:
                s = jnp.where(visible, s, NEG_BIG)
            return s

        def update(h, s):
            m_prev = m_ref[h]
            m_new = jnp.maximum(m_prev, jnp.max(s, axis=0, keepdims=True))
            alpha = jnp.exp2(m_prev - m_new)
            p = jnp.exp2((s - m_new).astype(BF16))
            v_ext = jnp.concatenate([vt_ref[0, h * MLA_V:(h + 1) * MLA_V, :], ones], axis=0)
            acc_ref[h] = alpha * acc_ref[h] + _dot(v_ext, p)
            m_ref[h] = m_new

        s_prev = scores(0)
        for h in range(1, ATTN_HEADS_PER_STEP):
            s_next = scores(h)
            update(h - 1, s_prev)
            s_prev = s_next
        update(ATTN_HEADS_PER_STEP - 1, s_prev)

    @pl.when(k0 + tk <= q0)
    def _():
        step(False)

    def finish():
        for h in range(ATTN_HEADS_PER_STEP):
            acc = acc_ref[h]
            o = acc[0:MLA_V, :] / acc[MLA_V:MLA_V + 1, :]
            o_ref[:, h * MLA_V:(h + 1) * MLA_V] = o.T.astype(BF16)

    @pl.when(k0 + tk > q0)
    def _():
        step(True)
        if tq == tk:
            finish()
        else:
            pl.when(ki == last_k)(finish)


def _mla_attn(qt_all, kn, kr, vt, batch, seq):
    t = kn.shape[0]
    tq = qt_all.shape[2]
    tk = vt.shape[2]
    nq = seq // tq
    nk = seq // tk
    hps = ATTN_HEADS_PER_STEP
    pairs = [(qi, ki) for qi in range(nq) for ki in range((qi + 1) * tq // tk)]
    qt = jnp.asarray(np.array([pr[0] for pr in pairs], np.int32))
    kt = jnp.asarray(np.array([pr[1] for pr in pairs], np.int32))
    return pl.pallas_call(
        _mla_attn_kernel,
        grid_spec=pltpu.PrefetchScalarGridSpec(
            num_scalar_prefetch=2,
            grid=(batch, MLA_HEADS // hps, len(pairs)),
            in_specs=[
                pl.BlockSpec((1, hps * 2 * LANES, tq), lambda b, h, pr, qt, kt: (b * nq + qt[pr], h, 0)),
                pl.BlockSpec((tk, hps * MLA_NOPE), lambda b, h, pr, qt, kt: (b * nk + kt[pr], h)),
                pl.BlockSpec((tk, LANES), lambda b, h, pr, qt, kt: (b * nk + kt[pr], 0)),
                pl.BlockSpec((1, hps * MLA_V, tk), lambda b, h, pr, qt, kt: (b * nk + kt[pr], h, 0)),
            ],
            out_specs=pl.BlockSpec((tq, hps * MLA_V), lambda b, h, pr, qt, kt: (b * nq + qt[pr], h)),
            scratch_shapes=[
                pltpu.VMEM((hps, 1, tq), F32),
                pltpu.VMEM((hps, MLA_V + ATTN_SUM_ROWS, tq), F32),
            ],
        ),
        out_shape=jax.ShapeDtypeStruct((t, MLA_HEADS * MLA_V), BF16),
        compiler_params=_cp(("parallel", "parallel", "arbitrary")),
        name="mla_attn",
    )(qt, kt, qt_all, kn, kr, vt)


MIX_SUBTILES = 2


def _mix_out_kernel(x_ref, oa_ref, ob_ref, ga_ref, gb_ref, wga_ref, wmo_ref, wout_ref, nw_ref,
                    wr_hi_ref, wr_lo_ref, br_ref, x1_ref, h2_ref, selt_ref, cwt_ref):
    sub = br_ref.shape[1]

    def mix(r):
        ya = _dot(oa_ref[r, :], wga_ref[...])
        yb = _dot(ob_ref[r, :], wmo_ref[...])
        merged = (jax.nn.sigmoid(ga_ref[r, :].astype(F32)) * ya
                  + jax.nn.sigmoid(gb_ref[r, :].astype(F32)) * yb)
        x1 = x_ref[r, :] + _dot(merged.astype(BF16), wout_ref[...])
        x1_ref[r, :] = x1
        h2 = _rms(x1, nw_ref[...])
        h_hi = h2.astype(BF16)
        h2_ref[r, :] = h_hi
        h_lo = (h2 - h_hi.astype(F32)).astype(BF16)
        return (_dot_nt(wr_hi_ref[...], h_hi) + _dot_nt(wr_lo_ref[...], h_hi) + _dot_nt(wr_hi_ref[...], h_lo)
                + br_ref[...])[0:N_EXPERTS, :]

    def route(logits, r):
        expert = lax.broadcasted_iota(jnp.int32, logits.shape, 0)
        work = logits
        sel = jnp.zeros(logits.shape, F32)
        cw = jnp.zeros(logits.shape, F32)
        top = None
        denom = None
        for kk in range(TOP_K):
            mx = jnp.max(work, axis=0, keepdims=True)
            am = jnp.min(jnp.where(work == mx, expert, N_EXPERTS), axis=0, keepdims=True)
            hit = expert == am
            if kk == 0:
                top = mx
                e = jnp.ones_like(mx)
                denom = e
            else:
                e = jnp.exp(mx - top)
                denom = denom + e
            sel = jnp.where(hit, 1.0, sel)
            cw = jnp.where(hit, e, cw)
            work = jnp.where(hit, -jnp.inf, work)
        selt_ref[:, r] = sel.astype(BF16)
        cwt_ref[:, r] = cw / denom

    subs = [slice(n * sub, (n + 1) * sub) for n in range(x_ref.shape[0] // sub)]
    logits = [mix(r) for r in subs]
    for lg, r in zip(logits, subs):
        route(lg, r)


def _mix_out(x2, oa, ob, p, wga, wmo, wout, nw, wr_hi, wr_lo, br):
    t, d = x2.shape
    tm = MIX_SUBTILES * br.shape[1]
    full = lambda i: (0, 0)
    return pl.pallas_call(
        _mix_out_kernel,
        grid=(t // tm,),
        in_specs=[
            pl.BlockSpec((tm, d), lambda i: (i, 0)),
            pl.BlockSpec((tm, d), lambda i: (i, 0)),
            pl.BlockSpec((tm, d), lambda i: (i, 0)),
            pl.BlockSpec((tm, d), lambda i: (i, 4)),
            pl.BlockSpec((tm, d), lambda i: (i, 5)),
            pl.BlockSpec((d, d), full),
            pl.BlockSpec((d, d), full),
            pl.BlockSpec((d, d), full),
            pl.BlockSpec((1, d), full),
            pl.BlockSpec((LANES, d), full),
            pl.BlockSpec((LANES, d), full),
            pl.BlockSpec(br.shape, full),
        ],
        out_specs=[
            pl.BlockSpec((tm, d), lambda i: (i, 0)),
            pl.BlockSpec((tm, d), lambda i: (i, 0)),
            pl.BlockSpec((N_EXPERTS, tm), lambda i: (0, i)),
            pl.BlockSpec((N_EXPERTS, tm), lambda i: (0, i)),
        ],
        out_shape=[
            jax.ShapeDtypeStruct((t, d), F32),
            jax.ShapeDtypeStruct((t, d), BF16),
            jax.ShapeDtypeStruct((N_EXPERTS, t), BF16),
            jax.ShapeDtypeStruct((N_EXPERTS, t), F32),
        ],
        compiler_params=_cp(("parallel",)),
        name="mix_out",
    )(x2, oa, ob, p, p, wga, wmo, wout, nw, wr_hi, wr_lo, br)


def _route_pos_kernel(selt_ref, lpos_ref, keyt_ref, offs_ref, cnt_ref, tot_ref, carry_ref):
    tm = selt_ref.shape[1]
    i = pl.program_id(0)

    @pl.when(i == 0)
    def _():
        carry_ref[...] = jnp.zeros_like(carry_ref)

    sel_t = selt_ref[...]
    sel_rows = jnp.concatenate([sel_t, jnp.zeros((LANES - N_EXPERTS, tm), BF16)], axis=0)
    row = lax.broadcasted_iota(jnp.int32, (tm, tm), 0)
    col = lax.broadcasted_iota(jnp.int32, (tm, tm), 1)
    lpos_ref[...] = _dot_nt((col < row).astype(BF16), sel_rows)
    pos_t = _dot(sel_t, (row < col).astype(BF16))
    keyt_ref[0] = jnp.where(sel_t > 0, pos_t, -1.0)

    n = _dot_nt(jnp.ones((8, tm), BF16), sel_rows)[0:1, :]
    carry = carry_ref[0:1, :]
    offs_ref[0] = carry.astype(jnp.int32)
    cnt_ref[0] = n.astype(jnp.int32)
    total = carry + jnp.ceil(n * (1.0 / SEG_ALIGN)) * SEG_ALIGN
    carry_ref[...] = jnp.broadcast_to(total, carry_ref.shape)
    tot_ref[...] = jnp.broadcast_to(total, tot_ref.shape).astype(jnp.int32)


def _route_pos(sel_t):
    t = sel_t.shape[1]
    tm = ROUTE_TILE
    nt = t // tm
    return pl.pallas_call(
        _route_pos_kernel,
        grid=(nt,),
        in_specs=[pl.BlockSpec((N_EXPERTS, tm), lambda i: (0, i))],
        out_specs=[
            pl.BlockSpec((tm, LANES), lambda i: (i, 0)),
            pl.BlockSpec((1, N_EXPERTS, tm), lambda i: (i, 0, 0)),
            pl.BlockSpec((1, 1, LANES), lambda i: (i, 0, 0)),
            pl.BlockSpec((1, 1, LANES), lambda i: (i, 0, 0)),
            pl.BlockSpec((8, LANES), lambda i: (0, 0)),
        ],
        out_shape=[
            jax.ShapeDtypeStruct((t, LANES), F32),
            jax.ShapeDtypeStruct((nt, N_EXPERTS, tm), F32),
            jax.ShapeDtypeStruct((nt, 1, LANES), jnp.int32),
            jax.ShapeDtypeStruct((nt, 1, LANES), jnp.int32),
            jax.ShapeDtypeStruct((8, LANES), jnp.int32),
        ],
        scratch_shapes=[pltpu.VMEM((8, LANES), F32)],
        compiler_params=_cp(("arbitrary",)),
        name="route_pos",
    )(sel_t)


def _rows(ref, start, n):
    return ref.at[pl.ds(pl.multiple_of(start, n), n)]


def _pack_pairs(x):
    half = x.shape[1] // 2
    hi = lax.bitcast_convert_type(x[:, :half], jnp.uint32)
    lo = lax.bitcast_convert_type(x[:, half:], jnp.uint32)
    return hi | lax.shift_right_logical(lo, jnp.uint32(16))


def _unpack_pairs(u):
    hi = lax.bitcast_convert_type(u & jnp.uint32(0xFFFF0000), F32).astype(BF16)
    lo = lax.bitcast_convert_type(lax.shift_left(u, jnp.uint32(16)), F32).astype(BF16)
    return hi, lo


def _seg_windows(cnt_ref, base):
    longest = lax.fori_loop(0, N_EXPERTS, lambda e, m: jnp.maximum(m, cnt_ref[base + e]), 0)
    return (longest + (SEG_WINDOW - 1)) // SEG_WINDOW


def _dispatch_kernel(seg_ref, cnt_ref, fill_lo_ref, fill_hi_ref, h_ref, keyt_ref, xs_ref,
                     stage_ref, zero_ref, sem):
    i = pl.program_id(0)
    tm = h_ref.shape[0]
    base = i * N_EXPERTS

    def zero_fills(wait):
        def act(cp):
            if wait:
                cp.wait()
            else:
                cp.start()

        def fill(c):
            return pltpu.make_async_copy(zero_ref.at[pl.ds(0, SEG_ALIGN)], _rows(xs_ref, c * SEG_ALIGN, SEG_ALIGN),
                                         sem.at[2])

        def per_expert(e, carry):
            lo = lax.shift_right_logical(fill_lo_ref[e], SEG_ALIGN.bit_length() - 1)
            hi = lax.shift_right_logical(fill_hi_ref[e], SEG_ALIGN.bit_length() - 1)
            return lax.fori_loop(lo, hi, lambda c, a: (act(fill(c)), a)[1], carry)

        lax.fori_loop(0, N_EXPERTS, per_expert, 0)

        def fill_tail(c):
            return pltpu.make_async_copy(zero_ref, _rows(xs_ref, c * ZERO_ROWS, ZERO_ROWS), sem.at[2])

        lo = lax.shift_right_logical(fill_hi_ref[N_EXPERTS - 1], ZERO_ROWS.bit_length() - 1)
        hi = xs_ref.shape[0] // ZERO_ROWS
        lax.fori_loop(lo, hi, lambda c, a: (act(fill_tail(c)), a)[1], 0)

    @pl.when(i == 0)
    def _():
        zero_ref[...] = jnp.zeros_like(zero_ref)
        zero_fills(wait=False)
        zero_fills(wait=True)

    buf = i % 2

    def send(tile_base, win, b, wait, only_live=False):
        for e in range(N_EXPERTS):
            def go(e=e):
                slot = seg_ref[tile_base + e] + win * SEG_WINDOW
                cp = pltpu.make_async_copy(stage_ref.at[b, pl.ds(e * SEG_WINDOW, SEG_WINDOW)],
                                           xs_ref.at[pl.ds(pl.multiple_of(slot, SEG_ALIGN), SEG_WINDOW)],
                                           sem.at[b])
                if wait:
                    cp.wait()
                else:
                    cp.start()

            if only_live:
                pl.when(cnt_ref[tile_base + e] > win * SEG_WINDOW)(go)
            else:
                go()

    half = N_EXPERTS // 2
    j = lax.broadcasted_iota(jnp.int32, (SEG_WINDOW, tm), 0).astype(F32)

    def build(win, b):
        key = keyt_ref[0] - jnp.asarray(win * SEG_WINDOW, F32)
        for hf in range(2):
            pick = jnp.concatenate([(key[e:e + 1, :] == j).astype(BF16)
                                    for e in range(hf * half, (hf + 1) * half)], axis=0)
            stage_ref[b, hf * half * SEG_WINDOW:(hf + 1) * half * SEG_WINDOW, :] = _pack_pairs(_dot(pick, h_ref[...]))

    n_win = _seg_windows(cnt_ref, base)
    build(0, buf)

    @pl.when(i > 0)
    def _():
        send(base - N_EXPERTS, 0, 1 - buf, wait=True)

    send(base, 0, buf, wait=False)

    def more(win, carry):
        build(win, 1 - buf)
        send(base, win, 1 - buf, wait=False, only_live=True)
        send(base, win, 1 - buf, wait=True, only_live=True)
        return carry

    lax.fori_loop(1, n_win, more, 0)

    @pl.when(i == pl.num_programs(0) - 1)
    def _():
        send(base, 0, buf, wait=True)


def _dispatch(seg, cnt, fill_lo, fill_hi, h2, keyt, n_pad):
    t, d = h2.shape
    tm = ROUTE_TILE
    return pl.pallas_call(
        _dispatch_kernel,
        grid_spec=pltpu.PrefetchScalarGridSpec(
            num_scalar_prefetch=4,
            grid=(t // tm,),
            in_specs=[
                pl.BlockSpec((tm, d), lambda i, *_: (i, 0)),
                pl.BlockSpec((1, N_EXPERTS, tm), lambda i, *_: (i, 0, 0)),
            ],
            out_specs=pl.BlockSpec(memory_space=pl.ANY),
            scratch_shapes=[pltpu.VMEM((2, N_EXPERTS * SEG_WINDOW, d // 2), jnp.uint32),
                            pltpu.VMEM((ZERO_ROWS, d // 2), jnp.uint32),
                            pltpu.SemaphoreType.DMA((3,))],
        ),
        out_shape=jax.ShapeDtypeStruct((n_pad, d // 2), jnp.uint32),
        compiler_params=_cp(("arbitrary",)),
        name="dispatch",
    )(seg, cnt, fill_lo, fill_hi, h2, keyt)


def _experts_kernel(be_ref, nv_ref, xs_ref, wgu_ref, wd_ref, bg_ref, bu_ref, bd_ref, ys_ref,
                    wg_s, wu_s, wd_s):
    j = pl.program_id(0)
    grp = 2 * LANES
    prev = be_ref[jnp.maximum(j - 1, 0)]

    @pl.when((j == 0) | (be_ref[j] != prev))
    def _():
        r = lax.broadcasted_iota(jnp.int32, (grp, grp), 0)
        c = lax.broadcasted_iota(jnp.int32, (grp, grp), 1)
        src = jnp.where(c < LANES, 2 * c, 2 * (c - LANES) + 1)
        pick = (r == src).astype(BF16)
        for g in range(wgu_ref.shape[2] // grp):
            y = _dot(wgu_ref[0, :, g * grp:(g + 1) * grp].astype(BF16), pick)
            wg_s[:, g * LANES:(g + 1) * LANES] = y[:, 0:LANES].astype(BF16)
            wu_s[:, g * LANES:(g + 1) * LANES] = y[:, LANES:grp].astype(BF16)
        wd_s[...] = wd_ref[0].astype(BF16)

    @pl.when(j < nv_ref[0])
    def _():
        x = jnp.concatenate(_unpack_pairs(xs_ref[...]), axis=1)
        g = _dot(x, wg_s[...]) + bg_ref[0]
        u = _dot(x, wu_s[...]) + bu_ref[0]
        gate = jnp.minimum(g, SWIGLU_LIMIT)
        up = jnp.clip(u, -SWIGLU_LIMIT, SWIGLU_LIMIT)
        act = (up + 1.0) * (gate * jax.nn.sigmoid(gate * SWIGLU_ALPHA))
        y = _dot(act.astype(BF16), wd_s[...]) + bd_ref[0]
        ys_ref[...] = _pack_pairs(y.astype(BF16).astype(F32))

    @pl.when(j >= nv_ref[0])
    def _():
        ys_ref[...] = jnp.zeros_like(ys_ref)


def _experts(block_e, n_valid, xs, wgu, wd, bg, bu, bd):
    n_pad = xs.shape[0]
    de, d = wd.shape[1:]
    blk = (MOE_ROWS, xs.shape[1])
    n_blocks = n_pad // MOE_ROWS
    xrow = lambda j, be, nv: (jnp.minimum(j, nv[0] - 1), 0)
    wsel = lambda j, be, nv: (be[j], 0, 0)
    return pl.pallas_call(
        _experts_kernel,
        grid_spec=pltpu.PrefetchScalarGridSpec(
            num_scalar_prefetch=2,
            grid=(n_blocks,),
            in_specs=[
                pl.BlockSpec(blk, xrow),
                pl.BlockSpec((1, d, 2 * de), wsel),
                pl.BlockSpec((1, de, d), wsel),
                pl.BlockSpec((1, 1, de), wsel),
                pl.BlockSpec((1, 1, de), wsel),
                pl.BlockSpec((1, 1, d), wsel),
            ],
            out_specs=pl.BlockSpec(blk, lambda j, be, nv: (j, 0)),
            scratch_shapes=[
                pltpu.VMEM((d, de), BF16),
                pltpu.VMEM((d, de), BF16),
                pltpu.VMEM((de, d), BF16),
            ],
        ),
        out_shape=jax.ShapeDtypeStruct(xs.shape, xs.dtype),
        compiler_params=pltpu.CompilerParams(dimension_semantics=("arbitrary",),
                                             vmem_limit_bytes=EXPERTS_VMEM_LIMIT),
        name="experts",
    )(block_e, n_valid, xs, wgu, wd, bg, bu, bd)


def _combine_kernel(seg_ref, cnt_ref, x1_ref, lpos_ref, cw_ref, nw_ref, ys_ref, o_ref, stage_ref, sem,
                    *, final_norm):
    i = pl.program_id(0)
    tm, d = x1_ref.shape
    base = i * N_EXPERTS

    buf = i % 2

    def gather(tile_base, win, b, wait):
        for e in range(N_EXPERTS):
            slot = seg_ref[tile_base + e] + win * SEG_WINDOW
            cp = pltpu.make_async_copy(ys_ref.at[pl.ds(pl.multiple_of(slot, SEG_ALIGN), SEG_WINDOW)],
                                       stage_ref.at[b, pl.ds(e * SEG_WINDOW, SEG_WINDOW)], sem.at[b])
            if wait:
                cp.wait()
            else:
                cp.start()

    @pl.when(i == 0)
    def _():
        gather(base, 0, buf, wait=False)

    @pl.when(i + 1 < pl.num_programs(0))
    def _():
        gather(base + N_EXPERTS, 0, 1 - buf, wait=False)

    n_stage = N_EXPERTS * SEG_WINDOW
    owner = lax.broadcasted_iota(jnp.int32, (LANES, n_stage), 1) // SEG_WINDOW
    expand = (owner == lax.broadcasted_iota(jnp.int32, (LANES, n_stage), 0)).astype(BF16)
    j = (lax.broadcasted_iota(jnp.int32, (tm, n_stage), 1) % SEG_WINDOW).astype(F32)
    cw_rows = jnp.concatenate([cw_ref[...].astype(BF16), jnp.zeros((LANES - N_EXPERTS, tm), BF16)], axis=0)
    cw_wide = _dot_tn(cw_rows, expand)

    def window(win, y):
        rank = (lpos_ref[...] - jnp.asarray(win * SEG_WINDOW, F32)).astype(BF16)
        take = jnp.where(_dot(rank, expand) == j, cw_wide, 0.0).astype(BF16)
        gather(base, win, buf, wait=True)
        hi, lo = _unpack_pairs(stage_ref[buf])
        return y + jnp.concatenate([_dot(take, hi), _dot(take, lo)], axis=1)

    def more(win, y):
        gather(base, win, buf, wait=False)
        return window(win, y)

    y = window(0, jnp.zeros((tm, d), F32))
    y = lax.fori_loop(1, _seg_windows(cnt_ref, base), more, y)
    out = x1_ref[...] + y
    if final_norm:
        out = _rms(out, nw_ref[...])
    o_ref[...] = out


def _combine(seg, cnt, x1, lpos, cw, nw, ys, final_norm):
    t, d = x1.shape
    tm = ROUTE_TILE
    kern = functools.partial(_combine_kernel, final_norm=final_norm)
    return pl.pallas_call(
        kern,
        grid_spec=pltpu.PrefetchScalarGridSpec(
            num_scalar_prefetch=2,
            grid=(t // tm,),
            in_specs=[
                pl.BlockSpec((tm, d), lambda i, *_: (i, 0)),
                pl.BlockSpec((tm, LANES), lambda i, *_: (i, 0)),
                pl.BlockSpec((N_EXPERTS, tm), lambda i, *_: (0, i)),
                pl.BlockSpec((1, d), lambda i, *_: (0, 0)),
                pl.BlockSpec(memory_space=pl.ANY),
            ],
            out_specs=pl.BlockSpec((tm, d), lambda i, *_: (i, 0)),
            scratch_shapes=[pltpu.VMEM((2, N_EXPERTS * SEG_WINDOW, d // 2), jnp.uint32),
                            pltpu.SemaphoreType.DMA((2,))],
        ),
        out_shape=jax.ShapeDtypeStruct((t, d), F32),
        compiler_params=_cp(("arbitrary",)),
        name="combine",
    )(seg, cnt, x1, lpos, cw, nw, ys)


def _rope_tables(seq):
    half = MLA_ROPE // 2
    inv = 1.0 / (ROPE_THETA ** (jnp.arange(0, MLA_ROPE, 2, dtype=F32) / MLA_ROPE))
    ang = jnp.arange(seq, dtype=F32)[:, None] * inv[None, :]
    cos, sin = jnp.cos(ang), jnp.sin(ang)
    zeros = jnp.zeros((seq, LANES - MLA_ROPE), F32)
    cos_row = jnp.concatenate([cos, cos, zeros], axis=-1)
    sin_row = jnp.concatenate([-sin, sin, zeros], axis=-1)
    del half
    return cos_row, sin_row, cos.T, sin.T


def _pad_cols(a, width):
    return jnp.pad(a, ((0, 0), (0, width - a.shape[1])))


def _layer(x2, batch, seq, final_norm_w, final_norm, rope_tables,
           norm_mix_w, w_in, gdn_conv_w, gdn_a_log, gdn_dt_bias, gdn_norm_w, w_gdn_o,
           mla_q_norm_w, w_mla_q_b, mla_kv_norm_w, w_mla_kv_b, w_mla_o, w_out,
           norm_ffn_w, w_router, b_router, w_gate_up, b_gate_up, w_down, b_down):
    t, d = x2.shape
    qk_w = GDN_HEADS * GDN_D
    o_b = 4 * qk_w
    o_a = o_b + GDN_HEADS
    o_cq = o_a + GDN_HEADS
    o_ckv = o_cq + MLA_Q_LORA
    o_kr = o_ckv + MLA_KV_LORA
    o_ga = o_kr + MLA_ROPE
    o_gb = o_ga + d
    w_p = jnp.concatenate([
        w_in[:, 0:o_b], w_in[:, o_ga:o_gb + d], w_in[:, o_cq:o_ckv], w_in[:, o_ckv:o_kr],
        _pad_cols(w_in[:, o_kr:o_ga], 2 * LANES)], axis=1).astype(BF16)
    w_ab = _pad_cols(jnp.concatenate([w_in[:, o_a:o_cq], w_in[:, o_b:o_a]], axis=1), LANES).astype(BF16)

    p, ab = _in_proj(x2, norm_mix_w[None, :], w_p, w_ab)

    alog_row = _pad_cols(gdn_a_log[None, :].astype(F32), LANES)
    dtb_row = _pad_cols(gdn_dt_bias[None, :].astype(F32), LANES)
    qkvn, cols, gct = _gdn_prep(p, ab, gdn_conv_w.astype(F32), alog_row, dtb_row, seq)
    o_gdn = _gdn_chunk(qkvn, p, cols, gct, gdn_norm_w[None, :].astype(F32), batch, seq)

    hd = MLA_NOPE + MLA_ROPE
    wq = w_mla_q_b.reshape(MLA_Q_LORA, MLA_HEADS, hd)
    wqt = jnp.pad(wq, ((0, 0), (0, 0), (0, 2 * LANES - hd))).reshape(MLA_Q_LORA, MLA_HEADS * 2 * LANES).T
    wkv = w_mla_kv_b.reshape(MLA_KV_LORA, MLA_HEADS, MLA_NOPE + MLA_V)
    wkn = wkv[:, :, :MLA_NOPE].reshape(MLA_KV_LORA, -1)
    wvt = wkv[:, :, MLA_NOPE:].reshape(MLA_KV_LORA, -1).T
    qt, kn, kr, vt = _mla_prep(p, rope_tables, mla_q_norm_w[None, :].astype(F32),
                               mla_kv_norm_w[None, :].astype(F32), wqt.astype(BF16), wkn.astype(BF16),
                               wvt.astype(BF16), seq)
    o_mla = _mla_attn(qt, kn, kr, vt, batch, seq)

    wr = _pad_cols(w_router.astype(F32), LANES).T
    wr_hi = wr.astype(BF16)
    wr_lo = (wr - wr_hi.astype(F32)).astype(BF16)
    br = jnp.broadcast_to(_pad_cols(b_router[None, :].astype(F32), LANES).T, (LANES, ROUTE_TILE))
    x1, h2, sel, cw = _mix_out(x2, o_gdn, o_mla, p, w_gdn_o.astype(BF16), w_mla_o.astype(BF16),
                               w_out.astype(BF16), norm_ffn_w[None, :].astype(F32), wr_hi, wr_lo, br)

    lpos, keyt, offs, cnt, tot = _route_pos(sel)
    n_tiles = t // ROUTE_TILE
    used = tot[0, :N_EXPERTS]
    padded = (used + SEG_WINDOW + MOE_ROWS - 1) // MOE_ROWS * MOE_ROWS
    pad_end = jnp.cumsum(padded)
    pad_start = pad_end - padded
    seg = (pad_start[None, :] + offs[:, 0, :N_EXPERTS]).astype(jnp.int32).reshape(-1)
    cnt = cnt[:, 0, :N_EXPERTS].reshape(-1)
    worst_used = t * TOP_K + n_tiles * N_EXPERTS * (SEG_ALIGN - 1) + N_EXPERTS * SEG_WINDOW
    n_pad = -(-worst_used // MOE_ROWS) * MOE_ROWS + (N_EXPERTS + 1) * MOE_ROWS
    n_blocks = n_pad // MOE_ROWS
    blk_start = jnp.arange(n_blocks, dtype=jnp.int32) * MOE_ROWS
    block_e = jnp.minimum(jnp.sum((pad_end[None, :] <= blk_start[:, None]).astype(jnp.int32), axis=1),
                          N_EXPERTS - 1).astype(jnp.int32)
    n_valid = (pad_end[-1:] // MOE_ROWS).astype(jnp.int32)

    xs = _dispatch(seg, cnt, (pad_start + used).astype(jnp.int32), pad_end.astype(jnp.int32), h2, keyt, n_pad)
    bg = b_gate_up[:, None, 0::2].astype(F32)
    bu = b_gate_up[:, None, 1::2].astype(F32)
    ys = _experts(block_e, n_valid, xs, w_gate_up, w_down, bg, bu, b_down[:, None, :].astype(F32))
    return _combine(seg, cnt, x1, lpos, cw, final_norm_w[None, :].astype(F32), ys, final_norm)


def kernel(x, norm_mix_w, w_in, gdn_conv_w, gdn_a_log, gdn_dt_bias, gdn_norm_w, w_gdn_o, mla_q_norm_w, w_mla_q_b, mla_kv_norm_w, w_mla_kv_b, w_mla_o, w_out, norm_ffn_w, w_router, b_router, w_gate_up, b_gate_up, w_down, b_down, norm_final_w):
    batch, seq, d = x.shape
    depth = w_in.shape[0]
    rope_tables = _rope_tables(seq)
    x2 = x.reshape(batch * seq, d)
    for layer in range(depth):
        x2 = _layer(x2, batch, seq, norm_final_w, layer == depth - 1, rope_tables,
                    norm_mix_w[layer], w_in[layer], gdn_conv_w[layer], gdn_a_log[layer],
                    gdn_dt_bias[layer], gdn_norm_w[layer], w_gdn_o[layer], mla_q_norm_w[layer],
                    w_mla_q_b[layer], mla_kv_norm_w[layer], w_mla_kv_b[layer], w_mla_o[layer],
                    w_out[layer], norm_ffn_w[layer], w_router[layer], b_router[layer],
                    w_gate_up[layer], b_gate_up[layer], w_down[layer], b_down[layer])
    return x2.reshape(batch, seq, d)
```

```python
import functools

import jax
import jax.numpy as jnp
import numpy as np
from jax import lax
from jax.experimental import pallas as pl
from jax.experimental.pallas import tpu as pltpu

F32 = jnp.float32
BF16 = jnp.bfloat16

CHUNK = 64
NORM_EPS = 1e-6
GDN_HEADS = 8
GDN_D = 128
GDN_CONV = 4
MLA_HEADS = 8
MLA_Q_LORA = 512
MLA_KV_LORA = 256
MLA_NOPE = 128
MLA_ROPE = 64
MLA_V = 128
ROPE_THETA = 10000.0
N_EXPERTS = 32
TOP_K = 4
SWIGLU_LIMIT = 7.0
SWIGLU_ALPHA = 1.702

LANES = 128
MOE_ROWS = 512
ROUTE_TILE = 256
SEG_ALIGN = 8
SEG_WINDOW = 64
ZERO_ROWS = 64
VMEM_LIMIT = 48 * 1024 * 1024
EXPERTS_VMEM_LIMIT = 56 * 1024 * 1024

P_GROUP = GDN_HEADS * GDN_D
P_Q, P_K, P_V, P_Z, P_GATE_A, P_GATE_B = (n * P_GROUP for n in range(6))
P_CQ = 6 * P_GROUP
P_CKV = P_CQ + MLA_Q_LORA
P_KR = P_CKV + MLA_KV_LORA
P_WIDTH = 7 * P_GROUP

NEG_BIG = -1e30
LOG2_E = 1.4426950408889634


def _cp(sem):
    return pltpu.CompilerParams(dimension_semantics=sem, vmem_limit_bytes=VMEM_LIMIT)


def _dot(a, b):
    return jnp.dot(a, b, preferred_element_type=F32)


def _dot_nt(a, b):
    return lax.dot_general(a, b, (((1,), (1,)), ((), ())), preferred_element_type=F32)


def _dot_tn(a, b):
    return lax.dot_general(a, b, (((0,), (0,)), ((), ())), preferred_element_type=F32)


def _split3(x):
    hi = x.astype(BF16)
    r = x - hi.astype(F32)
    mid = r.astype(BF16)
    lo = (r - mid.astype(F32)).astype(BF16)
    return hi, mid, lo


def _rms(x, w):
    ms = jnp.mean(x * x, axis=-1, keepdims=True)
    return x * lax.rsqrt(ms + NORM_EPS) * w


def _in_proj_kernel(x_ref, nw_ref, w_ref, wab_ref, p_ref, ab_ref, h_ref):
    @pl.when(pl.program_id(1) == 0)
    def _():
        hb = _rms(x_ref[...], nw_ref[...]).astype(BF16)
        h_ref[...] = hb
        ab_ref[...] = _dot(hb, wab_ref[...])

    p_ref[...] = _dot(h_ref[...], w_ref[...]).astype(BF16)


def _in_proj(x2, norm_w, w_p, w_ab, tm=1024, tn=3584):
    t, d = x2.shape
    n = w_p.shape[1]
    return pl.pallas_call(
        _in_proj_kernel,
        grid=(t // tm, n // tn),
        in_specs=[
            pl.BlockSpec((tm, d), lambda i, j: (i, 0)),
            pl.BlockSpec((1, d), lambda i, j: (0, 0)),
            pl.BlockSpec((d, tn), lambda i, j: (0, j)),
            pl.BlockSpec((d, LANES), lambda i, j: (0, 0)),
        ],
        out_specs=[
            pl.BlockSpec((tm, tn), lambda i, j: (i, j)),
            pl.BlockSpec((tm, LANES), lambda i, j: (i, 0)),
        ],
        out_shape=[
            jax.ShapeDtypeStruct((t, n), BF16),
            jax.ShapeDtypeStruct((t, LANES), F32),
        ],
        scratch_shapes=[pltpu.VMEM((tm, d), BF16)],
        compiler_params=_cp(("parallel", "arbitrary")),
        name="in_proj",
    )(x2, norm_w, w_p, w_ab)


def _gdn_prep_kernel(cur_ref, prev_ref, ab_ref, cw_ref, alog_ref, dtb_ref,
                     qkv_ref, cols_ref, gct_ref, *, tiles_per_seq):
    tm = cur_ref.shape[0]
    i = pl.program_id(0)
    halo_on = (i % tiles_per_seq) != 0
    q_scale = GDN_D ** -0.5
    grp = 2 * LANES
    row = lax.broadcasted_iota(jnp.int32, (tm, tm), 0)
    col = lax.broadcasted_iota(jnp.int32, (tm, tm), 1)
    shift = [(col == row - s).astype(BF16) for s in range(1, GDN_CONV)]
    for cg in range(cur_ref.shape[1] // grp):
        gs = slice(cg * grp, (cg + 1) * grp)
        cur_b = cur_ref[:, gs]
        cur = cur_b.astype(F32)
        w = cw_ref[:, gs]
        y = w[GDN_CONV - 1:GDN_CONV, :] * cur
        for s in range(1, GDN_CONV):
            y = y + w[GDN_CONV - 1 - s:GDN_CONV - s, :] * _dot(shift[s - 1], cur_b)
        halo = jnp.where(halo_on, prev_ref[:, gs].astype(F32)[8:16, :], 0.0)
        xe = jnp.concatenate([halo, cur[0:8, :]], axis=0)
        head = w[0:1, :] * xe[5:13, :]
        for j in range(1, GDN_CONV):
            head = head + w[j:j + 1, :] * xe[5 + j:13 + j, :]
        y = jnp.concatenate([head, y[8:, :]], axis=0)
        hy = 0.5 * y
        y = hy + hy * jnp.tanh(hy)
        for half in range(2):
            cb = 2 * cg + half
            yh = y[:, half * LANES:(half + 1) * LANES]
            if cb < 2 * GDN_HEADS:
                ss = jnp.sum(yh * yh, axis=-1, keepdims=True)
                yh = yh * lax.rsqrt(ss + NORM_EPS)
                if cb < GDN_HEADS:
                    yh = yh * q_scale
            qkv_ref[:, cb * LANES:(cb + 1) * LANES] = yh.astype(BF16)

    ab = ab_ref[...]
    g = -jnp.exp(alog_ref[...]) * jax.nn.softplus(ab + dtb_ref[...])
    row = lax.broadcasted_iota(jnp.int32, (tm, tm), 0)
    col = lax.broadcasted_iota(jnp.int32, (tm, tm), 1)
    tri = ((col <= row) & ((row // CHUNK) == (col // CHUNK))).astype(BF16)
    g_hi, g_mid, g_lo = _split3(g)
    gc = _dot(tri, g_hi) + _dot(tri, g_mid) + _dot(tri, g_lo)
    lane = lax.broadcasted_iota(jnp.int32, (tm, LANES), 1)
    cols_ref[...] = jnp.where(lane < GDN_HEADS, gc, jax.nn.sigmoid(ab))
    for c in range(tm // CHUNK):
        blk = gc[c * CHUNK:(c + 1) * CHUNK, :]
        blk = jnp.concatenate([blk, jnp.zeros_like(blk)], axis=0)
        gct_ref[c] = blk.T[0:GDN_HEADS, 0:CHUNK]


def _gdn_prep(p, ab, conv_w, alog_row, dtb_row, seq, tm=256):
    t = p.shape[0]
    cw = 3 * GDN_HEADS * GDN_D
    tiles_per_seq = seq // tm
    kern = functools.partial(_gdn_prep_kernel, tiles_per_seq=tiles_per_seq)
    return pl.pallas_call(
        kern,
        grid=(t // tm,),
        in_specs=[
            pl.BlockSpec((tm, cw), lambda i: (i, 0)),
            pl.BlockSpec((16, cw), lambda i: (jnp.maximum(i * (tm // 16) - 1, 0), 0)),
            pl.BlockSpec((tm, LANES), lambda i: (i, 0)),
            pl.BlockSpec((GDN_CONV, cw), lambda i: (0, 0)),
            pl.BlockSpec((1, LANES), lambda i: (0, 0)),
            pl.BlockSpec((1, LANES), lambda i: (0, 0)),
        ],
        out_specs=[
            pl.BlockSpec((tm, cw), lambda i: (i, 0)),
            pl.BlockSpec((tm, LANES), lambda i: (i, 0)),
            pl.BlockSpec((tm // CHUNK, GDN_HEADS, CHUNK), lambda i: (i, 0, 0)),
        ],
        out_shape=[
            jax.ShapeDtypeStruct((t, cw), BF16),
            jax.ShapeDtypeStruct((t, LANES), F32),
            jax.ShapeDtypeStruct((t // CHUNK, GDN_HEADS, CHUNK), F32),
        ],
        compiler_params=_cp(("parallel",)),
        name="gdn_prep",
    )(p, p, ab, conv_w, alog_row, dtb_row)


GDN_CHUNKS_PER_STEP = 2


def _gdn_chunk_kernel(q_ref, k_ref, v_ref, z_ref, cols_ref, gct_ref, nw_ref, o_ref, s_ref):
    c = CHUNK
    nb = q_ref.shape[0]
    chains = [(b, h) for b in range(nb) for h in range(GDN_HEADS)]
    units = [(g, b, h) for g in range(GDN_CHUNKS_PER_STEP) for b, h in chains]

    @pl.when(pl.program_id(0) == 0)
    def _():
        s_ref[...] = jnp.zeros_like(s_ref)

    ri = lax.broadcasted_iota(jnp.int32, (c, c), 0)
    ci = lax.broadcasted_iota(jnp.int32, (c, c), 1)
    incl = ri >= ci
    strict = ri > ci
    eye = (ri == ci).astype(F32)
    nw = nw_ref[...]

    def rows(g):
        return slice(g * c, (g + 1) * c)

    def head(h):
        return slice(h * GDN_D, (h + 1) * GDN_D)

    cols, e_g, e_kd, e_last, gct = {}, {}, {}, {}, {}
    for g in range(GDN_CHUNKS_PER_STEP):
        for b in range(nb):
            cb = cols_ref[b, rows(g), :]
            last = cb[c - 1:c, :]
            cols[g, b] = cb
            e_g[g, b] = jnp.exp(cb)
            e_kd[g, b] = jnp.exp(last - cb)
            e_last[g, b] = jnp.exp(last)
            gct[g, b] = gct_ref[b, g]

    kq, kb_l, kf_l = [], [], []
    for g, b, h in units:
        k = k_ref[b, rows(g), head(h)]
        kf = k.astype(F32)
        kb = kf * cols[g, b][:, GDN_HEADS + h:GDN_HEADS + h + 1]
        kq.append(_dot_nt(jnp.concatenate([kb.astype(BF16), q_ref[b, rows(g), head(h)]], axis=0), k))
        kb_l.append(kb)
        kf_l.append(kf)

    a_l, qk_l = [], []
    for i, (g, b, h) in enumerate(units):
        dec = jnp.exp(jnp.minimum(cols[g, b][:, h:h + 1] - gct[g, b][h:h + 1, :], 0.0))
        a_l.append(jnp.where(strict, -kq[i][0:c, :] * dec, 0.0))
        qk_l.append(jnp.where(incl, kq[i][c:2 * c, :] * dec, 0.0).astype(BF16))

    tinv = [eye + a for a in a_l]
    pw = a_l
    for _ in range(5):
        pwb = [x.astype(BF16) for x in pw]
        pw = [_dot(x, x) for x in pwb]
        tinv = [t + _dot(t.astype(BF16), x.astype(BF16)) for t, x in zip(tinv, pw)]

    u_l, lhs_l, kd_l = [], [], []
    for i, (g, b, h) in enumerate(units):
        beta = cols[g, b][:, GDN_HEADS + h:GDN_HEADS + h + 1]
        eg = e_g[g, b][:, h:h + 1]
        rhs = jnp.concatenate([v_ref[b, rows(g), head(h)].astype(F32) * beta, kb_l[i] * eg],
                              axis=1).astype(BF16)
        uw = _dot(tinv[i].astype(BF16), rhs)
        qd = (q_ref[b, rows(g), head(h)].astype(F32) * eg).astype(BF16)
        u_l.append(uw[:, 0:GDN_D])
        lhs_l.append(jnp.concatenate([uw[:, GDN_D:2 * GDN_D].astype(BF16), qd], axis=0))
        kd_l.append((kf_l[i] * e_kd[g, b][:, h:h + 1]).astype(BF16))

    for g in range(GDN_CHUNKS_PER_STEP):
        first = g * len(chains)
        r_l = [_dot(lhs_l[first + n], s_ref[n].astype(BF16)) for n in range(len(chains))]
        for n, (b, h) in enumerate(chains):
            i = first + n
            v_new = (u_l[i] - r_l[n][0:c, :]).astype(BF16)
            o = r_l[n][c:2 * c, :] + _dot(qk_l[i], v_new)
            s_ref[n] = s_ref[n] * e_last[g, b][:, h:h + 1] + _dot_tn(kd_l[i], v_new)
            z = z_ref[b, rows(g), head(h)].astype(F32)
            o_ref[b, rows(g), head(h)] = (_rms(o, nw) * (z * jax.nn.sigmoid(z))).astype(BF16)


def _gdn_chunk(qkvn, p, cols, gct, norm_w, batch, seq):
    nc = seq // CHUNK
    hw = GDN_HEADS * GDN_D
    g = GDN_CHUNKS_PER_STEP
    qkvn3 = qkvn.reshape(batch, seq, qkvn.shape[1])
    p3 = p.reshape(batch, seq, p.shape[1])
    cols3 = cols.reshape(batch, seq, LANES)
    gct4 = gct.reshape(batch, nc, GDN_HEADS, CHUNK)
    tile = lambda col: pl.BlockSpec((batch, g * CHUNK, hw), lambda c: (0, c, col))
    out = pl.pallas_call(
        _gdn_chunk_kernel,
        grid=(nc // g,),
        in_specs=[
            tile(0), tile(1), tile(2),
            tile(P_Z // P_GROUP),
            pl.BlockSpec((batch, g * CHUNK, LANES), lambda c: (0, c, 0)),
            pl.BlockSpec((batch, g, GDN_HEADS, CHUNK), lambda c: (0, c, 0, 0)),
            pl.BlockSpec((1, GDN_D), lambda c: (0, 0)),
        ],
        out_specs=pl.BlockSpec((batch, g * CHUNK, hw), lambda c: (0, c, 0)),
        out_shape=jax.ShapeDtypeStruct((batch, seq, hw), BF16),
        scratch_shapes=[pltpu.VMEM((batch * GDN_HEADS, GDN_D, GDN_D), F32)],
        compiler_params=_cp(("arbitrary",)),
        name="gdn_chunk",
    )(qkvn3, qkvn3, qkvn3, p3, cols3, gct4, norm_w)
    return out.reshape(batch * seq, hw)


def _rope(x, cos, sin_signed):
    lane = lax.broadcasted_iota(jnp.int32, x.shape, 1)
    fwd = pltpu.roll(x, LANES - MLA_ROPE // 2, 1)
    bwd = pltpu.roll(x, MLA_ROPE // 2, 1)
    rot = jnp.where(lane < MLA_ROPE // 2, fwd, bwd)
    return x * cos + rot * sin_signed


def _mla_prep_kernel(cq_ref, ckv_ref, kr_ref, cos_ref, sin_ref, cost_ref, sint_ref, qnw_ref, kvnw_ref,
                     wqt_ref, wkn_ref, wvt_ref, qt_ref, kn_ref, kro_ref, vt_ref):
    cos = cos_ref[...]
    sin = sin_ref[...]
    cos_t = cost_ref[...]
    sin_t = sint_ref[...]
    cq = _rms(cq_ref[...].astype(F32), qnw_ref[...]).astype(BF16)
    hd = 2 * LANES
    half = MLA_ROPE // 2
    scale = (MLA_NOPE + MLA_ROPE) ** -0.5 * LOG2_E
    for h in range(MLA_HEADS):
        qh = _dot_nt(wqt_ref[h * hd:(h + 1) * hd, :], cq) * scale
        lo = qh[MLA_NOPE:MLA_NOPE + half, :]
        hi = qh[MLA_NOPE + half:MLA_NOPE + MLA_ROPE, :]
        qt_ref[0, h * hd:h * hd + MLA_NOPE, :] = qh[0:MLA_NOPE, :].astype(BF16)
        qt_ref[0, h * hd + MLA_NOPE:h * hd + MLA_NOPE + half, :] = (lo * cos_t - hi * sin_t).astype(BF16)
        qt_ref[0, h * hd + MLA_NOPE + half:h * hd + MLA_NOPE + MLA_ROPE, :] = (hi * cos_t + lo * sin_t).astype(BF16)
        qt_ref[0, h * hd + MLA_NOPE + MLA_ROPE:(h + 1) * hd, :] = qh[MLA_NOPE + MLA_ROPE:hd, :].astype(BF16)
    kvl = _rms(ckv_ref[...].astype(F32), kvnw_ref[...]).astype(BF16)
    kn_ref[...] = _dot(kvl, wkn_ref[...]).astype(BF16)
    vt_ref[0] = _dot_nt(wvt_ref[...], kvl).astype(BF16)
    kro_ref[...] = _rope(kr_ref[...].astype(F32), cos, sin).astype(BF16)


def _mla_prep(p, tables, qnw, kvnw, wqt, wkn, wvt, seq, tm=512):
    cos_row, sin_row, cos_col, sin_col = tables
    t = p.shape[0]
    tiles_per_seq = seq // tm
    hw = MLA_HEADS * MLA_NOPE
    half = MLA_ROPE // 2
    cq_blk = P_CQ // MLA_Q_LORA
    ckv_blk = P_CKV // MLA_KV_LORA
    kr_blk = P_KR // LANES
    return pl.pallas_call(
        _mla_prep_kernel,
        grid=(t // tm,),
        in_specs=[
            pl.BlockSpec((tm, MLA_Q_LORA), lambda i: (i, cq_blk)),
            pl.BlockSpec((tm, MLA_KV_LORA), lambda i: (i, ckv_blk)),
            pl.BlockSpec((tm, LANES), lambda i: (i, kr_blk)),
            pl.BlockSpec((tm, LANES), lambda i: (i % tiles_per_seq, 0)),
            pl.BlockSpec((tm, LANES), lambda i: (i % tiles_per_seq, 0)),
            pl.BlockSpec((half, tm), lambda i: (0, i % tiles_per_seq)),
            pl.BlockSpec((half, tm), lambda i: (0, i % tiles_per_seq)),
            pl.BlockSpec((1, MLA_Q_LORA), lambda i: (0, 0)),
            pl.BlockSpec((1, MLA_KV_LORA), lambda i: (0, 0)),
            pl.BlockSpec((2 * hw, MLA_Q_LORA), lambda i: (0, 0)),
            pl.BlockSpec((MLA_KV_LORA, hw), lambda i: (0, 0)),
            pl.BlockSpec((hw, MLA_KV_LORA), lambda i: (0, 0)),
        ],
        out_specs=[
            pl.BlockSpec((1, 2 * hw, tm), lambda i: (i, 0, 0)),
            pl.BlockSpec((tm, hw), lambda i: (i, 0)),
            pl.BlockSpec((tm, LANES), lambda i: (i, 0)),
            pl.BlockSpec((1, hw, tm), lambda i: (i, 0, 0)),
        ],
        out_shape=[
            jax.ShapeDtypeStruct((t // tm, 2 * hw, tm), BF16),
            jax.ShapeDtypeStruct((t, hw), BF16),
            jax.ShapeDtypeStruct((t, LANES), BF16),
            jax.ShapeDtypeStruct((t // tm, hw, tm), BF16),
        ],
        compiler_params=_cp(("parallel",)),
        name="mla_prep",
    )(p, p, p, cos_row, sin_row, cos_col, sin_col, qnw, kvnw, wqt, wkn, wvt)


ATTN_HEADS_PER_STEP = 8
ATTN_SUM_ROWS = 16


def _mla_attn_kernel(qt_ref, kt_ref, q_ref, kn_ref, kr_ref, vt_ref, o_ref, m_ref, acc_ref):
    qi = qt_ref[pl.program_id(2)]
    ki = kt_ref[pl.program_id(2)]
    tq = q_ref.shape[2]
    tk = kn_ref.shape[0]
    hd = 2 * LANES
    q0 = qi * tq
    k0 = ki * tk
    last_k = (q0 + tq) // tk - 1

    @pl.when(ki == 0)
    def _():
        m_ref[...] = jnp.full_like(m_ref, NEG_BIG)
        acc_ref[...] = jnp.zeros_like(acc_ref)

    def step(masked):
        kr = kr_ref[...]
        ones = jnp.ones((ATTN_SUM_ROWS, tk), BF16)
        if masked:
            ck = lax.broadcasted_iota(jnp.int32, (tk, tq), 0) // CHUNK
            cq = lax.broadcasted_iota(jnp.int32, (tk, tq), 1) // CHUNK
            visible = (ck <= cq) if tq == tk else ((ck - cq) <= (q0 - k0) // CHUNK)

        def scores(h):
            k = jnp.concatenate([kn_ref[:, h * MLA_NOPE:(h + 1) * MLA_NOPE], kr], axis=1)
            s = _dot(k, q_ref[0, h * hd:(h + 1) * hd, :])
            if masked:
                s = jnp.where(visible, s, NEG_BIG)
            return s

        def update(h, s):
            m_prev = m_ref[h]
            m_new = jnp.maximum(m_prev, jnp.max(s, axis=0, keepdims=True))
            alpha = jnp.exp2(m_prev - m_new)
            p = jnp.exp2((s - m_new).astype(BF16))
            v_ext = jnp.concatenate([vt_ref[0, h * MLA_V:(h + 1) * MLA_V, :], ones], axis=0)
            acc_ref[h] = alpha * acc_ref[h] + _dot(v_ext, p)
            m_ref[h] = m_new

        s_prev = scores(0)
        for h in range(1, ATTN_HEADS_PER_STEP):
            s_next = scores(h)
            update(h - 1, s_prev)
            s_prev = s_next
        update(ATTN_HEADS_PER_STEP - 1, s_prev)

    @pl.when(k0 + tk <= q0)
    def _():
        step(False)

    def finish():
        for h in range(ATTN_HEADS_PER_STEP):
            acc = acc_ref[h]
            o = acc[0:MLA_V, :] / acc[MLA_V:MLA_V + 1, :]
            o_ref[:, h * MLA_V:(h + 1) * MLA_V] = o.T.astype(BF16)

    @pl.when(k0 + tk > q0)
    def _():
        step(True)
        if tq == tk:
            finish()
        else:
            pl.when(ki == last_k)(finish)


def _mla_attn(qt_all, kn, kr, vt, batch, seq):
    t = kn.shape[0]
    tq = qt_all.shape[2]
    tk = vt.shape[2]
    nq = seq // tq
    nk = seq // tk
    hps = ATTN_HEADS_PER_STEP
    pairs = [(qi, ki) for qi in range(nq) for ki in range((qi + 1) * tq // tk)]
    qt = jnp.asarray(np.array([pr[0] for pr in pairs], np.int32))
    kt = jnp.asarray(np.array([pr[1] for pr in pairs], np.int32))
    return pl.pallas_call(
        _mla_attn_kernel,
        grid_spec=pltpu.PrefetchScalarGridSpec(
            num_scalar_prefetch=2,
            grid=(batch, MLA_HEADS // hps, len(pairs)),
            in_specs=[
                pl.BlockSpec((1, hps * 2 * LANES, tq), lambda b, h, pr, qt, kt: (b * nq + qt[pr], h, 0)),
                pl.BlockSpec((tk, hps * MLA_NOPE), lambda b, h, pr, qt, kt: (b * nk + kt[pr], h)),
                pl.BlockSpec((tk, LANES), lambda b, h, pr, qt, kt: (b * nk + kt[pr], 0)),
                pl.BlockSpec((1, hps * MLA_V, tk), lambda b, h, pr, qt, kt: (b * nk + kt[pr], h, 0)),
            ],
            out_specs=pl.BlockSpec((tq, hps * MLA_V), lambda b, h, pr, qt, kt: (b * nq + qt[pr], h)),
            scratch_shapes=[
                pltpu.VMEM((hps, 1, tq), F32),
                pltpu.VMEM((hps, MLA_V + ATTN_SUM_ROWS, tq), F32),
            ],
        ),
        out_shape=jax.ShapeDtypeStruct((t, MLA_HEADS * MLA_V), BF16),
        compiler_params=_cp(("parallel", "parallel", "arbitrary")),
        name="mla_attn",
    )(qt, kt, qt_all, kn, kr, vt)


MIX_SUBTILES = 2


def _mix_out_kernel(x_ref, oa_ref, ob_ref, ga_ref, gb_ref, wga_ref, wmo_ref, wout_ref, nw_ref,
                    wr_hi_ref, wr_lo_ref, br_ref, x1_ref, h2_ref, selt_ref, cwt_ref):
    sub = br_ref.shape[1]

    def mix(r):
        ya = _dot(oa_ref[r, :], wga_ref[...])
        yb = _dot(ob_ref[r, :], wmo_ref[...])
        merged = (jax.nn.sigmoid(ga_ref[r, :].astype(F32)) * ya
                  + jax.nn.sigmoid(gb_ref[r, :].astype(F32)) * yb)
        x1 = x_ref[r, :] + _dot(merged.astype(BF16), wout_ref[...])
        x1_ref[r, :] = x1
        h2 = _rms(x1, nw_ref[...])
        h_hi = h2.astype(BF16)
        h2_ref[r, :] = h_hi
        h_lo = (h2 - h_hi.astype(F32)).astype(BF16)
        return (_dot_nt(wr_hi_ref[...], h_hi) + _dot_nt(wr_lo_ref[...], h_hi) + _dot_nt(wr_hi_ref[...], h_lo)
                + br_ref[...])[0:N_EXPERTS, :]

    def route(logits, r):
        expert = lax.broadcasted_iota(jnp.int32, logits.shape, 0)
        work = logits
        sel = jnp.zeros(logits.shape, F32)
        cw = jnp.zeros(logits.shape, F32)
        top = None
        denom = None
        for kk in range(TOP_K):
            mx = jnp.max(work, axis=0, keepdims=True)
            am = jnp.min(jnp.where(work == mx, expert, N_EXPERTS), axis=0, keepdims=True)
            hit = expert == am
            if kk == 0:
                top = mx
                e = jnp.ones_like(mx)
                denom = e
            else:
                e = jnp.exp(mx - top)
                denom = denom + e
            sel = jnp.where(hit, 1.0, sel)
            cw = jnp.where(hit, e, cw)
            work = jnp.where(hit, -jnp.inf, work)
        selt_ref[:, r] = sel.astype(BF16)
        cwt_ref[:, r] = cw / denom

    subs = [slice(n * sub, (n + 1) * sub) for n in range(x_ref.shape[0] // sub)]
    logits = [mix(r) for r in subs]
    for lg, r in zip(logits, subs):
        route(lg, r)


def _mix_out(x2, oa, ob, p, wga, wmo, wout, nw, wr_hi, wr_lo, br):
    t, d = x2.shape
    tm = MIX_SUBTILES * br.shape[1]
    full = lambda i: (0, 0)
    return pl.pallas_call(
        _mix_out_kernel,
        grid=(t // tm,),
        in_specs=[
            pl.BlockSpec((tm, d), lambda i: (i, 0)),
            pl.BlockSpec((tm, d), lambda i: (i, 0)),
            pl.BlockSpec((tm, d), lambda i: (i, 0)),
            pl.BlockSpec((tm, d), lambda i: (i, P_GATE_A // P_GROUP)),
            pl.BlockSpec((tm, d), lambda i: (i, P_GATE_B // P_GROUP)),
            pl.BlockSpec((d, d), full),
            pl.BlockSpec((d, d), full),
            pl.BlockSpec((d, d), full),
            pl.BlockSpec((1, d), full),
            pl.BlockSpec((LANES, d), full),
            pl.BlockSpec((LANES, d), full),
            pl.BlockSpec(br.shape, full),
        ],
        out_specs=[
            pl.BlockSpec((tm, d), lambda i: (i, 0)),
            pl.BlockSpec((tm, d), lambda i: (i, 0)),
            pl.BlockSpec((N_EXPERTS, tm), lambda i: (0, i)),
            pl.BlockSpec((N_EXPERTS, tm), lambda i: (0, i)),
        ],
        out_shape=[
            jax.ShapeDtypeStruct((t, d), F32),
            jax.ShapeDtypeStruct((t, d), BF16),
            jax.ShapeDtypeStruct((N_EXPERTS, t), BF16),
            jax.ShapeDtypeStruct((N_EXPERTS, t), F32),
        ],
        compiler_params=_cp(("parallel",)),
        name="mix_out",
    )(x2, oa, ob, p, p, wga, wmo, wout, nw, wr_hi, wr_lo, br)


def _route_pos_kernel(selt_ref, lpos_ref, keyt_ref, offs_ref, cnt_ref, tot_ref, carry_ref):
    tm = selt_ref.shape[1]
    i = pl.program_id(0)

    @pl.when(i == 0)
    def _():
        carry_ref[...] = jnp.zeros_like(carry_ref)

    sel_t = selt_ref[...]
    sel_rows = jnp.concatenate([sel_t, jnp.zeros((LANES - N_EXPERTS, tm), BF16)], axis=0)
    row = lax.broadcasted_iota(jnp.int32, (tm, tm), 0)
    col = lax.broadcasted_iota(jnp.int32, (tm, tm), 1)
    lpos_ref[...] = _dot_nt((col < row).astype(BF16), sel_rows)
    pos_t = _dot(sel_t, (row < col).astype(BF16))
    keyt_ref[0] = jnp.where(sel_t > 0, pos_t, -1.0)

    n = _dot_nt(jnp.ones((8, tm), BF16), sel_rows)[0:1, :]
    carry = carry_ref[0:1, :]
    offs_ref[0] = carry.astype(jnp.int32)
    cnt_ref[0] = n.astype(jnp.int32)
    total = carry + jnp.ceil(n * (1.0 / SEG_ALIGN)) * SEG_ALIGN
    carry_ref[...] = jnp.broadcast_to(total, carry_ref.shape)
    tot_ref[...] = jnp.broadcast_to(total, tot_ref.shape).astype(jnp.int32)


def _route_pos(sel_t):
    t = sel_t.shape[1]
    tm = ROUTE_TILE
    nt = t // tm
    return pl.pallas_call(
        _route_pos_kernel,
        grid=(nt,),
        in_specs=[pl.BlockSpec((N_EXPERTS, tm), lambda i: (0, i))],
        out_specs=[
            pl.BlockSpec((tm, LANES), lambda i: (i, 0)),
            pl.BlockSpec((1, N_EXPERTS, tm), lambda i: (i, 0, 0)),
            pl.BlockSpec((1, 1, LANES), lambda i: (i, 0, 0)),
            pl.BlockSpec((1, 1, LANES), lambda i: (i, 0, 0)),
            pl.BlockSpec((8, LANES), lambda i: (0, 0)),
        ],
        out_shape=[
            jax.ShapeDtypeStruct((t, LANES), F32),
            jax.ShapeDtypeStruct((nt, N_EXPERTS, tm), F32),
            jax.ShapeDtypeStruct((nt, 1, LANES), jnp.int32),
            jax.ShapeDtypeStruct((nt, 1, LANES), jnp.int32),
            jax.ShapeDtypeStruct((8, LANES), jnp.int32),
        ],
        scratch_shapes=[pltpu.VMEM((8, LANES), F32)],
        compiler_params=_cp(("arbitrary",)),
        name="route_pos",
    )(sel_t)


def _rows(ref, start, n):
    return ref.at[pl.ds(pl.multiple_of(start, n), n)]


def _pack_pairs(x):
    half = x.shape[1] // 2
    hi = lax.bitcast_convert_type(x[:, :half], jnp.uint32)
    lo = lax.bitcast_convert_type(x[:, half:], jnp.uint32)
    return hi | lax.shift_right_logical(lo, jnp.uint32(16))


def _unpack_pairs(u):
    hi = lax.bitcast_convert_type(u & jnp.uint32(0xFFFF0000), F32).astype(BF16)
    lo = lax.bitcast_convert_type(lax.shift_left(u, jnp.uint32(16)), F32).astype(BF16)
    return hi, lo


def _seg_windows(cnt_ref, base):
    longest = lax.fori_loop(0, N_EXPERTS, lambda e, m: jnp.maximum(m, cnt_ref[base + e]), 0)
    return lax.shift_right_logical(longest + (SEG_WINDOW - 1), SEG_WINDOW.bit_length() - 1)


def _dispatch_kernel(seg_ref, cnt_ref, fill_lo_ref, fill_hi_ref, h_ref, keyt_ref, xs_ref,
                     stage_ref, zero_ref, sem):
    i = pl.program_id(0)
    tm = h_ref.shape[0]
    base = i * N_EXPERTS

    def zero_fills(wait):
        def act(cp):
            if wait:
                cp.wait()
            else:
                cp.start()

        def fill(c):
            return pltpu.make_async_copy(zero_ref.at[pl.ds(0, SEG_ALIGN)], _rows(xs_ref, c * SEG_ALIGN, SEG_ALIGN),
                                         sem.at[2])

        def per_expert(e, carry):
            lo = lax.shift_right_logical(fill_lo_ref[e], SEG_ALIGN.bit_length() - 1)
            hi = lax.shift_right_logical(fill_hi_ref[e], SEG_ALIGN.bit_length() - 1)
            return lax.fori_loop(lo, hi, lambda c, a: (act(fill(c)), a)[1], carry)

        lax.fori_loop(0, N_EXPERTS, per_expert, 0)

        def fill_tail(c):
            return pltpu.make_async_copy(zero_ref, _rows(xs_ref, c * ZERO_ROWS, ZERO_ROWS), sem.at[2])

        lo = lax.shift_right_logical(fill_hi_ref[N_EXPERTS - 1], ZERO_ROWS.bit_length() - 1)
        hi = xs_ref.shape[0] // ZERO_ROWS
        lax.fori_loop(lo, hi, lambda c, a: (act(fill_tail(c)), a)[1], 0)

    @pl.when(i == 0)
    def _():
        zero_ref[...] = jnp.zeros_like(zero_ref)
        zero_fills(wait=False)
        zero_fills(wait=True)

    buf = i % 2

    def send(tile_base, win, b, wait, only_live=False):
        for e in range(N_EXPERTS):
            def go(e=e):
                slot = seg_ref[tile_base + e] + win * SEG_WINDOW
                cp = pltpu.make_async_copy(stage_ref.at[b, pl.ds(e * SEG_WINDOW, SEG_WINDOW)],
                                           xs_ref.at[pl.ds(pl.multiple_of(slot, SEG_ALIGN), SEG_WINDOW)],
                                           sem.at[b])
                if wait:
                    cp.wait()
                else:
                    cp.start()

            if only_live:
                pl.when(cnt_ref[tile_base + e] > win * SEG_WINDOW)(go)
            else:
                go()

    half = N_EXPERTS // 2
    j = lax.broadcasted_iota(jnp.int32, (SEG_WINDOW, tm), 0).astype(F32)

    def build(win, b):
        key = keyt_ref[0] - jnp.asarray(win * SEG_WINDOW, F32)
        for hf in range(2):
            pick = jnp.concatenate([(key[e:e + 1, :] == j).astype(BF16)
                                    for e in range(hf * half, (hf + 1) * half)], axis=0)
            stage_ref[b, hf * half * SEG_WINDOW:(hf + 1) * half * SEG_WINDOW, :] = _pack_pairs(_dot(pick, h_ref[...]))

    n_win = _seg_windows(cnt_ref, base)
    build(0, buf)

    @pl.when(i > 0)
    def _():
        send(base - N_EXPERTS, 0, 1 - buf, wait=True)

    send(base, 0, buf, wait=False)

    def more(win, carry):
        build(win, 1 - buf)
        send(base, win, 1 - buf, wait=False, only_live=True)
        send(base, win, 1 - buf, wait=True, only_live=True)
        return carry

    lax.fori_loop(1, n_win, more, 0)

    @pl.when(i == pl.num_programs(0) - 1)
    def _():
        send(base, 0, buf, wait=True)


def _dispatch(seg, cnt, fill_lo, fill_hi, h2, keyt, n_pad):
    t, d = h2.shape
    tm = ROUTE_TILE
    return pl.pallas_call(
        _dispatch_kernel,
        grid_spec=pltpu.PrefetchScalarGridSpec(
            num_scalar_prefetch=4,
            grid=(t // tm,),
            in_specs=[
                pl.BlockSpec((tm, d), lambda i, *_: (i, 0)),
                pl.BlockSpec((1, N_EXPERTS, tm), lambda i, *_: (i, 0, 0)),
            ],
            out_specs=pl.BlockSpec(memory_space=pl.ANY),
            scratch_shapes=[pltpu.VMEM((2, N_EXPERTS * SEG_WINDOW, d // 2), jnp.uint32),
                            pltpu.VMEM((ZERO_ROWS, d // 2), jnp.uint32),
                            pltpu.SemaphoreType.DMA((3,))],
        ),
        out_shape=jax.ShapeDtypeStruct((n_pad, d // 2), jnp.uint32),
        compiler_params=_cp(("arbitrary",)),
        name="dispatch",
    )(seg, cnt, fill_lo, fill_hi, h2, keyt)


def _experts_kernel(be_ref, nv_ref, xs_ref, wgu_ref, wd_ref, bg_ref, bu_ref, bd_ref, ys_ref,
                    wg_s, wu_s, wd_s):
    j = pl.program_id(0)
    grp = 2 * LANES
    prev = be_ref[jnp.maximum(j - 1, 0)]

    @pl.when((j == 0) | (be_ref[j] != prev))
    def _():
        r = lax.broadcasted_iota(jnp.int32, (grp, grp), 0)
        c = lax.broadcasted_iota(jnp.int32, (grp, grp), 1)
        src = jnp.where(c < LANES, 2 * c, 2 * (c - LANES) + 1)
        pick = (r == src).astype(BF16)
        for g in range(wgu_ref.shape[2] // grp):
            y = _dot(wgu_ref[0, :, g * grp:(g + 1) * grp].astype(BF16), pick)
            wg_s[:, g * LANES:(g + 1) * LANES] = y[:, 0:LANES].astype(BF16)
            wu_s[:, g * LANES:(g + 1) * LANES] = y[:, LANES:grp].astype(BF16)
        wd_s[...] = wd_ref[0].astype(BF16)

    @pl.when(j < nv_ref[0])
    def _():
        x = jnp.concatenate(_unpack_pairs(xs_ref[...]), axis=1)
        g = _dot(x, wg_s[...]) + bg_ref[0]
        u = _dot(x, wu_s[...]) + bu_ref[0]
        gate = jnp.minimum(g, SWIGLU_LIMIT)
        up = jnp.clip(u, -SWIGLU_LIMIT, SWIGLU_LIMIT)
        act = (up + 1.0) * (gate * jax.nn.sigmoid(gate * SWIGLU_ALPHA))
        y = _dot(act.astype(BF16), wd_s[...]) + bd_ref[0]
        ys_ref[...] = _pack_pairs(y.astype(BF16).astype(F32))

    @pl.when(j >= nv_ref[0])
    def _():
        ys_ref[...] = jnp.zeros_like(ys_ref)


def _experts(block_e, n_valid, xs, wgu, wd, bg, bu, bd):
    n_pad = xs.shape[0]
    de, d = wd.shape[1:]
    blk = (MOE_ROWS, xs.shape[1])
    n_blocks = n_pad // MOE_ROWS
    xrow = lambda j, be, nv: (jnp.minimum(j, nv[0] - 1), 0)
    wsel = lambda j, be, nv: (be[j], 0, 0)
    return pl.pallas_call(
        _experts_kernel,
        grid_spec=pltpu.PrefetchScalarGridSpec(
            num_scalar_prefetch=2,
            grid=(n_blocks,),
            in_specs=[
                pl.BlockSpec(blk, xrow),
                pl.BlockSpec((1, d, 2 * de), wsel),
                pl.BlockSpec((1, de, d), wsel),
                pl.BlockSpec((1, 1, de), wsel),
                pl.BlockSpec((1, 1, de), wsel),
                pl.BlockSpec((1, 1, d), wsel),
            ],
            out_specs=pl.BlockSpec(blk, lambda j, be, nv: (j, 0)),
            scratch_shapes=[
                pltpu.VMEM((d, de), BF16),
                pltpu.VMEM((d, de), BF16),
                pltpu.VMEM((de, d), BF16),
            ],
        ),
        out_shape=jax.ShapeDtypeStruct(xs.shape, xs.dtype),
        compiler_params=pltpu.CompilerParams(dimension_semantics=("arbitrary",),
                                             vmem_limit_bytes=EXPERTS_VMEM_LIMIT),
        name="experts",
    )(block_e, n_valid, xs, wgu, wd, bg, bu, bd)


def _combine_kernel(seg_ref, cnt_ref, x1_ref, lpos_ref, cw_ref, nw_ref, ys_ref, o_ref, stage_ref, sem,
                    *, final_norm):
    i = pl.program_id(0)
    tm, d = x1_ref.shape
    base = i * N_EXPERTS

    buf = i % 2

    def gather(tile_base, win, b, wait):
        for e in range(N_EXPERTS):
            slot = seg_ref[tile_base + e] + win * SEG_WINDOW
            cp = pltpu.make_async_copy(ys_ref.at[pl.ds(pl.multiple_of(slot, SEG_ALIGN), SEG_WINDOW)],
                                       stage_ref.at[b, pl.ds(e * SEG_WINDOW, SEG_WINDOW)], sem.at[b])
            if wait:
                cp.wait()
            else:
                cp.start()

    @pl.when(i == 0)
    def _():
        gather(base, 0, buf, wait=False)

    @pl.when(i + 1 < pl.num_programs(0))
    def _():
        gather(base + N_EXPERTS, 0, 1 - buf, wait=False)

    n_stage = N_EXPERTS * SEG_WINDOW
    owner = lax.broadcasted_iota(jnp.int32, (LANES, n_stage), 1) // SEG_WINDOW
    expand = (owner == lax.broadcasted_iota(jnp.int32, (LANES, n_stage), 0)).astype(BF16)
    j = (lax.broadcasted_iota(jnp.int32, (tm, n_stage), 1) % SEG_WINDOW).astype(F32)
    cw_rows = jnp.concatenate([cw_ref[...].astype(BF16), jnp.zeros((LANES - N_EXPERTS, tm), BF16)], axis=0)
    cw_wide = _dot_tn(cw_rows, expand)

    def window(win, y):
        rank = (lpos_ref[...] - jnp.asarray(win * SEG_WINDOW, F32)).astype(BF16)
        take = jnp.where(_dot(rank, expand) == j, cw_wide, 0.0).astype(BF16)
        gather(base, win, buf, wait=True)
        hi, lo = _unpack_pairs(stage_ref[buf])
        return y + jnp.concatenate([_dot(take, hi), _dot(take, lo)], axis=1)

    def more(win, y):
        gather(base, win, buf, wait=False)
        return window(win, y)

    y = window(0, jnp.zeros((tm, d), F32))
    y = lax.fori_loop(1, _seg_windows(cnt_ref, base), more, y)
    out = x1_ref[...] + y
    if final_norm:
        out = _rms(out, nw_ref[...])
    o_ref[...] = out


def _combine(seg, cnt, x1, lpos, cw, nw, ys, final_norm):
    t, d = x1.shape
    tm = ROUTE_TILE
    kern = functools.partial(_combine_kernel, final_norm=final_norm)
    return pl.pallas_call(
        kern,
        grid_spec=pltpu.PrefetchScalarGridSpec(
            num_scalar_prefetch=2,
            grid=(t // tm,),
            in_specs=[
                pl.BlockSpec((tm, d), lambda i, *_: (i, 0)),
                pl.BlockSpec((tm, LANES), lambda i, *_: (i, 0)),
                pl.BlockSpec((N_EXPERTS, tm), lambda i, *_: (0, i)),
                pl.BlockSpec((1, d), lambda i, *_: (0, 0)),
                pl.BlockSpec(memory_space=pl.ANY),
            ],
            out_specs=pl.BlockSpec((tm, d), lambda i, *_: (i, 0)),
            scratch_shapes=[pltpu.VMEM((2, N_EXPERTS * SEG_WINDOW, d // 2), jnp.uint32),
                            pltpu.SemaphoreType.DMA((2,))],
        ),
        out_shape=jax.ShapeDtypeStruct((t, d), F32),
        compiler_params=_cp(("arbitrary",)),
        name="combine",
    )(seg, cnt, x1, lpos, cw, nw, ys)


def _rope_tables(seq):
    inv = 1.0 / (ROPE_THETA ** (jnp.arange(0, MLA_ROPE, 2, dtype=F32) / MLA_ROPE))
    ang = jnp.arange(seq, dtype=F32)[:, None] * inv[None, :]
    cos, sin = jnp.cos(ang), jnp.sin(ang)
    zeros = jnp.zeros((seq, LANES - MLA_ROPE), F32)
    cos_row = jnp.concatenate([cos, cos, zeros], axis=-1)
    sin_row = jnp.concatenate([-sin, sin, zeros], axis=-1)
    return cos_row, sin_row, cos.T, sin.T


def _pad_cols(a, width):
    return jnp.pad(a, ((0, 0), (0, width - a.shape[1])))


def _layer(x2, batch, seq, final_norm_w, final_norm, rope_tables,
           norm_mix_w, w_in, gdn_conv_w, gdn_a_log, gdn_dt_bias, gdn_norm_w, w_gdn_o,
           mla_q_norm_w, w_mla_q_b, mla_kv_norm_w, w_mla_kv_b, w_mla_o, w_out,
           norm_ffn_w, w_router, b_router, w_gate_up, b_gate_up, w_down, b_down):
    t, d = x2.shape
    qk_w = GDN_HEADS * GDN_D
    assert d == P_GROUP and qk_w == P_GROUP
    o_b = 4 * qk_w
    o_a = o_b + GDN_HEADS
    o_cq = o_a + GDN_HEADS
    o_ckv = o_cq + MLA_Q_LORA
    o_kr = o_ckv + MLA_KV_LORA
    o_ga = o_kr + MLA_ROPE
    o_gb = o_ga + d
    w_p = jnp.concatenate([
        w_in[:, 0:o_b], w_in[:, o_ga:o_gb + d], w_in[:, o_cq:o_ckv], w_in[:, o_ckv:o_kr],
        _pad_cols(w_in[:, o_kr:o_ga], P_WIDTH - P_KR)], axis=1).astype(BF16)
    w_ab = _pad_cols(jnp.concatenate([w_in[:, o_a:o_cq], w_in[:, o_b:o_a]], axis=1), LANES).astype(BF16)

    p, ab = _in_proj(x2, norm_mix_w[None, :], w_p, w_ab)

    alog_row = _pad_cols(gdn_a_log[None, :].astype(F32), LANES)
    dtb_row = _pad_cols(gdn_dt_bias[None, :].astype(F32), LANES)
    qkvn, cols, gct = _gdn_prep(p, ab, gdn_conv_w.astype(F32), alog_row, dtb_row, seq)
    o_gdn = _gdn_chunk(qkvn, p, cols, gct, gdn_norm_w[None, :].astype(F32), batch, seq)

    hd = MLA_NOPE + MLA_ROPE
    wq = w_mla_q_b.reshape(MLA_Q_LORA, MLA_HEADS, hd)
    wqt = jnp.pad(wq, ((0, 0), (0, 0), (0, 2 * LANES - hd))).reshape(MLA_Q_LORA, MLA_HEADS * 2 * LANES).T
    wkv = w_mla_kv_b.reshape(MLA_KV_LORA, MLA_HEADS, MLA_NOPE + MLA_V)
    wkn = wkv[:, :, :MLA_NOPE].reshape(MLA_KV_LORA, -1)
    wvt = wkv[:, :, MLA_NOPE:].reshape(MLA_KV_LORA, -1).T
    qt, kn, kr, vt = _mla_prep(p, rope_tables, mla_q_norm_w[None, :].astype(F32),
                               mla_kv_norm_w[None, :].astype(F32), wqt.astype(BF16), wkn.astype(BF16),
                               wvt.astype(BF16), seq)
    o_mla = _mla_attn(qt, kn, kr, vt, batch, seq)

    wr = _pad_cols(w_router.astype(F32), LANES).T
    wr_hi = wr.astype(BF16)
    wr_lo = (wr - wr_hi.astype(F32)).astype(BF16)
    br = jnp.broadcast_to(_pad_cols(b_router[None, :].astype(F32), LANES).T, (LANES, ROUTE_TILE))
    x1, h2, sel, cw = _mix_out(x2, o_gdn, o_mla, p, w_gdn_o.astype(BF16), w_mla_o.astype(BF16),
                               w_out.astype(BF16), norm_ffn_w[None, :].astype(F32), wr_hi, wr_lo, br)

    lpos, keyt, offs, cnt, tot = _route_pos(sel)
    n_tiles = t // ROUTE_TILE
    used = tot[0, :N_EXPERTS]
    padded = (used + SEG_WINDOW + MOE_ROWS - 1) // MOE_ROWS * MOE_ROWS
    pad_end = jnp.cumsum(padded)
    pad_start = pad_end - padded
    seg = (pad_start[None, :] + offs[:, 0, :N_EXPERTS]).astype(jnp.int32).reshape(-1)
    cnt = cnt[:, 0, :N_EXPERTS].reshape(-1)
    worst_used = t * TOP_K + n_tiles * N_EXPERTS * (SEG_ALIGN - 1) + N_EXPERTS * SEG_WINDOW
    n_pad = -(-worst_used // MOE_ROWS) * MOE_ROWS + (N_EXPERTS + 1) * MOE_ROWS
    n_blocks = n_pad // MOE_ROWS
    blk_start = jnp.arange(n_blocks, dtype=jnp.int32) * MOE_ROWS
    block_e = jnp.minimum(jnp.sum((pad_end[None, :] <= blk_start[:, None]).astype(jnp.int32), axis=1),
                          N_EXPERTS - 1).astype(jnp.int32)
    n_valid = (pad_end[-1:] // MOE_ROWS).astype(jnp.int32)

    xs = _dispatch(seg, cnt, (pad_start + used).astype(jnp.int32), pad_end.astype(jnp.int32), h2, keyt, n_pad)
    bg = b_gate_up[:, None, 0::2].astype(F32)
    bu = b_gate_up[:, None, 1::2].astype(F32)
    ys = _experts(block_e, n_valid, xs, w_gate_up, w_down, bg, bu, b_down[:, None, :].astype(F32))
    return _combine(seg, cnt, x1, lpos, cw, final_norm_w[None, :].astype(F32), ys, final_norm)


def kernel(x, norm_mix_w, w_in, gdn_conv_w, gdn_a_log, gdn_dt_bias, gdn_norm_w, w_gdn_o, mla_q_norm_w, w_mla_q_b, mla_kv_norm_w, w_mla_kv_b, w_mla_o, w_out, norm_ffn_w, w_router, b_router, w_gate_up, b_gate_up, w_down, b_down, norm_final_w):
    batch, seq, d = x.shape
    depth = w_in.shape[0]
    rope_tables = _rope_tables(seq)
    x2 = x.reshape(batch * seq, d)
    for layer in range(depth):
        x2 = _layer(x2, batch, seq, norm_final_w, layer == depth - 1, rope_tables,
                    norm_mix_w[layer], w_in[layer], gdn_conv_w[layer], gdn_a_log[layer],
                    gdn_dt_bias[layer], gdn_norm_w[layer], w_gdn_o[layer], mla_q_norm_w[layer],
                    w_mla_q_b[layer], mla_kv_norm_w[layer], w_mla_kv_b[layer], w_mla_o[layer],
                    w_out[layer], norm_ffn_w[layer], w_router[layer], b_router[layer],
                    w_gate_up[layer], b_gate_up[layer], w_down[layer], b_down[layer])
    return x2.reshape(batch, seq, d)
```

```python
import functools

import jax
import jax.numpy as jnp
import numpy as np
from jax import lax
from jax.experimental import pallas as pl
from jax.experimental.pallas import tpu as pltpu

F32 = jnp.float32
BF16 = jnp.bfloat16

CHUNK = 64
NORM_EPS = 1e-6
GDN_HEADS = 8
GDN_D = 128
GDN_CONV = 4
MLA_HEADS = 8
MLA_Q_LORA = 512
MLA_KV_LORA = 256
MLA_NOPE = 128
MLA_ROPE = 64
MLA_V = 128
ROPE_THETA = 10000.0
N_EXPERTS = 32
TOP_K = 4
SWIGLU_LIMIT = 7.0
SWIGLU_ALPHA = 1.702

LANES = 128
MOE_ROWS = 512
ROUTE_TILE = 256
SEG_ALIGN = 8
SEG_WINDOW = 64
ZERO_ROWS = 64
VMEM_LIMIT = 48 * 1024 * 1024
EXPERTS_VMEM_LIMIT = 56 * 1024 * 1024

P_GROUP = GDN_HEADS * GDN_D
P_Q, P_K, P_V, P_Z, P_GATE_A, P_GATE_B = (n * P_GROUP for n in range(6))
P_CQ = 6 * P_GROUP
P_CKV = P_CQ + MLA_Q_LORA
P_KR = P_CKV + MLA_KV_LORA
P_WIDTH = 7 * P_GROUP

NEG_BIG = -1e30
LOG2_E = 1.4426950408889634


def _cp(sem):
    return pltpu.CompilerParams(dimension_semantics=sem, vmem_limit_bytes=VMEM_LIMIT)


def _dot(a, b):
    return jnp.dot(a, b, preferred_element_type=F32)


def _dot_nt(a, b):
    return lax.dot_general(a, b, (((1,), (1,)), ((), ())), preferred_element_type=F32)


def _dot_tn(a, b):
    return lax.dot_general(a, b, (((0,), (0,)), ((), ())), preferred_element_type=F32)


def _split3(x):
    hi = x.astype(BF16)
    r = x - hi.astype(F32)
    mid = r.astype(BF16)
    lo = (r - mid.astype(F32)).astype(BF16)
    return hi, mid, lo


def _rms(x, w):
    ms = jnp.mean(x * x, axis=-1, keepdims=True)
    return x * lax.rsqrt(ms + NORM_EPS) * w


def _in_proj_kernel(x_ref, nw_ref, w_ref, wab_ref, p_ref, ab_ref, h_ref):
    @pl.when(pl.program_id(1) == 0)
    def _():
        hb = _rms(x_ref[...], nw_ref[...]).astype(BF16)
        h_ref[...] = hb
        ab_ref[...] = _dot(hb, wab_ref[...])

    p_ref[...] = _dot(h_ref[...], w_ref[...]).astype(BF16)


def _in_proj(x2, norm_w, w_p, w_ab, tm=1024, tn=3584):
    t, d = x2.shape
    n = w_p.shape[1]
    return pl.pallas_call(
        _in_proj_kernel,
        grid=(t // tm, n // tn),
        in_specs=[
            pl.BlockSpec((tm, d), lambda i, j: (i, 0)),
            pl.BlockSpec((1, d), lambda i, j: (0, 0)),
            pl.BlockSpec((d, tn), lambda i, j: (0, j)),
            pl.BlockSpec((d, LANES), lambda i, j: (0, 0)),
        ],
        out_specs=[
            pl.BlockSpec((tm, tn), lambda i, j: (i, j)),
            pl.BlockSpec((tm, LANES), lambda i, j: (i, 0)),
        ],
        out_shape=[
            jax.ShapeDtypeStruct((t, n), BF16),
            jax.ShapeDtypeStruct((t, LANES), F32),
        ],
        scratch_shapes=[pltpu.VMEM((tm, d), BF16)],
        compiler_params=_cp(("parallel", "arbitrary")),
        name="in_proj",
    )(x2, norm_w, w_p, w_ab)


def _gdn_prep_kernel(cur_ref, prev_ref, ab_ref, cw_ref, alog_ref, dtb_ref,
                     qkv_ref, cols_ref, gct_ref, *, tiles_per_seq):
    tm = cur_ref.shape[0]
    i = pl.program_id(0)
    halo_on = (i % tiles_per_seq) != 0
    q_scale = GDN_D ** -0.5
    grp = 2 * LANES
    row = lax.broadcasted_iota(jnp.int32, (tm, tm), 0)
    col = lax.broadcasted_iota(jnp.int32, (tm, tm), 1)
    shift = [(col == row - s).astype(BF16) for s in range(1, GDN_CONV)]
    for cg in range(cur_ref.shape[1] // grp):
        gs = slice(cg * grp, (cg + 1) * grp)
        cur_b = cur_ref[:, gs]
        cur = cur_b.astype(F32)
        w = cw_ref[:, gs]
        y = w[GDN_CONV - 1:GDN_CONV, :] * cur
        for s in range(1, GDN_CONV):
            y = y + w[GDN_CONV - 1 - s:GDN_CONV - s, :] * _dot(shift[s - 1], cur_b)
        halo = jnp.where(halo_on, prev_ref[:, gs].astype(F32)[8:16, :], 0.0)
        xe = jnp.concatenate([halo, cur[0:8, :]], axis=0)
        head = w[0:1, :] * xe[5:13, :]
        for j in range(1, GDN_CONV):
            head = head + w[j:j + 1, :] * xe[5 + j:13 + j, :]
        y = jnp.concatenate([head, y[8:, :]], axis=0)
        hy = 0.5 * y
        y = hy + hy * jnp.tanh(hy)
        for half in range(2):
            cb = 2 * cg + half
            yh = y[:, half * LANES:(half + 1) * LANES]
            if cb < 2 * GDN_HEADS:
                ss = jnp.sum(yh * yh, axis=-1, keepdims=True)
                yh = yh * lax.rsqrt(ss + NORM_EPS)
                if cb < GDN_HEADS:
                    yh = yh * q_scale
            qkv_ref[:, cb * LANES:(cb + 1) * LANES] = yh.astype(BF16)

    ab = ab_ref[...]
    g = -jnp.exp(alog_ref[...]) * jax.nn.softplus(ab + dtb_ref[...])
    row = lax.broadcasted_iota(jnp.int32, (tm, tm), 0)
    col = lax.broadcasted_iota(jnp.int32, (tm, tm), 1)
    tri = ((col <= row) & ((row // CHUNK) == (col // CHUNK))).astype(BF16)
    g_hi, g_mid, g_lo = _split3(g)
    gc = _dot(tri, g_hi) + _dot(tri, g_mid) + _dot(tri, g_lo)
    lane = lax.broadcasted_iota(jnp.int32, (tm, LANES), 1)
    cols_ref[...] = jnp.where(lane < GDN_HEADS, gc, jax.nn.sigmoid(ab))
    for c in range(tm // CHUNK):
        blk = gc[c * CHUNK:(c + 1) * CHUNK, :]
        blk = jnp.concatenate([blk, jnp.zeros_like(blk)], axis=0)
        gct_ref[c] = blk.T[0:GDN_HEADS, 0:CHUNK]


def _gdn_prep(p, ab, conv_w, alog_row, dtb_row, seq, tm=256):
    t = p.shape[0]
    cw = 3 * GDN_HEADS * GDN_D
    tiles_per_seq = seq // tm
    kern = functools.partial(_gdn_prep_kernel, tiles_per_seq=tiles_per_seq)
    return pl.pallas_call(
        kern,
        grid=(t // tm,),
        in_specs=[
            pl.BlockSpec((tm, cw), lambda i: (i, 0)),
            pl.BlockSpec((16, cw), lambda i: (jnp.maximum(i * (tm // 16) - 1, 0), 0)),
            pl.BlockSpec((tm, LANES), lambda i: (i, 0)),
            pl.BlockSpec((GDN_CONV, cw), lambda i: (0, 0)),
            pl.BlockSpec((1, LANES), lambda i: (0, 0)),
            pl.BlockSpec((1, LANES), lambda i: (0, 0)),
        ],
        out_specs=[
            pl.BlockSpec((tm, cw), lambda i: (i, 0)),
            pl.BlockSpec((tm, LANES), lambda i: (i, 0)),
            pl.BlockSpec((tm // CHUNK, GDN_HEADS, CHUNK), lambda i: (i, 0, 0)),
        ],
        out_shape=[
            jax.ShapeDtypeStruct((t, cw), BF16),
            jax.ShapeDtypeStruct((t, LANES), F32),
            jax.ShapeDtypeStruct((t // CHUNK, GDN_HEADS, CHUNK), F32),
        ],
        compiler_params=_cp(("parallel",)),
        name="gdn_prep",
    )(p, p, ab, conv_w, alog_row, dtb_row)


GDN_CHUNKS_PER_STEP = 2


def _gdn_chunk_kernel(q_ref, k_ref, v_ref, z_ref, cols_ref, gct_ref, nw_ref, o_ref, s_ref):
    c = CHUNK
    nb = q_ref.shape[0]
    chains = [(b, h) for b in range(nb) for h in range(GDN_HEADS)]
    units = [(g, b, h) for g in range(GDN_CHUNKS_PER_STEP) for b, h in chains]

    @pl.when(pl.program_id(0) == 0)
    def _():
        s_ref[...] = jnp.zeros_like(s_ref)

    ri = lax.broadcasted_iota(jnp.int32, (c, c), 0)
    ci = lax.broadcasted_iota(jnp.int32, (c, c), 1)
    incl = ri >= ci
    strict = ri > ci
    eye = (ri == ci).astype(F32)
    nw = nw_ref[...]

    def rows(g):
        return slice(g * c, (g + 1) * c)

    def head(h):
        return slice(h * GDN_D, (h + 1) * GDN_D)

    cols, e_g, e_kd, e_last, gct = {}, {}, {}, {}, {}
    for g in range(GDN_CHUNKS_PER_STEP):
        for b in range(nb):
            cb = cols_ref[b, rows(g), :]
            last = cb[c - 1:c, :]
            cols[g, b] = cb
            e_g[g, b] = jnp.exp(cb)
            e_kd[g, b] = jnp.exp(last - cb)
            e_last[g, b] = jnp.exp(last)
            gct[g, b] = gct_ref[b, g]

    kq, kb_l, kf_l = [], [], []
    for g, b, h in units:
        k = k_ref[b, rows(g), head(h)]
        kf = k.astype(F32)
        kb = kf * cols[g, b][:, GDN_HEADS + h:GDN_HEADS + h + 1]
        kq.append(_dot_nt(jnp.concatenate([kb.astype(BF16), q_ref[b, rows(g), head(h)]], axis=0), k))
        kb_l.append(kb)
        kf_l.append(kf)

    a_l, qk_l = [], []
    for i, (g, b, h) in enumerate(units):
        dec = jnp.exp(jnp.minimum(cols[g, b][:, h:h + 1] - gct[g, b][h:h + 1, :], 0.0))
        a_l.append(jnp.where(strict, -kq[i][0:c, :] * dec, 0.0))
        qk_l.append(jnp.where(incl, kq[i][c:2 * c, :] * dec, 0.0).astype(BF16))

    tinv = [eye + a for a in a_l]
    pw = a_l
    for _ in range(5):
        pwb = [x.astype(BF16) for x in pw]
        pw = [_dot(x, x) for x in pwb]
        tinv = [t + _dot(t.astype(BF16), x.astype(BF16)) for t, x in zip(tinv, pw)]

    u_l, lhs_l, kd_l = [], [], []
    for i, (g, b, h) in enumerate(units):
        beta = cols[g, b][:, GDN_HEADS + h:GDN_HEADS + h + 1]
        eg = e_g[g, b][:, h:h + 1]
        rhs = jnp.concatenate([v_ref[b, rows(g), head(h)].astype(F32) * beta, kb_l[i] * eg],
                              axis=1).astype(BF16)
        uw = _dot(tinv[i].astype(BF16), rhs)
        qd = (q_ref[b, rows(g), head(h)].astype(F32) * eg).astype(BF16)
        u_l.append(uw[:, 0:GDN_D])
        lhs_l.append(jnp.concatenate([uw[:, GDN_D:2 * GDN_D].astype(BF16), qd], axis=0))
        kd_l.append((kf_l[i] * e_kd[g, b][:, h:h + 1]).astype(BF16))

    for g in range(GDN_CHUNKS_PER_STEP):
        first = g * len(chains)
        r_l = [_dot(lhs_l[first + n], s_ref[n].astype(BF16)) for n in range(len(chains))]
        for n, (b, h) in enumerate(chains):
            i = first + n
            v_new = (u_l[i] - r_l[n][0:c, :]).astype(BF16)
            o = r_l[n][c:2 * c, :] + _dot(qk_l[i], v_new)
            s_ref[n] = s_ref[n] * e_last[g, b][:, h:h + 1] + _dot_tn(kd_l[i], v_new)
            z = z_ref[b, rows(g), head(h)].astype(F32)
            o_ref[b, rows(g), head(h)] = (_rms(o, nw) * (z * jax.nn.sigmoid(z))).astype(BF16)


def _gdn_chunk(qkvn, p, cols, gct, norm_w, batch, seq):
    nc = seq // CHUNK
    hw = GDN_HEADS * GDN_D
    g = GDN_CHUNKS_PER_STEP
    qkvn3 = qkvn.reshape(batch, seq, qkvn.shape[1])
    p3 = p.reshape(batch, seq, p.shape[1])
    cols3 = cols.reshape(batch, seq, LANES)
    gct4 = gct.reshape(batch, nc, GDN_HEADS, CHUNK)
    tile = lambda col: pl.BlockSpec((batch, g * CHUNK, hw), lambda c: (0, c, col))
    out = pl.pallas_call(
        _gdn_chunk_kernel,
        grid=(nc // g,),
        in_specs=[
            tile(0), tile(1), tile(2),
            tile(P_Z // P_GROUP),
            pl.BlockSpec((batch, g * CHUNK, LANES), lambda c: (0, c, 0)),
            pl.BlockSpec((batch, g, GDN_HEADS, CHUNK), lambda c: (0, c, 0, 0)),
            pl.BlockSpec((1, GDN_D), lambda c: (0, 0)),
        ],
        out_specs=pl.BlockSpec((batch, g * CHUNK, hw), lambda c: (0, c, 0)),
        out_shape=jax.ShapeDtypeStruct((batch, seq, hw), BF16),
        scratch_shapes=[pltpu.VMEM((batch * GDN_HEADS, GDN_D, GDN_D), F32)],
        compiler_params=_cp(("arbitrary",)),
        name="gdn_chunk",
    )(qkvn3, qkvn3, qkvn3, p3, cols3, gct4, norm_w)
    return out.reshape(batch * seq, hw)


def _rope(x, cos, sin_signed):
    lane = lax.broadcasted_iota(jnp.int32, x.shape, 1)
    fwd = pltpu.roll(x, LANES - MLA_ROPE // 2, 1)
    bwd = pltpu.roll(x, MLA_ROPE // 2, 1)
    rot = jnp.where(lane < MLA_ROPE // 2, fwd, bwd)
    return x * cos + rot * sin_signed


def _mla_prep_kernel(cq_ref, ckv_ref, kr_ref, cos_ref, sin_ref, cost_ref, sint_ref, qnw_ref, kvnw_ref,
                     wqt_ref, wkn_ref, wvt_ref, qt_ref, kn_ref, kro_ref, vt_ref):
    cos = cos_ref[...]
    sin = sin_ref[...]
    cos_t = cost_ref[...]
    sin_t = sint_ref[...]
    cq = _rms(cq_ref[...].astype(F32), qnw_ref[...]).astype(BF16)
    hd = 2 * LANES
    half = MLA_ROPE // 2
    scale = (MLA_NOPE + MLA_ROPE) ** -0.5 * LOG2_E
    for h in range(MLA_HEADS):
        qh = _dot_nt(wqt_ref[h * hd:(h + 1) * hd, :], cq) * scale
        lo = qh[MLA_NOPE:MLA_NOPE + half, :]
        hi = qh[MLA_NOPE + half:MLA_NOPE + MLA_ROPE, :]
        qt_ref[0, h * hd:h * hd + MLA_NOPE, :] = qh[0:MLA_NOPE, :].astype(BF16)
        qt_ref[0, h * hd + MLA_NOPE:h * hd + MLA_NOPE + half, :] = (lo * cos_t - hi * sin_t).astype(BF16)
        qt_ref[0, h * hd + MLA_NOPE + half:h * hd + MLA_NOPE + MLA_ROPE, :] = (hi * cos_t + lo * sin_t).astype(BF16)
        qt_ref[0, h * hd + MLA_NOPE + MLA_ROPE:(h + 1) * hd, :] = qh[MLA_NOPE + MLA_ROPE:hd, :].astype(BF16)
    kvl = _rms(ckv_ref[...].astype(F32), kvnw_ref[...]).astype(BF16)
    kn_ref[...] = _dot(kvl, wkn_ref[...]).astype(BF16)
    vt_ref[0] = _dot_nt(wvt_ref[...], kvl).astype(BF16)
    kro_ref[...] = _rope(kr_ref[...].astype(F32), cos, sin).astype(BF16)


def _mla_prep(p, tables, qnw, kvnw, wqt, wkn, wvt, seq, tm=512):
    cos_row, sin_row, cos_col, sin_col = tables
    t = p.shape[0]
    tiles_per_seq = seq // tm
    hw = MLA_HEADS * MLA_NOPE
    half = MLA_ROPE // 2
    cq_blk = P_CQ // MLA_Q_LORA
    ckv_blk = P_CKV // MLA_KV_LORA
    kr_blk = P_KR // LANES
    return pl.pallas_call(
        _mla_prep_kernel,
        grid=(t // tm,),
        in_specs=[
            pl.BlockSpec((tm, MLA_Q_LORA), lambda i: (i, cq_blk)),
            pl.BlockSpec((tm, MLA_KV_LORA), lambda i: (i, ckv_blk)),
            pl.BlockSpec((tm, LANES), lambda i: (i, kr_blk)),
            pl.BlockSpec((tm, LANES), lambda i: (i % tiles_per_seq, 0)),
            pl.BlockSpec((tm, LANES), lambda i: (i % tiles_per_seq, 0)),
            pl.BlockSpec((half, tm), lambda i: (0, i % tiles_per_seq)),
            pl.BlockSpec((half, tm), lambda i: (0, i % tiles_per_seq)),
            pl.BlockSpec((1, MLA_Q_LORA), lambda i: (0, 0)),
            pl.BlockSpec((1, MLA_KV_LORA), lambda i: (0, 0)),
            pl.BlockSpec((2 * hw, MLA_Q_LORA), lambda i: (0, 0)),
            pl.BlockSpec((MLA_KV_LORA, hw), lambda i: (0, 0)),
            pl.BlockSpec((hw, MLA_KV_LORA), lambda i: (0, 0)),
        ],
        out_specs=[
            pl.BlockSpec((1, 2 * hw, tm), lambda i: (i, 0, 0)),
            pl.BlockSpec((tm, hw), lambda i: (i, 0)),
            pl.BlockSpec((tm, LANES), lambda i: (i, 0)),
            pl.BlockSpec((1, hw, tm), lambda i: (i, 0, 0)),
        ],
        out_shape=[
            jax.ShapeDtypeStruct((t // tm, 2 * hw, tm), BF16),
            jax.ShapeDtypeStruct((t, hw), BF16),
            jax.ShapeDtypeStruct((t, LANES), BF16),
            jax.ShapeDtypeStruct((t // tm, hw, tm), BF16),
        ],
        compiler_params=_cp(("parallel",)),
        name="mla_prep",
    )(p, p, p, cos_row, sin_row, cos_col, sin_col, qnw, kvnw, wqt, wkn, wvt)


ATTN_HEADS_PER_STEP = 8
ATTN_SUM_ROWS = 16


def _mla_attn_kernel(qt_ref, kt_ref, q_ref, kn_ref, kr_ref, vt_ref, o_ref, m_ref, acc_ref):
    qi = qt_ref[pl.program_id(2)]
    ki = kt_ref[pl.program_id(2)]
    tq = q_ref.shape[2]
    tk = kn_ref.shape[0]
    hd = 2 * LANES
    q0 = qi * tq
    k0 = ki * tk
    last_k = (q0 + tq) // tk - 1

    @pl.when(ki == 0)
    def _():
        m_ref[...] = jnp.full_like(m_ref, NEG_BIG)
        acc_ref[...] = jnp.zeros_like(acc_ref)

    def step(masked):
        kr = kr_ref[...]
        ones = jnp.ones((ATTN_SUM_ROWS, tk), BF16)
        if masked:
            ck = lax.broadcasted_iota(jnp.int32, (tk, tq), 0) // CHUNK
            cq = lax.broadcasted_iota(jnp.int32, (tk, tq), 1) // CHUNK
            visible = (ck <= cq) if tq == tk else ((ck - cq) <= (q0 - k0) // CHUNK)

        def scores(h):
            k = jnp.concatenate([kn_ref[:, h * MLA_NOPE:(h + 1) * MLA_NOPE], kr], axis=1)
            s = _dot(k, q_ref[0, h * hd:(h + 1) * hd, :])
            if masked:
                s = jnp.where(visible, s, NEG_BIG)
            return s

        def update(h, s):
            m_prev = m_ref[h]
            m_new = jnp.maximum(m_prev, jnp.max(s, axis=0, keepdims=True))
            alpha = jnp.exp2(m_prev - m_new)
            v_ext = jnp.concatenate([vt_ref[0, h * MLA_V:(h + 1) * MLA_V, :], ones], axis=0)
            hk = tk // 2
            pv = _dot(v_ext[:, :hk], jnp.exp2((s[:hk] - m_new).astype(BF16)))
            pv = pv + _dot(v_ext[:, hk:], jnp.exp2((s[hk:] - m_new).astype(BF16)))
            acc_ref[h] = alpha * acc_ref[h] + pv
            m_ref[h] = m_new

        s_prev = scores(0)
        for h in range(1, ATTN_HEADS_PER_STEP):
            s_next = scores(h)
            update(h - 1, s_prev)
            s_prev = s_next
        update(ATTN_HEADS_PER_STEP - 1, s_prev)

    @pl.when(k0 + tk <= q0)
    def _():
        step(False)

    def finish():
        for h in range(ATTN_HEADS_PER_STEP):
            acc = acc_ref[h]
            o = acc[0:MLA_V, :] / acc[MLA_V:MLA_V + 1, :]
            o_ref[:, h * MLA_V:(h + 1) * MLA_V] = o.T.astype(BF16)

    @pl.when(k0 + tk > q0)
    def _():
        step(True)
        if tq == tk:
            finish()
        else:
            pl.when(ki == last_k)(finish)


def _mla_attn(qt_all, kn, kr, vt, batch, seq):
    t = kn.shape[0]
    tq = qt_all.shape[2]
    tk = vt.shape[2]
    nq = seq // tq
    nk = seq // tk
    hps = ATTN_HEADS_PER_STEP
    pairs = [(qi, ki) for qi in range(nq) for ki in range((qi + 1) * tq // tk)]
    qt = jnp.asarray(np.array([pr[0] for pr in pairs], np.int32))
    kt = jnp.asarray(np.array([pr[1] for pr in pairs], np.int32))
    return pl.pallas_call(
        _mla_attn_kernel,
        grid_spec=pltpu.PrefetchScalarGridSpec(
            num_scalar_prefetch=2,
            grid=(batch, MLA_HEADS // hps, len(pairs)),
            in_specs=[
                pl.BlockSpec((1, hps * 2 * LANES, tq), lambda b, h, pr, qt, kt: (b * nq + qt[pr], h, 0)),
                pl.BlockSpec((tk, hps * MLA_NOPE), lambda b, h, pr, qt, kt: (b * nk + kt[pr], h)),
                pl.BlockSpec((tk, LANES), lambda b, h, pr, qt, kt: (b * nk + kt[pr], 0)),
                pl.BlockSpec((1, hps * MLA_V, tk), lambda b, h, pr, qt, kt: (b * nk + kt[pr], h, 0)),
            ],
            out_specs=pl.BlockSpec((tq, hps * MLA_V), lambda b, h, pr, qt, kt: (b * nq + qt[pr], h)),
            scratch_shapes=[
                pltpu.VMEM((hps, 1, tq), F32),
                pltpu.VMEM((hps, MLA_V + ATTN_SUM_ROWS, tq), F32),
            ],
        ),
        out_shape=jax.ShapeDtypeStruct((t, MLA_HEADS * MLA_V), BF16),
        compiler_params=_cp(("parallel", "parallel", "arbitrary")),
        name="mla_attn",
    )(qt, kt, qt_all, kn, kr, vt)


MIX_SUBTILES = 2


def _mix_out_kernel(x_ref, oa_ref, ob_ref, ga_ref, gb_ref, wga_ref, wmo_ref, wout_ref, nw_ref,
                    wr_hi_ref, wr_lo_ref, br_ref, x1_ref, h2_ref, selt_ref, cwt_ref):
    sub = br_ref.shape[1]

    def mix(r):
        ya = _dot(oa_ref[r, :], wga_ref[...])
        yb = _dot(ob_ref[r, :], wmo_ref[...])
        merged = (jax.nn.sigmoid(ga_ref[r, :].astype(F32)) * ya
                  + jax.nn.sigmoid(gb_ref[r, :].astype(F32)) * yb)
        x1 = x_ref[r, :] + _dot(merged.astype(BF16), wout_ref[...])
        x1_ref[r, :] = x1
        h2 = _rms(x1, nw_ref[...])
        h_hi = h2.astype(BF16)
        h2_ref[r, :] = h_hi
        h_lo = (h2 - h_hi.astype(F32)).astype(BF16)
        return (_dot_nt(wr_hi_ref[...], h_hi) + _dot_nt(wr_lo_ref[...], h_hi) + _dot_nt(wr_hi_ref[...], h_lo)
                + br_ref[...])[0:N_EXPERTS, :]

    def route(logits, r):
        expert = lax.broadcasted_iota(jnp.int32, logits.shape, 0)
        work = logits
        sel = jnp.zeros(logits.shape, F32)
        cw = jnp.zeros(logits.shape, F32)
        top = None
        denom = None
        for kk in range(TOP_K):
            mx = jnp.max(work, axis=0, keepdims=True)
            am = jnp.min(jnp.where(work == mx, expert, N_EXPERTS), axis=0, keepdims=True)
            hit = expert == am
            if kk == 0:
                top = mx
                e = jnp.ones_like(mx)
                denom = e
            else:
                e = jnp.exp(mx - top)
                denom = denom + e
            sel = jnp.where(hit, 1.0, sel)
            cw = jnp.where(hit, e, cw)
            work = jnp.where(hit, -jnp.inf, work)
        selt_ref[:, r] = sel.astype(BF16)
        cwt_ref[:, r] = cw / denom

    subs = [slice(n * sub, (n + 1) * sub) for n in range(x_ref.shape[0] // sub)]
    logits = [mix(r) for r in subs]
    for lg, r in zip(logits, subs):
        route(lg, r)


def _mix_out(x2, oa, ob, p, wga, wmo, wout, nw, wr_hi, wr_lo, br):
    t, d = x2.shape
    tm = MIX_SUBTILES * br.shape[1]
    full = lambda i: (0, 0)
    return pl.pallas_call(
        _mix_out_kernel,
        grid=(t // tm,),
        in_specs=[
            pl.BlockSpec((tm, d), lambda i: (i, 0)),
            pl.BlockSpec((tm, d), lambda i: (i, 0)),
            pl.BlockSpec((tm, d), lambda i: (i, 0)),
            pl.BlockSpec((tm, d), lambda i: (i, P_GATE_A // P_GROUP)),
            pl.BlockSpec((tm, d), lambda i: (i, P_GATE_B // P_GROUP)),
            pl.BlockSpec((d, d), full),
            pl.BlockSpec((d, d), full),
            pl.BlockSpec((d, d), full),
            pl.BlockSpec((1, d), full),
            pl.BlockSpec((LANES, d), full),
            pl.BlockSpec((LANES, d), full),
            pl.BlockSpec(br.shape, full),
        ],
        out_specs=[
            pl.BlockSpec((tm, d), lambda i: (i, 0)),
            pl.BlockSpec((tm, d), lambda i: (i, 0)),
            pl.BlockSpec((N_EXPERTS, tm), lambda i: (0, i)),
            pl.BlockSpec((N_EXPERTS, tm), lambda i: (0, i)),
        ],
        out_shape=[
            jax.ShapeDtypeStruct((t, d), F32),
            jax.ShapeDtypeStruct((t, d), BF16),
            jax.ShapeDtypeStruct((N_EXPERTS, t), BF16),
            jax.ShapeDtypeStruct((N_EXPERTS, t), F32),
        ],
        compiler_params=_cp(("parallel",)),
        name="mix_out",
    )(x2, oa, ob, p, p, wga, wmo, wout, nw, wr_hi, wr_lo, br)


def _route_pos_kernel(selt_ref, lpos_ref, keyt_ref, offs_ref, cnt_ref, tot_ref, carry_ref):
    tm = selt_ref.shape[1]
    i = pl.program_id(0)

    @pl.when(i == 0)
    def _():
        carry_ref[...] = jnp.zeros_like(carry_ref)

    sel_t = selt_ref[...]
    sel_rows = jnp.concatenate([sel_t, jnp.zeros((LANES - N_EXPERTS, tm), BF16)], axis=0)
    row = lax.broadcasted_iota(jnp.int32, (tm, tm), 0)
    col = lax.broadcasted_iota(jnp.int32, (tm, tm), 1)
    lpos_ref[...] = _dot_nt((col < row).astype(BF16), sel_rows)
    pos_t = _dot(sel_t, (row < col).astype(BF16))
    keyt_ref[0] = jnp.where(sel_t > 0, pos_t, -1.0)

    n = _dot_nt(jnp.ones((8, tm), BF16), sel_rows)[0:1, :]
    carry = carry_ref[0:1, :]
    offs_ref[0] = carry.astype(jnp.int32)
    cnt_ref[0] = n.astype(jnp.int32)
    total = carry + jnp.ceil(n * (1.0 / SEG_ALIGN)) * SEG_ALIGN
    carry_ref[...] = jnp.broadcast_to(total, carry_ref.shape)
    tot_ref[...] = jnp.broadcast_to(total, tot_ref.shape).astype(jnp.int32)


def _route_pos(sel_t):
    t = sel_t.shape[1]
    tm = ROUTE_TILE
    nt = t // tm
    return pl.pallas_call(
        _route_pos_kernel,
        grid=(nt,),
        in_specs=[pl.BlockSpec((N_EXPERTS, tm), lambda i: (0, i))],
        out_specs=[
            pl.BlockSpec((tm, LANES), lambda i: (i, 0)),
            pl.BlockSpec((1, N_EXPERTS, tm), lambda i: (i, 0, 0)),
            pl.BlockSpec((1, 1, LANES), lambda i: (i, 0, 0)),
            pl.BlockSpec((1, 1, LANES), lambda i: (i, 0, 0)),
            pl.BlockSpec((8, LANES), lambda i: (0, 0)),
        ],
        out_shape=[
            jax.ShapeDtypeStruct((t, LANES), F32),
            jax.ShapeDtypeStruct((nt, N_EXPERTS, tm), F32),
            jax.ShapeDtypeStruct((nt, 1, LANES), jnp.int32),
            jax.ShapeDtypeStruct((nt, 1, LANES), jnp.int32),
            jax.ShapeDtypeStruct((8, LANES), jnp.int32),
        ],
        scratch_shapes=[pltpu.VMEM((8, LANES), F32)],
        compiler_params=_cp(("arbitrary",)),
        name="route_pos",
    )(sel_t)


def _rows(ref, start, n):
    return ref.at[pl.ds(pl.multiple_of(start, n), n)]


def _pack_pairs(x):
    half = x.shape[1] // 2
    hi = lax.bitcast_convert_type(x[:, :half], jnp.uint32)
    lo = lax.bitcast_convert_type(x[:, half:], jnp.uint32)
    return hi | lax.shift_right_logical(lo, jnp.uint32(16))


def _unpack_pairs(u):
    hi = lax.bitcast_convert_type(u & jnp.uint32(0xFFFF0000), F32).astype(BF16)
    lo = lax.bitcast_convert_type(lax.shift_left(u, jnp.uint32(16)), F32).astype(BF16)
    return hi, lo


def _seg_windows(cnt_ref, base):
    longest = lax.fori_loop(0, N_EXPERTS, lambda e, m: jnp.maximum(m, cnt_ref[base + e]), 0)
    return lax.shift_right_logical(longest + (SEG_WINDOW - 1), SEG_WINDOW.bit_length() - 1)


def _dispatch_kernel(seg_ref, cnt_ref, fill_lo_ref, fill_hi_ref, h_ref, keyt_ref, xs_ref,
                     stage_ref, zero_ref, sem):
    i = pl.program_id(0)
    tm = h_ref.shape[0]
    base = i * N_EXPERTS

    def zero_fills(wait):
        def act(cp):
            if wait:
                cp.wait()
            else:
                cp.start()

        def fill(c):
            return pltpu.make_async_copy(zero_ref.at[pl.ds(0, SEG_ALIGN)], _rows(xs_ref, c * SEG_ALIGN, SEG_ALIGN),
                                         sem.at[2])

        def per_expert(e, carry):
            lo = lax.shift_right_logical(fill_lo_ref[e], SEG_ALIGN.bit_length() - 1)
            hi = lax.shift_right_logical(fill_hi_ref[e], SEG_ALIGN.bit_length() - 1)
            return lax.fori_loop(lo, hi, lambda c, a: (act(fill(c)), a)[1], carry)

        lax.fori_loop(0, N_EXPERTS, per_expert, 0)

        def fill_tail(c):
            return pltpu.make_async_copy(zero_ref, _rows(xs_ref, c * ZERO_ROWS, ZERO_ROWS), sem.at[2])

        lo = lax.shift_right_logical(fill_hi_ref[N_EXPERTS - 1], ZERO_ROWS.bit_length() - 1)
        hi = xs_ref.shape[0] // ZERO_ROWS
        lax.fori_loop(lo, hi, lambda c, a: (act(fill_tail(c)), a)[1], 0)

    @pl.when(i == 0)
    def _():
        zero_ref[...] = jnp.zeros_like(zero_ref)
        zero_fills(wait=False)
        zero_fills(wait=True)

    buf = i % 2

    def send(tile_base, win, b, wait, only_live=False):
        for e in range(N_EXPERTS):
            def go(e=e):
                slot = seg_ref[tile_base + e] + win * SEG_WINDOW
                cp = pltpu.make_async_copy(stage_ref.at[b, pl.ds(e * SEG_WINDOW, SEG_WINDOW)],
                                           xs_ref.at[pl.ds(pl.multiple_of(slot, SEG_ALIGN), SEG_WINDOW)],
                                           sem.at[b])
                if wait:
                    cp.wait()
                else:
                    cp.start()

            if only_live:
                pl.when(cnt_ref[tile_base + e] > win * SEG_WINDOW)(go)
            else:
                go()

    half = N_EXPERTS // 2
    j = lax.broadcasted_iota(jnp.int32, (SEG_WINDOW, tm), 0).astype(F32)

    def build(win, b):
        key = keyt_ref[0] - jnp.asarray(win * SEG_WINDOW, F32)
        for hf in range(2):
            pick = jnp.concatenate([(key[e:e + 1, :] == j).astype(BF16)
                                    for e in range(hf * half, (hf + 1) * half)], axis=0)
            stage_ref[b, hf * half * SEG_WINDOW:(hf + 1) * half * SEG_WINDOW, :] = _pack_pairs(_dot(pick, h_ref[...]))

    n_win = _seg_windows(cnt_ref, base)
    build(0, buf)

    @pl.when(i > 0)
    def _():
        send(base - N_EXPERTS, 0, 1 - buf, wait=True)

    send(base, 0, buf, wait=False)

    def more(win, carry):
        build(win, 1 - buf)
        send(base, win, 1 - buf, wait=False, only_live=True)
        send(base, win, 1 - buf, wait=True, only_live=True)
        return carry

    lax.fori_loop(1, n_win, more, 0)

    @pl.when(i == pl.num_programs(0) - 1)
    def _():
        send(base, 0, buf, wait=True)


def _dispatch(seg, cnt, fill_lo, fill_hi, h2, keyt, n_pad):
    t, d = h2.shape
    tm = ROUTE_TILE
    return pl.pallas_call(
        _dispatch_kernel,
        grid_spec=pltpu.PrefetchScalarGridSpec(
            num_scalar_prefetch=4,
            grid=(t // tm,),
            in_specs=[
                pl.BlockSpec((tm, d), lambda i, *_: (i, 0)),
                pl.BlockSpec((1, N_EXPERTS, tm), lambda i, *_: (i, 0, 0)),
            ],
            out_specs=pl.BlockSpec(memory_space=pl.ANY),
            scratch_shapes=[pltpu.VMEM((2, N_EXPERTS * SEG_WINDOW, d // 2), jnp.uint32),
                            pltpu.VMEM((ZERO_ROWS, d // 2), jnp.uint32),
                            pltpu.SemaphoreType.DMA((3,))],
        ),
        out_shape=jax.ShapeDtypeStruct((n_pad, d // 2), jnp.uint32),
        compiler_params=_cp(("arbitrary",)),
        name="dispatch",
    )(seg, cnt, fill_lo, fill_hi, h2, keyt)


def _experts_kernel(be_ref, nv_ref, xs_ref, wgu_ref, wd_ref, bg_ref, bu_ref, bd_ref, ys_ref,
                    wg_s, wu_s, wd_s):
    j = pl.program_id(0)
    grp = 2 * LANES
    prev = be_ref[jnp.maximum(j - 1, 0)]

    @pl.when((j == 0) | (be_ref[j] != prev))
    def _():
        r = lax.broadcasted_iota(jnp.int32, (grp, grp), 0)
        c = lax.broadcasted_iota(jnp.int32, (grp, grp), 1)
        src = jnp.where(c < LANES, 2 * c, 2 * (c - LANES) + 1)
        pick = (r == src).astype(BF16)
        for g in range(wgu_ref.shape[2] // grp):
            y = _dot(wgu_ref[0, :, g * grp:(g + 1) * grp].astype(BF16), pick)
            wg_s[:, g * LANES:(g + 1) * LANES] = y[:, 0:LANES].astype(BF16)
            wu_s[:, g * LANES:(g + 1) * LANES] = y[:, LANES:grp].astype(BF16)
        wd_s[...] = wd_ref[0].astype(BF16)

    @pl.when(j < nv_ref[0])
    def _():
        x = jnp.concatenate(_unpack_pairs(xs_ref[...]), axis=1)
        g = _dot(x, wg_s[...]) + bg_ref[0]
        u = _dot(x, wu_s[...]) + bu_ref[0]
        gate = jnp.minimum(g, SWIGLU_LIMIT)
        up = jnp.clip(u, -SWIGLU_LIMIT, SWIGLU_LIMIT)
        act = (up + 1.0) * (gate * jax.nn.sigmoid(gate * SWIGLU_ALPHA))
        y = _dot(act.astype(BF16), wd_s[...]) + bd_ref[0]
        ys_ref[...] = _pack_pairs(y.astype(BF16).astype(F32))

    @pl.when(j >= nv_ref[0])
    def _():
        ys_ref[...] = jnp.zeros_like(ys_ref)


def _experts(block_e, n_valid, xs, wgu, wd, bg, bu, bd):
    n_pad = xs.shape[0]
    de, d = wd.shape[1:]
    blk = (MOE_ROWS, xs.shape[1])
    n_blocks = n_pad // MOE_ROWS
    xrow = lambda j, be, nv: (jnp.minimum(j, nv[0] - 1), 0)
    wsel = lambda j, be, nv: (be[j], 0, 0)
    return pl.pallas_call(
        _experts_kernel,
        grid_spec=pltpu.PrefetchScalarGridSpec(
            num_scalar_prefetch=2,
            grid=(n_blocks,),
            in_specs=[
                pl.BlockSpec(blk, xrow),
                pl.BlockSpec((1, d, 2 * de), wsel),
                pl.BlockSpec((1, de, d), wsel),
                pl.BlockSpec((1, 1, de), wsel),
                pl.BlockSpec((1, 1, de), wsel),
                pl.BlockSpec((1, 1, d), wsel),
            ],
            out_specs=pl.BlockSpec(blk, lambda j, be, nv: (j, 0)),
            scratch_shapes=[
                pltpu.VMEM((d, de), BF16),
                pltpu.VMEM((d, de), BF16),
                pltpu.VMEM((de, d), BF16),
            ],
        ),
        out_shape=jax.ShapeDtypeStruct(xs.shape, xs.dtype),
        compiler_params=pltpu.CompilerParams(dimension_semantics=("arbitrary",),
                                             vmem_limit_bytes=EXPERTS_VMEM_LIMIT),
        name="experts",
    )(block_e, n_valid, xs, wgu, wd, bg, bu, bd)


def _combine_kernel(seg_ref, cnt_ref, x1_ref, lpos_ref, cw_ref, nw_ref, ys_ref, o_ref, stage_ref, sem,
                    *, final_norm):
    i = pl.program_id(0)
    tm, d = x1_ref.shape
    base = i * N_EXPERTS

    buf = i % 2

    def gather(tile_base, win, b, wait):
        for e in range(N_EXPERTS):
            slot = seg_ref[tile_base + e] + win * SEG_WINDOW
            cp = pltpu.make_async_copy(ys_ref.at[pl.ds(pl.multiple_of(slot, SEG_ALIGN), SEG_WINDOW)],
                                       stage_ref.at[b, pl.ds(e * SEG_WINDOW, SEG_WINDOW)], sem.at[b])
            if wait:
                cp.wait()
            else:
                cp.start()

    @pl.when(i == 0)
    def _():
        gather(base, 0, buf, wait=False)

    @pl.when(i + 1 < pl.num_programs(0))
    def _():
        gather(base + N_EXPERTS, 0, 1 - buf, wait=False)

    n_stage = N_EXPERTS * SEG_WINDOW
    owner = lax.broadcasted_iota(jnp.int32, (LANES, n_stage), 1) // SEG_WINDOW
    expand = (owner == lax.broadcasted_iota(jnp.int32, (LANES, n_stage), 0)).astype(BF16)
    j = (lax.broadcasted_iota(jnp.int32, (tm, n_stage), 1) % SEG_WINDOW).astype(F32)
    cw_rows = jnp.concatenate([cw_ref[...].astype(BF16), jnp.zeros((LANES - N_EXPERTS, tm), BF16)], axis=0)
    cw_wide = _dot_tn(cw_rows, expand)

    def window(win, y):
        rank = (lpos_ref[...] - jnp.asarray(win * SEG_WINDOW, F32)).astype(BF16)
        take = jnp.where(_dot(rank, expand) == j, cw_wide, 0.0).astype(BF16)
        gather(base, win, buf, wait=True)
        hi, lo = _unpack_pairs(stage_ref[buf])
        return y + jnp.concatenate([_dot(take, hi), _dot(take, lo)], axis=1)

    def more(win, y):
        gather(base, win, buf, wait=False)
        return window(win, y)

    y = window(0, jnp.zeros((tm, d), F32))
    y = lax.fori_loop(1, _seg_windows(cnt_ref, base), more, y)
    out = x1_ref[...] + y
    if final_norm:
        out = _rms(out, nw_ref[...])
    o_ref[...] = out


def _combine(seg, cnt, x1, lpos, cw, nw, ys, final_norm):
    t, d = x1.shape
    tm = ROUTE_TILE
    kern = functools.partial(_combine_kernel, final_norm=final_norm)
    return pl.pallas_call(
        kern,
        grid_spec=pltpu.PrefetchScalarGridSpec(
            num_scalar_prefetch=2,
            grid=(t // tm,),
            in_specs=[
                pl.BlockSpec((tm, d), lambda i, *_: (i, 0)),
                pl.BlockSpec((tm, LANES), lambda i, *_: (i, 0)),
                pl.BlockSpec((N_EXPERTS, tm), lambda i, *_: (0, i)),
                pl.BlockSpec((1, d), lambda i, *_: (0, 0)),
                pl.BlockSpec(memory_space=pl.ANY),
            ],
            out_specs=pl.BlockSpec((tm, d), lambda i, *_: (i, 0)),
            scratch_shapes=[pltpu.VMEM((2, N_EXPERTS * SEG_WINDOW, d // 2), jnp.uint32),
                            pltpu.SemaphoreType.DMA((2,))],
        ),
        out_shape=jax.ShapeDtypeStruct((t, d), F32),
        compiler_params=_cp(("arbitrary",)),
        name="combine",
    )(seg, cnt, x1, lpos, cw, nw, ys)


def _rope_tables(seq):
    inv = 1.0 / (ROPE_THETA ** (jnp.arange(0, MLA_ROPE, 2, dtype=F32) / MLA_ROPE))
    ang = jnp.arange(seq, dtype=F32)[:, None] * inv[None, :]
    cos, sin = jnp.cos(ang), jnp.sin(ang)
    zeros = jnp.zeros((seq, LANES - MLA_ROPE), F32)
    cos_row = jnp.concatenate([cos, cos, zeros], axis=-1)
    sin_row = jnp.concatenate([-sin, sin, zeros], axis=-1)
    return cos_row, sin_row, cos.T, sin.T


def _pad_cols(a, width):
    return jnp.pad(a, ((0, 0), (0, width - a.shape[1])))


def _layer(x2, batch, seq, final_norm_w, final_norm, rope_tables,
           norm_mix_w, w_in, gdn_conv_w, gdn_a_log, gdn_dt_bias, gdn_norm_w, w_gdn_o,
           mla_q_norm_w, w_mla_q_b, mla_kv_norm_w, w_mla_kv_b, w_mla_o, w_out,
           norm_ffn_w, w_router, b_router, w_gate_up, b_gate_up, w_down, b_down):
    t, d = x2.shape
    qk_w = GDN_HEADS * GDN_D
    assert d == P_GROUP and qk_w == P_GROUP
    o_b = 4 * qk_w
    o_a = o_b + GDN_HEADS
    o_cq = o_a + GDN_HEADS
    o_ckv = o_cq + MLA_Q_LORA
    o_kr = o_ckv + MLA_KV_LORA
    o_ga = o_kr + MLA_ROPE
    o_gb = o_ga + d
    w_p = jnp.concatenate([
        w_in[:, 0:o_b], w_in[:, o_ga:o_gb + d], w_in[:, o_cq:o_ckv], w_in[:, o_ckv:o_kr],
        _pad_cols(w_in[:, o_kr:o_ga], P_WIDTH - P_KR)], axis=1).astype(BF16)
    w_ab = _pad_cols(jnp.concatenate([w_in[:, o_a:o_cq], w_in[:, o_b:o_a]], axis=1), LANES).astype(BF16)

    p, ab = _in_proj(x2, norm_mix_w[None, :], w_p, w_ab)

    alog_row = _pad_cols(gdn_a_log[None, :].astype(F32), LANES)
    dtb_row = _pad_cols(gdn_dt_bias[None, :].astype(F32), LANES)
    qkvn, cols, gct = _gdn_prep(p, ab, gdn_conv_w.astype(F32), alog_row, dtb_row, seq)
    o_gdn = _gdn_chunk(qkvn, p, cols, gct, gdn_norm_w[None, :].astype(F32), batch, seq)

    hd = MLA_NOPE + MLA_ROPE
    wq = w_mla_q_b.reshape(MLA_Q_LORA, MLA_HEADS, hd)
    wqt = jnp.pad(wq, ((0, 0), (0, 0), (0, 2 * LANES - hd))).reshape(MLA_Q_LORA, MLA_HEADS * 2 * LANES).T
    wkv = w_mla_kv_b.reshape(MLA_KV_LORA, MLA_HEADS, MLA_NOPE + MLA_V)
    wkn = wkv[:, :, :MLA_NOPE].reshape(MLA_KV_LORA, -1)
    wvt = wkv[:, :, MLA_NOPE:].reshape(MLA_KV_LORA, -1).T
    qt, kn, kr, vt = _mla_prep(p, rope_tables, mla_q_norm_w[None, :].astype(F32),
                               mla_kv_norm_w[None, :].astype(F32), wqt.astype(BF16), wkn.astype(BF16),
                               wvt.astype(BF16), seq)
    o_mla = _mla_attn(qt, kn, kr, vt, batch, seq)

    wr = _pad_cols(w_router.astype(F32), LANES).T
    wr_hi = wr.astype(BF16)
    wr_lo = (wr - wr_hi.astype(F32)).astype(BF16)
    br = jnp.broadcast_to(_pad_cols(b_router[None, :].astype(F32), LANES).T, (LANES, ROUTE_TILE))
    x1, h2, sel, cw = _mix_out(x2, o_gdn, o_mla, p, w_gdn_o.astype(BF16), w_mla_o.astype(BF16),
                               w_out.astype(BF16), norm_ffn_w[None, :].astype(F32), wr_hi, wr_lo, br)

    lpos, keyt, offs, cnt, tot = _route_pos(sel)
    n_tiles = t // ROUTE_TILE
    used = tot[0, :N_EXPERTS]
    padded = (used + SEG_WINDOW + MOE_ROWS - 1) // MOE_ROWS * MOE_ROWS
    pad_end = jnp.cumsum(padded)
    pad_start = pad_end - padded
    seg = (pad_start[None, :] + offs[:, 0, :N_EXPERTS]).astype(jnp.int32).reshape(-1)
    cnt = cnt[:, 0, :N_EXPERTS].reshape(-1)
    worst_used = t * TOP_K + n_tiles * N_EXPERTS * (SEG_ALIGN - 1) + N_EXPERTS * SEG_WINDOW
    n_pad = -(-worst_used // MOE_ROWS) * MOE_ROWS + (N_EXPERTS + 1) * MOE_ROWS
    n_blocks = n_pad // MOE_ROWS
    blk_start = jnp.arange(n_blocks, dtype=jnp.int32) * MOE_ROWS
    block_e = jnp.minimum(jnp.sum((pad_end[None, :] <= blk_start[:, None]).astype(jnp.int32), axis=1),
                          N_EXPERTS - 1).astype(jnp.int32)
    n_valid = (pad_end[-1:] // MOE_ROWS).astype(jnp.int32)

    xs = _dispatch(seg, cnt, (pad_start + used).astype(jnp.int32), pad_end.astype(jnp.int32), h2, keyt, n_pad)
    bg = b_gate_up[:, None, 0::2].astype(F32)
    bu = b_gate_up[:, None, 1::2].astype(F32)
    ys = _experts(block_e, n_valid, xs, w_gate_up, w_down, bg, bu, b_down[:, None, :].astype(F32))
    return _combine(seg, cnt, x1, lpos, cw, final_norm_w[None, :].astype(F32), ys, final_norm)


def kernel(x, norm_mix_w, w_in, gdn_conv_w, gdn_a_log, gdn_dt_bias, gdn_norm_w, w_gdn_o, mla_q_norm_w, w_mla_q_b, mla_kv_norm_w, w_mla_kv_b, w_mla_o, w_out, norm_ffn_w, w_router, b_router, w_gate_up, b_gate_up, w_down, b_down, norm_final_w):
    batch, seq, d = x.shape
    depth = w_in.shape[0]
    rope_tables = _rope_tables(seq)
    x2 = x.reshape(batch * seq, d)
    for layer in range(depth):
        x2 = _layer(x2, batch, seq, norm_final_w, layer == depth - 1, rope_tables,
                    norm_mix_w[layer], w_in[layer], gdn_conv_w[layer], gdn_a_log[layer],
                    gdn_dt_bias[layer], gdn_norm_w[layer], w_gdn_o[layer], mla_q_norm_w[layer],
                    w_mla_q_b[layer], mla_kv_norm_w[layer], w_mla_kv_b[layer], w_mla_o[layer],
                    w_out[layer], norm_ffn_w[layer], w_router[layer], b_router[layer],
                    w_gate_up[layer], b_gate_up[layer], w_down[layer], b_down[layer])
    return x2.reshape(batch, seq, d)
```

```python
import functools

import jax
import jax.numpy as jnp
import numpy as np
from jax import lax
from jax.experimental import pallas as pl
from jax.experimental.pallas import tpu as pltpu

F32 = jnp.float32
BF16 = jnp.bfloat16

CHUNK = 64
NORM_EPS = 1e-6
GDN_HEADS = 8
GDN_D = 128
GDN_CONV = 4
MLA_HEADS = 8
MLA_Q_LORA = 512
MLA_KV_LORA = 256
MLA_NOPE = 128
MLA_ROPE = 64
MLA_V = 128
ROPE_THETA = 10000.0
N_EXPERTS = 32
TOP_K = 4
SWIGLU_LIMIT = 7.0
SWIGLU_ALPHA = 1.702

LANES = 128
MOE_ROWS = 512
ROUTE_TILE = 256
SEG_ALIGN = 8
SEG_WINDOW = 64
ZERO_ROWS = 64
VMEM_LIMIT = 48 * 1024 * 1024
EXPERTS_VMEM_LIMIT = 56 * 1024 * 1024

P_GROUP = GDN_HEADS * GDN_D
P_Q, P_K, P_V, P_Z, P_GATE_A, P_GATE_B = (n * P_GROUP for n in range(6))
P_CQ = 6 * P_GROUP
P_CKV = P_CQ + MLA_Q_LORA
P_KR = P_CKV + MLA_KV_LORA
P_WIDTH = 7 * P_GROUP

NEG_BIG = -1e30
LOG2_E = 1.4426950408889634


def _cp(sem):
    return pltpu.CompilerParams(dimension_semantics=sem, vmem_limit_bytes=VMEM_LIMIT)


def _dot(a, b):
    return jnp.dot(a, b, preferred_element_type=F32)


def _dot_nt(a, b):
    return lax.dot_general(a, b, (((1,), (1,)), ((), ())), preferred_element_type=F32)


def _dot_tn(a, b):
    return lax.dot_general(a, b, (((0,), (0,)), ((), ())), preferred_element_type=F32)


def _split3(x):
    hi = x.astype(BF16)
    r = x - hi.astype(F32)
    mid = r.astype(BF16)
    lo = (r - mid.astype(F32)).astype(BF16)
    return hi, mid, lo


def _sigmoid(x):
    return 0.5 + 0.5 * jnp.tanh(0.5 * x)


def _rms(x, w):
    ms = jnp.mean(x * x, axis=-1, keepdims=True)
    return x * lax.rsqrt(ms + NORM_EPS) * w


def _in_proj_kernel(x_ref, nw_ref, w_ref, wab_ref, p_ref, ab_ref, h_ref):
    @pl.when(pl.program_id(1) == 0)
    def _():
        hb = _rms(x_ref[...], nw_ref[...]).astype(BF16)
        h_ref[...] = hb
        ab_ref[...] = _dot(hb, wab_ref[...])

    p_ref[...] = _dot(h_ref[...], w_ref[...]).astype(BF16)


def _in_proj(x2, norm_w, w_p, w_ab, tm=1024, tn=3584):
    t, d = x2.shape
    n = w_p.shape[1]
    return pl.pallas_call(
        _in_proj_kernel,
        grid=(t // tm, n // tn),
        in_specs=[
            pl.BlockSpec((tm, d), lambda i, j: (i, 0)),
            pl.BlockSpec((1, d), lambda i, j: (0, 0)),
            pl.BlockSpec((d, tn), lambda i, j: (0, j)),
            pl.BlockSpec((d, LANES), lambda i, j: (0, 0)),
        ],
        out_specs=[
            pl.BlockSpec((tm, tn), lambda i, j: (i, j)),
            pl.BlockSpec((tm, LANES), lambda i, j: (i, 0)),
        ],
        out_shape=[
            jax.ShapeDtypeStruct((t, n), BF16),
            jax.ShapeDtypeStruct((t, LANES), F32),
        ],
        scratch_shapes=[pltpu.VMEM((tm, d), BF16)],
        compiler_params=_cp(("parallel", "arbitrary")),
        name="in_proj",
    )(x2, norm_w, w_p, w_ab)


def _gdn_prep_kernel(cur_ref, prev_ref, ab_ref, cw_ref, alog_ref, dtb_ref,
                     qkv_ref, cols_ref, gct_ref, *, tiles_per_seq):
    tm = cur_ref.shape[0]
    i = pl.program_id(0)
    halo_on = (i % tiles_per_seq) != 0
    q_scale = GDN_D ** -0.5
    grp = 2 * LANES
    row = lax.broadcasted_iota(jnp.int32, (tm, tm), 0)
    col = lax.broadcasted_iota(jnp.int32, (tm, tm), 1)
    shift = [(col == row - s).astype(BF16) for s in range(1, GDN_CONV)]
    for cg in range(cur_ref.shape[1] // grp):
        gs = slice(cg * grp, (cg + 1) * grp)
        cur_b = cur_ref[:, gs]
        cur = cur_b.astype(F32)
        w = cw_ref[:, gs]
        y = w[GDN_CONV - 1:GDN_CONV, :] * cur
        for s in range(1, GDN_CONV):
            y = y + w[GDN_CONV - 1 - s:GDN_CONV - s, :] * _dot(shift[s - 1], cur_b)
        halo = jnp.where(halo_on, prev_ref[:, gs].astype(F32)[8:16, :], 0.0)
        xe = jnp.concatenate([halo, cur[0:8, :]], axis=0)
        head = w[0:1, :] * xe[5:13, :]
        for j in range(1, GDN_CONV):
            head = head + w[j:j + 1, :] * xe[5 + j:13 + j, :]
        y = jnp.concatenate([head, y[8:, :]], axis=0)
        hy = 0.5 * y
        y = hy + hy * jnp.tanh(hy)
        for half in range(2):
            cb = 2 * cg + half
            yh = y[:, half * LANES:(half + 1) * LANES]
            if cb < 2 * GDN_HEADS:
                ss = jnp.sum(yh * yh, axis=-1, keepdims=True)
                yh = yh * lax.rsqrt(ss + NORM_EPS)
                if cb < GDN_HEADS:
                    yh = yh * q_scale
            qkv_ref[:, cb * LANES:(cb + 1) * LANES] = yh.astype(BF16)

    ab = ab_ref[...]
    g = -jnp.exp(alog_ref[...]) * jax.nn.softplus(ab + dtb_ref[...])
    row = lax.broadcasted_iota(jnp.int32, (tm, tm), 0)
    col = lax.broadcasted_iota(jnp.int32, (tm, tm), 1)
    tri = ((col <= row) & ((row // CHUNK) == (col // CHUNK))).astype(BF16)
    g_hi, g_mid, g_lo = _split3(g)
    gc = _dot(tri, g_hi) + _dot(tri, g_mid) + _dot(tri, g_lo)
    lane = lax.broadcasted_iota(jnp.int32, (tm, LANES), 1)
    cols_ref[...] = jnp.where(lane < GDN_HEADS, gc, jax.nn.sigmoid(ab))
    for c in range(tm // CHUNK):
        blk = gc[c * CHUNK:(c + 1) * CHUNK, :]
        blk = jnp.concatenate([blk, jnp.zeros_like(blk)], axis=0)
        gct_ref[c] = blk.T[0:GDN_HEADS, 0:CHUNK]


def _gdn_prep(p, ab, conv_w, alog_row, dtb_row, seq, tm=256):
    t = p.shape[0]
    cw = 3 * GDN_HEADS * GDN_D
    tiles_per_seq = seq // tm
    kern = functools.partial(_gdn_prep_kernel, tiles_per_seq=tiles_per_seq)
    return pl.pallas_call(
        kern,
        grid=(t // tm,),
        in_specs=[
            pl.BlockSpec((tm, cw), lambda i: (i, 0)),
            pl.BlockSpec((16, cw), lambda i: (jnp.maximum(i * (tm // 16) - 1, 0), 0)),
            pl.BlockSpec((tm, LANES), lambda i: (i, 0)),
            pl.BlockSpec((GDN_CONV, cw), lambda i: (0, 0)),
            pl.BlockSpec((1, LANES), lambda i: (0, 0)),
            pl.BlockSpec((1, LANES), lambda i: (0, 0)),
        ],
        out_specs=[
            pl.BlockSpec((tm, cw), lambda i: (i, 0)),
            pl.BlockSpec((tm, LANES), lambda i: (i, 0)),
            pl.BlockSpec((tm // CHUNK, GDN_HEADS, CHUNK), lambda i: (i, 0, 0)),
        ],
        out_shape=[
            jax.ShapeDtypeStruct((t, cw), BF16),
            jax.ShapeDtypeStruct((t, LANES), F32),
            jax.ShapeDtypeStruct((t // CHUNK, GDN_HEADS, CHUNK), F32),
        ],
        compiler_params=_cp(("parallel",)),
        name="gdn_prep",
    )(p, p, ab, conv_w, alog_row, dtb_row)


GDN_CHUNKS_PER_STEP = 2


def _gdn_chunk_kernel(q_ref, k_ref, v_ref, z_ref, cols_ref, gct_ref, nw_ref, o_ref, s_ref):
    c = CHUNK
    nb = q_ref.shape[0]
    chains = [(b, h) for b in range(nb) for h in range(GDN_HEADS)]
    units = [(g, b, h) for g in range(GDN_CHUNKS_PER_STEP) for b, h in chains]

    @pl.when(pl.program_id(0) == 0)
    def _():
        s_ref[...] = jnp.zeros_like(s_ref)

    ri = lax.broadcasted_iota(jnp.int32, (c, c), 0)
    ci = lax.broadcasted_iota(jnp.int32, (c, c), 1)
    incl = ri >= ci
    strict = ri > ci
    eye = (ri == ci).astype(F32)
    nw = nw_ref[...]

    def rows(g):
        return slice(g * c, (g + 1) * c)

    def head(h):
        return slice(h * GDN_D, (h + 1) * GDN_D)

    cols, e_g, e_kd, e_last, gct = {}, {}, {}, {}, {}
    for g in range(GDN_CHUNKS_PER_STEP):
        for b in range(nb):
            cb = cols_ref[b, rows(g), :]
            last = cb[c - 1:c, :]
            cols[g, b] = cb
            e_g[g, b] = jnp.exp(cb)
            e_kd[g, b] = jnp.exp(last - cb)
            e_last[g, b] = jnp.exp(last)
            gct[g, b] = gct_ref[b, g]

    kq, kb_l, kf_l = [], [], []
    for g, b, h in units:
        k = k_ref[b, rows(g), head(h)]
        kf = k.astype(F32)
        kb = kf * cols[g, b][:, GDN_HEADS + h:GDN_HEADS + h + 1]
        kq.append(_dot_nt(jnp.concatenate([kb.astype(BF16), q_ref[b, rows(g), head(h)]], axis=0), k))
        kb_l.append(kb)
        kf_l.append(kf)

    a_l, qk_l = [], []
    for i, (g, b, h) in enumerate(units):
        dec = jnp.exp(jnp.minimum(cols[g, b][:, h:h + 1] - gct[g, b][h:h + 1, :], 0.0))
        a_l.append(jnp.where(strict, -kq[i][0:c, :] * dec, 0.0))
        qk_l.append(jnp.where(incl, kq[i][c:2 * c, :] * dec, 0.0).astype(BF16))

    tinv = [eye + a for a in a_l]
    pw = a_l
    for _ in range(5):
        pwb = [x.astype(BF16) for x in pw]
        pw = [_dot(x, x) for x in pwb]
        tinv = [t + _dot(t.astype(BF16), x.astype(BF16)) for t, x in zip(tinv, pw)]

    u_l, lhs_l, kd_l = [], [], []
    for i, (g, b, h) in enumerate(units):
        beta = cols[g, b][:, GDN_HEADS + h:GDN_HEADS + h + 1]
        eg = e_g[g, b][:, h:h + 1]
        rhs = jnp.concatenate([v_ref[b, rows(g), head(h)].astype(F32) * beta, kb_l[i] * eg],
                              axis=1).astype(BF16)
        uw = _dot(tinv[i].astype(BF16), rhs)
        qd = (q_ref[b, rows(g), head(h)].astype(F32) * eg).astype(BF16)
        u_l.append(uw[:, 0:GDN_D])
        lhs_l.append(jnp.concatenate([uw[:, GDN_D:2 * GDN_D].astype(BF16), qd], axis=0))
        kd_l.append((kf_l[i] * e_kd[g, b][:, h:h + 1]).astype(BF16))

    for g in range(GDN_CHUNKS_PER_STEP):
        first = g * len(chains)
        r_l = [_dot(lhs_l[first + n], s_ref[n].astype(BF16)) for n in range(len(chains))]
        for n, (b, h) in enumerate(chains):
            i = first + n
            v_new = (u_l[i] - r_l[n][0:c, :]).astype(BF16)
            o = r_l[n][c:2 * c, :] + _dot(qk_l[i], v_new)
            s_ref[n] = s_ref[n] * e_last[g, b][:, h:h + 1] + _dot_tn(kd_l[i], v_new)
            z = z_ref[b, rows(g), head(h)].astype(F32)
            o_ref[b, rows(g), head(h)] = (_rms(o, nw) * (z * _sigmoid(z))).astype(BF16)


def _gdn_chunk(qkvn, p, cols, gct, norm_w, batch, seq):
    nc = seq // CHUNK
    hw = GDN_HEADS * GDN_D
    g = GDN_CHUNKS_PER_STEP
    qkvn3 = qkvn.reshape(batch, seq, qkvn.shape[1])
    p3 = p.reshape(batch, seq, p.shape[1])
    cols3 = cols.reshape(batch, seq, LANES)
    gct4 = gct.reshape(batch, nc, GDN_HEADS, CHUNK)
    tile = lambda col: pl.BlockSpec((batch, g * CHUNK, hw), lambda c: (0, c, col))
    out = pl.pallas_call(
        _gdn_chunk_kernel,
        grid=(nc // g,),
        in_specs=[
            tile(0), tile(1), tile(2),
            tile(P_Z // P_GROUP),
            pl.BlockSpec((batch, g * CHUNK, LANES), lambda c: (0, c, 0)),
            pl.BlockSpec((batch, g, GDN_HEADS, CHUNK), lambda c: (0, c, 0, 0)),
            pl.BlockSpec((1, GDN_D), lambda c: (0, 0)),
        ],
        out_specs=pl.BlockSpec((batch, g * CHUNK, hw), lambda c: (0, c, 0)),
        out_shape=jax.ShapeDtypeStruct((batch, seq, hw), BF16),
        scratch_shapes=[pltpu.VMEM((batch * GDN_HEADS, GDN_D, GDN_D), F32)],
        compiler_params=_cp(("arbitrary",)),
        name="gdn_chunk",
    )(qkvn3, qkvn3, qkvn3, p3, cols3, gct4, norm_w)
    return out.reshape(batch * seq, hw)


def _rope(x, cos, sin_signed):
    lane = lax.broadcasted_iota(jnp.int32, x.shape, 1)
    fwd = pltpu.roll(x, LANES - MLA_ROPE // 2, 1)
    bwd = pltpu.roll(x, MLA_ROPE // 2, 1)
    rot = jnp.where(lane < MLA_ROPE // 2, fwd, bwd)
    return x * cos + rot * sin_signed


def _mla_prep_kernel(cq_ref, ckv_ref, kr_ref, cos_ref, sin_ref, cost_ref, sint_ref, qnw_ref, kvnw_ref,
                     wqt_ref, wkn_ref, wvt_ref, qt_ref, kn_ref, kro_ref, vt_ref):
    cos = cos_ref[...]
    sin = sin_ref[...]
    cos_t = cost_ref[...]
    sin_t = sint_ref[...]
    cq = _rms(cq_ref[...].astype(F32), qnw_ref[...]).astype(BF16)
    hd = 2 * LANES
    half = MLA_ROPE // 2
    scale = (MLA_NOPE + MLA_ROPE) ** -0.5 * LOG2_E
    for h in range(MLA_HEADS):
        qh = _dot_nt(wqt_ref[h * hd:(h + 1) * hd, :], cq) * scale
        lo = qh[MLA_NOPE:MLA_NOPE + half, :]
        hi = qh[MLA_NOPE + half:MLA_NOPE + MLA_ROPE, :]
        qt_ref[0, h * hd:h * hd + MLA_NOPE, :] = qh[0:MLA_NOPE, :].astype(BF16)
        qt_ref[0, h * hd + MLA_NOPE:h * hd + MLA_NOPE + half, :] = (lo * cos_t - hi * sin_t).astype(BF16)
        qt_ref[0, h * hd + MLA_NOPE + half:h * hd + MLA_NOPE + MLA_ROPE, :] = (hi * cos_t + lo * sin_t).astype(BF16)
        qt_ref[0, h * hd + MLA_NOPE + MLA_ROPE:(h + 1) * hd, :] = qh[MLA_NOPE + MLA_ROPE:hd, :].astype(BF16)
    kvl = _rms(ckv_ref[...].astype(F32), kvnw_ref[...]).astype(BF16)
    kn_ref[...] = _dot(kvl, wkn_ref[...]).astype(BF16)
    vt_ref[0] = _dot_nt(wvt_ref[...], kvl).astype(BF16)
    kro_ref[...] = _rope(kr_ref[...].astype(F32), cos, sin).astype(BF16)


def _mla_prep(p, tables, qnw, kvnw, wqt, wkn, wvt, seq, tm=512):
    cos_row, sin_row, cos_col, sin_col = tables
    t = p.shape[0]
    tiles_per_seq = seq // tm
    hw = MLA_HEADS * MLA_NOPE
    half = MLA_ROPE // 2
    cq_blk = P_CQ // MLA_Q_LORA
    ckv_blk = P_CKV // MLA_KV_LORA
    kr_blk = P_KR // LANES
    return pl.pallas_call(
        _mla_prep_kernel,
        grid=(t // tm,),
        in_specs=[
            pl.BlockSpec((tm, MLA_Q_LORA), lambda i: (i, cq_blk)),
            pl.BlockSpec((tm, MLA_KV_LORA), lambda i: (i, ckv_blk)),
            pl.BlockSpec((tm, LANES), lambda i: (i, kr_blk)),
            pl.BlockSpec((tm, LANES), lambda i: (i % tiles_per_seq, 0)),
            pl.BlockSpec((tm, LANES), lambda i: (i % tiles_per_seq, 0)),
            pl.BlockSpec((half, tm), lambda i: (0, i % tiles_per_seq)),
            pl.BlockSpec((half, tm), lambda i: (0, i % tiles_per_seq)),
            pl.BlockSpec((1, MLA_Q_LORA), lambda i: (0, 0)),
            pl.BlockSpec((1, MLA_KV_LORA), lambda i: (0, 0)),
            pl.BlockSpec((2 * hw, MLA_Q_LORA), lambda i: (0, 0)),
            pl.BlockSpec((MLA_KV_LORA, hw), lambda i: (0, 0)),
            pl.BlockSpec((hw, MLA_KV_LORA), lambda i: (0, 0)),
        ],
        out_specs=[
            pl.BlockSpec((1, 2 * hw, tm), lambda i: (i, 0, 0)),
            pl.BlockSpec((tm, hw), lambda i: (i, 0)),
            pl.BlockSpec((tm, LANES), lambda i: (i, 0)),
            pl.BlockSpec((1, hw, tm), lambda i: (i, 0, 0)),
        ],
        out_shape=[
            jax.ShapeDtypeStruct((t // tm, 2 * hw, tm), BF16),
            jax.ShapeDtypeStruct((t, hw), BF16),
            jax.ShapeDtypeStruct((t, LANES), BF16),
            jax.ShapeDtypeStruct((t // tm, hw, tm), BF16),
        ],
        compiler_params=_cp(("parallel",)),
        name="mla_prep",
    )(p, p, p, cos_row, sin_row, cos_col, sin_col, qnw, kvnw, wqt, wkn, wvt)


ATTN_HEADS_PER_STEP = 8
ATTN_SUM_ROWS = 16


def _mla_attn_kernel(qt_ref, kt_ref, q_ref, kn_ref, kr_ref, vt_ref, o_ref, m_ref, acc_ref):
    qi = qt_ref[pl.program_id(2)]
    ki = kt_ref[pl.program_id(2)]
    tq = q_ref.shape[2]
    tk = kn_ref.shape[0]
    hd = 2 * LANES
    q0 = qi * tq
    k0 = ki * tk
    last_k = (q0 + tq) // tk - 1

    @pl.when(ki == 0)
    def _():
        m_ref[...] = jnp.full_like(m_ref, NEG_BIG)
        acc_ref[...] = jnp.zeros_like(acc_ref)

    def step(masked):
        kr = kr_ref[...]
        ones = jnp.ones((ATTN_SUM_ROWS, tk), BF16)
        if masked:
            ck = lax.broadcasted_iota(jnp.int32, (tk, tq), 0) // CHUNK
            cq = lax.broadcasted_iota(jnp.int32, (tk, tq), 1) // CHUNK
            visible = (ck <= cq) if tq == tk else ((ck - cq) <= (q0 - k0) // CHUNK)

        def scores(h):
            k = jnp.concatenate([kn_ref[:, h * MLA_NOPE:(h + 1) * MLA_NOPE], kr], axis=1)
            s = _dot(k, q_ref[0, h * hd:(h + 1) * hd, :])
            if masked:
                s = jnp.where(visible, s, NEG_BIG)
            return s

        def update(h, s):
            m_prev = m_ref[h]
            m_new = jnp.maximum(m_prev, jnp.max(s, axis=0, keepdims=True))
            alpha = jnp.exp2(m_prev - m_new)
            p = jnp.exp2((s - m_new).astype(BF16))
            v_ext = jnp.concatenate([vt_ref[0, h * MLA_V:(h + 1) * MLA_V, :], ones], axis=0)
            acc_ref[h] = alpha * acc_ref[h] + _dot(v_ext, p)
            m_ref[h] = m_new

        s_prev = scores(0)
        for h in range(1, ATTN_HEADS_PER_STEP):
            s_next = scores(h)
            update(h - 1, s_prev)
            s_prev = s_next
        update(ATTN_HEADS_PER_STEP - 1, s_prev)

    @pl.when(k0 + tk <= q0)
    def _():
        step(False)

    def finish():
        for h in range(ATTN_HEADS_PER_STEP):
            acc = acc_ref[h]
            o = acc[0:MLA_V, :] / acc[MLA_V:MLA_V + 1, :]
            o_ref[:, h * MLA_V:(h + 1) * MLA_V] = o.T.astype(BF16)

    @pl.when(k0 + tk > q0)
    def _():
        step(True)
        if tq == tk:
            finish()
        else:
            pl.when(ki == last_k)(finish)


def _mla_attn(qt_all, kn, kr, vt, batch, seq):
    t = kn.shape[0]
    tq = qt_all.shape[2]
    tk = vt.shape[2]
    nq = seq // tq
    nk = seq // tk
    hps = ATTN_HEADS_PER_STEP
    pairs = [(qi, ki) for qi in range(nq) for ki in range((qi + 1) * tq // tk)]
    qt = jnp.asarray(np.array([pr[0] for pr in pairs], np.int32))
    kt = jnp.asarray(np.array([pr[1] for pr in pairs], np.int32))
    return pl.pallas_call(
        _mla_attn_kernel,
        grid_spec=pltpu.PrefetchScalarGridSpec(
            num_scalar_prefetch=2,
            grid=(batch, MLA_HEADS // hps, len(pairs)),
            in_specs=[
                pl.BlockSpec((1, hps * 2 * LANES, tq), lambda b, h, pr, qt, kt: (b * nq + qt[pr], h, 0)),
                pl.BlockSpec((tk, hps * MLA_NOPE), lambda b, h, pr, qt, kt: (b * nk + kt[pr], h)),
                pl.BlockSpec((tk, LANES), lambda b, h, pr, qt, kt: (b * nk + kt[pr], 0)),
                pl.BlockSpec((1, hps * MLA_V, tk), lambda b, h, pr, qt, kt: (b * nk + kt[pr], h, 0)),
            ],
            out_specs=pl.BlockSpec((tq, hps * MLA_V), lambda b, h, pr, qt, kt: (b * nq + qt[pr], h)),
            scratch_shapes=[
                pltpu.VMEM((hps, 1, tq), F32),
                pltpu.VMEM((hps, MLA_V + ATTN_SUM_ROWS, tq), F32),
            ],
        ),
        out_shape=jax.ShapeDtypeStruct((t, MLA_HEADS * MLA_V), BF16),
        compiler_params=_cp(("parallel", "parallel", "arbitrary")),
        name="mla_attn",
    )(qt, kt, qt_all, kn, kr, vt)


MIX_SUBTILES = 2


def _mix_out_kernel(x_ref, oa_ref, ob_ref, ga_ref, gb_ref, wga_ref, wmo_ref, wout_ref, nw_ref,
                    wr_hi_ref, wr_lo_ref, br_ref, x1_ref, h2_ref, selt_ref, cwt_ref):
    sub = br_ref.shape[1]

    def mix(r):
        ya = _dot(oa_ref[r, :], wga_ref[...])
        yb = _dot(ob_ref[r, :], wmo_ref[...])
        merged = (_sigmoid(ga_ref[r, :].astype(F32)) * ya
                  + _sigmoid(gb_ref[r, :].astype(F32)) * yb)
        x1 = x_ref[r, :] + _dot(merged.astype(BF16), wout_ref[...])
        x1_ref[r, :] = x1
        h2 = _rms(x1, nw_ref[...])
        h_hi = h2.astype(BF16)
        h2_ref[r, :] = h_hi
        h_lo = (h2 - h_hi.astype(F32)).astype(BF16)
        return (_dot_nt(wr_hi_ref[...], h_hi) + _dot_nt(wr_lo_ref[...], h_hi) + _dot_nt(wr_hi_ref[...], h_lo)
                + br_ref[...])[0:N_EXPERTS, :]

    def route(logits, r):
        expert = lax.broadcasted_iota(jnp.int32, logits.shape, 0)
        work = logits
        sel = jnp.zeros(logits.shape, F32)
        cw = jnp.zeros(logits.shape, F32)
        top = None
        denom = None
        for kk in range(TOP_K):
            mx = jnp.max(work, axis=0, keepdims=True)
            am = jnp.min(jnp.where(work == mx, expert, N_EXPERTS), axis=0, keepdims=True)
            hit = expert == am
            if kk == 0:
                top = mx
                e = jnp.ones_like(mx)
                denom = e
            else:
                e = jnp.exp(mx - top)
                denom = denom + e
            sel = jnp.where(hit, 1.0, sel)
            cw = jnp.where(hit, e, cw)
            work = jnp.where(hit, -jnp.inf, work)
        selt_ref[:, r] = sel.astype(BF16)
        cwt_ref[:, r] = cw / denom

    subs = [slice(n * sub, (n + 1) * sub) for n in range(x_ref.shape[0] // sub)]
    logits = [mix(r) for r in subs]
    for lg, r in zip(logits, subs):
        route(lg, r)


def _mix_out(x2, oa, ob, p, wga, wmo, wout, nw, wr_hi, wr_lo, br):
    t, d = x2.shape
    tm = MIX_SUBTILES * br.shape[1]
    full = lambda i: (0, 0)
    return pl.pallas_call(
        _mix_out_kernel,
        grid=(t // tm,),
        in_specs=[
            pl.BlockSpec((tm, d), lambda i: (i, 0)),
            pl.BlockSpec((tm, d), lambda i: (i, 0)),
            pl.BlockSpec((tm, d), lambda i: (i, 0)),
            pl.BlockSpec((tm, d), lambda i: (i, P_GATE_A // P_GROUP)),
            pl.BlockSpec((tm, d), lambda i: (i, P_GATE_B // P_GROUP)),
            pl.BlockSpec((d, d), full),
            pl.BlockSpec((d, d), full),
            pl.BlockSpec((d, d), full),
            pl.BlockSpec((1, d), full),
            pl.BlockSpec((LANES, d), full),
            pl.BlockSpec((LANES, d), full),
            pl.BlockSpec(br.shape, full),
        ],
        out_specs=[
            pl.BlockSpec((tm, d), lambda i: (i, 0)),
            pl.BlockSpec((tm, d), lambda i: (i, 0)),
            pl.BlockSpec((N_EXPERTS, tm), lambda i: (0, i)),
            pl.BlockSpec((N_EXPERTS, tm), lambda i: (0, i)),
        ],
        out_shape=[
            jax.ShapeDtypeStruct((t, d), F32),
            jax.ShapeDtypeStruct((t, d), BF16),
            jax.ShapeDtypeStruct((N_EXPERTS, t), BF16),
            jax.ShapeDtypeStruct((N_EXPERTS, t), F32),
        ],
        compiler_params=_cp(("parallel",)),
        name="mix_out",
    )(x2, oa, ob, p, p, wga, wmo, wout, nw, wr_hi, wr_lo, br)


def _route_pos_kernel(selt_ref, lpos_ref, keyt_ref, offs_ref, cnt_ref, tot_ref, carry_ref):
    tm = selt_ref.shape[1]
    i = pl.program_id(0)

    @pl.when(i == 0)
    def _():
        carry_ref[...] = jnp.zeros_like(carry_ref)

    sel_t = selt_ref[...]
    sel_rows = jnp.concatenate([sel_t, jnp.zeros((LANES - N_EXPERTS, tm), BF16)], axis=0)
    row = lax.broadcasted_iota(jnp.int32, (tm, tm), 0)
    col = lax.broadcasted_iota(jnp.int32, (tm, tm), 1)
    lpos_ref[...] = _dot_nt((col < row).astype(BF16), sel_rows)
    pos_t = _dot(sel_t, (row < col).astype(BF16))
    keyt_ref[0] = jnp.where(sel_t > 0, pos_t, -1.0)

    n = _dot_nt(jnp.ones((8, tm), BF16), sel_rows)[0:1, :]
    carry = carry_ref[0:1, :]
    offs_ref[0] = carry.astype(jnp.int32)
    cnt_ref[0] = n.astype(jnp.int32)
    total = carry + jnp.ceil(n * (1.0 / SEG_ALIGN)) * SEG_ALIGN
    carry_ref[...] = jnp.broadcast_to(total, carry_ref.shape)
    tot_ref[...] = jnp.broadcast_to(total, tot_ref.shape).astype(jnp.int32)


def _route_pos(sel_t):
    t = sel_t.shape[1]
    tm = ROUTE_TILE
    nt = t // tm
    return pl.pallas_call(
        _route_pos_kernel,
        grid=(nt,),
        in_specs=[pl.BlockSpec((N_EXPERTS, tm), lambda i: (0, i))],
        out_specs=[
            pl.BlockSpec((tm, LANES), lambda i: (i, 0)),
            pl.BlockSpec((1, N_EXPERTS, tm), lambda i: (i, 0, 0)),
            pl.BlockSpec((1, 1, LANES), lambda i: (i, 0, 0)),
            pl.BlockSpec((1, 1, LANES), lambda i: (i, 0, 0)),
            pl.BlockSpec((8, LANES), lambda i: (0, 0)),
        ],
        out_shape=[
            jax.ShapeDtypeStruct((t, LANES), F32),
            jax.ShapeDtypeStruct((nt, N_EXPERTS, tm), F32),
            jax.ShapeDtypeStruct((nt, 1, LANES), jnp.int32),
            jax.ShapeDtypeStruct((nt, 1, LANES), jnp.int32),
            jax.ShapeDtypeStruct((8, LANES), jnp.int32),
        ],
        scratch_shapes=[pltpu.VMEM((8, LANES), F32)],
        compiler_params=_cp(("arbitrary",)),
        name="route_pos",
    )(sel_t)


def _rows(ref, start, n):
    return ref.at[pl.ds(pl.multiple_of(start, n), n)]


def _pack_pairs(x):
    half = x.shape[1] // 2
    hi = lax.bitcast_convert_type(x[:, :half], jnp.uint32)
    lo = lax.bitcast_convert_type(x[:, half:], jnp.uint32)
    return hi | lax.shift_right_logical(lo, jnp.uint32(16))


def _unpack_pairs(u):
    hi = lax.bitcast_convert_type(u & jnp.uint32(0xFFFF0000), F32).astype(BF16)
    lo = lax.bitcast_convert_type(lax.shift_left(u, jnp.uint32(16)), F32).astype(BF16)
    return hi, lo


def _seg_windows(cnt_ref, base):
    longest = lax.fori_loop(0, N_EXPERTS, lambda e, m: jnp.maximum(m, cnt_ref[base + e]), 0)
    return lax.shift_right_logical(longest + (SEG_WINDOW - 1), SEG_WINDOW.bit_length() - 1)


def _dispatch_kernel(seg_ref, cnt_ref, fill_lo_ref, fill_hi_ref, h_ref, keyt_ref, xs_ref,
                     stage_ref, zero_ref, sem):
    i = pl.program_id(0)
    tm = h_ref.shape[0]
    base = i * N_EXPERTS

    def zero_fills(wait):
        def act(cp):
            if wait:
                cp.wait()
            else:
                cp.start()

        def fill(c):
            return pltpu.make_async_copy(zero_ref.at[pl.ds(0, SEG_ALIGN)], _rows(xs_ref, c * SEG_ALIGN, SEG_ALIGN),
                                         sem.at[2])

        def per_expert(e, carry):
            lo = lax.shift_right_logical(fill_lo_ref[e], SEG_ALIGN.bit_length() - 1)
            hi = lax.shift_right_logical(fill_hi_ref[e], SEG_ALIGN.bit_length() - 1)
            return lax.fori_loop(lo, hi, lambda c, a: (act(fill(c)), a)[1], carry)

        lax.fori_loop(0, N_EXPERTS, per_expert, 0)

        def fill_tail(c):
            return pltpu.make_async_copy(zero_ref, _rows(xs_ref, c * ZERO_ROWS, ZERO_ROWS), sem.at[2])

        lo = lax.shift_right_logical(fill_hi_ref[N_EXPERTS - 1], ZERO_ROWS.bit_length() - 1)
        hi = xs_ref.shape[0] // ZERO_ROWS
        lax.fori_loop(lo, hi, lambda c, a: (act(fill_tail(c)), a)[1], 0)

    @pl.when(i == 0)
    def _():
        zero_ref[...] = jnp.zeros_like(zero_ref)
        zero_fills(wait=False)
        zero_fills(wait=True)

    buf = i % 2

    def send(tile_base, win, b, wait, only_live=False):
        for e in range(N_EXPERTS):
            def go(e=e):
                slot = seg_ref[tile_base + e] + win * SEG_WINDOW
                cp = pltpu.make_async_copy(stage_ref.at[b, pl.ds(e * SEG_WINDOW, SEG_WINDOW)],
                                           xs_ref.at[pl.ds(pl.multiple_of(slot, SEG_ALIGN), SEG_WINDOW)],
                                           sem.at[b])
                if wait:
                    cp.wait()
                else:
                    cp.start()

            if only_live:
                pl.when(cnt_ref[tile_base + e] > win * SEG_WINDOW)(go)
            else:
                go()

    half = N_EXPERTS // 2
    j = lax.broadcasted_iota(jnp.int32, (SEG_WINDOW, tm), 0).astype(F32)

    def build(win, b):
        key = keyt_ref[0] - jnp.asarray(win * SEG_WINDOW, F32)
        for hf in range(2):
            pick = jnp.concatenate([(key[e:e + 1, :] == j).astype(BF16)
                                    for e in range(hf * half, (hf + 1) * half)], axis=0)
            stage_ref[b, hf * half * SEG_WINDOW:(hf + 1) * half * SEG_WINDOW, :] = _pack_pairs(_dot(pick, h_ref[...]))

    n_win = _seg_windows(cnt_ref, base)
    build(0, buf)

    @pl.when(i > 0)
    def _():
        send(base - N_EXPERTS, 0, 1 - buf, wait=True)

    send(base, 0, buf, wait=False)

    def more(win, carry):
        build(win, 1 - buf)
        send(base, win, 1 - buf, wait=False, only_live=True)
        send(base, win, 1 - buf, wait=True, only_live=True)
        return carry

    lax.fori_loop(1, n_win, more, 0)

    @pl.when(i == pl.num_programs(0) - 1)
    def _():
        send(base, 0, buf, wait=True)


def _dispatch(seg, cnt, fill_lo, fill_hi, h2, keyt, n_pad):
    t, d = h2.shape
    tm = ROUTE_TILE
    return pl.pallas_call(
        _dispatch_kernel,
        grid_spec=pltpu.PrefetchScalarGridSpec(
            num_scalar_prefetch=4,
            grid=(t // tm,),
            in_specs=[
                pl.BlockSpec((tm, d), lambda i, *_: (i, 0)),
                pl.BlockSpec((1, N_EXPERTS, tm), lambda i, *_: (i, 0, 0)),
            ],
            out_specs=pl.BlockSpec(memory_space=pl.ANY),
            scratch_shapes=[pltpu.VMEM((2, N_EXPERTS * SEG_WINDOW, d // 2), jnp.uint32),
                            pltpu.VMEM((ZERO_ROWS, d // 2), jnp.uint32),
                            pltpu.SemaphoreType.DMA((3,))],
        ),
        out_shape=jax.ShapeDtypeStruct((n_pad, d // 2), jnp.uint32),
        compiler_params=_cp(("arbitrary",)),
        name="dispatch",
    )(seg, cnt, fill_lo, fill_hi, h2, keyt)


def _experts_kernel(be_ref, nv_ref, xs_ref, wgu_ref, wd_ref, bg_ref, bu_ref, bd_ref, ys_ref,
                    wg_s, wu_s, wd_s):
    j = pl.program_id(0)
    grp = 2 * LANES
    prev = be_ref[jnp.maximum(j - 1, 0)]

    @pl.when((j == 0) | (be_ref[j] != prev))
    def _():
        r = lax.broadcasted_iota(jnp.int32, (grp, grp), 0)
        c = lax.broadcasted_iota(jnp.int32, (grp, grp), 1)
        src = jnp.where(c < LANES, 2 * c, 2 * (c - LANES) + 1)
        pick = (r == src).astype(BF16)
        for g in range(wgu_ref.shape[2] // grp):
            y = _dot(wgu_ref[0, :, g * grp:(g + 1) * grp].astype(BF16), pick)
            wg_s[:, g * LANES:(g + 1) * LANES] = y[:, 0:LANES].astype(BF16)
            wu_s[:, g * LANES:(g + 1) * LANES] = y[:, LANES:grp].astype(BF16)
        wd_s[...] = wd_ref[0].astype(BF16)

    @pl.when(j < nv_ref[0])
    def _():
        x = jnp.concatenate(_unpack_pairs(xs_ref[...]), axis=1)
        g = _dot(x, wg_s[...]) + bg_ref[0]
        u = _dot(x, wu_s[...]) + bu_ref[0]
        gate = jnp.minimum(g, SWIGLU_LIMIT)
        up = jnp.clip(u, -SWIGLU_LIMIT, SWIGLU_LIMIT)
        act = (up + 1.0) * (gate * _sigmoid(gate * SWIGLU_ALPHA))
        y = _dot(act.astype(BF16), wd_s[...]) + bd_ref[0]
        ys_ref[...] = _pack_pairs(y.astype(BF16).astype(F32))

    @pl.when(j >= nv_ref[0])
    def _():
        ys_ref[...] = jnp.zeros_like(ys_ref)


def _experts(block_e, n_valid, xs, wgu, wd, bg, bu, bd):
    n_pad = xs.shape[0]
    de, d = wd.shape[1:]
    blk = (MOE_ROWS, xs.shape[1])
    n_blocks = n_pad // MOE_ROWS
    xrow = lambda j, be, nv: (jnp.minimum(j, nv[0] - 1), 0)
    wsel = lambda j, be, nv: (be[j], 0, 0)
    return pl.pallas_call(
        _experts_kernel,
        grid_spec=pltpu.PrefetchScalarGridSpec(
            num_scalar_prefetch=2,
            grid=(n_blocks,),
            in_specs=[
                pl.BlockSpec(blk, xrow),
                pl.BlockSpec((1, d, 2 * de), wsel),
                pl.BlockSpec((1, de, d), wsel),
                pl.BlockSpec((1, 1, de), wsel),
                pl.BlockSpec((1, 1, de), wsel),
                pl.BlockSpec((1, 1, d), wsel),
            ],
            out_specs=pl.BlockSpec(blk, lambda j, be, nv: (j, 0)),
            scratch_shapes=[
                pltpu.VMEM((d, de), BF16),
                pltpu.VMEM((d, de), BF16),
                pltpu.VMEM((de, d), BF16),
            ],
        ),
        out_shape=jax.ShapeDtypeStruct(xs.shape, xs.dtype),
        compiler_params=pltpu.CompilerParams(dimension_semantics=("arbitrary",),
                                             vmem_limit_bytes=EXPERTS_VMEM_LIMIT),
        name="experts",
    )(block_e, n_valid, xs, wgu, wd, bg, bu, bd)


def _combine_kernel(seg_ref, cnt_ref, x1_ref, lpos_ref, cw_ref, nw_ref, ys_ref, o_ref, stage_ref, sem,
                    *, final_norm):
    i = pl.program_id(0)
    tm, d = x1_ref.shape
    base = i * N_EXPERTS

    buf = i % 2

    def gather(tile_base, win, b, wait):
        for e in range(N_EXPERTS):
            slot = seg_ref[tile_base + e] + win * SEG_WINDOW
            cp = pltpu.make_async_copy(ys_ref.at[pl.ds(pl.multiple_of(slot, SEG_ALIGN), SEG_WINDOW)],
                                       stage_ref.at[b, pl.ds(e * SEG_WINDOW, SEG_WINDOW)], sem.at[b])
            if wait:
                cp.wait()
            else:
                cp.start()

    @pl.when(i == 0)
    def _():
        gather(base, 0, buf, wait=False)

    @pl.when(i + 1 < pl.num_programs(0))
    def _():
        gather(base + N_EXPERTS, 0, 1 - buf, wait=False)

    n_stage = N_EXPERTS * SEG_WINDOW
    owner = lax.broadcasted_iota(jnp.int32, (LANES, n_stage), 1) // SEG_WINDOW
    expand = (owner == lax.broadcasted_iota(jnp.int32, (LANES, n_stage), 0)).astype(BF16)
    j = (lax.broadcasted_iota(jnp.int32, (tm, n_stage), 1) % SEG_WINDOW).astype(F32)
    cw_rows = jnp.concatenate([cw_ref[...].astype(BF16), jnp.zeros((LANES - N_EXPERTS, tm), BF16)], axis=0)
    cw_wide = _dot_tn(cw_rows, expand)

    def window(win, y):
        rank = (lpos_ref[...] - jnp.asarray(win * SEG_WINDOW, F32)).astype(BF16)
        take = jnp.where(_dot(rank, expand) == j, cw_wide, 0.0).astype(BF16)
        gather(base, win, buf, wait=True)
        hi, lo = _unpack_pairs(stage_ref[buf])
        return y + jnp.concatenate([_dot(take, hi), _dot(take, lo)], axis=1)

    def more(win, y):
        gather(base, win, buf, wait=False)
        return window(win, y)

    y = window(0, jnp.zeros((tm, d), F32))
    y = lax.fori_loop(1, _seg_windows(cnt_ref, base), more, y)
    out = x1_ref[...] + y
    if final_norm:
        out = _rms(out, nw_ref[...])
    o_ref[...] = out


def _combine(seg, cnt, x1, lpos, cw, nw, ys, final_norm):
    t, d = x1.shape
    tm = ROUTE_TILE
    kern = functools.partial(_combine_kernel, final_norm=final_norm)
    return pl.pallas_call(
        kern,
        grid_spec=pltpu.PrefetchScalarGridSpec(
            num_scalar_prefetch=2,
            grid=(t // tm,),
            in_specs=[
                pl.BlockSpec((tm, d), lambda i, *_: (i, 0)),
                pl.BlockSpec((tm, LANES), lambda i, *_: (i, 0)),
                pl.BlockSpec((N_EXPERTS, tm), lambda i, *_: (0, i)),
                pl.BlockSpec((1, d), lambda i, *_: (0, 0)),
                pl.BlockSpec(memory_space=pl.ANY),
            ],
            out_specs=pl.BlockSpec((tm, d), lambda i, *_: (i, 0)),
            scratch_shapes=[pltpu.VMEM((2, N_EXPERTS * SEG_WINDOW, d // 2), jnp.uint32),
                            pltpu.SemaphoreType.DMA((2,))],
        ),
        out_shape=jax.ShapeDtypeStruct((t, d), F32),
        compiler_params=_cp(("arbitrary",)),
        name="combine",
    )(seg, cnt, x1, lpos, cw, nw, ys)


def _rope_tables(seq):
    inv = 1.0 / (ROPE_THETA ** (jnp.arange(0, MLA_ROPE, 2, dtype=F32) / MLA_ROPE))
    ang = jnp.arange(seq, dtype=F32)[:, None] * inv[None, :]
    cos, sin = jnp.cos(ang), jnp.sin(ang)
    zeros = jnp.zeros((seq, LANES - MLA_ROPE), F32)
    cos_row = jnp.concatenate([cos, cos, zeros], axis=-1)
    sin_row = jnp.concatenate([-sin, sin, zeros], axis=-1)
    return cos_row, sin_row, cos.T, sin.T


def _pad_cols(a, width):
    return jnp.pad(a, ((0, 0), (0, width - a.shape[1])))


def _layer(x2, batch, seq, final_norm_w, final_norm, rope_tables,
           norm_mix_w, w_in, gdn_conv_w, gdn_a_log, gdn_dt_bias, gdn_norm_w, w_gdn_o,
           mla_q_norm_w, w_mla_q_b, mla_kv_norm_w, w_mla_kv_b, w_mla_o, w_out,
           norm_ffn_w, w_router, b_router, w_gate_up, b_gate_up, w_down, b_down):
    t, d = x2.shape
    qk_w = GDN_HEADS * GDN_D
    assert d == P_GROUP and qk_w == P_GROUP
    o_b = 4 * qk_w
    o_a = o_b + GDN_HEADS
    o_cq = o_a + GDN_HEADS
    o_ckv = o_cq + MLA_Q_LORA
    o_kr = o_ckv + MLA_KV_LORA
    o_ga = o_kr + MLA_ROPE
    o_gb = o_ga + d
    w_p = jnp.concatenate([
        w_in[:, 0:o_b], w_in[:, o_ga:o_gb + d], w_in[:, o_cq:o_ckv], w_in[:, o_ckv:o_kr],
        _pad_cols(w_in[:, o_kr:o_ga], P_WIDTH - P_KR)], axis=1).astype(BF16)
    w_ab = _pad_cols(jnp.concatenate([w_in[:, o_a:o_cq], w_in[:, o_b:o_a]], axis=1), LANES).astype(BF16)

    p, ab = _in_proj(x2, norm_mix_w[None, :], w_p, w_ab)

    alog_row = _pad_cols(gdn_a_log[None, :].astype(F32), LANES)
    dtb_row = _pad_cols(gdn_dt_bias[None, :].astype(F32), LANES)
    qkvn, cols, gct = _gdn_prep(p, ab, gdn_conv_w.astype(F32), alog_row, dtb_row, seq)
    o_gdn = _gdn_chunk(qkvn, p, cols, gct, gdn_norm_w[None, :].astype(F32), batch, seq)

    hd = MLA_NOPE + MLA_ROPE
    wq = w_mla_q_b.reshape(MLA_Q_LORA, MLA_HEADS, hd)
    wqt = jnp.pad(wq, ((0, 0), (0, 0), (0, 2 * LANES - hd))).reshape(MLA_Q_LORA, MLA_HEADS * 2 * LANES).T
    wkv = w_mla_kv_b.reshape(MLA_KV_LORA, MLA_HEADS, MLA_NOPE + MLA_V)
    wkn = wkv[:, :, :MLA_NOPE].reshape(MLA_KV_LORA, -1)
    wvt = wkv[:, :, MLA_NOPE:].reshape(MLA_KV_LORA, -1).T
    qt, kn, kr, vt = _mla_prep(p, rope_tables, mla_q_norm_w[None, :].astype(F32),
                               mla_kv_norm_w[None, :].astype(F32), wqt.astype(BF16), wkn.astype(BF16),
                               wvt.astype(BF16), seq)
    o_mla = _mla_attn(qt, kn, kr, vt, batch, seq)

    wr = _pad_cols(w_router.astype(F32), LANES).T
    wr_hi = wr.astype(BF16)
    wr_lo = (wr - wr_hi.astype(F32)).astype(BF16)
    br = jnp.broadcast_to(_pad_cols(b_router[None, :].astype(F32), LANES).T, (LANES, ROUTE_TILE))
    x1, h2, sel, cw = _mix_out(x2, o_gdn, o_mla, p, w_gdn_o.astype(BF16), w_mla_o.astype(BF16),
                               w_out.astype(BF16), norm_ffn_w[None, :].astype(F32), wr_hi, wr_lo, br)

    lpos, keyt, offs, cnt, tot = _route_pos(sel)
    n_tiles = t // ROUTE_TILE
    used = tot[0, :N_EXPERTS]
    padded = (used + SEG_WINDOW + MOE_ROWS - 1) // MOE_ROWS * MOE_ROWS
    pad_end = jnp.cumsum(padded)
    pad_start = pad_end - padded
    seg = (pad_start[None, :] + offs[:, 0, :N_EXPERTS]).astype(jnp.int32).reshape(-1)
    cnt = cnt[:, 0, :N_EXPERTS].reshape(-1)
    worst_used = t * TOP_K + n_tiles * N_EXPERTS * (SEG_ALIGN - 1) + N_EXPERTS * SEG_WINDOW
    n_pad = -(-worst_used // MOE_ROWS) * MOE_ROWS + (N_EXPERTS + 1) * MOE_ROWS
    n_blocks = n_pad // MOE_ROWS
    blk_start = jnp.arange(n_blocks, dtype=jnp.int32) * MOE_ROWS
    block_e = jnp.minimum(jnp.sum((pad_end[None, :] <= blk_start[:, None]).astype(jnp.int32), axis=1),
                          N_EXPERTS - 1).astype(jnp.int32)
    n_valid = (pad_end[-1:] // MOE_ROWS).astype(jnp.int32)

    xs = _dispatch(seg, cnt, (pad_start + used).astype(jnp.int32), pad_end.astype(jnp.int32), h2, keyt, n_pad)
    bg = b_gate_up[:, None, 0::2].astype(F32)
    bu = b_gate_up[:, None, 1::2].astype(F32)
    ys = _experts(block_e, n_valid, xs, w_gate_up, w_down, bg, bu, b_down[:, None, :].astype(F32))
    return _combine(seg, cnt, x1, lpos, cw, final_norm_w[None, :].astype(F32), ys, final_norm)


def kernel(x, norm_mix_w, w_in, gdn_conv_w, gdn_a_log, gdn_dt_bias, gdn_norm_w, w_gdn_o, mla_q_norm_w, w_mla_q_b, mla_kv_norm_w, w_mla_kv_b, w_mla_o, w_out, norm_ffn_w, w_router, b_router, w_gate_up, b_gate_up, w_down, b_down, norm_final_w):
    batch, seq, d = x.shape
    depth = w_in.shape[0]
    rope_tables = _rope_tables(seq)
    x2 = x.reshape(batch * seq, d)
    for layer in range(depth):
        x2 = _layer(x2, batch, seq, norm_final_w, layer == depth - 1, rope_tables,
                    norm_mix_w[layer], w_in[layer], gdn_conv_w[layer], gdn_a_log[layer],
                    gdn_dt_bias[layer], gdn_norm_w[layer], w_gdn_o[layer], mla_q_norm_w[layer],
                    w_mla_q_b[layer], mla_kv_norm_w[layer], w_mla_kv_b[layer], w_mla_o[layer],
                    w_out[layer], norm_ffn_w[layer], w_router[layer], b_router[layer],
                    w_gate_up[layer], b_gate_up[layer], w_down[layer], b_down[layer])
    return x2.reshape(batch, seq, d)
```
